```python
import functools
import jax, jax.numpy as jnp
from jax import lax
import numpy as np

D_MODEL = 1024
BATCH = 32
SEQ = 256
DEPTH = 2
DEC_BATCH = 4
DEC_SEQ = 2048
PAST_LEN = 512

GRID_W = 64
HEAD_DIM = 64
H_A = 8
KV_A = 2
H_B = 8
KV_B = 2
H_C = 8
WINDOW_A = 128
Q_BLOCK = 128
NA_ROWS = 8
NA_COLS = 16
ROPE_THETA = 10000.0
ROPE_PAIRS = HEAD_DIM // 4
N_EXPERTS = 32
TOP_K = 4
D_FF = D_MODEL
SWIGLU_ALPHA = 1.702
SWIGLU_LIMIT = 7.0
MOE_BLOCK = 128
EPS = 1e-6
W_A = H_A * HEAD_DIM
W_B = H_B * HEAD_DIM
W_C = H_C * HEAD_DIM
IN_SIZES = (W_A, KV_A * HEAD_DIM, KV_A * HEAD_DIM,
            W_B, KV_B * HEAD_DIM, KV_B * HEAD_DIM,
            W_C, W_C, W_C, 3 * D_MODEL)
IN_SPLITS = tuple(sum(IN_SIZES[:i + 1]) for i in range(len(IN_SIZES) - 1))
D_IN = sum(IN_SIZES)

kernel_name = 'hybrid_diffusion_trunk_step'


def rmsnorm(x, g):
    xf = x.astype(jnp.float32)
    y = xf * lax.rsqrt(jnp.mean(xf * xf, axis=-1, keepdims=True) + EPS)
    return (y * g.astype(jnp.float32)).astype(x.dtype)


def adaln(cond, w, b):
    return jnp.split(jax.nn.silu(cond) @ w + b, 6, axis=-1)


def gqa(q, n_kv):
    B, S, H, hd = q.shape
    return q.reshape(B, S, n_kv, H // n_kv, hd)


def axial_rope_angles(n_tok):
    t = jnp.arange(n_tok)
    inv = ROPE_THETA ** (-jnp.arange(ROPE_PAIRS, dtype=jnp.float32) / ROPE_PAIRS)
    row = (t // GRID_W).astype(jnp.float32)[:, None] * inv
    col = (t % GRID_W).astype(jnp.float32)[:, None] * inv
    return jnp.stack([row, col], axis=1)


def apply_rope(x, ang):
    xr = x.reshape(*x.shape[:-1], 2, 2, ROPE_PAIRS)
    x1, x2 = xr[..., 0, :], xr[..., 1, :]
    cos = jnp.cos(ang)[:, None].astype(x.dtype)
    sin = jnp.sin(ang)[:, None].astype(x.dtype)
    return jnp.stack([x1 * cos - x2 * sin, x1 * sin + x2 * cos], axis=-2).reshape(x.shape)


def softmax_with_sink(s, sink):
    if sink is None:
        return jax.nn.softmax(s, axis=-1)
    col = jnp.broadcast_to(sink.astype(jnp.float32)[:, :, None, None], s.shape[:-1] + (1,))
    return jax.nn.softmax(jnp.concatenate([s, col], axis=-1), axis=-1)[..., :-1]


def dense_attention(q, k, v, sink=None):
    B, Sq = q.shape[:2]
    nb = Sq // Q_BLOCK
    q_blocks = jnp.moveaxis(q.reshape(B, nb, Q_BLOCK, *q.shape[2:]), 1, 0)
    scale = HEAD_DIM ** -0.5

    def attend_block(qb):
        s = jnp.einsum('bqkgd,bskd->bkgqs', qb, k, preferred_element_type=jnp.float32) * scale
        p = softmax_with_sink(s, sink).astype(v.dtype)
        return jnp.einsum('bkgqs,bskd->bqkgd', p, v)

    o = lax.map(attend_block, q_blocks)
    return jnp.moveaxis(o, 0, 1).reshape(q.shape)


def window_attention(q, k, v, ck, cv, sink):
    B, S, KV, G, hd = q.shape
    nb = S // Q_BLOCK
    scale = HEAD_DIM ** -0.5
    pad = ((0, 0), (Q_BLOCK, Q_BLOCK), (0, 0), (0, 0))
    kp = jnp.pad(k, pad).reshape(B, nb + 2, Q_BLOCK, KV, hd)
    vp = jnp.pad(v, pad).reshape(B, nb + 2, Q_BLOCK, KV, hd)
    k_band = jnp.concatenate([kp[:, :-2], kp[:, 1:-1], kp[:, 2:]], axis=2)
    v_band = jnp.concatenate([vp[:, :-2], vp[:, 1:-1], vp[:, 2:]], axis=2)
    qb = q.reshape(B, nb, Q_BLOCK, KV, G, hd)
    s_loc = jnp.einsum('bnqkgd,bnskd->bnkgqs', qb, k_band, preferred_element_type=jnp.float32) * scale
    qpos = jnp.arange(Q_BLOCK)[:, None]
    kofs = jnp.arange(3 * Q_BLOCK)[None, :] - Q_BLOCK
    kabs = jnp.arange(nb)[:, None, None] * Q_BLOCK + kofs
    mask = (jnp.abs(kofs - qpos) <= WINDOW_A)[None] & (kabs >= 0) & (kabs < S)
    s_loc = jnp.where(mask[None, :, None, None], s_loc, -jnp.inf)
    s_ctx = jnp.einsum('bnqkgd,bpkd->bnkgqp', qb, ck, preferred_element_type=jnp.float32) * scale
    p = softmax_with_sink(jnp.concatenate([s_loc, s_ctx], axis=-1), sink).astype(v.dtype)
    p_loc, p_ctx = p[..., :3 * Q_BLOCK], p[..., 3 * Q_BLOCK:]
    o = (jnp.einsum('bnkgqs,bnskd->bnqkgd', p_loc, v_band)
         + jnp.einsum('bnkgqp,bpkd->bnqkgd', p_ctx, cv))
    return o.reshape(B, S, KV, G, hd)


def neighbourhood_attention(q, k, v, ck, cv, rpb):
    B, S, H, hd = q.shape
    rows = S // GRID_W
    win_r = min(NA_ROWS, rows)
    n_cb = GRID_W // NA_COLS
    slab_w = 2 * NA_COLS
    scale = HEAD_DIM ** -0.5
    r = jnp.arange(rows)
    row_idx = jnp.clip(r - win_r // 2, 0, rows - win_r)[:, None] + jnp.arange(win_r)
    cb = jnp.arange(n_cb)
    col_idx = jnp.clip(cb * NA_COLS - NA_COLS // 2, 0, GRID_W - slab_w)[:, None] + jnp.arange(slab_w)
    kg = k.reshape(B, rows, GRID_W, H, hd)
    vg = v.reshape(B, rows, GRID_W, H, hd)
    n_keys = win_r * slab_w
    k_nb = kg[:, row_idx[:, None, :, None], col_idx[None, :, None, :]].reshape(B, rows, n_cb, n_keys, H, hd)
    v_nb = vg[:, row_idx[:, None, :, None], col_idx[None, :, None, :]].reshape(B, rows, n_cb, n_keys, H, hd)
    qg = q.reshape(B, rows, n_cb, NA_COLS, H, hd)
    s_loc = jnp.einsum('brcqhd,brcshd->brchqs', qg, k_nb, preferred_element_type=jnp.float32) * scale
    q_col = cb[:, None] * NA_COLS + jnp.arange(NA_COLS)
    c_start = jnp.clip(q_col - NA_COLS // 2, 0, GRID_W - NA_COLS)
    key_col = col_idx[:, None, :]
    valid = (key_col >= c_start[..., None]) & (key_col < c_start[..., None] + NA_COLS)
    valid = jnp.broadcast_to(valid[:, :, None, :], (n_cb, NA_COLS, win_r, slab_w)).reshape(n_cb, NA_COLS, n_keys)
    dr = row_idx - r[:, None]
    dc = key_col - q_col[..., None]
    bias = rpb[:, dr[:, None, None, :, None] + NA_ROWS - 1, dc[None, :, :, None, :] + NA_COLS - 1]
    bias = jnp.transpose(bias.reshape(H, rows, n_cb, NA_COLS, n_keys), (1, 2, 0, 3, 4))
    s_loc = jnp.where(valid[None, None, :, None], s_loc + bias.astype(jnp.float32)[None], -jnp.inf)
    s_ctx = jnp.einsum('brcqhd,bphd->brchqp', qg, ck, preferred_element_type=jnp.float32) * scale
    p = jax.nn.softmax(jnp.concatenate([s_loc, s_ctx], axis=-1), axis=-1).astype(v.dtype)
    o = (jnp.einsum('brchqs,brcshd->brcqhd', p[..., :n_keys], v_nb)
         + jnp.einsum('brchqp,bphd->brcqhd', p[..., n_keys:], cv))
    return o.reshape(B, S, H, hd)


def routed_ffn(x, lw):
    D = x.shape[-1]
    xt = x.reshape(-1, D)
    T = xt.shape[0]
    logits = (xt @ lw['w_router']).astype(jnp.float32) + lw['b_router'].astype(jnp.float32)
    top_val, top_idx = lax.top_k(logits, TOP_K)
    gate = jax.nn.softmax(top_val, axis=-1)
    flat_e = top_idx.reshape(-1)
    order = jnp.argsort(flat_e)
    sorted_e = flat_e[order]
    tok = order // TOP_K
    counts = jnp.bincount(flat_e, length=N_EXPERTS)
    padded = (counts + MOE_BLOCK - 1) // MOE_BLOCK * MOE_BLOCK
    pad_end = jnp.cumsum(padded)
    pad_start = pad_end - padded
    start = jnp.cumsum(counts) - counts
    dest = pad_start[sorted_e] + jnp.arange(T * TOP_K) - start[sorted_e]
    n_blocks = -(-(T * TOP_K) // MOE_BLOCK) + N_EXPERTS
    slot_tok = jnp.full((n_blocks * MOE_BLOCK,), T, dtype=jnp.int32).at[dest].set(tok.astype(jnp.int32))
    block_e = jnp.minimum(jnp.searchsorted(pad_end, jnp.arange(n_blocks) * MOE_BLOCK, side='right'), N_EXPERTS - 1)
    x_pad = jnp.concatenate([xt, jnp.zeros((1, D), xt.dtype)], axis=0)
    x_blocks = x_pad[slot_tok].reshape(n_blocks, MOE_BLOCK, D)

    def expert_block(args):
        xb, e = args
        h = xb @ lw['w_gate_up'][e] + lw['b_gate_up'][e]
        x_glu = jnp.minimum(h[:, :D_FF], SWIGLU_LIMIT)
        x_lin = jnp.clip(h[:, D_FF:], -SWIGLU_LIMIT, SWIGLU_LIMIT)
        act = x_glu * jax.nn.sigmoid(SWIGLU_ALPHA * x_glu) * (x_lin + 1)
        return act @ lw['w_down'][e] + lw['b_down'][e]

    y_slots = lax.map(expert_block, (x_blocks, block_e)).reshape(-1, D)
    y_assign = y_slots[dest] * gate.reshape(-1)[order][:, None].astype(xt.dtype)
    return jnp.zeros_like(xt).at[tok].add(y_assign).reshape(x.shape)


def project_heads(hn, lw):
    B, S, _ = hn.shape
    qa, ka, va, qb, kb, vb, qc, kc, vc, gl = jnp.split(hn @ lw['w_in'], IN_SPLITS, axis=-1)
    heads = lambda t, n: t.reshape(B, S, n, HEAD_DIM)
    qb = rmsnorm(heads(qb, H_B), lw['g_q_b'])
    kb = rmsnorm(heads(kb, KV_B), lw['g_k_b'])
    return (heads(qa, H_A), heads(ka, KV_A), heads(va, KV_A), qb, kb, heads(vb, KV_B),
            heads(qc, H_C), heads(kc, H_C), heads(vc, H_C)), gl


def context_mixers(heads, lw):
    qa, ka, va, qb, kb, vb, qc, kc, vc = heads
    oa = dense_attention(gqa(qa, KV_A), ka, va, lw['sink_a'].reshape(KV_A, H_A // KV_A))
    ob = dense_attention(gqa(qb, KV_B), kb, vb)
    oc = dense_attention(gqa(qc, H_C), kc, vc)
    state = (jnp.stack([ka, va], axis=1), jnp.stack([kb, vb], axis=1), jnp.stack([kc, vc], axis=1))
    return (oa, ob, oc), state


def latent_mixers(heads, lw, ca, cb, cc, ang):
    qa, ka, va, qb, kb, vb, qc, kc, vc = heads
    oa = window_attention(gqa(apply_rope(qa, ang), KV_A), apply_rope(ka, ang), va,
                          ca[:, 0], ca[:, 1], lw['sink_a'].reshape(KV_A, H_A // KV_A))
    k_all = jnp.concatenate([apply_rope(kb, ang), cb[:, 0]], axis=1)
    v_all = jnp.concatenate([vb, cb[:, 1]], axis=1)
    ob = dense_attention(gqa(apply_rope(qb, ang), KV_B), k_all, v_all)
    oc = neighbourhood_attention(qc, kc, vc, cc[:, 0], cc[:, 1], lw['rpb_c'])
    return (oa, ob, oc), None


def sandwich_layer(h, mods, lw, mixers):
    sh1, sc1, g1, sh2, sc2, g2 = mods
    B, S, _ = h.shape
    hn = rmsnorm(h, lw['g_pre_mix']) * (1 + sc1) + sh1
    heads, gl = project_heads(hn, lw)
    (oa, ob, oc), state = mixers(heads)
    ga, gb, gc = jnp.split(jax.nn.sigmoid(gl), 3, axis=-1)
    merged = (ga * (oa.reshape(B, S, W_A) @ lw['w_br_a'])
              + gb * (ob.reshape(B, S, W_B) @ lw['w_br_b'])
              + gc * (oc.reshape(B, S, W_C) @ lw['w_br_c']))
    h = h + g1 * rmsnorm(merged @ lw['w_out'], lw['g_post_mix'])
    hn = rmsnorm(h, lw['g_pre_ffn']) * (1 + sc2) + sh2
    h = h + g2 * rmsnorm(routed_ffn(hn, lw), lw['g_post_ffn'])
    return h, state


def setup_inputs(seed: int = 0) -> dict:
    key = jax.random.key(seed)
    ks = jax.random.split(key, 32)
    f32 = jnp.float32
    nrm = lambda k, shape, s: jax.random.normal(k, shape, f32) * s
    gain = lambda k, shape: 1.0 + 0.05 * jax.random.normal(k, shape, f32)
    D = D_MODEL
    return {
        'x_prompt': nrm(ks[0], (BATCH, SEQ, D), 1.0),
        'x_sample': nrm(ks[1], (DEC_BATCH, DEC_SEQ, D), 1.0),
        'cache_a': nrm(ks[2], (DEC_BATCH, DEPTH, 2, PAST_LEN, KV_A, HEAD_DIM), 1.0),
        'cache_b': nrm(ks[3], (DEC_BATCH, DEPTH, 2, PAST_LEN, KV_B, HEAD_DIM), 1.0),
        'cache_c': nrm(ks[4], (DEC_BATCH, DEPTH, 2, PAST_LEN, H_C, HEAD_DIM), 1.0),
        'c': nrm(ks[5], (DEC_BATCH, D), 1.0),
        'c_ctx': nrm(ks[6], (D,), 1.0),
        'w_ada': nrm(ks[7], (DEPTH, D, 6 * D), 0.5 * D ** -0.5),
        'b_ada': nrm(ks[8], (DEPTH, 6 * D), 0.01),
        'g_pre_mix': gain(ks[9], (DEPTH, D)),
        'g_post_mix': gain(ks[10], (DEPTH, D)),
        'g_pre_ffn': gain(ks[11], (DEPTH, D)),
        'g_post_ffn': gain(ks[12], (DEPTH, D)),
        'w_in': nrm(ks[13], (DEPTH, D, D_IN), D ** -0.5),
        'g_q_b': gain(ks[14], (DEPTH, HEAD_DIM)),
        'g_k_b': gain(ks[15], (DEPTH, HEAD_DIM)),
        'sink_a': nrm(ks[16], (DEPTH, H_A), 0.5),
        'rpb_c': nrm(ks[17], (DEPTH, H_C, 2 * NA_ROWS - 1, 2 * NA_COLS - 1), 0.5),
        'w_br_a': nrm(ks[18], (DEPTH, W_A, D), W_A ** -0.5),
        'w_br_b': nrm(ks[19], (DEPTH, W_B, D), W_B ** -0.5),
        'w_br_c': nrm(ks[20], (DEPTH, W_C, D), W_C ** -0.5),
        'w_out': nrm(ks[21], (DEPTH, D, D), D ** -0.5),
        'w_router': nrm(ks[22], (DEPTH, D, N_EXPERTS), D ** -0.5),
        'b_router': nrm(ks[23], (DEPTH, N_EXPERTS), 0.01),
        'w_gate_up': nrm(ks[24], (DEPTH, N_EXPERTS, D, 2 * D_FF), D ** -0.5),
        'b_gate_up': nrm(ks[25], (DEPTH, N_EXPERTS, 2 * D_FF), 0.01),
        'w_down': nrm(ks[26], (DEPTH, N_EXPERTS, D_FF, D), D_FF ** -0.5),
        'b_down': nrm(ks[27], (DEPTH, N_EXPERTS, D), 0.01),
    }


def reference(x_prompt, x_sample, cache_a, cache_b, cache_c, c, c_ctx, w_ada, b_ada,
              g_pre_mix, g_post_mix, g_pre_ffn, g_post_ffn, w_in, g_q_b, g_k_b, sink_a, rpb_c,
              w_br_a, w_br_b, w_br_c, w_out, w_router, b_router, w_gate_up, b_gate_up,
              w_down, b_down):
    ang = axial_rope_angles(x_sample.shape[1])
    hp, hs = x_prompt, x_sample
    states_a, states_b, states_c = [], [], []
    for l in range(DEPTH):
        lw = {'g_pre_mix': g_pre_mix[l], 'g_post_mix': g_post_mix[l],
              'g_pre_ffn': g_pre_ffn[l], 'g_post_ffn': g_post_ffn[l],
              'w_in': w_in[l], 'g_q_b': g_q_b[l], 'g_k_b': g_k_b[l],
              'sink_a': sink_a[l], 'rpb_c': rpb_c[l],
              'w_br_a': w_br_a[l], 'w_br_b': w_br_b[l], 'w_br_c': w_br_c[l], 'w_out': w_out[l],
              'w_router': w_router[l], 'b_router': b_router[l],
              'w_gate_up': w_gate_up[l], 'b_gate_up': b_gate_up[l],
              'w_down': w_down[l], 'b_down': b_down[l]}
        mods_ctx = adaln(c_ctx, w_ada[l], b_ada[l])
        mods_lat = [m[:, None, :] for m in adaln(c, w_ada[l], b_ada[l])]
        hp, (sa, sb, sc) = sandwich_layer(hp, mods_ctx, lw, functools.partial(context_mixers, lw=lw))
        states_a.append(sa)
        states_b.append(sb)
        states_c.append(sc)
        hs, _ = sandwich_layer(hs, mods_lat, lw, functools.partial(
            latent_mixers, lw=lw, ca=cache_a[:, l], cb=cache_b[:, l], cc=cache_c[:, l], ang=ang))
    state_a = jnp.stack(states_a, axis=1)
    state_b = jnp.stack(states_b, axis=1)
    state_c = jnp.stack(states_c, axis=1)
    return (hp, hs, state_a, state_b, state_c)
```

```python
import functools

import jax
import jax.numpy as jnp
from jax import lax
from jax.experimental import pallas as pl
from jax.experimental.pallas import tpu as pltpu

D_MODEL = 1024
BATCH = 32
SEQ = 256
DEPTH = 2
DEC_BATCH = 4
DEC_SEQ = 2048
PAST_LEN = 512
GRID_W = 64
HEAD_DIM = 64
H_A = 8
KV_A = 2
H_B = 8
KV_B = 2
H_C = 8
WINDOW_A = 128
NA_ROWS = 8
NA_COLS = 16
ROPE_THETA = 10000.0
ROPE_PAIRS = HEAD_DIM // 4
N_EXPERTS = 32
TOP_K = 4
D_FF = D_MODEL
SWIGLU_ALPHA = 1.702
SWIGLU_LIMIT = 7.0
EPS = 1e-6

W_HEADS = H_A * HEAD_DIM
W_KV = KV_A * HEAD_DIM
N_CTX = BATCH * SEQ
N_LAT = DEC_BATCH * DEC_SEQ
N_TOK = N_CTX + N_LAT
GRID_ROWS = DEC_SEQ // GRID_W
D_IN = 3 * W_HEADS + 4 * W_KV + 2 * W_HEADS + 3 * D_MODEL

TM = 256
N_TILES = N_TOK // TM
CTX_TILES = N_CTX // TM
LAT_TILES_PER_BATCH = DEC_SEQ // TM
TQ = 128
MOE_TILE = 256
N_SLOTS = N_TOK * TOP_K + N_EXPERTS * MOE_TILE
N_MOE_BLOCKS = N_SLOTS // MOE_TILE
NEG = -1e30
VMEM_LIMIT = 56 * 1024 * 1024

BF16 = jnp.bfloat16
F32 = jnp.float32


def _cparams(sem):
    return pltpu.CompilerParams(dimension_semantics=sem, vmem_limit_bytes=VMEM_LIMIT)


def _mod_index(i):
    return jnp.where(i < CTX_TILES, 0, 1 + (i - CTX_TILES) // LAT_TILES_PER_BATCH)


def _rms(x):
    return x * lax.rsqrt(jnp.mean(x * x, axis=-1, keepdims=True) + EPS)


def _adaln_kernel(c_ref, w_ref, b_ref, o_ref):
    c = c_ref[...]
    s = c / (1.0 + jnp.exp(-c))
    o_ref[...] = jnp.dot(s, w_ref[...], preferred_element_type=F32,
                         precision=lax.Precision.HIGHEST) + b_ref[...]


def _adaln(cond8, w_ada, b_ada3, l):
    tn = 1536
    return pl.pallas_call(
        _adaln_kernel,
        grid=(6 * D_MODEL // tn,),
        in_specs=[pl.BlockSpec((8, D_MODEL), lambda j: (0, 0)),
                  pl.BlockSpec((None, D_MODEL, tn), lambda j: (l, 0, j)),
                  pl.BlockSpec((None, 1, tn), lambda j: (l, 0, j))],
        out_specs=pl.BlockSpec((8, tn), lambda j: (0, j)),
        out_shape=jax.ShapeDtypeStruct((8, 6 * D_MODEL), F32),
        compiler_params=_cparams(("arbitrary",)),
        name="adaln",
    )(cond8, w_ada, b_ada3)


C_QA, C_QB, C_QC, C_KC, C_VC, C_KAB, C_GL = 0, 512, 1024, 1536, 2048, 2560, 3072


def _proj_kernel(h_ref, sh_ref, sc_ref, gpre_ref, w_ref, gq_ref, gk_ref, bd_ref,
                 cos_ref, s1_ref, s2_ref, q_ref, kvab_ref, kvc_ref, gates_ref, st_ref):
    i = pl.program_id(0)
    hn = _rms(h_ref[...]) * gpre_ref[...]
    hb = (hn * (1.0 + sc_ref[...]) + sh_ref[...]).astype(BF16)
    cos, s1, s2 = cos_ref[...], s1_ref[...], s2_ref[...]
    bd = bd_ref[...]

    def proj(c0, n):
        return jnp.dot(hb, w_ref[:, c0:c0 + n], preferred_element_type=F32)

    def rope(t):
        parts = []
        for g in range(t.shape[1] // 128):
            tg = t[:, g * 128:(g + 1) * 128]
            parts.append(tg * cos + pltpu.roll(tg, 112, 1) * s1 + pltpu.roll(tg, 16, 1) * s2)
        return parts[0] if len(parts) == 1 else jnp.concatenate(parts, axis=1)

    def headnorm(t, g):
        sq = (t * t).astype(BF16)
        n = t.shape[1]
        if n == 128:
            ms = jnp.dot(sq, bd[:128, :128], preferred_element_type=F32)
        else:
            ms = jnp.concatenate(
                [jnp.dot(sq[:, c:c + 256], bd, preferred_element_type=F32) for c in range(0, n, 256)],
                axis=1)
        return t * lax.rsqrt(ms + EPS) * g

    q_ref[:, 0:512] = rope(proj(C_QA, 512)).astype(BF16)
    q_ref[:, 512:1024] = rope(headnorm(proj(C_QB, 512), gq_ref[...])).astype(BF16)
    q_ref[:, 1024:1536] = proj(C_QC, 512).astype(BF16)
    kc = proj(C_KC, 512)
    vc = proj(C_VC, 512)
    kvc_ref[:, 0:512] = kc.astype(BF16)
    kvc_ref[:, 512:1024] = vc.astype(BF16)
    kab = proj(C_KAB, 512)
    ka = kab[:, 0:128]
    kb = headnorm(kab[:, 128:256], gk_ref[...])
    kvab_ref[:, 0:128] = rope(ka).astype(BF16)
    kvab_ref[:, 128:256] = rope(kb).astype(BF16)
    kvab_ref[:, 256:512] = kab[:, 256:512].astype(BF16)
    for j in range(6):
        gl = proj(C_GL + j * 512, 512)
        gates_ref[:, j * 512:(j + 1) * 512] = (1.0 / (1.0 + jnp.exp(-gl))).astype(BF16)

    @pl.when(i < CTX_TILES)
    def _():
        st_ref[:, 0:128] = ka
        st_ref[:, 128:256] = kb
        st_ref[:, 256:512] = kab[:, 256:512]
        st_ref[:, 512:1024] = kc
        st_ref[:, 1024:1536] = vc


def _proj(h, mods3, g_pre3, w_in_p, gq, gk, bd, cos_t, s1_t, s2_t, l):
    def rope_idx(i):
        return jnp.where(i < CTX_TILES, LAT_TILES_PER_BATCH, (i - CTX_TILES) % LAT_TILES_PER_BATCH)

    const = lambda shape: pl.BlockSpec(shape, lambda i: (0,) * len(shape), pipeline_mode=pl.Buffered(1))
    rope_spec = pl.BlockSpec((TM, 128), lambda i: (rope_idx(i), 0))
    row = lambda n: pl.BlockSpec((TM, n), lambda i: (i, 0))
    return pl.pallas_call(
        _proj_kernel,
        grid=(N_TILES,),
        in_specs=[row(D_MODEL),
                  pl.BlockSpec((None, 1, D_MODEL), lambda i: (_mod_index(i), 0, 0)),
                  pl.BlockSpec((None, 1, D_MODEL), lambda i: (_mod_index(i), 0, 1)),
                  pl.BlockSpec((None, 1, D_MODEL), lambda i: (l, 0, 0)),
                  const((D_MODEL, D_IN)), const((1, 512)), const((1, 128)), const((256, 256)),
                  rope_spec, rope_spec, rope_spec],
        out_specs=[row(1536), row(512), row(1024), row(3072),
                   pl.BlockSpec((TM, 1536), lambda i: (jnp.minimum(i, CTX_TILES - 1), 0))],
        out_shape=[jax.ShapeDtypeStruct((N_TOK, 1536), BF16),
                   jax.ShapeDtypeStruct((N_TOK, 512), BF16),
                   jax.ShapeDtypeStruct((N_TOK, 1024), BF16),
                   jax.ShapeDtypeStruct((N_TOK, 3072), BF16),
                   jax.ShapeDtypeStruct((N_CTX, 1536), F32)],
        compiler_params=_cparams(("arbitrary",)),
        name="proj",
    )(h, mods3, mods3, g_pre3, w_in_p, gq, gk, bd, cos_t, s1_t, s2_t)


def _qk(q, k):
    return lax.dot_general(q, k, (((1,), (1,)), ((), ())), preferred_element_type=F32)


def _softmax_pv(scores, values, sink=None):
    m = functools.reduce(jnp.maximum, [jnp.max(s, axis=-1, keepdims=True) for s in scores])
    if sink is not None:
        m = jnp.maximum(m, sink)
    ps = [jnp.exp(s - m) for s in scores]
    den = functools.reduce(jnp.add, [jnp.sum(p, axis=-1, keepdims=True) for p in ps])
    if sink is not None:
        den = den + jnp.exp(sink - m)
    o = functools.reduce(jnp.add, [jnp.dot(p.astype(BF16), v, preferred_element_type=F32)
                                   for p, v in zip(ps, values)])
    return o / den


def _stack_heads(q, first, n):
    return jnp.concatenate([q[:, (first + g) * HEAD_DIM:(first + g + 1) * HEAD_DIM] for g in range(n)], axis=0)


def _unstack_heads(o, n):
    rows = o.shape[0] // n
    return jnp.concatenate([o[g * rows:(g + 1) * rows] for g in range(n)], axis=1)


def _sink_column(sink_ref, l, first, n, rows):
    return jnp.concatenate([jnp.full((rows, 1), sink_ref[l, first + g], F32) for g in range(n)], axis=0)


def _ctx_attn_kernel(l, sink_ref, q_ref, kvab_ref, kvc_ref, o_ref):
    q = q_ref[...]
    kvab = kvab_ref[...]
    kvc = kvc_ref[...]
    grp = H_A // KV_A
    for kv in range(KV_A):
        k = kvab[:, kv * 64:(kv + 1) * 64]
        v = kvab[:, 256 + kv * 64:256 + (kv + 1) * 64]
        qs = _stack_heads(q[:, 0:512], kv * grp, grp)
        o = _softmax_pv([_qk(qs, k)], [v], _sink_column(sink_ref, l, kv * grp, grp, TM))
        o_ref[:, kv * 256:(kv + 1) * 256] = _unstack_heads(o, grp).astype(BF16)
    for kv in range(KV_B):
        k = kvab[:, 128 + kv * 64:128 + (kv + 1) * 64]
        v = kvab[:, 384 + kv * 64:384 + (kv + 1) * 64]
        qs = _stack_heads(q[:, 512:1024], kv * grp, grp)
        o = _softmax_pv([_qk(qs, k)], [v])
        o_ref[:, 512 + kv * 256:512 + (kv + 1) * 256] = _unstack_heads(o, grp).astype(BF16)
    for hp in range(H_C // 2):
        outs = []
        for h in (2 * hp, 2 * hp + 1):
            qh = q[:, 1024 + h * 64:1024 + (h + 1) * 64]
            k = kvc[:, h * 64:(h + 1) * 64]
            v = kvc[:, 512 + h * 64:512 + (h + 1) * 64]
            outs.append(_softmax_pv([_qk(qh, k)], [v]))
        o_ref[:, 1024 + hp * 128:1024 + (hp + 1) * 128] = jnp.concatenate(outs, axis=1).astype(BF16)


def _ctx_attn(sink_a, q_all, kvab, kvc, l):
    row = lambda n: pl.BlockSpec((SEQ, n), lambda b: (b, 0))
    return pl.pallas_call(
        functools.partial(_ctx_attn_kernel, l),
        grid=(BATCH,),
        in_specs=[pl.BlockSpec(memory_space=pltpu.SMEM), row(1536), row(512), row(1024)],
        out_specs=row(1536),
        out_shape=jax.ShapeDtypeStruct((N_CTX, 1536), BF16),
        compiler_params=_cparams(("arbitrary",)),
        name="ctx_attn",
    )(sink_a, q_all, kvab, kvc)


def _win_attn_kernel(l, sink_ref, q_ref, prev_ref, cur_ref, nxt_ref, ck_ref, cv_ref, o_ref):
    n = pl.program_id(1)
    nb = DEC_SEQ // TQ
    q = q_ref[...]
    grp = H_A // KV_A
    rows = grp * TQ
    qpos = lax.broadcasted_iota(jnp.int32, (rows, TQ), 0) % TQ
    kpos = lax.broadcasted_iota(jnp.int32, (rows, TQ), 1)
    mask_prev = (kpos >= qpos) & (n > 0)
    mask_next = (kpos <= qpos) & (n < nb - 1)
    for kv in range(KV_A):
        ks = slice(kv * 64, (kv + 1) * 64)
        vs = slice(256 + kv * 64, 256 + (kv + 1) * 64)
        qs = _stack_heads(q, kv * grp, grp)
        s_prev = jnp.where(mask_prev, _qk(qs, prev_ref[:, ks]), NEG)
        s_cur = _qk(qs, cur_ref[:, ks])
        s_next = jnp.where(mask_next, _qk(qs, nxt_ref[:, ks]), NEG)
        s_ctx = _qk(qs, ck_ref[:, ks])
        o = _softmax_pv([s_prev, s_cur, s_next, s_ctx],
                        [prev_ref[:, vs], cur_ref[:, vs], nxt_ref[:, vs], cv_ref[:, ks]],
                        _sink_column(sink_ref, l, kv * grp, grp, TQ))
        o_ref[:, kv * 256:(kv + 1) * 256] = _unstack_heads(o, grp).astype(BF16)


def _win_attn(sink_a, q_all, kvab, cache_a, l):
    nb = DEC_SEQ // TQ
    base = N_CTX // TQ
    kv_spec = lambda f: pl.BlockSpec((TQ, 512), lambda b, n: (base + b * nb + f(n), 0))
    cache_spec = lambda s: pl.BlockSpec((None, None, None, PAST_LEN, W_KV), lambda b, n: (b, l, s, 0, 0))
    return pl.pallas_call(
        functools.partial(_win_attn_kernel, l),
        grid=(DEC_BATCH, nb),
        in_specs=[pl.BlockSpec(memory_space=pltpu.SMEM),
                  pl.BlockSpec((TQ, 512), lambda b, n: (base + b * nb + n, 0)),
                  kv_spec(lambda n: jnp.maximum(n - 1, 0)), kv_spec(lambda n: n),
                  kv_spec(lambda n: jnp.minimum(n + 1, nb - 1)),
                  cache_spec(0), cache_spec(1)],
        out_specs=pl.BlockSpec((TQ, 512), lambda b, n: (b * nb + n, 0)),
        out_shape=jax.ShapeDtypeStruct((N_LAT, 512), BF16),
        compiler_params=_cparams(("arbitrary", "arbitrary")),
        name="win_attn",
    )(sink_a, q_all, kvab, kvab, kvab, cache_a, cache_a)


def _dense_attn_kernel(q_ref, kv_ref, ck_ref, cv_ref, o_ref):
    q = q_ref[...]
    grp = H_B // KV_B
    for kv in range(KV_B):
        qs = _stack_heads(q, kv * grp, grp)
        k = kv_ref[:, 128 + kv * 64:128 + (kv + 1) * 64]
        v = kv_ref[:, 384 + kv * 64:384 + (kv + 1) * 64]
        ks = slice(kv * 64, (kv + 1) * 64)
        o = _softmax_pv([_qk(qs, k), _qk(qs, ck_ref[:, ks])], [v, cv_ref[:, ks]])
        o_ref[:, kv * 256:(kv + 1) * 256] = _unstack_heads(o, grp).astype(BF16)


def _dense_attn(q_all, kvab, cache_b, l):
    nb = DEC_SEQ // TQ
    base = N_CTX // TQ
    cache_spec = lambda s: pl.BlockSpec((None, None, None, PAST_LEN, W_KV), lambda b, n: (b, l, s, 0, 0))
    return pl.pallas_call(
        _dense_attn_kernel,
        grid=(DEC_BATCH, nb),
        in_specs=[pl.BlockSpec((TQ, 512), lambda b, n: (base + b * nb + n, 1)),
                  pl.BlockSpec((DEC_SEQ, 512), lambda b, n: (N_CTX // DEC_SEQ + b, 0)),
                  cache_spec(0), cache_spec(1)],
        out_specs=pl.BlockSpec((TQ, 512), lambda b, n: (b * nb + n, 0)),
        out_shape=jax.ShapeDtypeStruct((N_LAT, 512), BF16),
        compiler_params=_cparams(("arbitrary", "arbitrary")),
        name="dense_attn",
    )(q_all, kvab, cache_b, cache_b)


def _row_start(r):
    return jnp.clip(r - NA_ROWS // 2, 0, GRID_ROWS - NA_ROWS)


def _nbr_attn_kernel(q_ref, kv_ref, ck_ref, cv_ref, bias_ref, o_ref):
    r = pl.program_id(1)
    k0 = pl.multiple_of(_row_start(r) * GRID_W, GRID_W)
    n_keys = NA_ROWS * GRID_W
    q = q_ref[...]
    for hp in range(H_C // 2):
        outs = []
        for h in (2 * hp, 2 * hp + 1):
            hs = slice(h * 64, (h + 1) * 64)
            k = kv_ref[pl.ds(k0, n_keys), hs]
            v = kv_ref[pl.ds(k0, n_keys), 512 + h * 64:512 + (h + 1) * 64]
            qh = q[:, hs]
            s_loc = _qk(qh, k) + bias_ref[h]
            outs.append(_softmax_pv([s_loc, _qk(qh, ck_ref[:, hs])], [v, cv_ref[:, hs]]))
        o_ref[:, hp * 128:(hp + 1) * 128] = jnp.concatenate(outs, axis=1).astype(BF16)


def _nbr_attn(q_all, kvc, cache_c, bias_t, l):
    base = N_CTX // GRID_W
    cache_spec = lambda s: pl.BlockSpec((None, None, None, PAST_LEN, W_HEADS), lambda b, r: (b, l, s, 0, 0))
    return pl.pallas_call(
        _nbr_attn_kernel,
        grid=(DEC_BATCH, GRID_ROWS),
        in_specs=[pl.BlockSpec((GRID_W, 512), lambda b, r: (base + b * GRID_ROWS + r, 2)),
                  pl.BlockSpec((DEC_SEQ, 1024), lambda b, r: (N_CTX // DEC_SEQ + b, 0)),
                  cache_spec(0), cache_spec(1),
                  pl.BlockSpec((None, H_C, GRID_W, NA_ROWS * GRID_W),
                               lambda b, r: (_row_start(r) - r + NA_ROWS - 1, 0, 0, 0))],
        out_specs=pl.BlockSpec((GRID_W, 512), lambda b, r: (b * GRID_ROWS + r, 0)),
        out_shape=jax.ShapeDtypeStruct((N_LAT, 512), BF16),
        compiler_params=_cparams(("arbitrary", "arbitrary")),
        name="nbr_attn",
    )(q_all, kvc, cache_c, cache_c, bias_t)


def _nbr_bias_table(rpb_l):
    c = jnp.arange(GRID_W)[:, None]
    kc = jnp.arange(GRID_W)[None, :]
    c_start = jnp.clip(c - NA_COLS // 2, 0, GRID_W - NA_COLS)
    valid = (kc >= c_start) & (kc < c_start + NA_COLS)
    dc = jnp.clip(kc - c + NA_COLS - 1, 0, 2 * NA_COLS - 2)
    t = jnp.where(valid[None, None], rpb_l[:, :, dc].astype(F32), NEG)
    t = jnp.stack([t[:, o:o + NA_ROWS] for o in range(NA_ROWS)], axis=0)
    return jnp.transpose(t, (0, 1, 3, 2, 4)).reshape(NA_ROWS, H_C, GRID_W, NA_ROWS * GRID_W)


def _merge_kernel(octx_ref, oa_ref, ob_ref, oc_ref, gates_ref, h_ref, g1_ref, sh2_ref, sc2_ref,
                  wbr_ref, wout_ref, gpost_ref, gpre_ref, wr_ref, br_ref,
                  h2_ref, hn2_ref, logit_ref):
    i = pl.program_id(0)
    is_ctx = i < CTX_TILES
    merged = None
    for j, lat_ref in enumerate((oa_ref, ob_ref, oc_ref)):
        o = jnp.where(is_ctx, octx_ref[:, j * 512:(j + 1) * 512], lat_ref[...])
        br = jnp.dot(o, wbr_ref[j], preferred_element_type=F32)
        term = gates_ref[:, j * D_MODEL:(j + 1) * D_MODEL].astype(F32) * br
        merged = term if merged is None else merged + term
    t = jnp.dot(merged.astype(BF16), wout_ref[...], preferred_element_type=F32)
    h2 = h_ref[...] + g1_ref[...] * (_rms(t) * gpost_ref[...])
    h2_ref[...] = h2
    hn2 = _rms(h2) * gpre_ref[...] * (1.0 + sc2_ref[...]) + sh2_ref[...]
    hn2_ref[...] = hn2.astype(BF16)
    logit_ref[...] = jnp.dot(hn2, wr_ref[...], preferred_element_type=F32,
                             precision=lax.Precision.HIGHEST) + br_ref[...]


def _merge(o_ctx, o_a, o_b, o_c, gates, h, mods3, w_br, w_out_b, g_post3, g_pre_ffn3, w_router, b_router3, l):
    lat = lambda: pl.BlockSpec((TM, 512), lambda i: (jnp.maximum(i - CTX_TILES, 0), 0))
    mod = lambda j: pl.BlockSpec((None, 1, D_MODEL), lambda i: (_mod_index(i), 0, j))
    lw = lambda: pl.BlockSpec((None, 1, D_MODEL), lambda i: (l, 0, 0))
    row = lambda n: pl.BlockSpec((TM, n), lambda i: (i, 0))
    const = lambda shape: pl.BlockSpec(shape, lambda i: (0,) * len(shape), pipeline_mode=pl.Buffered(1))
    return pl.pallas_call(
        _merge_kernel,
        grid=(N_TILES,),
        in_specs=[pl.BlockSpec((TM, 1536), lambda i: (jnp.minimum(i, CTX_TILES - 1), 0)),
                  lat(), lat(), lat(), row(3072), row(D_MODEL), mod(2), mod(3), mod(4),
                  const((3, 512, D_MODEL)), const((D_MODEL, D_MODEL)), lw(), lw(),
                  pl.BlockSpec((None, D_MODEL, N_EXPERTS), lambda i: (l, 0, 0)),
                  pl.BlockSpec((None, 1, N_EXPERTS), lambda i: (l, 0, 0))],
        out_specs=[row(D_MODEL), row(D_MODEL), row(N_EXPERTS)],
        out_shape=[jax.ShapeDtypeStruct((N_TOK, D_MODEL), F32),
                   jax.ShapeDtypeStruct((N_TOK, D_MODEL), BF16),
                   jax.ShapeDtypeStruct((N_TOK, N_EXPERTS), F32)],
        compiler_params=_cparams(("arbitrary",)),
        name="merge",
    )(o_ctx, o_a, o_b, o_c, gates, h, mods3, mods3, mods3, w_br, w_out_b, g_post3, g_pre_ffn3,
      w_router, b_router3)


def _moe_kernel(l, be_ref, nv_ref, x_ref, wgu_ref, bgu_ref, wd_ref, bd_ref, y_ref, wgu_bf, wd_bf):
    i = pl.program_id(0)
    e = be_ref[i]
    prev = be_ref[jnp.maximum(i - 1, 0)]

    @pl.when((i == 0) | (e != prev))
    def _():
        wgu_bf[...] = wgu_ref[...].astype(BF16)
        wd_bf[...] = wd_ref[...].astype(BF16)

    @pl.when(i < nv_ref[0])
    def _():
        x = x_ref[...]
        b = bgu_ref[...]
        glu = jnp.dot(x, wgu_bf[:, :D_FF], preferred_element_type=F32) + b[:, :D_FF]
        lin = jnp.dot(x, wgu_bf[:, D_FF:], preferred_element_type=F32) + b[:, D_FF:]
        glu = jnp.minimum(glu, SWIGLU_LIMIT)
        lin = jnp.clip(lin, -SWIGLU_LIMIT, SWIGLU_LIMIT)
        act = glu * (1.0 / (1.0 + jnp.exp(-SWIGLU_ALPHA * glu))) * (lin + 1.0)
        y = jnp.dot(act.astype(BF16), wd_bf[...], preferred_element_type=F32) + bd_ref[...]
        y_ref[...] = y.astype(BF16)

    @pl.when(i >= nv_ref[0])
    def _():
        y_ref[...] = jnp.zeros_like(y_ref)


def _moe(block_e, n_valid, x_slots, w_gate_up, b_gate_up4, w_down, b_down4, l):
    grid_spec = pltpu.PrefetchScalarGridSpec(
        num_scalar_prefetch=2,
        grid=(N_MOE_BLOCKS,),
        in_specs=[pl.BlockSpec((MOE_TILE, D_MODEL), lambda i, be, nv: (i, 0)),
                  pl.BlockSpec((None, None, D_MODEL, 2 * D_FF), lambda i, be, nv: (l, be[i], 0, 0)),
                  pl.BlockSpec((None, None, 1, 2 * D_FF), lambda i, be, nv: (l, be[i], 0, 0)),
                  pl.BlockSpec((None, None, D_FF, D_MODEL), lambda i, be, nv: (l, be[i], 0, 0)),
                  pl.BlockSpec((None, None, 1, D_MODEL), lambda i, be, nv: (l, be[i], 0, 0))],
        out_specs=pl.BlockSpec((MOE_TILE, D_MODEL), lambda i, be, nv: (i, 0)),
        scratch_shapes=[pltpu.VMEM((D_MODEL, 2 * D_FF), BF16), pltpu.VMEM((D_FF, D_MODEL), BF16)])
    return pl.pallas_call(
        functools.partial(_moe_kernel, l),
        grid_spec=grid_spec,
        out_shape=jax.ShapeDtypeStruct((N_SLOTS, D_MODEL), BF16),
        compiler_params=_cparams(("arbitrary",)),
        name="moe",
    )(block_e, n_valid, x_slots, w_gate_up, b_gate_up4, w_down, b_down4)


def _combine_kernel(y_ref, gate_ref, h_ref, g2_ref, gpost_ref, o_ref):
    gate = gate_ref[...]
    ffn = None
    for k in range(TOP_K):
        term = gate[:, k:k + 1] * y_ref[:, k * D_MODEL:(k + 1) * D_MODEL].astype(F32)
        ffn = term if ffn is None else ffn + term
    o_ref[...] = h_ref[...] + g2_ref[...] * (_rms(ffn) * gpost_ref[...])


def _combine(y_tok, gate, h2, mods3, g_post_ffn3, l):
    row = lambda n: pl.BlockSpec((TM, n), lambda i: (i, 0))
    return pl.pallas_call(
        _combine_kernel,
        grid=(N_TILES,),
        in_specs=[row(TOP_K * D_MODEL), row(TOP_K), row(D_MODEL),
                  pl.BlockSpec((None, 1, D_MODEL), lambda i: (_mod_index(i), 0, 5)),
                  pl.BlockSpec((None, 1, D_MODEL), lambda i: (l, 0, 0))],
        out_specs=row(D_MODEL),
        out_shape=jax.ShapeDtypeStruct((N_TOK, D_MODEL), F32),
        compiler_params=_cparams(("arbitrary",)),
        name="combine",
    )(y_tok, gate, h2, mods3, g_post_ffn3)


def _route(logits):
    top_val, top_idx = lax.top_k(logits, TOP_K)
    gate = jax.nn.softmax(top_val, axis=-1)
    flat_e = top_idx.reshape(-1)
    onehot = (flat_e[:, None] == jnp.arange(N_EXPERTS)[None, :]).astype(jnp.int32)
    csum = jnp.cumsum(onehot, axis=0)
    rank = jnp.sum(csum * onehot, axis=1) - 1
    counts = csum[-1]
    padded = (counts + MOE_TILE - 1) // MOE_TILE * MOE_TILE
    pad_end = jnp.cumsum(padded)
    dest = (pad_end - padded)[flat_e] + rank
    tok = jnp.arange(N_TOK * TOP_K, dtype=jnp.int32) // TOP_K
    slot_tok = jnp.zeros((N_SLOTS,), jnp.int32).at[dest].set(tok)
    block_e = jnp.minimum(
        jnp.searchsorted(pad_end, jnp.arange(N_MOE_BLOCKS) * MOE_TILE, side='right'), N_EXPERTS - 1)
    n_valid = (pad_end[-1:] // MOE_TILE).astype(jnp.int32)
    return gate, dest, slot_tok, block_e.astype(jnp.int32), n_valid


def _rope_tables():
    t = jnp.arange(DEC_SEQ)
    inv = ROPE_THETA ** (-jnp.arange(ROPE_PAIRS, dtype=F32) / ROPE_PAIRS)
    row = (t // GRID_W).astype(F32)[:, None] * inv
    col = (t % GRID_W).astype(F32)[:, None] * inv
    zeros = jnp.zeros_like(row)
    cos = jnp.concatenate([jnp.cos(row), jnp.cos(row), jnp.cos(col), jnp.cos(col)], axis=1)
    s1 = jnp.concatenate([-jnp.sin(row), zeros, -jnp.sin(col), zeros], axis=1)
    s2 = jnp.concatenate([zeros, jnp.sin(row), zeros, jnp.sin(col)], axis=1)
    ident = lambda v: jnp.full((TM, HEAD_DIM), v, F32)
    tables = [jnp.concatenate([x, ident(v)], axis=0) for x, v in ((cos, 1.0), (s1, 0.0), (s2, 0.0))]
    return [jnp.tile(x, (1, 2)) for x in tables]


def kernel(x_prompt, x_sample, cache_a, cache_b, cache_c, c, c_ctx, w_ada, b_ada, g_pre_mix, g_post_mix,
           g_pre_ffn, g_post_ffn, w_in, g_q_b, g_k_b, sink_a, rpb_c, w_br_a, w_br_b, w_br_c, w_out,
           w_router, b_router, w_gate_up, b_gate_up, w_down, b_down):
    h = jnp.concatenate([x_prompt.reshape(N_CTX, D_MODEL), x_sample.reshape(N_LAT, D_MODEL)], axis=0)
    cond8 = jnp.concatenate([c_ctx[None], c, jnp.zeros((3, D_MODEL), F32)], axis=0)
    cache_a = cache_a.astype(BF16).reshape(DEC_BATCH, DEPTH, 2, PAST_LEN, W_KV)
    cache_b = cache_b.astype(BF16).reshape(DEC_BATCH, DEPTH, 2, PAST_LEN, W_KV)
    cache_c = cache_c.astype(BF16).reshape(DEC_BATCH, DEPTH, 2, PAST_LEN, W_HEADS)
    cos_t, s1_t, s2_t = _rope_tables()
    bd = jnp.kron(jnp.eye(256 // HEAD_DIM, dtype=F32),
                  jnp.full((HEAD_DIM, HEAD_DIM), 1.0 / HEAD_DIM, F32)).astype(BF16)
    vec3 = lambda a: a.reshape(DEPTH, 1, a.shape[-1])
    scale = HEAD_DIM ** -0.5
    states = []
    for l in range(DEPTH):
        w = w_in[l]
        w_in_p = jnp.concatenate(
            [w[:, 0:512] * scale, w[:, 768:1280], w[:, 1536:2048] * scale, w[:, 2048:2560], w[:, 2560:3072],
             w[:, 512:640], w[:, 1280:1408], w[:, 640:768], w[:, 1408:1536], w[:, 3072:]], axis=1).astype(BF16)
        gq = (jnp.tile(g_q_b[l], H_B) * scale)[None]
        gk = jnp.tile(g_k_b[l], KV_B)[None]
        w_br = jnp.stack([w_br_a[l], w_br_b[l], w_br_c[l]], axis=0).astype(BF16)
        w_out_b = w_out[l].astype(BF16)

        mods3 = _adaln(cond8, w_ada, vec3(b_ada), l).reshape(8, 1, 6 * D_MODEL)
        q_all, kvab, kvc, gates, st = _proj(h, mods3, vec3(g_pre_mix), w_in_p, gq, gk, bd,
                                            cos_t, s1_t, s2_t, l)
        states.append(st)
        o_ctx = _ctx_attn(sink_a, q_all, kvab, kvc, l)
        o_a = _win_attn(sink_a, q_all, kvab, cache_a, l)
        o_b = _dense_attn(q_all, kvab, cache_b, l)
        o_c = _nbr_attn(q_all, kvc, cache_c, _nbr_bias_table(rpb_c[l]), l)
        h2, hn2, logits = _merge(o_ctx, o_a, o_b, o_c, gates, h, mods3, w_br, w_out_b,
                                 vec3(g_post_mix), vec3(g_pre_ffn), w_router, vec3(b_router), l)
        gate, dest, slot_tok, block_e, n_valid = _route(logits)
        x_slots = jnp.take(hn2, slot_tok, axis=0)
        y_slots = _moe(block_e, n_valid, x_slots, w_gate_up,
                       b_gate_up.reshape(DEPTH, N_EXPERTS, 1, 2 * D_FF), w_down,
                       b_down.reshape(DEPTH, N_EXPERTS, 1, D_MODEL), l)
        y_tok = jnp.take(y_slots, dest, axis=0).reshape(N_TOK, TOP_K * D_MODEL)
        h = _combine(y_tok, gate, h2, mods3, vec3(g_post_ffn), l)

    def state(c0, c1, width, heads):
        per_layer = [jnp.stack([st[:, c0:c0 + width], st[:, c1:c1 + width]], axis=0) for st in states]
        s = jnp.stack(per_layer, axis=0).reshape(DEPTH, 2, BATCH, SEQ, heads, HEAD_DIM)
        return jnp.transpose(s, (2, 0, 1, 3, 4, 5))

    return (h[:N_CTX].reshape(BATCH, SEQ, D_MODEL), h[N_CTX:].reshape(DEC_BATCH, DEC_SEQ, D_MODEL),
            state(0, 256, W_KV, KV_A), state(128, 384, W_KV, KV_B), state(512, 1024, W_HEADS, H_C))
```

```python
import functools

import jax
import jax.numpy as jnp
from jax import lax
from jax.experimental import pallas as pl
from jax.experimental.pallas import tpu as pltpu
from jax.experimental.pallas import tpu_sc as plsc

D_MODEL = 1024
BATCH = 32
SEQ = 256
DEPTH = 2
DEC_BATCH = 4
DEC_SEQ = 2048
PAST_LEN = 512
GRID_W = 64
HEAD_DIM = 64
H_A = 8
KV_A = 2
H_B = 8
KV_B = 2
H_C = 8
WINDOW_A = 128
NA_ROWS = 8
NA_COLS = 16
ROPE_THETA = 10000.0
ROPE_PAIRS = HEAD_DIM // 4
N_EXPERTS = 32
TOP_K = 4
D_FF = D_MODEL
SWIGLU_ALPHA = 1.702
SWIGLU_LIMIT = 7.0
EPS = 1e-6

W_HEADS = H_A * HEAD_DIM
W_KV = KV_A * HEAD_DIM
N_CTX = BATCH * SEQ
N_LAT = DEC_BATCH * DEC_SEQ
N_TOK = N_CTX + N_LAT
GRID_ROWS = DEC_SEQ // GRID_W
D_IN = 3 * W_HEADS + 4 * W_KV + 2 * W_HEADS + 3 * D_MODEL

TM = 256
N_TILES = N_TOK // TM
CTX_TILES = N_CTX // TM
LAT_TILES_PER_BATCH = DEC_SEQ // TM
TQ = 128
MOE_TILE = 256
N_SLOTS = N_TOK * TOP_K + N_EXPERTS * MOE_TILE
N_MOE_BLOCKS = N_SLOTS // MOE_TILE
NEG = -1e30
VMEM_LIMIT = 56 * 1024 * 1024

BF16 = jnp.bfloat16
F32 = jnp.float32


def _cparams(sem):
    return pltpu.CompilerParams(dimension_semantics=sem, vmem_limit_bytes=VMEM_LIMIT)


def _mod_index(i):
    return jnp.where(i < CTX_TILES, 0, 1 + (i - CTX_TILES) // LAT_TILES_PER_BATCH)


def _rms(x):
    return x * lax.rsqrt(jnp.mean(x * x, axis=-1, keepdims=True) + EPS)


HALF = D_MODEL // 2


def _pack_halves(x):
    hi = lax.bitcast_convert_type(x[:, :HALF].astype(BF16).astype(F32), jnp.uint32)
    lo = lax.bitcast_convert_type(x[:, HALF:].astype(BF16).astype(F32), jnp.uint32)
    return lax.bitcast_convert_type(hi | (lo >> 16), jnp.int32)


def _unpack_halves(w):
    u = lax.bitcast_convert_type(w, jnp.uint32)
    return (lax.bitcast_convert_type(u & jnp.uint32(0xFFFF0000), F32),
            lax.bitcast_convert_type(u << 16, F32))


GATHER_CHUNK = 64


def _gather_rows(table, idx):
    n = idx.shape[0]
    width = table.shape[1]
    info = plsc.get_sparse_core_info()
    n_workers = info.num_cores * info.num_subcores
    per_worker = n // n_workers
    n_chunks = per_worker // GATHER_CHUNK
    assert per_worker * n_workers == n and n_chunks * GATHER_CHUNK == per_worker and n_chunks % 2 == 0
    mesh = plsc.VectorSubcoreMesh(core_axis_name="core", subcore_axis_name="subcore")

    @functools.partial(
        pl.kernel, out_type=jax.ShapeDtypeStruct((n, width), table.dtype), mesh=mesh,
        scratch_types=[pltpu.VMEM((per_worker,), jnp.int32),
                       pltpu.VMEM((2, GATHER_CHUNK, width), table.dtype),
                       pltpu.SemaphoreType.DMA((2,)), pltpu.SemaphoreType.DMA((2,))])
    def gather(table_hbm, idx_hbm, out_hbm, idx_v, rows_v, gather_sem, write_sem):
        worker = lax.axis_index("subcore") * info.num_cores + lax.axis_index("core")
        base = worker * per_worker
        pltpu.sync_copy(idx_hbm.at[pl.ds(base, per_worker)], idx_v)

        def fetch(chunk, slot):
            rows = idx_v.at[pl.ds(chunk * GATHER_CHUNK, GATHER_CHUNK)]
            return pltpu.make_async_copy(table_hbm.at[rows], rows_v.at[slot], gather_sem.at[slot])

        def write(chunk, slot):
            dst = out_hbm.at[pl.ds(base + chunk * GATHER_CHUNK, GATHER_CHUNK)]
            return pltpu.make_async_copy(rows_v.at[slot], dst, write_sem.at[slot])

        fetch(0, 0).start()

        @pl.loop(0, n_chunks, step=2)
        def _(c):
            @pl.when(c > 0)
            def _():
                write(c - 1, 1).wait()

            fetch(c + 1, 1).start()
            fetch(c, 0).wait()
            write(c, 0).start()
            write(c, 0).wait()

            @pl.when(c + 2 < n_chunks)
            def _():
                fetch(c + 2, 0).start()

            fetch(c + 1, 1).wait()
            write(c + 1, 1).start()

        write(n_chunks - 1, 1).wait()

    return gather(table, idx)


def _adaln_kernel(c_ref, w_ref, b_ref, o_ref):
    c = c_ref[...]
    s = c / (1.0 + jnp.exp(-c))
    o_ref[...] = jnp.dot(s, w_ref[...], preferred_element_type=F32,
                         precision=lax.Precision.HIGHEST) + b_ref[...]


def _adaln(cond8, w_ada, b_ada3, l):
    tn = 1536
    return pl.pallas_call(
        _adaln_kernel,
        grid=(6 * D_MODEL // tn,),
        in_specs=[pl.BlockSpec((8, D_MODEL), lambda j: (0, 0)),
                  pl.BlockSpec((None, D_MODEL, tn), lambda j: (l, 0, j)),
                  pl.BlockSpec((None, 1, tn), lambda j: (l, 0, j))],
        out_specs=pl.BlockSpec((8, tn), lambda j: (0, j)),
        out_shape=jax.ShapeDtypeStruct((8, 6 * D_MODEL), F32),
        compiler_params=_cparams(("arbitrary",)),
        name="adaln",
    )(cond8, w_ada, b_ada3)


C_QA, C_QB, C_QC, C_KC, C_VC, C_KAB, C_GL = 0, 512, 1024, 1536, 2048, 2560, 3072


def _proj_kernel(h_ref, sh_ref, sc_ref, gpre_ref, w_ref, gq_ref, gk_ref, bd_ref,
                 cos_ref, s1_ref, s2_ref, q_ref, kvab_ref, kvc_ref, gates_ref, st_ref):
    i = pl.program_id(0)
    hn = _rms(h_ref[...]) * gpre_ref[...]
    hb = (hn * (1.0 + sc_ref[...]) + sh_ref[...]).astype(BF16)
    cos, s1, s2 = cos_ref[...], s1_ref[...], s2_ref[...]
    bd = bd_ref[...]

    def proj(c0, n):
        return jnp.dot(hb, w_ref[:, c0:c0 + n], preferred_element_type=F32)

    def rope(t):
        parts = []
        for g in range(t.shape[1] // 128):
            tg = t[:, g * 128:(g + 1) * 128]
            parts.append(tg * cos + pltpu.roll(tg, 112, 1) * s1 + pltpu.roll(tg, 16, 1) * s2)
        return parts[0] if len(parts) == 1 else jnp.concatenate(parts, axis=1)

    def headnorm(t, g):
        sq = (t * t).astype(BF16)
        n = t.shape[1]
        if n == 128:
            ms = jnp.dot(sq, bd[:128, :128], preferred_element_type=F32)
        else:
            ms = jnp.concatenate(
                [jnp.dot(sq[:, c:c + 256], bd, preferred_element_type=F32) for c in range(0, n, 256)],
                axis=1)
        return t * lax.rsqrt(ms + EPS) * g

    q_ref[:, 0:512] = rope(proj(C_QA, 512)).astype(BF16)
    q_ref[:, 512:1024] = rope(headnorm(proj(C_QB, 512), gq_ref[...])).astype(BF16)
    q_ref[:, 1024:1536] = proj(C_QC, 512).astype(BF16)
    kc = proj(C_KC, 512)
    vc = proj(C_VC, 512)
    kvc_ref[:, 0:512] = kc.astype(BF16)
    kvc_ref[:, 512:1024] = vc.astype(BF16)
    kab = proj(C_KAB, 512)
    ka = kab[:, 0:128]
    kb = headnorm(kab[:, 128:256], gk_ref[...])
    kvab_ref[:, 0:128] = rope(ka).astype(BF16)
    kvab_ref[:, 128:256] = rope(kb).astype(BF16)
    kvab_ref[:, 256:512] = kab[:, 256:512].astype(BF16)
    for j in range(6):
        gl = proj(C_GL + j * 512, 512)
        gates_ref[:, j * 512:(j + 1) * 512] = (1.0 / (1.0 + jnp.exp(-gl))).astype(BF16)

    @pl.when(i < CTX_TILES)
    def _():
        st_ref[:, 0:128] = ka
        st_ref[:, 128:256] = kb
        st_ref[:, 256:512] = kab[:, 256:512]
        st_ref[:, 512:1024] = kc
        st_ref[:, 1024:1536] = vc


def _proj(h, mods3, g_pre3, w_in_p, gq, gk, bd, cos_t, s1_t, s2_t, l):
    def rope_idx(i):
        return jnp.where(i < CTX_TILES, LAT_TILES_PER_BATCH, (i - CTX_TILES) % LAT_TILES_PER_BATCH)

    const = lambda shape: pl.BlockSpec(shape, lambda i: (0,) * len(shape), pipeline_mode=pl.Buffered(1))
    rope_spec = pl.BlockSpec((TM, 128), lambda i: (rope_idx(i), 0))
    row = lambda n: pl.BlockSpec((TM, n), lambda i: (i, 0))
    return pl.pallas_call(
        _proj_kernel,
        grid=(N_TILES,),
        in_specs=[row(D_MODEL),
                  pl.BlockSpec((None, 1, D_MODEL), lambda i: (_mod_index(i), 0, 0)),
                  pl.BlockSpec((None, 1, D_MODEL), lambda i: (_mod_index(i), 0, 1)),
                  pl.BlockSpec((None, 1, D_MODEL), lambda i: (l, 0, 0)),
                  const((D_MODEL, D_IN)), const((1, 512)), const((1, 128)), const((256, 256)),
                  rope_spec, rope_spec, rope_spec],
        out_specs=[row(1536), row(512), row(1024), row(3072),
                   pl.BlockSpec((TM, 1536), lambda i: (jnp.minimum(i, CTX_TILES - 1), 0))],
        out_shape=[jax.ShapeDtypeStruct((N_TOK, 1536), BF16),
                   jax.ShapeDtypeStruct((N_TOK, 512), BF16),
                   jax.ShapeDtypeStruct((N_TOK, 1024), BF16),
                   jax.ShapeDtypeStruct((N_TOK, 3072), BF16),
                   jax.ShapeDtypeStruct((N_CTX, 1536), F32)],
        compiler_params=_cparams(("arbitrary",)),
        name="proj",
    )(h, mods3, mods3, g_pre3, w_in_p, gq, gk, bd, cos_t, s1_t, s2_t)


def _qk(q, k):
    return lax.dot_general(q, k, (((1,), (1,)), ((), ())), preferred_element_type=F32)


def _softmax_pv(scores, values, sink=None):
    m = functools.reduce(jnp.maximum, [jnp.max(s, axis=-1, keepdims=True) for s in scores])
    if sink is not None:
        m = jnp.maximum(m, sink)
    ps = [jnp.exp(s - m) for s in scores]
    den = functools.reduce(jnp.add, [jnp.sum(p, axis=-1, keepdims=True) for p in ps])
    if sink is not None:
        den = den + jnp.exp(sink - m)
    o = functools.reduce(jnp.add, [jnp.dot(p.astype(BF16), v, preferred_element_type=F32)
                                   for p, v in zip(ps, values)])
    return o / den


def _stack_heads(q, first, n):
    return jnp.concatenate([q[:, (first + g) * HEAD_DIM:(first + g + 1) * HEAD_DIM] for g in range(n)], axis=0)


def _unstack_heads(o, n):
    rows = o.shape[0] // n
    return jnp.concatenate([o[g * rows:(g + 1) * rows] for g in range(n)], axis=1)


def _sink_column(sink_ref, l, first, n, rows):
    return jnp.concatenate([jnp.full((rows, 1), sink_ref[l, first + g], F32) for g in range(n)], axis=0)


def _ctx_attn_kernel(l, sink_ref, q_ref, kvab_ref, kvc_ref, o_ref):
    q = q_ref[...]
    kvab = kvab_ref[...]
    kvc = kvc_ref[...]
    grp = H_A // KV_A
    for kv in range(KV_A):
        k = kvab[:, kv * 64:(kv + 1) * 64]
        v = kvab[:, 256 + kv * 64:256 + (kv + 1) * 64]
        qs = _stack_heads(q[:, 0:512], kv * grp, grp)
        o = _softmax_pv([_qk(qs, k)], [v], _sink_column(sink_ref, l, kv * grp, grp, TM))
        o_ref[:, kv * 256:(kv + 1) * 256] = _unstack_heads(o, grp).astype(BF16)
    for kv in range(KV_B):
        k = kvab[:, 128 + kv * 64:128 + (kv + 1) * 64]
        v = kvab[:, 384 + kv * 64:384 + (kv + 1) * 64]
        qs = _stack_heads(q[:, 512:1024], kv * grp, grp)
        o = _softmax_pv([_qk(qs, k)], [v])
        o_ref[:, 512 + kv * 256:512 + (kv + 1) * 256] = _unstack_heads(o, grp).astype(BF16)
    for hp in range(H_C // 2):
        outs = []
        for h in (2 * hp, 2 * hp + 1):
            qh = q[:, 1024 + h * 64:1024 + (h + 1) * 64]
            k = kvc[:, h * 64:(h + 1) * 64]
            v = kvc[:, 512 + h * 64:512 + (h + 1) * 64]
            outs.append(_softmax_pv([_qk(qh, k)], [v]))
        o_ref[:, 1024 + hp * 128:1024 + (hp + 1) * 128] = jnp.concatenate(outs, axis=1).astype(BF16)


def _ctx_attn(sink_a, q_all, kvab, kvc, l):
    row = lambda n: pl.BlockSpec((SEQ, n), lambda b: (b, 0))
    return pl.pallas_call(
        functools.partial(_ctx_attn_kernel, l),
        grid=(BATCH,),
        in_specs=[pl.BlockSpec(memory_space=pltpu.SMEM), row(1536), row(512), row(1024)],
        out_specs=row(1536),
        out_shape=jax.ShapeDtypeStruct((N_CTX, 1536), BF16),
        compiler_params=_cparams(("arbitrary",)),
        name="ctx_attn",
    )(sink_a, q_all, kvab, kvc)


def _win_attn_kernel(l, sink_ref, q_ref, prev_ref, cur_ref, nxt_ref, ck_ref, cv_ref, o_ref):
    n = pl.program_id(1)
    nb = DEC_SEQ // TQ
    q = q_ref[...]
    grp = H_A // KV_A
    rows = grp * TQ
    qpos = lax.broadcasted_iota(jnp.int32, (rows, TQ), 0) % TQ
    kpos = lax.broadcasted_iota(jnp.int32, (rows, TQ), 1)
    mask_prev = (kpos >= qpos) & (n > 0)
    mask_next = (kpos <= qpos) & (n < nb - 1)
    for kv in range(KV_A):
        ks = slice(kv * 64, (kv + 1) * 64)
        vs = slice(256 + kv * 64, 256 + (kv + 1) * 64)
        qs = _stack_heads(q, kv * grp, grp)
        s_prev = jnp.where(mask_prev, _qk(qs, prev_ref[:, ks]), NEG)
        s_cur = _qk(qs, cur_ref[:, ks])
        s_next = jnp.where(mask_next, _qk(qs, nxt_ref[:, ks]), NEG)
        s_ctx = _qk(qs, ck_ref[:, ks])
        o = _softmax_pv([s_prev, s_cur, s_next, s_ctx],
                        [prev_ref[:, vs], cur_ref[:, vs], nxt_ref[:, vs], cv_ref[:, ks]],
                        _sink_column(sink_ref, l, kv * grp, grp, TQ))
        o_ref[:, kv * 256:(kv + 1) * 256] = _unstack_heads(o, grp).astype(BF16)


def _win_attn(sink_a, q_all, kvab, cache_a, l):
    nb = DEC_SEQ // TQ
    base = N_CTX // TQ
    kv_spec = lambda f: pl.BlockSpec((TQ, 512), lambda b, n: (base + b * nb + f(n), 0))
    cache_spec = lambda s: pl.BlockSpec((None, None, None, PAST_LEN, W_KV), lambda b, n: (b, l, s, 0, 0))
    return pl.pallas_call(
        functools.partial(_win_attn_kernel, l),
        grid=(DEC_BATCH, nb),
        in_specs=[pl.BlockSpec(memory_space=pltpu.SMEM),
                  pl.BlockSpec((TQ, 512), lambda b, n: (base + b * nb + n, 0)),
                  kv_spec(lambda n: jnp.maximum(n - 1, 0)), kv_spec(lambda n: n),
                  kv_spec(lambda n: jnp.minimum(n + 1, nb - 1)),
                  cache_spec(0), cache_spec(1)],
        out_specs=pl.BlockSpec((TQ, 512), lambda b, n: (b * nb + n, 0)),
        out_shape=jax.ShapeDtypeStruct((N_LAT, 512), BF16),
        compiler_params=_cparams(("arbitrary", "arbitrary")),
        name="win_attn",
    )(sink_a, q_all, kvab, kvab, kvab, cache_a, cache_a)


def _dense_attn_kernel(q_ref, kv_ref, ck_ref, cv_ref, o_ref):
    q = q_ref[...]
    grp = H_B // KV_B
    for kv in range(KV_B):
        qs = _stack_heads(q, kv * grp, grp)
        k = kv_ref[:, 128 + kv * 64:128 + (kv + 1) * 64]
        v = kv_ref[:, 384 + kv * 64:384 + (kv + 1) * 64]
        ks = slice(kv * 64, (kv + 1) * 64)
        o = _softmax_pv([_qk(qs, k), _qk(qs, ck_ref[:, ks])], [v, cv_ref[:, ks]])
        o_ref[:, kv * 256:(kv + 1) * 256] = _unstack_heads(o, grp).astype(BF16)


def _dense_attn(q_all, kvab, cache_b, l):
    nb = DEC_SEQ // TQ
    base = N_CTX // TQ
    cache_spec = lambda s: pl.BlockSpec((None, None, None, PAST_LEN, W_KV), lambda b, n: (b, l, s, 0, 0))
    return pl.pallas_call(
        _dense_attn_kernel,
        grid=(DEC_BATCH, nb),
        in_specs=[pl.BlockSpec((TQ, 512), lambda b, n: (base + b * nb + n, 1)),
                  pl.BlockSpec((DEC_SEQ, 512), lambda b, n: (N_CTX // DEC_SEQ + b, 0)),
                  cache_spec(0), cache_spec(1)],
        out_specs=pl.BlockSpec((TQ, 512), lambda b, n: (b * nb + n, 0)),
        out_shape=jax.ShapeDtypeStruct((N_LAT, 512), BF16),
        compiler_params=_cparams(("arbitrary", "arbitrary")),
        name="dense_attn",
    )(q_all, kvab, cache_b, cache_b)


def _row_start(r):
    return jnp.clip(r - NA_ROWS // 2, 0, GRID_ROWS - NA_ROWS)


def _nbr_attn_kernel(q_ref, kv_ref, ck_ref, cv_ref, bias_ref, o_ref):
    r = pl.program_id(1)
    k0 = pl.multiple_of(_row_start(r) * GRID_W, GRID_W)
    n_keys = NA_ROWS * GRID_W
    q = q_ref[...]
    for hp in range(H_C // 2):
        outs = []
        for h in (2 * hp, 2 * hp + 1):
            hs = slice(h * 64, (h + 1) * 64)
            k = kv_ref[pl.ds(k0, n_keys), hs]
            v = kv_ref[pl.ds(k0, n_keys), 512 + h * 64:512 + (h + 1) * 64]
            qh = q[:, hs]
            s_loc = _qk(qh, k) + bias_ref[h]
            outs.append(_softmax_pv([s_loc, _qk(qh, ck_ref[:, hs])], [v, cv_ref[:, hs]]))
        o_ref[:, hp * 128:(hp + 1) * 128] = jnp.concatenate(outs, axis=1).astype(BF16)


def _nbr_attn(q_all, kvc, cache_c, bias_t, l):
    base = N_CTX // GRID_W
    cache_spec = lambda s: pl.BlockSpec((None, None, None, PAST_LEN, W_HEADS), lambda b, r: (b, l, s, 0, 0))
    return pl.pallas_call(
        _nbr_attn_kernel,
        grid=(DEC_BATCH, GRID_ROWS),
        in_specs=[pl.BlockSpec((GRID_W, 512), lambda b, r: (base + b * GRID_ROWS + r, 2)),
                  pl.BlockSpec((DEC_SEQ, 1024), lambda b, r: (N_CTX // DEC_SEQ + b, 0)),
                  cache_spec(0), cache_spec(1),
                  pl.BlockSpec((None, H_C, GRID_W, NA_ROWS * GRID_W),
                               lambda b, r: (_row_start(r) - r + NA_ROWS - 1, 0, 0, 0))],
        out_specs=pl.BlockSpec((GRID_W, 512), lambda b, r: (b * GRID_ROWS + r, 0)),
        out_shape=jax.ShapeDtypeStruct((N_LAT, 512), BF16),
        compiler_params=_cparams(("arbitrary", "arbitrary")),
        name="nbr_attn",
    )(q_all, kvc, cache_c, cache_c, bias_t)


def _nbr_bias_table(rpb_l):
    c = jnp.arange(GRID_W)[:, None]
    kc = jnp.arange(GRID_W)[None, :]
    c_start = jnp.clip(c - NA_COLS // 2, 0, GRID_W - NA_COLS)
    valid = (kc >= c_start) & (kc < c_start + NA_COLS)
    dc = jnp.clip(kc - c + NA_COLS - 1, 0, 2 * NA_COLS - 2)
    t = jnp.where(valid[None, None], rpb_l[:, :, dc].astype(F32), NEG)
    t = jnp.stack([t[:, o:o + NA_ROWS] for o in range(NA_ROWS)], axis=0)
    return jnp.transpose(t, (0, 1, 3, 2, 4)).reshape(NA_ROWS, H_C, GRID_W, NA_ROWS * GRID_W)


def _merge_kernel(octx_ref, oa_ref, ob_ref, oc_ref, gates_ref, h_ref, g1_ref, sh2_ref, sc2_ref,
                  wbr_ref, wout_ref, gpost_ref, gpre_ref, wr_ref, br_ref,
                  h2_ref, hn2_ref, logit_ref):
    i = pl.program_id(0)
    is_ctx = i < CTX_TILES
    merged = None
    for j, lat_ref in enumerate((oa_ref, ob_ref, oc_ref)):
        o = jnp.where(is_ctx, octx_ref[:, j * 512:(j + 1) * 512], lat_ref[...])
        br = jnp.dot(o, wbr_ref[j], preferred_element_type=F32)
        term = gates_ref[:, j * D_MODEL:(j + 1) * D_MODEL].astype(F32) * br
        merged = term if merged is None else merged + term
    t = jnp.dot(merged.astype(BF16), wout_ref[...], preferred_element_type=F32)
    h2 = h_ref[...] + g1_ref[...] * (_rms(t) * gpost_ref[...])
    h2_ref[...] = h2
    hn2 = _rms(h2) * gpre_ref[...] * (1.0 + sc2_ref[...]) + sh2_ref[...]
    hn2_ref[...] = _pack_halves(hn2)
    logit_ref[...] = jnp.dot(hn2, wr_ref[...], preferred_element_type=F32,
                             precision=lax.Precision.HIGHEST) + br_ref[...]


def _merge(o_ctx, o_a, o_b, o_c, gates, h, mods3, w_br, w_out_b, g_post3, g_pre_ffn3, w_router, b_router3, l):
    lat = lambda: pl.BlockSpec((TM, 512), lambda i: (jnp.maximum(i - CTX_TILES, 0), 0))
    mod = lambda j: pl.BlockSpec((None, 1, D_MODEL), lambda i: (_mod_index(i), 0, j))
    lw = lambda: pl.BlockSpec((None, 1, D_MODEL), lambda i: (l, 0, 0))
    row = lambda n: pl.BlockSpec((TM, n), lambda i: (i, 0))
    const = lambda shape: pl.BlockSpec(shape, lambda i: (0,) * len(shape), pipeline_mode=pl.Buffered(1))
    return pl.pallas_call(
        _merge_kernel,
        grid=(N_TILES,),
        in_specs=[pl.BlockSpec((TM, 1536), lambda i: (jnp.minimum(i, CTX_TILES - 1), 0)),
                  lat(), lat(), lat(), row(3072), row(D_MODEL), mod(2), mod(3), mod(4),
                  const((3, 512, D_MODEL)), const((D_MODEL, D_MODEL)), lw(), lw(),
                  pl.BlockSpec((None, D_MODEL, N_EXPERTS), lambda i: (l, 0, 0)),
                  pl.BlockSpec((None, 1, N_EXPERTS), lambda i: (l, 0, 0))],
        out_specs=[row(D_MODEL), row(HALF), row(N_EXPERTS)],
        out_shape=[jax.ShapeDtypeStruct((N_TOK, D_MODEL), F32),
                   jax.ShapeDtypeStruct((N_TOK, HALF), jnp.int32),
                   jax.ShapeDtypeStruct((N_TOK, N_EXPERTS), F32)],
        compiler_params=_cparams(("arbitrary",)),
        name="merge",
    )(o_ctx, o_a, o_b, o_c, gates, h, mods3, mods3, mods3, w_br, w_out_b, g_post3, g_pre_ffn3,
      w_router, b_router3)


def _moe_kernel(l, be_ref, nv_ref, x_ref, wgu_ref, bgu_ref, wd_ref, bd_ref, y_ref, wgu_bf, wd_bf):
    i = pl.program_id(0)
    e = be_ref[i]
    prev = be_ref[jnp.maximum(i - 1, 0)]

    @pl.when((i == 0) | (e != prev))
    def _():
        wgu_bf[...] = wgu_ref[...].astype(BF16)
        wd_bf[...] = wd_ref[...].astype(BF16)

    @pl.when(i < nv_ref[0])
    def _():
        xa, xb = _unpack_halves(x_ref[...])
        x = jnp.concatenate([xa.astype(BF16), xb.astype(BF16)], axis=1)
        b = bgu_ref[...]
        glu = jnp.dot(x, wgu_bf[:, :D_FF], preferred_element_type=F32) + b[:, :D_FF]
        lin = jnp.dot(x, wgu_bf[:, D_FF:], preferred_element_type=F32) + b[:, D_FF:]
        glu = jnp.minimum(glu, SWIGLU_LIMIT)
        lin = jnp.clip(lin, -SWIGLU_LIMIT, SWIGLU_LIMIT)
        act = glu * (1.0 / (1.0 + jnp.exp(-SWIGLU_ALPHA * glu))) * (lin + 1.0)
        y = jnp.dot(act.astype(BF16), wd_bf[...], preferred_element_type=F32) + bd_ref[...]
        y_ref[...] = _pack_halves(y)

    @pl.when(i >= nv_ref[0])
    def _():
        y_ref[...] = jnp.zeros_like(y_ref)


def _moe(block_e, n_valid, x_slots, w_gate_up, b_gate_up4, w_down, b_down4, l):
    grid_spec = pltpu.PrefetchScalarGridSpec(
        num_scalar_prefetch=2,
        grid=(N_MOE_BLOCKS,),
        in_specs=[pl.BlockSpec((MOE_TILE, HALF), lambda i, be, nv: (i, 0)),
                  pl.BlockSpec((None, None, D_MODEL, 2 * D_FF), lambda i, be, nv: (l, be[i], 0, 0)),
                  pl.BlockSpec((None, None, 1, 2 * D_FF), lambda i, be, nv: (l, be[i], 0, 0)),
                  pl.BlockSpec((None, None, D_FF, D_MODEL), lambda i, be, nv: (l, be[i], 0, 0)),
                  pl.BlockSpec((None, None, 1, D_MODEL), lambda i, be, nv: (l, be[i], 0, 0))],
        out_specs=pl.BlockSpec((MOE_TILE, HALF), lambda i, be, nv: (i, 0)),
        scratch_shapes=[pltpu.VMEM((D_MODEL, 2 * D_FF), BF16), pltpu.VMEM((D_FF, D_MODEL), BF16)])
    return pl.pallas_call(
        functools.partial(_moe_kernel, l),
        grid_spec=grid_spec,
        out_shape=jax.ShapeDtypeStruct((N_SLOTS, HALF), jnp.int32),
        compiler_params=_cparams(("arbitrary",)),
        name="moe",
    )(block_e, n_valid, x_slots, w_gate_up, b_gate_up4, w_down, b_down4)


def _combine_kernel(y0_ref, y1_ref, y2_ref, y3_ref, gate_ref, h_ref, g2_ref, gpost_ref, o_ref):
    gate = gate_ref[...]
    ffn = None
    for k, y_ref in enumerate((y0_ref, y1_ref, y2_ref, y3_ref)):
        ya, yb = _unpack_halves(y_ref[...])
        term = gate[:, k:k + 1] * jnp.concatenate([ya, yb], axis=1)
        ffn = term if ffn is None else ffn + term
    o_ref[...] = h_ref[...] + g2_ref[...] * (_rms(ffn) * gpost_ref[...])


def _combine(y_tok, gate, h2, mods3, g_post_ffn3, l):
    row = lambda n: pl.BlockSpec((TM, n), lambda i: (i, 0))
    choice = lambda k: pl.BlockSpec((TM, HALF), lambda i: (k * N_TILES + i, 0))
    return pl.pallas_call(
        _combine_kernel,
        grid=(N_TILES,),
        in_specs=[choice(0), choice(1), choice(2), choice(3), row(TOP_K), row(D_MODEL),
                  pl.BlockSpec((None, 1, D_MODEL), lambda i: (_mod_index(i), 0, 5)),
                  pl.BlockSpec((None, 1, D_MODEL), lambda i: (l, 0, 0))],
        out_specs=row(D_MODEL),
        out_shape=jax.ShapeDtypeStruct((N_TOK, D_MODEL), F32),
        compiler_params=_cparams(("arbitrary",)),
        name="combine",
    )(y_tok, y_tok, y_tok, y_tok, gate, h2, mods3, g_post_ffn3)


def _route(logits):
    top_val, top_idx = lax.top_k(logits, TOP_K)
    gate = jax.nn.softmax(top_val, axis=-1)
    flat_e = top_idx.reshape(-1)
    onehot = (flat_e[:, None] == jnp.arange(N_EXPERTS)[None, :]).astype(jnp.int32)
    csum = jnp.cumsum(onehot, axis=0)
    rank = jnp.sum(csum * onehot, axis=1) - 1
    counts = csum[-1]
    padded = (counts + MOE_TILE - 1) // MOE_TILE * MOE_TILE
    pad_end = jnp.cumsum(padded)
    dest = (pad_end - padded)[flat_e] + rank
    tok = jnp.arange(N_TOK * TOP_K, dtype=jnp.int32) // TOP_K
    slot_tok = (jnp.arange(N_SLOTS, dtype=jnp.int32) % N_TOK).at[dest].set(tok)
    block_start = jnp.arange(N_MOE_BLOCKS, dtype=jnp.int32) * MOE_TILE
    block_e = jnp.minimum(jnp.sum((pad_end[None, :] <= block_start[:, None]).astype(jnp.int32), axis=1),
                          N_EXPERTS - 1)
    n_valid = (pad_end[-1:] // MOE_TILE).astype(jnp.int32)
    dest_by_choice = dest.reshape(N_TOK, TOP_K).T.reshape(-1).astype(jnp.int32)
    return gate, dest_by_choice, slot_tok, block_e.astype(jnp.int32), n_valid


def _rope_tables():
    t = jnp.arange(DEC_SEQ)
    inv = ROPE_THETA ** (-jnp.arange(ROPE_PAIRS, dtype=F32) / ROPE_PAIRS)
    row = (t // GRID_W).astype(F32)[:, None] * inv
    col = (t % GRID_W).astype(F32)[:, None] * inv
    zeros = jnp.zeros_like(row)
    cos = jnp.concatenate([jnp.cos(row), jnp.cos(row), jnp.cos(col), jnp.cos(col)], axis=1)
    s1 = jnp.concatenate([-jnp.sin(row), zeros, -jnp.sin(col), zeros], axis=1)
    s2 = jnp.concatenate([zeros, jnp.sin(row), zeros, jnp.sin(col)], axis=1)
    ident = lambda v: jnp.full((TM, HEAD_DIM), v, F32)
    tables = [jnp.concatenate([x, ident(v)], axis=0) for x, v in ((cos, 1.0), (s1, 0.0), (s2, 0.0))]
    return [jnp.tile(x, (1, 2)) for x in tables]


def kernel(x_prompt, x_sample, cache_a, cache_b, cache_c, c, c_ctx, w_ada, b_ada, g_pre_mix, g_post_mix,
           g_pre_ffn, g_post_ffn, w_in, g_q_b, g_k_b, sink_a, rpb_c, w_br_a, w_br_b, w_br_c, w_out,
           w_router, b_router, w_gate_up, b_gate_up, w_down, b_down):
    h = jnp.concatenate([x_prompt.reshape(N_CTX, D_MODEL), x_sample.reshape(N_LAT, D_MODEL)], axis=0)
    cond8 = jnp.concatenate([c_ctx[None], c, jnp.zeros((3, D_MODEL), F32)], axis=0)
    cache_a = cache_a.astype(BF16).reshape(DEC_BATCH, DEPTH, 2, PAST_LEN, W_KV)
    cache_b = cache_b.astype(BF16).reshape(DEC_BATCH, DEPTH, 2, PAST_LEN, W_KV)
    cache_c = cache_c.astype(BF16).reshape(DEC_BATCH, DEPTH, 2, PAST_LEN, W_HEADS)
    cos_t, s1_t, s2_t = _rope_tables()
    bd = jnp.kron(jnp.eye(256 // HEAD_DIM, dtype=F32),
                  jnp.full((HEAD_DIM, HEAD_DIM), 1.0 / HEAD_DIM, F32)).astype(BF16)
    vec3 = lambda a: a.reshape(DEPTH, 1, a.shape[-1])
    scale = HEAD_DIM ** -0.5
    states = []
    for l in range(DEPTH):
        w = w_in[l]
        w_in_p = jnp.concatenate(
            [w[:, 0:512] * scale, w[:, 768:1280], w[:, 1536:2048] * scale, w[:, 2048:2560], w[:, 2560:3072],
             w[:, 512:640], w[:, 1280:1408], w[:, 640:768], w[:, 1408:1536], w[:, 3072:]], axis=1).astype(BF16)
        gq = (jnp.tile(g_q_b[l], H_B) * scale)[None]
        gk = jnp.tile(g_k_b[l], KV_B)[None]
        w_br = jnp.stack([w_br_a[l], w_br_b[l], w_br_c[l]], axis=0).astype(BF16)
        w_out_b = w_out[l].astype(BF16)

        mods3 = _adaln(cond8, w_ada, vec3(b_ada), l).reshape(8, 1, 6 * D_MODEL)
        q_all, kvab, kvc, gates, st = _proj(h, mods3, vec3(g_pre_mix), w_in_p, gq, gk, bd,
                                            cos_t, s1_t, s2_t, l)
        states.append(st)
        o_ctx = _ctx_attn(sink_a, q_all, kvab, kvc, l)
        o_a = _win_attn(sink_a, q_all, kvab, cache_a, l)
        o_b = _dense_attn(q_all, kvab, cache_b, l)
        o_c = _nbr_attn(q_all, kvc, cache_c, _nbr_bias_table(rpb_c[l]), l)
        h2, hn2, logits = _merge(o_ctx, o_a, o_b, o_c, gates, h, mods3, w_br, w_out_b,
                                 vec3(g_post_mix), vec3(g_pre_ffn), w_router, vec3(b_router), l)
        gate, dest, slot_tok, block_e, n_valid = _route(logits)
        x_slots = _gather_rows(hn2, slot_tok)
        y_slots = _moe(block_e, n_valid, x_slots, w_gate_up,
                       b_gate_up.reshape(DEPTH, N_EXPERTS, 1, 2 * D_FF), w_down,
                       b_down.reshape(DEPTH, N_EXPERTS, 1, D_MODEL), l)
        y_tok = _gather_rows(y_slots, dest)
        h = _combine(y_tok, gate, h2, mods3, vec3(g_post_ffn), l)

    def state(c0, c1, width, heads):
        per_layer = [jnp.stack([st[:, c0:c0 + width], st[:, c1:c1 + width]], axis=0) for st in states]
        s = jnp.stack(per_layer, axis=0).reshape(DEPTH, 2, BATCH, SEQ, heads, HEAD_DIM)
        return jnp.transpose(s, (2, 0, 1, 3, 4, 5))

    return (h[:N_CTX].reshape(BATCH, SEQ, D_MODEL), h[N_CTX:].reshape(DEC_BATCH, DEC_SEQ, D_MODEL),
            state(0, 256, W_KV, KV_A), state(128, 384, W_KV, KV_B), state(512, 1024, W_HEADS, H_C))
```

```python
import functools

import jax
import jax.numpy as jnp
from jax import lax
from jax.experimental import pallas as pl
from jax.experimental.pallas import tpu as pltpu
from jax.experimental.pallas import tpu_sc as plsc

D_MODEL = 1024
BATCH = 32
SEQ = 256
DEPTH = 2
DEC_BATCH = 4
DEC_SEQ = 2048
PAST_LEN = 512
GRID_W = 64
HEAD_DIM = 64
H_A = 8
KV_A = 2
H_B = 8
KV_B = 2
H_C = 8
WINDOW_A = 128
NA_ROWS = 8
NA_COLS = 16
ROPE_THETA = 10000.0
ROPE_PAIRS = HEAD_DIM // 4
N_EXPERTS = 32
TOP_K = 4
D_FF = D_MODEL
SWIGLU_ALPHA = 1.702
SWIGLU_LIMIT = 7.0
EPS = 1e-6

W_HEADS = H_A * HEAD_DIM
W_KV = KV_A * HEAD_DIM
N_CTX = BATCH * SEQ
N_LAT = DEC_BATCH * DEC_SEQ
N_TOK = N_CTX + N_LAT
GRID_ROWS = DEC_SEQ // GRID_W
D_IN = 3 * W_HEADS + 4 * W_KV + 2 * W_HEADS + 3 * D_MODEL

TM = 256
N_TILES = N_TOK // TM
CTX_TILES = N_CTX // TM
LAT_TILES_PER_BATCH = DEC_SEQ // TM
TQ = 128
MOE_TILE = 256
N_SLOTS = N_TOK * TOP_K + N_EXPERTS * MOE_TILE
N_MOE_BLOCKS = N_SLOTS // MOE_TILE
NEG = -1e30
VMEM_LIMIT = 56 * 1024 * 1024

BF16 = jnp.bfloat16
F32 = jnp.float32


def _cparams(sem):
    return pltpu.CompilerParams(dimension_semantics=sem, vmem_limit_bytes=VMEM_LIMIT)


def _mod_index(i):
    return jnp.where(i < CTX_TILES, 0, 1 + (i - CTX_TILES) // LAT_TILES_PER_BATCH)


def _rms(x):
    return x * lax.rsqrt(jnp.mean(x * x, axis=-1, keepdims=True) + EPS)


HALF = D_MODEL // 2


def _pack_halves(x):
    hi = lax.bitcast_convert_type(x[:, :HALF].astype(BF16).astype(F32), jnp.uint32)
    lo = lax.bitcast_convert_type(x[:, HALF:].astype(BF16).astype(F32), jnp.uint32)
    return lax.bitcast_convert_type(hi | (lo >> 16), jnp.int32)


def _unpack_halves(w):
    u = lax.bitcast_convert_type(w, jnp.uint32)
    return (lax.bitcast_convert_type(u & jnp.uint32(0xFFFF0000), F32),
            lax.bitcast_convert_type(u << 16, F32))


GATHER_CHUNK = 64


def _gather_rows(table, idx):
    n = idx.shape[0]
    width = table.shape[1]
    info = plsc.get_sparse_core_info()
    n_workers = info.num_cores * info.num_subcores
    per_worker = n // n_workers
    n_chunks = per_worker // GATHER_CHUNK
    assert per_worker * n_workers == n and n_chunks * GATHER_CHUNK == per_worker and n_chunks % 2 == 0
    mesh = plsc.VectorSubcoreMesh(core_axis_name="core", subcore_axis_name="subcore")

    @functools.partial(
        pl.kernel, out_type=jax.ShapeDtypeStruct((n, width), table.dtype), mesh=mesh,
        scratch_types=[pltpu.VMEM((per_worker,), jnp.int32),
                       pltpu.VMEM((2, GATHER_CHUNK, width), table.dtype),
                       pltpu.SemaphoreType.DMA((2,)), pltpu.SemaphoreType.DMA((2,))])
    def gather(table_hbm, idx_hbm, out_hbm, idx_v, rows_v, gather_sem, write_sem):
        worker = lax.axis_index("subcore") * info.num_cores + lax.axis_index("core")
        base = worker * per_worker
        pltpu.sync_copy(idx_hbm.at[pl.ds(base, per_worker)], idx_v)

        def fetch(chunk, slot):
            rows = idx_v.at[pl.ds(chunk * GATHER_CHUNK, GATHER_CHUNK)]
            return pltpu.make_async_copy(table_hbm.at[rows], rows_v.at[slot], gather_sem.at[slot])

        def write(chunk, slot):
            dst = out_hbm.at[pl.ds(base + chunk * GATHER_CHUNK, GATHER_CHUNK)]
            return pltpu.make_async_copy(rows_v.at[slot], dst, write_sem.at[slot])

        fetch(0, 0).start()

        @pl.loop(0, n_chunks, step=2)
        def _(c):
            @pl.when(c > 0)
            def _():
                write(c - 1, 1).wait()

            fetch(c + 1, 1).start()
            fetch(c, 0).wait()
            write(c, 0).start()
            write(c, 0).wait()

            @pl.when(c + 2 < n_chunks)
            def _():
                fetch(c + 2, 0).start()

            fetch(c + 1, 1).wait()
            write(c + 1, 1).start()

        write(n_chunks - 1, 1).wait()

    return gather(table, idx)


def _adaln_kernel(c_ref, w_ref, b_ref, o_ref):
    c = c_ref[...]
    s = c / (1.0 + jnp.exp(-c))
    o_ref[...] = jnp.dot(s, w_ref[...], preferred_element_type=F32,
                         precision=lax.Precision.HIGHEST) + b_ref[...]


def _adaln(cond8, w_ada, b_ada3, l):
    tn = 1536
    return pl.pallas_call(
        _adaln_kernel,
        grid=(6 * D_MODEL // tn,),
        in_specs=[pl.BlockSpec((8, D_MODEL), lambda j: (0, 0)),
                  pl.BlockSpec((None, D_MODEL, tn), lambda j: (l, 0, j)),
                  pl.BlockSpec((None, 1, tn), lambda j: (l, 0, j))],
        out_specs=pl.BlockSpec((8, tn), lambda j: (0, j)),
        out_shape=jax.ShapeDtypeStruct((8, 6 * D_MODEL), F32),
        compiler_params=_cparams(("arbitrary",)),
        name="adaln",
    )(cond8, w_ada, b_ada3)


C_QA, C_QB, C_QC, C_KC, C_VC, C_KAB, C_GL = 0, 512, 1024, 1536, 2048, 2560, 3072


def _proj_kernel(h_ref, sh_ref, sc_ref, gpre_ref, w_ref, gq_ref, gk_ref, bd_ref,
                 cos_ref, s1_ref, s2_ref, q_ref, kvab_ref, kvc_ref, gates_ref, st_ref):
    i = pl.program_id(0)
    hn = _rms(h_ref[...]) * gpre_ref[...]
    hb = (hn * (1.0 + sc_ref[...]) + sh_ref[...]).astype(BF16)
    cos, s1, s2 = cos_ref[...], s1_ref[...], s2_ref[...]
    bd = bd_ref[...]

    def proj(c0, n):
        return jnp.dot(hb, w_ref[:, c0:c0 + n], preferred_element_type=F32)

    def rope(t):
        parts = []
        for g in range(t.shape[1] // 128):
            tg = t[:, g * 128:(g + 1) * 128]
            parts.append(tg * cos + pltpu.roll(tg, 112, 1) * s1 + pltpu.roll(tg, 16, 1) * s2)
        return parts[0] if len(parts) == 1 else jnp.concatenate(parts, axis=1)

    def headnorm(t, g):
        sq = (t * t).astype(BF16)
        n = t.shape[1]
        if n == 128:
            ms = jnp.dot(sq, bd[:128, :128], preferred_element_type=F32)
        else:
            ms = jnp.concatenate(
                [jnp.dot(sq[:, c:c + 256], bd, preferred_element_type=F32) for c in range(0, n, 256)],
                axis=1)
        return t * lax.rsqrt(ms + EPS) * g

    q_ref[:, 0:512] = rope(proj(C_QA, 512)).astype(BF16)
    q_ref[:, 512:1024] = rope(headnorm(proj(C_QB, 512), gq_ref[...])).astype(BF16)
    q_ref[:, 1024:1536] = proj(C_QC, 512).astype(BF16)
    kc = proj(C_KC, 512)
    vc = proj(C_VC, 512)
    kvc_ref[:, 0:512] = kc.astype(BF16)
    kvc_ref[:, 512:1024] = vc.astype(BF16)
    kab = proj(C_KAB, 512)
    ka = kab[:, 0:128]
    kb = headnorm(kab[:, 128:256], gk_ref[...])
    kvab_ref[:, 0:128] = rope(ka).astype(BF16)
    kvab_ref[:, 128:256] = rope(kb).astype(BF16)
    kvab_ref[:, 256:512] = kab[:, 256:512].astype(BF16)
    for j in range(6):
        gl = proj(C_GL + j * 512, 512)
        gates_ref[:, j * 512:(j + 1) * 512] = (1.0 / (1.0 + jnp.exp(-gl))).astype(BF16)

    @pl.when(i < CTX_TILES)
    def _():
        st_ref[:, 0:128] = ka
        st_ref[:, 128:256] = kb
        st_ref[:, 256:512] = kab[:, 256:512]
        st_ref[:, 512:1024] = kc
        st_ref[:, 1024:1536] = vc


def _proj(h, mods3, g_pre3, w_in_p, gq, gk, bd, cos_t, s1_t, s2_t, l):
    def rope_idx(i):
        return jnp.where(i < CTX_TILES, LAT_TILES_PER_BATCH, (i - CTX_TILES) % LAT_TILES_PER_BATCH)

    const = lambda shape: pl.BlockSpec(shape, lambda i: (0,) * len(shape), pipeline_mode=pl.Buffered(1))
    rope_spec = pl.BlockSpec((TM, 128), lambda i: (rope_idx(i), 0))
    row = lambda n: pl.BlockSpec((TM, n), lambda i: (i, 0))
    return pl.pallas_call(
        _proj_kernel,
        grid=(N_TILES,),
        in_specs=[row(D_MODEL),
                  pl.BlockSpec((None, 1, D_MODEL), lambda i: (_mod_index(i), 0, 0)),
                  pl.BlockSpec((None, 1, D_MODEL), lambda i: (_mod_index(i), 0, 1)),
                  pl.BlockSpec((None, 1, D_MODEL), lambda i: (l, 0, 0)),
                  const((D_MODEL, D_IN)), const((1, 512)), const((1, 128)), const((256, 256)),
                  rope_spec, rope_spec, rope_spec],
        out_specs=[row(1536), row(512), row(1024), row(3072),
                   pl.BlockSpec((TM, 1536), lambda i: (jnp.minimum(i, CTX_TILES - 1), 0))],
        out_shape=[jax.ShapeDtypeStruct((N_TOK, 1536), BF16),
                   jax.ShapeDtypeStruct((N_TOK, 512), BF16),
                   jax.ShapeDtypeStruct((N_TOK, 1024), BF16),
                   jax.ShapeDtypeStruct((N_TOK, 3072), BF16),
                   jax.ShapeDtypeStruct((N_CTX, 1536), F32)],
        compiler_params=_cparams(("arbitrary",)),
        name="proj",
    )(h, mods3, mods3, g_pre3, w_in_p, gq, gk, bd, cos_t, s1_t, s2_t)


def _qk(q, k):
    return lax.dot_general(q, k, (((1,), (1,)), ((), ())), preferred_element_type=F32)


def _softmax_pv(scores, values, sink=None):
    m = functools.reduce(jnp.maximum, [jnp.max(s, axis=-1, keepdims=True) for s in scores])
    if sink is not None:
        m = jnp.maximum(m, sink)
    ps = [jnp.exp(s - m) for s in scores]
    den = functools.reduce(jnp.add, [jnp.sum(p, axis=-1, keepdims=True) for p in ps])
    if sink is not None:
        den = den + jnp.exp(sink - m)
    o = functools.reduce(jnp.add, [jnp.dot(p.astype(BF16), v, preferred_element_type=F32)
                                   for p, v in zip(ps, values)])
    return o / den


def _lo_lanes(rows):
    return lax.broadcasted_iota(jnp.int32, (rows, 128), 1) < HEAD_DIM


def _stack_pairs(q, n_pairs):
    lo = _lo_lanes(q.shape[0])
    zero = jnp.zeros((q.shape[0], 128), q.dtype)
    pairs = [q[:, p * 128:(p + 1) * 128] for p in range(n_pairs)]
    return jnp.concatenate([jnp.where(lo, x, zero) for x in pairs] + [jnp.where(lo, zero, x) for x in pairs],
                           axis=0)


def _unstack_pairs(o, n_pairs):
    rows = o.shape[0] // (2 * n_pairs)
    lo = _lo_lanes(rows)
    return jnp.concatenate(
        [jnp.where(lo, o[p * rows:(p + 1) * rows], o[(n_pairs + p) * rows:(n_pairs + p + 1) * rows])
         for p in range(n_pairs)], axis=1)


def _sink_column(sink_ref, l, rows):
    return jnp.concatenate([jnp.full((rows, 1), sink_ref[l, h], F32) for h in range(H_A)], axis=0)


def _ctx_attn_kernel(l, sink_ref, q_ref, kvab_ref, kvc_ref, o_ref):
    qa = _stack_pairs(q_ref[:, 0:512], 4)
    o = _softmax_pv([_qk(qa, kvab_ref[:, 0:128])], [kvab_ref[:, 256:384]], _sink_column(sink_ref, l, SEQ))
    o_ref[:, 0:512] = _unstack_pairs(o, 4).astype(BF16)
    qb = _stack_pairs(q_ref[:, 512:1024], 4)
    o = _softmax_pv([_qk(qb, kvab_ref[:, 128:256])], [kvab_ref[:, 384:512]])
    o_ref[:, 512:1024] = _unstack_pairs(o, 4).astype(BF16)
    for hp in range(H_C // 2):
        cs = slice(hp * 128, (hp + 1) * 128)
        qc = _stack_pairs(q_ref[:, 1024 + hp * 128:1024 + (hp + 1) * 128], 1)
        o = _softmax_pv([_qk(qc, kvc_ref[:, cs])], [kvc_ref[:, 512 + hp * 128:512 + (hp + 1) * 128]])
        o_ref[:, 1024 + hp * 128:1024 + (hp + 1) * 128] = _unstack_pairs(o, 1).astype(BF16)


def _ctx_attn(sink_a, q_all, kvab, kvc, l):
    row = lambda n: pl.BlockSpec((SEQ, n), lambda b: (b, 0))
    return pl.pallas_call(
        functools.partial(_ctx_attn_kernel, l),
        grid=(BATCH,),
        in_specs=[pl.BlockSpec(memory_space=pltpu.SMEM), row(1536), row(512), row(1024)],
        out_specs=row(1536),
        out_shape=jax.ShapeDtypeStruct((N_CTX, 1536), BF16),
        compiler_params=_cparams(("arbitrary",)),
        name="ctx_attn",
    )(sink_a, q_all, kvab, kvc)


def _win_attn_kernel(l, sink_ref, q_ref, prev_ref, cur_ref, nxt_ref, ck_ref, cv_ref, o_ref):
    n = pl.program_id(1)
    nb = DEC_SEQ // TQ
    rows = H_A * TQ
    qpos = lax.broadcasted_iota(jnp.int32, (rows, TQ), 0) % TQ
    kpos = lax.broadcasted_iota(jnp.int32, (rows, TQ), 1)
    mask_prev = (kpos >= qpos) & (n > 0)
    mask_next = (kpos <= qpos) & (n < nb - 1)
    ks, vs = slice(0, 128), slice(256, 384)
    qs = _stack_pairs(q_ref[...], 4)
    s_prev = jnp.where(mask_prev, _qk(qs, prev_ref[:, ks]), NEG)
    s_cur = _qk(qs, cur_ref[:, ks])
    s_next = jnp.where(mask_next, _qk(qs, nxt_ref[:, ks]), NEG)
    s_ctx = _qk(qs, ck_ref[...])
    o = _softmax_pv([s_prev, s_cur, s_next, s_ctx],
                    [prev_ref[:, vs], cur_ref[:, vs], nxt_ref[:, vs], cv_ref[...]],
                    _sink_column(sink_ref, l, TQ))
    o_ref[...] = _unstack_pairs(o, 4).astype(BF16)


def _win_attn(sink_a, q_all, kvab, cache_a, l):
    nb = DEC_SEQ // TQ
    base = N_CTX // TQ
    kv_spec = lambda f: pl.BlockSpec((TQ, 512), lambda b, n: (base + b * nb + f(n), 0))
    cache_spec = lambda s: pl.BlockSpec((None, None, None, PAST_LEN, W_KV), lambda b, n: (b, l, s, 0, 0))
    return pl.pallas_call(
        functools.partial(_win_attn_kernel, l),
        grid=(DEC_BATCH, nb),
        in_specs=[pl.BlockSpec(memory_space=pltpu.SMEM),
                  pl.BlockSpec((TQ, 512), lambda b, n: (base + b * nb + n, 0)),
                  kv_spec(lambda n: jnp.maximum(n - 1, 0)), kv_spec(lambda n: n),
                  kv_spec(lambda n: jnp.minimum(n + 1, nb - 1)),
                  cache_spec(0), cache_spec(1)],
        out_specs=pl.BlockSpec((TQ, 512), lambda b, n: (b * nb + n, 0)),
        out_shape=jax.ShapeDtypeStruct((N_LAT, 512), BF16),
        compiler_params=_cparams(("arbitrary", "arbitrary")),
        name="win_attn",
    )(sink_a, q_all, kvab, kvab, kvab, cache_a, cache_a)


def _dense_attn_kernel(q_ref, kv_ref, ck_ref, cv_ref, o_ref):
    qs = _stack_pairs(q_ref[...], 4)
    half = qs.shape[0] // 2
    outs = []
    for g in range(KV_B):
        qg = qs[g * half:(g + 1) * half]
        outs.append(_softmax_pv([_qk(qg, kv_ref[:, 128:256]), _qk(qg, ck_ref[...])],
                                [kv_ref[:, 384:512], cv_ref[...]]))
    o_ref[...] = _unstack_pairs(jnp.concatenate(outs, axis=0), 4).astype(BF16)


def _dense_attn(q_all, kvab, cache_b, l):
    nb = DEC_SEQ // TQ
    base = N_CTX // TQ
    cache_spec = lambda s: pl.BlockSpec((None, None, None, PAST_LEN, W_KV), lambda b, n: (b, l, s, 0, 0))
    return pl.pallas_call(
        _dense_attn_kernel,
        grid=(DEC_BATCH, nb),
        in_specs=[pl.BlockSpec((TQ, 512), lambda b, n: (base + b * nb + n, 1)),
                  pl.BlockSpec((DEC_SEQ, 512), lambda b, n: (N_CTX // DEC_SEQ + b, 0)),
                  cache_spec(0), cache_spec(1)],
        out_specs=pl.BlockSpec((TQ, 512), lambda b, n: (b * nb + n, 0)),
        out_shape=jax.ShapeDtypeStruct((N_LAT, 512), BF16),
        compiler_params=_cparams(("arbitrary", "arbitrary")),
        name="dense_attn",
    )(q_all, kvab, cache_b, cache_b)


NBR_BAND = 4
NBR_Q = NBR_BAND * GRID_W
NBR_WIN_ROWS = 12
NBR_N_BANDS = GRID_ROWS // NBR_BAND
NBR_KBLK = NBR_Q
NBR_WIN_BLOCKS = NBR_WIN_ROWS * GRID_W // NBR_KBLK
NBR_LAST_KB = (GRID_ROWS - NBR_WIN_ROWS) * GRID_W // NBR_KBLK


def _nbr_window_block(band):
    return jnp.clip(band - 1, 0, NBR_LAST_KB)


def _nbr_attn_kernel(q_ref, k0_ref, k1_ref, k2_ref, ck_ref, cv_ref, bias_ref, o_ref):
    k_refs = (k0_ref, k1_ref, k2_ref)
    for hp in range(H_C // 2):
        cs = slice(hp * 128, (hp + 1) * 128)
        vs = slice(512 + hp * 128, 512 + (hp + 1) * 128)
        qs = _stack_pairs(q_ref[:, cs], 1)
        bias = jnp.concatenate([bias_ref[2 * hp], bias_ref[2 * hp + 1]], axis=0)
        scores = [_qk(qs, kr[:, cs]) + bias[:, j * NBR_KBLK:(j + 1) * NBR_KBLK] for j, kr in enumerate(k_refs)]
        scores.append(_qk(qs, ck_ref[:, cs]))
        o = _softmax_pv(scores, [kr[:, vs] for kr in k_refs] + [cv_ref[:, cs]])
        o_ref[:, cs] = _unstack_pairs(o, 1).astype(BF16)


def _nbr_attn(q_all, kvc, cache_c, bias_t, l):
    q_base = N_CTX // NBR_Q
    k_base = N_CTX // NBR_KBLK
    blocks_per_batch = DEC_SEQ // NBR_KBLK
    cache_spec = lambda s: pl.BlockSpec((None, None, None, PAST_LEN, W_HEADS), lambda band, b: (b, l, s, 0, 0))
    key_spec = lambda j: pl.BlockSpec(
        (NBR_KBLK, 1024), lambda band, b: (k_base + b * blocks_per_batch + _nbr_window_block(band) + j, 0))
    band_type = lambda band: jnp.where(band == 0, 0, jnp.where(band == NBR_N_BANDS - 1, 2, 1))
    return pl.pallas_call(
        _nbr_attn_kernel,
        grid=(NBR_N_BANDS, DEC_BATCH),
        in_specs=[pl.BlockSpec((NBR_Q, 512), lambda band, b: (q_base + b * NBR_N_BANDS + band, 2)),
                  key_spec(0), key_spec(1), key_spec(2), cache_spec(0), cache_spec(1),
                  pl.BlockSpec((None, H_C, NBR_Q, NBR_WIN_ROWS * GRID_W),
                               lambda band, b: (band_type(band), 0, 0, 0))],
        out_specs=pl.BlockSpec((NBR_Q, 512), lambda band, b: (b * NBR_N_BANDS + band, 0)),
        out_shape=jax.ShapeDtypeStruct((N_LAT, 512), BF16),
        compiler_params=_cparams(("arbitrary", "arbitrary")),
        name="nbr_attn",
    )(q_all, kvc, kvc, kvc, cache_c, cache_c, bias_t)


def _nbr_bias_table(rpb_l):
    c = jnp.arange(GRID_W)[:, None]
    kc = jnp.arange(GRID_W)[None, :]
    c_start = jnp.clip(c - NA_COLS // 2, 0, GRID_W - NA_COLS)
    valid = (kc >= c_start) & (kc < c_start + NA_COLS)
    dc = jnp.clip(kc - c + NA_COLS - 1, 0, 2 * NA_COLS - 2)
    t = jnp.where(valid[None, None], rpb_l[:, :, dc].astype(F32), NEG)
    neg = jnp.full((H_C, GRID_W, GRID_W), NEG, F32)
    tables = []
    for r0 in (0, NBR_BAND, GRID_ROWS - NBR_BAND):
        k0 = min(max(r0 - NA_ROWS // 2, 0), GRID_ROWS - NBR_WIN_ROWS)
        rows = []
        for dq in range(NBR_BAND):
            r = r0 + dq
            start = min(max(r - NA_ROWS // 2, 0), GRID_ROWS - NA_ROWS)
            cols = []
            for i in range(NBR_WIN_ROWS):
                kr = k0 + i
                cols.append(t[:, kr - r + NA_ROWS - 1] if start <= kr < start + NA_ROWS else neg)
            rows.append(jnp.concatenate(cols, axis=2))
        tables.append(jnp.concatenate(rows, axis=1))
    return jnp.stack(tables, axis=0)


def _merge_kernel(octx_ref, oa_ref, ob_ref, oc_ref, gates_ref, h_ref, g1_ref, sh2_ref, sc2_ref,
                  wbr_ref, wout_ref, gpost_ref, gpre_ref, wr_ref, br_ref,
                  h2_ref, hn2_ref, logit_ref):
    i = pl.program_id(0)
    is_ctx = i < CTX_TILES
    merged = None
    for j, lat_ref in enumerate((oa_ref, ob_ref, oc_ref)):
        o = jnp.where(is_ctx, octx_ref[:, j * 512:(j + 1) * 512], lat_ref[...])
        br = jnp.dot(o, wbr_ref[j], preferred_element_type=F32)
        term = gates_ref[:, j * D_MODEL:(j + 1) * D_MODEL].astype(F32) * br
        merged = term if merged is None else merged + term
    t = jnp.dot(merged.astype(BF16), wout_ref[...], preferred_element_type=F32)
    h2 = h_ref[...] + g1_ref[...] * (_rms(t) * gpost_ref[...])
    h2_ref[...] = h2
    hn2 = _rms(h2) * gpre_ref[...] * (1.0 + sc2_ref[...]) + sh2_ref[...]
    hn2_ref[...] = _pack_halves(hn2)
    x_hi = hn2.astype(BF16)
    x_lo = (hn2 - x_hi.astype(F32)).astype(BF16)
    hi = jnp.dot(x_hi, wr_ref[...], preferred_element_type=F32)
    lo = jnp.dot(x_lo, wr_ref[:, :N_EXPERTS], preferred_element_type=F32)
    logit_ref[...] = hi[:, :N_EXPERTS] + (hi[:, N_EXPERTS:] + lo) + br_ref[...]


def _merge(o_ctx, o_a, o_b, o_c, gates, h, mods3, w_br, w_out_b, g_post3, g_pre_ffn3, w_router, b_router3, l):
    lat = lambda: pl.BlockSpec((TM, 512), lambda i: (jnp.maximum(i - CTX_TILES, 0), 0))
    mod = lambda j: pl.BlockSpec((None, 1, D_MODEL), lambda i: (_mod_index(i), 0, j))
    lw = lambda: pl.BlockSpec((None, 1, D_MODEL), lambda i: (l, 0, 0))
    row = lambda n: pl.BlockSpec((TM, n), lambda i: (i, 0))
    const = lambda shape: pl.BlockSpec(shape, lambda i: (0,) * len(shape), pipeline_mode=pl.Buffered(1))
    return pl.pallas_call(
        _merge_kernel,
        grid=(N_TILES,),
        in_specs=[pl.BlockSpec((TM, 1536), lambda i: (jnp.minimum(i, CTX_TILES - 1), 0)),
                  lat(), lat(), lat(), row(3072), row(D_MODEL), mod(2), mod(3), mod(4),
                  const((3, 512, D_MODEL)), const((D_MODEL, D_MODEL)), lw(), lw(),
                  const((D_MODEL, 2 * N_EXPERTS)),
                  pl.BlockSpec((None, 1, N_EXPERTS), lambda i: (l, 0, 0))],
        out_specs=[row(D_MODEL), row(HALF), row(N_EXPERTS)],
        out_shape=[jax.ShapeDtypeStruct((N_TOK, D_MODEL), F32),
                   jax.ShapeDtypeStruct((N_TOK, HALF), jnp.int32),
                   jax.ShapeDtypeStruct((N_TOK, N_EXPERTS), F32)],
        compiler_params=_cparams(("arbitrary",)),
        name="merge",
    )(o_ctx, o_a, o_b, o_c, gates, h, mods3, mods3, mods3, w_br, w_out_b, g_post3, g_pre_ffn3,
      w_router, b_router3)


def _moe_kernel(l, be_ref, nv_ref, x_ref, wgu_ref, bgu_ref, wd_ref, bd_ref, y_ref, wgu_bf, wd_bf):
    i = pl.program_id(0)
    e = be_ref[i]
    prev = be_ref[jnp.maximum(i - 1, 0)]

    @pl.when((i == 0) | (e != prev))
    def _():
        wgu_bf[...] = wgu_ref[...].astype(BF16)
        wd_bf[...] = wd_ref[...].astype(BF16)

    @pl.when(i < nv_ref[0])
    def _():
        xa, xb = _unpack_halves(x_ref[...])
        x = jnp.concatenate([xa.astype(BF16), xb.astype(BF16)], axis=1)
        b = bgu_ref[...]
        glu = jnp.dot(x, wgu_bf[:, :D_FF], preferred_element_type=F32) + b[:, :D_FF]
        lin = jnp.dot(x, wgu_bf[:, D_FF:], preferred_element_type=F32) + b[:, D_FF:]
        glu = jnp.minimum(glu, SWIGLU_LIMIT)
        lin = jnp.clip(lin, -SWIGLU_LIMIT, SWIGLU_LIMIT)
        act = glu * (1.0 / (1.0 + jnp.exp(-SWIGLU_ALPHA * glu))) * (lin + 1.0)
        y = jnp.dot(act.astype(BF16), wd_bf[...], preferred_element_type=F32) + bd_ref[...]
        y_ref[...] = _pack_halves(y)

    @pl.when(i >= nv_ref[0])
    def _():
        y_ref[...] = jnp.zeros_like(y_ref)


def _moe(block_e, n_valid, x_slots, w_gate_up, b_gate_up4, w_down, b_down4, l):
    grid_spec = pltpu.PrefetchScalarGridSpec(
        num_scalar_prefetch=2,
        grid=(N_MOE_BLOCKS,),
        in_specs=[pl.BlockSpec((MOE_TILE, HALF), lambda i, be, nv: (i, 0)),
                  pl.BlockSpec((None, None, D_MODEL, 2 * D_FF), lambda i, be, nv: (l, be[i], 0, 0)),
                  pl.BlockSpec((None, None, 1, 2 * D_FF), lambda i, be, nv: (l, be[i], 0, 0)),
                  pl.BlockSpec((None, None, D_FF, D_MODEL), lambda i, be, nv: (l, be[i], 0, 0)),
                  pl.BlockSpec((None, None, 1, D_MODEL), lambda i, be, nv: (l, be[i], 0, 0))],
        out_specs=pl.BlockSpec((MOE_TILE, HALF), lambda i, be, nv: (i, 0)),
        scratch_shapes=[pltpu.VMEM((D_MODEL, 2 * D_FF), BF16), pltpu.VMEM((D_FF, D_MODEL), BF16)])
    return pl.pallas_call(
        functools.partial(_moe_kernel, l),
        grid_spec=grid_spec,
        out_shape=jax.ShapeDtypeStruct((N_SLOTS, HALF), jnp.int32),
        compiler_params=_cparams(("arbitrary",)),
        name="moe",
    )(block_e, n_valid, x_slots, w_gate_up, b_gate_up4, w_down, b_down4)


def _combine_kernel(y0_ref, y1_ref, y2_ref, y3_ref, gate_ref, h_ref, g2_ref, gpost_ref, o_ref):
    gate = gate_ref[...]
    ffn = None
    for k, y_ref in enumerate((y0_ref, y1_ref, y2_ref, y3_ref)):
        ya, yb = _unpack_halves(y_ref[...])
        term = gate[:, k:k + 1] * jnp.concatenate([ya, yb], axis=1)
        ffn = term if ffn is None else ffn + term
    o_ref[...] = h_ref[...] + g2_ref[...] * (_rms(ffn) * gpost_ref[...])


def _combine(y_tok, gate, h2, mods3, g_post_ffn3, l):
    row = lambda n: pl.BlockSpec((TM, n), lambda i: (i, 0))
    choice = lambda k: pl.BlockSpec((TM, HALF), lambda i: (k * N_TILES + i, 0))
    return pl.pallas_call(
        _combine_kernel,
        grid=(N_TILES,),
        in_specs=[choice(0), choice(1), choice(2), choice(3), row(TOP_K), row(D_MODEL),
                  pl.BlockSpec((None, 1, D_MODEL), lambda i: (_mod_index(i), 0, 5)),
                  pl.BlockSpec((None, 1, D_MODEL), lambda i: (l, 0, 0))],
        out_specs=row(D_MODEL),
        out_shape=jax.ShapeDtypeStruct((N_TOK, D_MODEL), F32),
        compiler_params=_cparams(("arbitrary",)),
        name="combine",
    )(y_tok, y_tok, y_tok, y_tok, gate, h2, mods3, g_post_ffn3)


def _route(logits):
    top_val, top_idx = lax.top_k(logits, TOP_K)
    gate = jax.nn.softmax(top_val, axis=-1)
    flat_e = top_idx.reshape(-1)
    onehot = (flat_e[:, None] == jnp.arange(N_EXPERTS)[None, :]).astype(jnp.int32)
    csum = jnp.cumsum(onehot, axis=0)
    rank = jnp.sum(csum * onehot, axis=1) - 1
    counts = csum[-1]
    padded = (counts + MOE_TILE - 1) // MOE_TILE * MOE_TILE
    pad_end = jnp.cumsum(padded)
    dest = (pad_end - padded)[flat_e] + rank
    tok = jnp.arange(N_TOK * TOP_K, dtype=jnp.int32) // TOP_K
    slot_tok = (jnp.arange(N_SLOTS, dtype=jnp.int32) % N_TOK).at[dest].set(tok)
    block_start = jnp.arange(N_MOE_BLOCKS, dtype=jnp.int32) * MOE_TILE
    block_e = jnp.minimum(jnp.sum((pad_end[None, :] <= block_start[:, None]).astype(jnp.int32), axis=1),
                          N_EXPERTS - 1)
    n_valid = (pad_end[-1:] // MOE_TILE).astype(jnp.int32)
    dest_by_choice = dest.reshape(N_TOK, TOP_K).T.reshape(-1).astype(jnp.int32)
    return gate, dest_by_choice, slot_tok, block_e.astype(jnp.int32), n_valid


def _rope_tables():
    t = jnp.arange(DEC_SEQ)
    inv = ROPE_THETA ** (-jnp.arange(ROPE_PAIRS, dtype=F32) / ROPE_PAIRS)
    row = (t // GRID_W).astype(F32)[:, None] * inv
    col = (t % GRID_W).astype(F32)[:, None] * inv
    zeros = jnp.zeros_like(row)
    cos = jnp.concatenate([jnp.cos(row), jnp.cos(row), jnp.cos(col), jnp.cos(col)], axis=1)
    s1 = jnp.concatenate([-jnp.sin(row), zeros, -jnp.sin(col), zeros], axis=1)
    s2 = jnp.concatenate([zeros, jnp.sin(row), zeros, jnp.sin(col)], axis=1)
    ident = lambda v: jnp.full((TM, HEAD_DIM), v, F32)
    tables = [jnp.concatenate([x, ident(v)], axis=0) for x, v in ((cos, 1.0), (s1, 0.0), (s2, 0.0))]
    return [jnp.tile(x, (1, 2)) for x in tables]


def kernel(x_prompt, x_sample, cache_a, cache_b, cache_c, c, c_ctx, w_ada, b_ada, g_pre_mix, g_post_mix,
           g_pre_ffn, g_post_ffn, w_in, g_q_b, g_k_b, sink_a, rpb_c, w_br_a, w_br_b, w_br_c, w_out,
           w_router, b_router, w_gate_up, b_gate_up, w_down, b_down):
    h = jnp.concatenate([x_prompt.reshape(N_CTX, D_MODEL), x_sample.reshape(N_LAT, D_MODEL)], axis=0)
    cond8 = jnp.concatenate([c_ctx[None], c, jnp.zeros((3, D_MODEL), F32)], axis=0)
    cache_a = cache_a.astype(BF16).reshape(DEC_BATCH, DEPTH, 2, PAST_LEN, W_KV)
    cache_b = cache_b.astype(BF16).reshape(DEC_BATCH, DEPTH, 2, PAST_LEN, W_KV)
    cache_c = cache_c.astype(BF16).reshape(DEC_BATCH, DEPTH, 2, PAST_LEN, W_HEADS)
    cos_t, s1_t, s2_t = _rope_tables()
    bd = jnp.kron(jnp.eye(256 // HEAD_DIM, dtype=F32),
                  jnp.full((HEAD_DIM, HEAD_DIM), 1.0 / HEAD_DIM, F32)).astype(BF16)
    vec3 = lambda a: a.reshape(DEPTH, 1, a.shape[-1])
    scale = HEAD_DIM ** -0.5
    states = []
    for l in range(DEPTH):
        w = w_in[l]
        pair_cols = lambda a: a.reshape(D_MODEL, 2, 4, HEAD_DIM).transpose(0, 2, 1, 3).reshape(D_MODEL, 512)
        pair_rows = lambda a: a.reshape(2, 4, HEAD_DIM, D_MODEL).transpose(1, 0, 2, 3).reshape(512, D_MODEL)
        w_in_p = jnp.concatenate(
            [pair_cols(w[:, 0:512]) * scale, pair_cols(w[:, 768:1280]), w[:, 1536:2048] * scale,
             w[:, 2048:2560], w[:, 2560:3072],
             w[:, 512:640], w[:, 1280:1408], w[:, 640:768], w[:, 1408:1536], w[:, 3072:]], axis=1).astype(BF16)
        gq = (jnp.tile(g_q_b[l], H_B) * scale)[None]
        gk = jnp.tile(g_k_b[l], KV_B)[None]
        w_br = jnp.stack([pair_rows(w_br_a[l]), pair_rows(w_br_b[l]), w_br_c[l]], axis=0).astype(BF16)
        w_out_b = w_out[l].astype(BF16)
        w_r_hi = w_router[l].astype(BF16)
        w_r_lo = (w_router[l] - w_r_hi.astype(F32)).astype(BF16)
        w_router2 = jnp.concatenate([w_r_hi, w_r_lo], axis=1)

        mods3 = _adaln(cond8, w_ada, vec3(b_ada), l).reshape(8, 1, 6 * D_MODEL)
        q_all, kvab, kvc, gates, st = _proj(h, mods3, vec3(g_pre_mix), w_in_p, gq, gk, bd,
                                            cos_t, s1_t, s2_t, l)
        states.append(st)
        o_ctx = _ctx_attn(sink_a, q_all, kvab, kvc, l)
        o_a = _win_attn(sink_a, q_all, kvab, cache_a, l)
        o_b = _dense_attn(q_all, kvab, cache_b, l)
        o_c = _nbr_attn(q_all, kvc, cache_c, _nbr_bias_table(rpb_c[l]), l)
        h2, hn2, logits = _merge(o_ctx, o_a, o_b, o_c, gates, h, mods3, w_br, w_out_b,
                                 vec3(g_post_mix), vec3(g_pre_ffn), w_router2, vec3(b_router), l)
        gate, dest, slot_tok, block_e, n_valid = _route(logits)
        x_slots = _gather_rows(hn2, slot_tok)
        y_slots = _moe(block_e, n_valid, x_slots, w_gate_up,
                       b_gate_up.reshape(DEPTH, N_EXPERTS, 1, 2 * D_FF), w_down,
                       b_down.reshape(DEPTH, N_EXPERTS, 1, D_MODEL), l)
        y_tok = _gather_rows(y_slots, dest)
        h = _combine(y_tok, gate, h2, mods3, vec3(g_post_ffn), l)

    def state(c0, c1, width, heads):
        per_layer = [jnp.stack([st[:, c0:c0 + width], st[:, c1:c1 + width]], axis=0) for st in states]
        s = jnp.stack(per_layer, axis=0).reshape(DEPTH, 2, BATCH, SEQ, heads, HEAD_DIM)
        return jnp.transpose(s, (2, 0, 1, 3, 4, 5))

    return (h[:N_CTX].reshape(BATCH, SEQ, D_MODEL), h[N_CTX:].reshape(DEC_BATCH, DEC_SEQ, D_MODEL),
            state(0, 256, W_KV, KV_A), state(128, 384, W_KV, KV_B), state(512, 1024, W_HEADS, H_C))
```

```python
import functools

import jax
import jax.numpy as jnp
from jax import lax
from jax.experimental import pallas as pl
from jax.experimental.pallas import tpu as pltpu
from jax.experimental.pallas import tpu_sc as plsc

D_MODEL = 1024
BATCH = 32
SEQ = 256
DEPTH = 2
DEC_BATCH = 4
DEC_SEQ = 2048
PAST_LEN = 512
GRID_W = 64
HEAD_DIM = 64
H_A = 8
KV_A = 2
H_B = 8
KV_B = 2
H_C = 8
WINDOW_A = 128
NA_ROWS = 8
NA_COLS = 16
ROPE_THETA = 10000.0
ROPE_PAIRS = HEAD_DIM // 4
N_EXPERTS = 32
TOP_K = 4
D_FF = D_MODEL
SWIGLU_ALPHA = 1.702
SWIGLU_LIMIT = 7.0
EPS = 1e-6

W_HEADS = H_A * HEAD_DIM
W_KV = KV_A * HEAD_DIM
N_CTX = BATCH * SEQ
N_LAT = DEC_BATCH * DEC_SEQ
N_TOK = N_CTX + N_LAT
GRID_ROWS = DEC_SEQ // GRID_W
D_IN = 3 * W_HEADS + 4 * W_KV + 2 * W_HEADS + 3 * D_MODEL

TM = 256
N_TILES = N_TOK // TM
CTX_TILES = N_CTX // TM
LAT_TILES_PER_BATCH = DEC_SEQ // TM
TQ = 128
MOE_TILE = 256
N_SLOTS = N_TOK * TOP_K + N_EXPERTS * MOE_TILE
N_MOE_BLOCKS = N_SLOTS // MOE_TILE
NEG = -1e30
VMEM_LIMIT = 56 * 1024 * 1024

BF16 = jnp.bfloat16
F32 = jnp.float32


def _cparams(sem):
    return pltpu.CompilerParams(dimension_semantics=sem, vmem_limit_bytes=VMEM_LIMIT)


def _mod_index(i):
    return jnp.where(i < CTX_TILES, 0, 1 + (i - CTX_TILES) // LAT_TILES_PER_BATCH)


def _rms(x):
    return x * lax.rsqrt(jnp.mean(x * x, axis=-1, keepdims=True) + EPS)


HALF = D_MODEL // 2


def _pack_halves(x):
    hi = lax.bitcast_convert_type(x[:, :HALF].astype(BF16).astype(F32), jnp.uint32)
    lo = lax.bitcast_convert_type(x[:, HALF:].astype(BF16).astype(F32), jnp.uint32)
    return lax.bitcast_convert_type(hi | (lo >> 16), jnp.int32)


def _unpack_halves(w):
    u = lax.bitcast_convert_type(w, jnp.uint32)
    return (lax.bitcast_convert_type(u & jnp.uint32(0xFFFF0000), F32),
            lax.bitcast_convert_type(u << 16, F32))


GATHER_CHUNK = 64


def _gather_rows(table, idx):
    n = idx.shape[0]
    width = table.shape[1]
    info = plsc.get_sparse_core_info()
    n_workers = info.num_cores * info.num_subcores
    per_worker = n // n_workers
    n_chunks = per_worker // GATHER_CHUNK
    assert per_worker * n_workers == n and n_chunks * GATHER_CHUNK == per_worker and n_chunks % 2 == 0
    mesh = plsc.VectorSubcoreMesh(core_axis_name="core", subcore_axis_name="subcore")

    @functools.partial(
        pl.kernel, out_type=jax.ShapeDtypeStruct((n, width), table.dtype), mesh=mesh,
        scratch_types=[pltpu.VMEM((per_worker,), jnp.int32),
                       pltpu.VMEM((2, GATHER_CHUNK, width), table.dtype),
                       pltpu.SemaphoreType.DMA((2,)), pltpu.SemaphoreType.DMA((2,))])
    def gather(table_hbm, idx_hbm, out_hbm, idx_v, rows_v, gather_sem, write_sem):
        worker = lax.axis_index("subcore") * info.num_cores + lax.axis_index("core")
        base = worker * per_worker
        pltpu.sync_copy(idx_hbm.at[pl.ds(base, per_worker)], idx_v)

        def fetch(chunk, slot):
            rows = idx_v.at[pl.ds(chunk * GATHER_CHUNK, GATHER_CHUNK)]
            return pltpu.make_async_copy(table_hbm.at[rows], rows_v.at[slot], gather_sem.at[slot])

        def write(chunk, slot):
            dst = out_hbm.at[pl.ds(base + chunk * GATHER_CHUNK, GATHER_CHUNK)]
            return pltpu.make_async_copy(rows_v.at[slot], dst, write_sem.at[slot])

        fetch(0, 0).start()

        @pl.loop(0, n_chunks, step=2)
        def _(c):
            @pl.when(c > 0)
            def _():
                write(c - 1, 1).wait()

            fetch(c + 1, 1).start()
            fetch(c, 0).wait()
            write(c, 0).start()
            write(c, 0).wait()

            @pl.when(c + 2 < n_chunks)
            def _():
                fetch(c + 2, 0).start()

            fetch(c + 1, 1).wait()
            write(c + 1, 1).start()

        write(n_chunks - 1, 1).wait()

    return gather(table, idx)


def _adaln_kernel(c_ref, w_ref, b_ref, o_ref):
    c = c_ref[...]
    s = c / (1.0 + jnp.exp(-c))
    o_ref[...] = jnp.dot(s, w_ref[...], preferred_element_type=F32,
                         precision=lax.Precision.HIGHEST) + b_ref[...]


def _adaln(cond8, w_ada, b_ada3, l):
    tn = 1536
    return pl.pallas_call(
        _adaln_kernel,
        grid=(6 * D_MODEL // tn,),
        in_specs=[pl.BlockSpec((8, D_MODEL), lambda j: (0, 0)),
                  pl.BlockSpec((None, D_MODEL, tn), lambda j: (l, 0, j)),
                  pl.BlockSpec((None, 1, tn), lambda j: (l, 0, j))],
        out_specs=pl.BlockSpec((8, tn), lambda j: (0, j)),
        out_shape=jax.ShapeDtypeStruct((8, 6 * D_MODEL), F32),
        compiler_params=_cparams(("arbitrary",)),
        name="adaln",
    )(cond8, w_ada, b_ada3)


C_QA, C_QB, C_QC, C_KC, C_VC, C_KAB, C_GL = 0, 512, 1024, 1536, 2048, 2560, 3072


def _proj_kernel(h_ref, sh_ref, sc_ref, gpre_ref, w_ref, gq_ref, gk_ref, bd_ref,
                 cos_ref, s1_ref, s2_ref, *rest):
    q_ref, kvab_ref, kvc_ref, gates_ref, sta_ref, stb_ref, stc_ref = rest[-7:]
    i = pl.program_id(0)
    hn = _rms(h_ref[...]) * gpre_ref[...]
    hb = (hn * (1.0 + sc_ref[...]) + sh_ref[...]).astype(BF16)
    cos, s1, s2 = cos_ref[...], s1_ref[...], s2_ref[...]
    bd = bd_ref[...]

    def proj(c0, n):
        return jnp.dot(hb, w_ref[:, c0:c0 + n], preferred_element_type=F32)

    def rope(t):
        parts = []
        for g in range(t.shape[1] // 128):
            tg = t[:, g * 128:(g + 1) * 128]
            parts.append(tg * cos + pltpu.roll(tg, 112, 1) * s1 + pltpu.roll(tg, 16, 1) * s2)
        return parts[0] if len(parts) == 1 else jnp.concatenate(parts, axis=1)

    def headnorm(t, g):
        sq = (t * t).astype(BF16)
        n = t.shape[1]
        if n == 128:
            ms = jnp.dot(sq, bd[:128, :128], preferred_element_type=F32)
        else:
            ms = jnp.concatenate(
                [jnp.dot(sq[:, c:c + 256], bd, preferred_element_type=F32) for c in range(0, n, 256)],
                axis=1)
        return t * lax.rsqrt(ms + EPS) * g

    q_ref[:, 0:512] = rope(proj(C_QA, 512)).astype(BF16)
    q_ref[:, 512:1024] = rope(headnorm(proj(C_QB, 512), gq_ref[...])).astype(BF16)
    q_ref[:, 1024:1536] = proj(C_QC, 512).astype(BF16)
    kc = proj(C_KC, 512)
    vc = proj(C_VC, 512)
    kvc_ref[:, 0:512] = kc.astype(BF16)
    kvc_ref[:, 512:1024] = vc.astype(BF16)
    kab = proj(C_KAB, 512)
    ka = kab[:, 0:128]
    kb = headnorm(kab[:, 128:256], gk_ref[...])
    kvab_ref[:, 0:128] = rope(ka).astype(BF16)
    kvab_ref[:, 128:256] = rope(kb).astype(BF16)
    kvab_ref[:, 256:512] = kab[:, 256:512].astype(BF16)
    for j in range(6):
        gl = proj(C_GL + j * 512, 512)
        gates_ref[:, j * 512:(j + 1) * 512] = (1.0 / (1.0 + jnp.exp(-gl))).astype(BF16)

    @pl.when(i < CTX_TILES)
    def _():
        sta_ref[0] = ka
        sta_ref[1] = kab[:, 256:384]
        stb_ref[0] = kb
        stb_ref[1] = kab[:, 384:512]
        stc_ref[0] = kc
        stc_ref[1] = vc


def _proj(h, mods3, g_pre3, w_in_p, gq, gk, bd, cos_t, s1_t, s2_t, prev_states, l):
    def rope_idx(i):
        return jnp.where(i < CTX_TILES, LAT_TILES_PER_BATCH, (i - CTX_TILES) % LAT_TILES_PER_BATCH)

    const = lambda shape: pl.BlockSpec(shape, lambda i: (0,) * len(shape), pipeline_mode=pl.Buffered(1))
    rope_spec = pl.BlockSpec((TM, 128), lambda i: (rope_idx(i), 0))
    row = lambda n: pl.BlockSpec((TM, n), lambda i: (i, 0))
    state_spec = lambda n: pl.BlockSpec((None, None, 2, SEQ, n),
                                        lambda i: (jnp.minimum(i, CTX_TILES - 1), l, 0, 0, 0))
    state_shape = lambda n: jax.ShapeDtypeStruct((BATCH, DEPTH, 2, SEQ, n), F32)
    n_in = 11
    return pl.pallas_call(
        _proj_kernel,
        grid=(N_TILES,),
        in_specs=[row(D_MODEL),
                  pl.BlockSpec((None, 1, D_MODEL), lambda i: (_mod_index(i), 0, 0)),
                  pl.BlockSpec((None, 1, D_MODEL), lambda i: (_mod_index(i), 0, 1)),
                  pl.BlockSpec((None, 1, D_MODEL), lambda i: (l, 0, 0)),
                  const((D_MODEL, D_IN)), const((1, 512)), const((1, 128)), const((256, 256)),
                  rope_spec, rope_spec, rope_spec] + [pl.BlockSpec(memory_space=pl.ANY)] * len(prev_states),
        out_specs=[row(1536), row(512), row(1024), row(3072),
                   state_spec(W_KV), state_spec(W_KV), state_spec(W_HEADS)],
        out_shape=[jax.ShapeDtypeStruct((N_TOK, 1536), BF16),
                   jax.ShapeDtypeStruct((N_TOK, 512), BF16),
                   jax.ShapeDtypeStruct((N_TOK, 1024), BF16),
                   jax.ShapeDtypeStruct((N_TOK, 3072), BF16),
                   state_shape(W_KV), state_shape(W_KV), state_shape(W_HEADS)],
        input_output_aliases={n_in + j: 4 + j for j in range(len(prev_states))},
        compiler_params=_cparams(("arbitrary",)),
        name="proj",
    )(h, mods3, mods3, g_pre3, w_in_p, gq, gk, bd, cos_t, s1_t, s2_t, *prev_states)


def _qk(q, k):
    return lax.dot_general(q, k, (((1,), (1,)), ((), ())), preferred_element_type=F32)


def _softmax_pv(scores, values, sink=None):
    m = functools.reduce(jnp.maximum, [jnp.max(s, axis=-1, keepdims=True) for s in scores])
    if sink is not None:
        m = jnp.maximum(m, sink)
    ps = [jnp.exp(s - m) for s in scores]
    den = functools.reduce(jnp.add, [jnp.sum(p, axis=-1, keepdims=True) for p in ps])
    if sink is not None:
        den = den + jnp.exp(sink - m)
    o = functools.reduce(jnp.add, [jnp.dot(p.astype(BF16), v, preferred_element_type=F32)
                                   for p, v in zip(ps, values)])
    return o / den


def _lo_lanes(rows):
    return lax.broadcasted_iota(jnp.int32, (rows, 128), 1) < HEAD_DIM


def _stack_pairs(q, n_pairs):
    lo = _lo_lanes(q.shape[0])
    zero = jnp.zeros((q.shape[0], 128), q.dtype)
    pairs = [q[:, p * 128:(p + 1) * 128] for p in range(n_pairs)]
    return jnp.concatenate([jnp.where(lo, x, zero) for x in pairs] + [jnp.where(lo, zero, x) for x in pairs],
                           axis=0)


def _unstack_pairs(o, n_pairs):
    rows = o.shape[0] // (2 * n_pairs)
    lo = _lo_lanes(rows)
    return jnp.concatenate(
        [jnp.where(lo, o[p * rows:(p + 1) * rows], o[(n_pairs + p) * rows:(n_pairs + p + 1) * rows])
         for p in range(n_pairs)], axis=1)


def _sink_column(sink_ref, l, rows):
    return jnp.concatenate([jnp.full((rows, 1), sink_ref[l, h], F32) for h in range(H_A)], axis=0)


def _ctx_attn_kernel(l, sink_ref, q_ref, kvab_ref, kvc_ref, o_ref):
    qa = _stack_pairs(q_ref[:, 0:512], 4)
    o = _softmax_pv([_qk(qa, kvab_ref[:, 0:128])], [kvab_ref[:, 256:384]], _sink_column(sink_ref, l, SEQ))
    o_ref[:, 0:512] = _unstack_pairs(o, 4).astype(BF16)
    qb = _stack_pairs(q_ref[:, 512:1024], 4)
    o = _softmax_pv([_qk(qb, kvab_ref[:, 128:256])], [kvab_ref[:, 384:512]])
    o_ref[:, 512:1024] = _unstack_pairs(o, 4).astype(BF16)
    for hp in range(H_C // 2):
        cs = slice(hp * 128, (hp + 1) * 128)
        qc = _stack_pairs(q_ref[:, 1024 + hp * 128:1024 + (hp + 1) * 128], 1)
        o = _softmax_pv([_qk(qc, kvc_ref[:, cs])], [kvc_ref[:, 512 + hp * 128:512 + (hp + 1) * 128]])
        o_ref[:, 1024 + hp * 128:1024 + (hp + 1) * 128] = _unstack_pairs(o, 1).astype(BF16)


def _ctx_attn(sink_a, q_all, kvab, kvc, l):
    row = lambda n: pl.BlockSpec((SEQ, n), lambda b: (b, 0))
    return pl.pallas_call(
        functools.partial(_ctx_attn_kernel, l),
        grid=(BATCH,),
        in_specs=[pl.BlockSpec(memory_space=pltpu.SMEM), row(1536), row(512), row(1024)],
        out_specs=row(1536),
        out_shape=jax.ShapeDtypeStruct((N_CTX, 1536), BF16),
        compiler_params=_cparams(("arbitrary",)),
        name="ctx_attn",
    )(sink_a, q_all, kvab, kvc)


def _win_attn_kernel(l, sink_ref, q_ref, prev_ref, cur_ref, nxt_ref, ck_ref, cv_ref, o_ref):
    n = pl.program_id(1)
    nb = DEC_SEQ // TQ
    rows = H_A * TQ
    qpos = lax.broadcasted_iota(jnp.int32, (rows, TQ), 0) % TQ
    kpos = lax.broadcasted_iota(jnp.int32, (rows, TQ), 1)
    mask_prev = (kpos >= qpos) & (n > 0)
    mask_next = (kpos <= qpos) & (n < nb - 1)
    ks, vs = slice(0, 128), slice(256, 384)
    qs = _stack_pairs(q_ref[...], 4)
    s_prev = jnp.where(mask_prev, _qk(qs, prev_ref[:, ks]), NEG)
    s_cur = _qk(qs, cur_ref[:, ks])
    s_next = jnp.where(mask_next, _qk(qs, nxt_ref[:, ks]), NEG)
    s_ctx = _qk(qs, ck_ref[...])
    o = _softmax_pv([s_prev, s_cur, s_next, s_ctx],
                    [prev_ref[:, vs], cur_ref[:, vs], nxt_ref[:, vs], cv_ref[...]],
                    _sink_column(sink_ref, l, TQ))
    o_ref[...] = _unstack_pairs(o, 4).astype(BF16)


def _win_attn(sink_a, q_all, kvab, cache_a, l):
    nb = DEC_SEQ // TQ
    base = N_CTX // TQ
    kv_spec = lambda f: pl.BlockSpec((TQ, 512), lambda b, n: (base + b * nb + f(n), 0))
    cache_spec = lambda s: pl.BlockSpec((None, None, None, PAST_LEN, W_KV), lambda b, n: (b, l, s, 0, 0))
    return pl.pallas_call(
        functools.partial(_win_attn_kernel, l),
        grid=(DEC_BATCH, nb),
        in_specs=[pl.BlockSpec(memory_space=pltpu.SMEM),
                  pl.BlockSpec((TQ, 512), lambda b, n: (base + b * nb + n, 0)),
                  kv_spec(lambda n: jnp.maximum(n - 1, 0)), kv_spec(lambda n: n),
                  kv_spec(lambda n: jnp.minimum(n + 1, nb - 1)),
                  cache_spec(0), cache_spec(1)],
        out_specs=pl.BlockSpec((TQ, 512), lambda b, n: (b * nb + n, 0)),
        out_shape=jax.ShapeDtypeStruct((N_LAT, 512), BF16),
        compiler_params=_cparams(("arbitrary", "arbitrary")),
        name="win_attn",
    )(sink_a, q_all, kvab, kvab, kvab, cache_a, cache_a)


def _dense_attn_kernel(q_ref, kv_ref, ck_ref, cv_ref, o_ref):
    qs = _stack_pairs(q_ref[...], 4)
    half = qs.shape[0] // 2
    outs = []
    for g in range(KV_B):
        qg = qs[g * half:(g + 1) * half]
        outs.append(_softmax_pv([_qk(qg, kv_ref[:, 128:256]), _qk(qg, ck_ref[...])],
                                [kv_ref[:, 384:512], cv_ref[...]]))
    o_ref[...] = _unstack_pairs(jnp.concatenate(outs, axis=0), 4).astype(BF16)


def _dense_attn(q_all, kvab, cache_b, l):
    nb = DEC_SEQ // TQ
    base = N_CTX // TQ
    cache_spec = lambda s: pl.BlockSpec((None, None, None, PAST_LEN, W_KV), lambda b, n: (b, l, s, 0, 0))
    return pl.pallas_call(
        _dense_attn_kernel,
        grid=(DEC_BATCH, nb),
        in_specs=[pl.BlockSpec((TQ, 512), lambda b, n: (base + b * nb + n, 1)),
                  pl.BlockSpec((DEC_SEQ, 512), lambda b, n: (N_CTX // DEC_SEQ + b, 0)),
                  cache_spec(0), cache_spec(1)],
        out_specs=pl.BlockSpec((TQ, 512), lambda b, n: (b * nb + n, 0)),
        out_shape=jax.ShapeDtypeStruct((N_LAT, 512), BF16),
        compiler_params=_cparams(("arbitrary", "arbitrary")),
        name="dense_attn",
    )(q_all, kvab, cache_b, cache_b)


NBR_BAND = 4
NBR_Q = NBR_BAND * GRID_W
NBR_WIN_ROWS = 12
NBR_N_BANDS = GRID_ROWS // NBR_BAND
NBR_KBLK = NBR_Q
NBR_WIN_BLOCKS = NBR_WIN_ROWS * GRID_W // NBR_KBLK
NBR_LAST_KB = (GRID_ROWS - NBR_WIN_ROWS) * GRID_W // NBR_KBLK


def _nbr_window_block(band):
    return jnp.clip(band - 1, 0, NBR_LAST_KB)


def _nbr_attn_kernel(q_ref, k0_ref, k1_ref, k2_ref, ck_ref, cv_ref, bias_ref, o_ref):
    k_refs = (k0_ref, k1_ref, k2_ref)
    for hp in range(H_C // 2):
        cs = slice(hp * 128, (hp + 1) * 128)
        vs = slice(512 + hp * 128, 512 + (hp + 1) * 128)
        qs = _stack_pairs(q_ref[:, cs], 1)
        bias = jnp.concatenate([bias_ref[2 * hp], bias_ref[2 * hp + 1]], axis=0)
        scores = [_qk(qs, kr[:, cs]) + bias[:, j * NBR_KBLK:(j + 1) * NBR_KBLK] for j, kr in enumerate(k_refs)]
        scores.append(_qk(qs, ck_ref[:, cs]))
        o = _softmax_pv(scores, [kr[:, vs] for kr in k_refs] + [cv_ref[:, cs]])
        o_ref[:, cs] = _unstack_pairs(o, 1).astype(BF16)


def _nbr_attn(q_all, kvc, cache_c, bias_t, l):
    q_base = N_CTX // NBR_Q
    k_base = N_CTX // NBR_KBLK
    blocks_per_batch = DEC_SEQ // NBR_KBLK
    cache_spec = lambda s: pl.BlockSpec((None, None, None, PAST_LEN, W_HEADS), lambda band, b: (b, l, s, 0, 0))
    key_spec = lambda j: pl.BlockSpec(
        (NBR_KBLK, 1024), lambda band, b: (k_base + b * blocks_per_batch + _nbr_window_block(band) + j, 0))
    band_type = lambda band: jnp.where(band == 0, 0, jnp.where(band == NBR_N_BANDS - 1, 2, 1))
    return pl.pallas_call(
        _nbr_attn_kernel,
        grid=(NBR_N_BANDS, DEC_BATCH),
        in_specs=[pl.BlockSpec((NBR_Q, 512), lambda band, b: (q_base + b * NBR_N_BANDS + band, 2)),
                  key_spec(0), key_spec(1), key_spec(2), cache_spec(0), cache_spec(1),
                  pl.BlockSpec((None, H_C, NBR_Q, NBR_WIN_ROWS * GRID_W),
                               lambda band, b: (band_type(band), 0, 0, 0))],
        out_specs=pl.BlockSpec((NBR_Q, 512), lambda band, b: (b * NBR_N_BANDS + band, 0)),
        out_shape=jax.ShapeDtypeStruct((N_LAT, 512), BF16),
        compiler_params=_cparams(("arbitrary", "arbitrary")),
        name="nbr_attn",
    )(q_all, kvc, kvc, kvc, cache_c, cache_c, bias_t)


def _nbr_bias_table(rpb_l):
    c = jnp.arange(GRID_W)[:, None]
    kc = jnp.arange(GRID_W)[None, :]
    c_start = jnp.clip(c - NA_COLS // 2, 0, GRID_W - NA_COLS)
    valid = (kc >= c_start) & (kc < c_start + NA_COLS)
    dc = jnp.clip(kc - c + NA_COLS - 1, 0, 2 * NA_COLS - 2)
    t = jnp.where(valid[None, None], rpb_l[:, :, dc].astype(F32), NEG)
    neg = jnp.full((H_C, GRID_W, GRID_W), NEG, F32)
    tables = []
    for r0 in (0, NBR_BAND, GRID_ROWS - NBR_BAND):
        k0 = min(max(r0 - NA_ROWS // 2, 0), GRID_ROWS - NBR_WIN_ROWS)
        rows = []
        for dq in range(NBR_BAND):
            r = r0 + dq
            start = min(max(r - NA_ROWS // 2, 0), GRID_ROWS - NA_ROWS)
            cols = []
            for i in range(NBR_WIN_ROWS):
                kr = k0 + i
                cols.append(t[:, kr - r + NA_ROWS - 1] if start <= kr < start + NA_ROWS else neg)
            rows.append(jnp.concatenate(cols, axis=2))
        tables.append(jnp.concatenate(rows, axis=1))
    return jnp.stack(tables, axis=0)


def _merge_kernel(octx_ref, oa_ref, ob_ref, oc_ref, gates_ref, h_ref, g1_ref, sh2_ref, sc2_ref,
                  wbr_ref, wout_ref, gpost_ref, gpre_ref, wr_ref, br_ref,
                  h2_ref, hn2_ref, logit_ref):
    i = pl.program_id(0)
    is_ctx = i < CTX_TILES
    merged = None
    for j, lat_ref in enumerate((oa_ref, ob_ref, oc_ref)):
        o = jnp.where(is_ctx, octx_ref[:, j * 512:(j + 1) * 512], lat_ref[...])
        br = jnp.dot(o, wbr_ref[j], preferred_element_type=F32)
        term = gates_ref[:, j * D_MODEL:(j + 1) * D_MODEL].astype(F32) * br
        merged = term if merged is None else merged + term
    t = jnp.dot(merged.astype(BF16), wout_ref[...], preferred_element_type=F32)
    h2 = h_ref[...] + g1_ref[...] * (_rms(t) * gpost_ref[...])
    h2_ref[...] = h2
    hn2 = _rms(h2) * gpre_ref[...] * (1.0 + sc2_ref[...]) + sh2_ref[...]
    hn2_ref[...] = _pack_halves(hn2)
    x_hi = hn2.astype(BF16)
    x_lo = (hn2 - x_hi.astype(F32)).astype(BF16)
    hi = jnp.dot(x_hi, wr_ref[...], preferred_element_type=F32)
    lo = jnp.dot(x_lo, wr_ref[:, :N_EXPERTS], preferred_element_type=F32)
    logit_ref[...] = hi[:, :N_EXPERTS] + (hi[:, N_EXPERTS:] + lo) + br_ref[...]


def _merge(o_ctx, o_a, o_b, o_c, gates, h, mods3, w_br, w_out_b, g_post3, g_pre_ffn3, w_router, b_router3, l):
    lat = lambda: pl.BlockSpec((TM, 512), lambda i: (jnp.maximum(i - CTX_TILES, 0), 0))
    mod = lambda j: pl.BlockSpec((None, 1, D_MODEL), lambda i: (_mod_index(i), 0, j))
    lw = lambda: pl.BlockSpec((None, 1, D_MODEL), lambda i: (l, 0, 0))
    row = lambda n: pl.BlockSpec((TM, n), lambda i: (i, 0))
    const = lambda shape: pl.BlockSpec(shape, lambda i: (0,) * len(shape), pipeline_mode=pl.Buffered(1))
    return pl.pallas_call(
        _merge_kernel,
        grid=(N_TILES,),
        in_specs=[pl.BlockSpec((TM, 1536), lambda i: (jnp.minimum(i, CTX_TILES - 1), 0)),
                  lat(), lat(), lat(), row(3072), row(D_MODEL), mod(2), mod(3), mod(4),
                  const((3, 512, D_MODEL)), const((D_MODEL, D_MODEL)), lw(), lw(),
                  const((D_MODEL, 2 * N_EXPERTS)),
                  pl.BlockSpec((None, 1, N_EXPERTS), lambda i: (l, 0, 0))],
        out_specs=[row(D_MODEL), row(HALF), row(N_EXPERTS)],
        out_shape=[jax.ShapeDtypeStruct((N_TOK, D_MODEL), F32),
                   jax.ShapeDtypeStruct((N_TOK, HALF), jnp.int32),
                   jax.ShapeDtypeStruct((N_TOK, N_EXPERTS), F32)],
        compiler_params=_cparams(("arbitrary",)),
        name="merge",
    )(o_ctx, o_a, o_b, o_c, gates, h, mods3, mods3, mods3, w_br, w_out_b, g_post3, g_pre_ffn3,
      w_router, b_router3)


def _moe_kernel(l, be_ref, nv_ref, first_ref, slot_ref, nxt_ref, x_ref, wgu_hbm, bgu_ref, wd_hbm, bd_ref, y_ref,
                wgu_f32, wd_f32, wgu_bf, wd_bf, sem):
    i = pl.program_id(0)

    def fetch(e, s):
        return (pltpu.make_async_copy(wgu_hbm.at[l, e], wgu_f32.at[s], sem.at[0, s]),
                pltpu.make_async_copy(wd_hbm.at[l, e], wd_f32.at[s], sem.at[1, s]))

    @pl.when(first_ref[i] == 1)
    def _():
        s = slot_ref[i]

        @pl.when(i == 0)
        def _():
            for cp in fetch(be_ref[i], s):
                cp.start()

        for cp in fetch(be_ref[i], s):
            cp.wait()

        @pl.when(nxt_ref[i] >= 0)
        def _():
            for cp in fetch(nxt_ref[i], 1 - s):
                cp.start()

        wgu_bf[...] = wgu_f32[s].astype(BF16)
        wd_bf[...] = wd_f32[s].astype(BF16)

    @pl.when(i < nv_ref[0])
    def _():
        xa, xb = _unpack_halves(x_ref[...])
        x = jnp.concatenate([xa.astype(BF16), xb.astype(BF16)], axis=1)
        b = bgu_ref[...]
        glu = jnp.dot(x, wgu_bf[:, :D_FF], preferred_element_type=F32) + b[:, :D_FF]
        lin = jnp.dot(x, wgu_bf[:, D_FF:], preferred_element_type=F32) + b[:, D_FF:]
        glu = jnp.minimum(glu, SWIGLU_LIMIT)
        lin = jnp.clip(lin, -SWIGLU_LIMIT, SWIGLU_LIMIT)
        act = glu * (1.0 / (1.0 + jnp.exp(-SWIGLU_ALPHA * glu))) * (lin + 1.0)
        y = jnp.dot(act.astype(BF16), wd_bf[...], preferred_element_type=F32) + bd_ref[...]
        y_ref[...] = _pack_halves(y)

    @pl.when(i >= nv_ref[0])
    def _():
        y_ref[...] = jnp.zeros_like(y_ref)


def _moe(plan, x_slots, w_gate_up, b_gate_up4, w_down, b_down4, l):
    grid_spec = pltpu.PrefetchScalarGridSpec(
        num_scalar_prefetch=5,
        grid=(N_MOE_BLOCKS,),
        in_specs=[pl.BlockSpec((MOE_TILE, HALF), lambda i, be, *_: (i, 0)),
                  pl.BlockSpec(memory_space=pl.ANY),
                  pl.BlockSpec((None, None, 1, 2 * D_FF), lambda i, be, *_: (l, be[i], 0, 0)),
                  pl.BlockSpec(memory_space=pl.ANY),
                  pl.BlockSpec((None, None, 1, D_MODEL), lambda i, be, *_: (l, be[i], 0, 0))],
        out_specs=pl.BlockSpec((MOE_TILE, HALF), lambda i, be, *_: (i, 0)),
        scratch_shapes=[pltpu.VMEM((2, D_MODEL, 2 * D_FF), F32), pltpu.VMEM((2, D_FF, D_MODEL), F32),
                        pltpu.VMEM((D_MODEL, 2 * D_FF), BF16), pltpu.VMEM((D_FF, D_MODEL), BF16),
                        pltpu.SemaphoreType.DMA((2, 2))])
    return pl.pallas_call(
        functools.partial(_moe_kernel, l),
        grid_spec=grid_spec,
        out_shape=jax.ShapeDtypeStruct((N_SLOTS, HALF), jnp.int32),
        compiler_params=_cparams(("arbitrary",)),
        name="moe",
    )(*plan, x_slots, w_gate_up, b_gate_up4, w_down, b_down4)


def _combine_kernel(y0_ref, y1_ref, y2_ref, y3_ref, gate_ref, h_ref, g2_ref, gpost_ref, o_ref):
    gate = gate_ref[...]
    ffn = None
    for k, y_ref in enumerate((y0_ref, y1_ref, y2_ref, y3_ref)):
        ya, yb = _unpack_halves(y_ref[...])
        term = gate[:, k:k + 1] * jnp.concatenate([ya, yb], axis=1)
        ffn = term if ffn is None else ffn + term
    o_ref[...] = h_ref[...] + g2_ref[...] * (_rms(ffn) * gpost_ref[...])


def _combine(y_tok, gate, h2, mods3, g_post_ffn3, l):
    row = lambda n: pl.BlockSpec((TM, n), lambda i: (i, 0))
    choice = lambda k: pl.BlockSpec((TM, HALF), lambda i: (k * N_TILES + i, 0))
    return pl.pallas_call(
        _combine_kernel,
        grid=(N_TILES,),
        in_specs=[choice(0), choice(1), choice(2), choice(3), row(TOP_K), row(D_MODEL),
                  pl.BlockSpec((None, 1, D_MODEL), lambda i: (_mod_index(i), 0, 5)),
                  pl.BlockSpec((None, 1, D_MODEL), lambda i: (l, 0, 0))],
        out_specs=row(D_MODEL),
        out_shape=jax.ShapeDtypeStruct((N_TOK, D_MODEL), F32),
        compiler_params=_cparams(("arbitrary",)),
        name="combine",
    )(y_tok, y_tok, y_tok, y_tok, gate, h2, mods3, g_post_ffn3)


def _route_kernel(logit_ref, dest_ref, gate_ref, count_ref, base_ref, start_ref):
    sweep = pl.program_id(0)
    i = pl.program_id(1)

    @pl.when((sweep == 1) & (i == 0))
    def _():
        total = base_ref[...]
        padded = jnp.floor((total + (MOE_TILE - 1.0)) * (1.0 / MOE_TILE)) * MOE_TILE
        before = (lax.broadcasted_iota(jnp.int32, (N_EXPERTS, N_EXPERTS), 0)
                  < lax.broadcasted_iota(jnp.int32, (N_EXPERTS, N_EXPERTS), 1)).astype(F32)
        start_ref[...] = jnp.dot(padded, before, preferred_element_type=F32, precision=lax.Precision.HIGHEST)
        count_ref[...] = total

    @pl.when(i == 0)
    def _():
        base_ref[...] = jnp.zeros_like(base_ref)

    logits = logit_ref[...]
    lane = lax.broadcasted_iota(jnp.int32, (TM, N_EXPERTS), 1).astype(F32)
    vals, hots = [], []
    for _ in range(TOP_K):
        m = jnp.max(logits, axis=1, keepdims=True)
        idx = jnp.min(jnp.where(logits == m, lane, float(N_EXPERTS)), axis=1, keepdims=True)
        hot = lane == idx
        vals.append(m)
        hots.append(hot)
        logits = jnp.where(hot, -jnp.inf, logits)
    chosen = functools.reduce(jnp.add, [h.astype(F32) for h in hots])
    earlier_row = (lax.broadcasted_iota(jnp.int32, (TM, TM), 1)
                   < lax.broadcasted_iota(jnp.int32, (TM, TM), 0)).astype(BF16)
    earlier = jnp.dot(earlier_row, chosen.astype(BF16), preferred_element_type=F32)
    tile_count = jnp.sum(chosen, axis=0, keepdims=True)

    @pl.when(sweep == 1)
    def _():
        offs = start_ref[0:1, :] + base_ref[0:1, :] + earlier
        dest = [jnp.sum(jnp.where(h, offs, 0.0), axis=1, keepdims=True) for h in hots]
        dest_ref[...] = jnp.concatenate(dest, axis=1).astype(jnp.int32)
        e = [jnp.exp(v - vals[0]) for v in vals]
        den = functools.reduce(jnp.add, e)
        gate_ref[...] = jnp.concatenate(e, axis=1) / den

    base_ref[...] = base_ref[...] + tile_count


def _route(logits):
    tile = lambda n: pl.BlockSpec((TM, n), lambda s, i: (s * i, 0))
    dest, gate, counts = pl.pallas_call(
        _route_kernel,
        grid=(2, N_TILES),
        in_specs=[pl.BlockSpec((TM, N_EXPERTS), lambda s, i: (i, 0))],
        out_specs=[tile(TOP_K), tile(TOP_K), pl.BlockSpec((8, N_EXPERTS), lambda s, i: (0, 0))],
        out_shape=[jax.ShapeDtypeStruct((N_TOK, TOP_K), jnp.int32),
                   jax.ShapeDtypeStruct((N_TOK, TOP_K), F32),
                   jax.ShapeDtypeStruct((8, N_EXPERTS), F32)],
        scratch_shapes=[pltpu.VMEM((8, N_EXPERTS), F32), pltpu.VMEM((8, N_EXPERTS), F32)],
        compiler_params=_cparams(("arbitrary", "arbitrary")),
        name="route",
    )(logits)
    counts = counts[0].astype(jnp.int32)
    padded = (counts + MOE_TILE - 1) // MOE_TILE * MOE_TILE
    pad_end = jnp.cumsum(padded)
    block = jnp.arange(N_MOE_BLOCKS, dtype=jnp.int32)
    block_e = jnp.minimum(jnp.sum((pad_end[None, :] <= block[:, None] * MOE_TILE).astype(jnp.int32), axis=1),
                          N_EXPERTS - 1)
    n_valid = pad_end[-1] // MOE_TILE
    first = (block < n_valid) & ((block == 0) | (block_e != jnp.roll(block_e, 1)))
    slot = (jnp.cumsum(first.astype(jnp.int32)) - 1) % 2
    next_block = pad_end[block_e] // MOE_TILE
    nxt = jnp.where(next_block < n_valid, block_e[jnp.minimum(next_block, N_MOE_BLOCKS - 1)], -1)
    plan = (block_e, n_valid[None].astype(jnp.int32), first.astype(jnp.int32), slot.astype(jnp.int32),
            nxt.astype(jnp.int32))
    tok = jnp.arange(N_TOK * TOP_K, dtype=jnp.int32) // TOP_K
    slot_tok = (jnp.arange(N_SLOTS, dtype=jnp.int32) % N_TOK).at[dest.reshape(-1)].set(tok)
    return gate, dest.T.reshape(-1), slot_tok, plan


def _rope_tables():
    t = jnp.arange(DEC_SEQ)
    inv = ROPE_THETA ** (-jnp.arange(ROPE_PAIRS, dtype=F32) / ROPE_PAIRS)
    row = (t // GRID_W).astype(F32)[:, None] * inv
    col = (t % GRID_W).astype(F32)[:, None] * inv
    zeros = jnp.zeros_like(row)
    cos = jnp.concatenate([jnp.cos(row), jnp.cos(row), jnp.cos(col), jnp.cos(col)], axis=1)
    s1 = jnp.concatenate([-jnp.sin(row), zeros, -jnp.sin(col), zeros], axis=1)
    s2 = jnp.concatenate([zeros, jnp.sin(row), zeros, jnp.sin(col)], axis=1)
    ident = lambda v: jnp.full((TM, HEAD_DIM), v, F32)
    tables = [jnp.concatenate([x, ident(v)], axis=0) for x, v in ((cos, 1.0), (s1, 0.0), (s2, 0.0))]
    return [jnp.tile(x, (1, 2)) for x in tables]


def kernel(x_prompt, x_sample, cache_a, cache_b, cache_c, c, c_ctx, w_ada, b_ada, g_pre_mix, g_post_mix,
           g_pre_ffn, g_post_ffn, w_in, g_q_b, g_k_b, sink_a, rpb_c, w_br_a, w_br_b, w_br_c, w_out,
           w_router, b_router, w_gate_up, b_gate_up, w_down, b_down):
    h = jnp.concatenate([x_prompt.reshape(N_CTX, D_MODEL), x_sample.reshape(N_LAT, D_MODEL)], axis=0)
    cond8 = jnp.concatenate([c_ctx[None], c, jnp.zeros((3, D_MODEL), F32)], axis=0)
    cache_a = cache_a.astype(BF16).reshape(DEC_BATCH, DEPTH, 2, PAST_LEN, W_KV)
    cache_b = cache_b.astype(BF16).reshape(DEC_BATCH, DEPTH, 2, PAST_LEN, W_KV)
    cache_c = cache_c.astype(BF16).reshape(DEC_BATCH, DEPTH, 2, PAST_LEN, W_HEADS)
    cos_t, s1_t, s2_t = _rope_tables()
    bd = jnp.kron(jnp.eye(256 // HEAD_DIM, dtype=F32),
                  jnp.full((HEAD_DIM, HEAD_DIM), 1.0 / HEAD_DIM, F32)).astype(BF16)
    vec3 = lambda a: a.reshape(DEPTH, 1, a.shape[-1])
    scale = HEAD_DIM ** -0.5
    states = []
    for l in range(DEPTH):
        w = w_in[l]
        pair_cols = lambda a: a.reshape(D_MODEL, 2, 4, HEAD_DIM).transpose(0, 2, 1, 3).reshape(D_MODEL, 512)
        pair_rows = lambda a: a.reshape(2, 4, HEAD_DIM, D_MODEL).transpose(1, 0, 2, 3).reshape(512, D_MODEL)
        w_in_p = jnp.concatenate(
            [pair_cols(w[:, 0:512]) * scale, pair_cols(w[:, 768:1280]), w[:, 1536:2048] * scale,
             w[:, 2048:2560], w[:, 2560:3072],
             w[:, 512:640], w[:, 1280:1408], w[:, 640:768], w[:, 1408:1536], w[:, 3072:]], axis=1).astype(BF16)
        gq = (jnp.tile(g_q_b[l], H_B) * scale)[None]
        gk = jnp.tile(g_k_b[l], KV_B)[None]
        w_br = jnp.stack([pair_rows(w_br_a[l]), pair_rows(w_br_b[l]), w_br_c[l]], axis=0).astype(BF16)
        w_out_b = w_out[l].astype(BF16)
        w_r_hi = w_router[l].astype(BF16)
        w_r_lo = (w_router[l] - w_r_hi.astype(F32)).astype(BF16)
        w_router2 = jnp.concatenate([w_r_hi, w_r_lo], axis=1)

        mods3 = _adaln(cond8, w_ada, vec3(b_ada), l).reshape(8, 1, 6 * D_MODEL)
        q_all, kvab, kvc, gates, *states = _proj(h, mods3, vec3(g_pre_mix), w_in_p, gq, gk, bd,
                                                 cos_t, s1_t, s2_t, tuple(states), l)
        o_ctx = _ctx_attn(sink_a, q_all, kvab, kvc, l)
        o_a = _win_attn(sink_a, q_all, kvab, cache_a, l)
        o_b = _dense_attn(q_all, kvab, cache_b, l)
        o_c = _nbr_attn(q_all, kvc, cache_c, _nbr_bias_table(rpb_c[l]), l)
        h2, hn2, logits = _merge(o_ctx, o_a, o_b, o_c, gates, h, mods3, w_br, w_out_b,
                                 vec3(g_post_mix), vec3(g_pre_ffn), w_router2, vec3(b_router), l)
        gate, dest, slot_tok, plan = _route(logits)
        x_slots = _gather_rows(hn2, slot_tok)
        y_slots = _moe(plan, x_slots, w_gate_up,
                       b_gate_up.reshape(DEPTH, N_EXPERTS, 1, 2 * D_FF), w_down,
                       b_down.reshape(DEPTH, N_EXPERTS, 1, D_MODEL), l)
        y_tok = _gather_rows(y_slots, dest)
        h = _combine(y_tok, gate, h2, mods3, vec3(g_post_ffn), l)

    state_a, state_b, state_c = states
    heads = lambda s, n: s.reshape(BATCH, DEPTH, 2, SEQ, n, HEAD_DIM)
    return (h[:N_CTX].reshape(BATCH, SEQ, D_MODEL), h[N_CTX:].reshape(DEC_BATCH, DEC_SEQ, D_MODEL),
            heads(state_a, KV_A), heads(state_b, KV_B), heads(state_c, H_C))
```

```python
import functools

import jax
import jax.numpy as jnp
from jax import lax
from jax.experimental import pallas as pl
from jax.experimental.pallas import tpu as pltpu
from jax.experimental.pallas import tpu_sc as plsc

D_MODEL = 1024
BATCH = 32
SEQ = 256
DEPTH = 2
DEC_BATCH = 4
DEC_SEQ = 2048
PAST_LEN = 512
GRID_W = 64
HEAD_DIM = 64
H_A = 8
KV_A = 2
H_B = 8
KV_B = 2
H_C = 8
WINDOW_A = 128
NA_ROWS = 8
NA_COLS = 16
ROPE_THETA = 10000.0
ROPE_PAIRS = HEAD_DIM // 4
N_EXPERTS = 32
TOP_K = 4
D_FF = D_MODEL
SWIGLU_ALPHA = 1.702
SWIGLU_LIMIT = 7.0
EPS = 1e-6

W_HEADS = H_A * HEAD_DIM
W_KV = KV_A * HEAD_DIM
N_CTX = BATCH * SEQ
N_LAT = DEC_BATCH * DEC_SEQ
N_TOK = N_CTX + N_LAT
GRID_ROWS = DEC_SEQ // GRID_W
D_IN = 3 * W_HEADS + 4 * W_KV + 2 * W_HEADS + 3 * D_MODEL

TM = 256
N_TILES = N_TOK // TM
CTX_TILES = N_CTX // TM
LAT_TILES_PER_BATCH = DEC_SEQ // TM
TQ = 128
MOE_TILE = 256
N_SLOTS = N_TOK * TOP_K + N_EXPERTS * MOE_TILE
N_MOE_BLOCKS = N_SLOTS // MOE_TILE
NEG = -1e30
VMEM_LIMIT = 56 * 1024 * 1024

BF16 = jnp.bfloat16
F32 = jnp.float32


def _cparams(sem):
    return pltpu.CompilerParams(dimension_semantics=sem, vmem_limit_bytes=VMEM_LIMIT)


def _mod_index(i):
    return jnp.where(i < CTX_TILES, 0, 1 + (i - CTX_TILES) // LAT_TILES_PER_BATCH)


def _rms(x):
    return x * lax.rsqrt(jnp.mean(x * x, axis=-1, keepdims=True) + EPS)


HALF = D_MODEL // 2


def _pack_halves(x):
    hi = lax.bitcast_convert_type(x[:, :HALF].astype(BF16).astype(F32), jnp.uint32)
    lo = lax.bitcast_convert_type(x[:, HALF:].astype(BF16).astype(F32), jnp.uint32)
    return lax.bitcast_convert_type(hi | (lo >> 16), jnp.int32)


def _unpack_halves(w):
    u = lax.bitcast_convert_type(w, jnp.uint32)
    return (lax.bitcast_convert_type(u & jnp.uint32(0xFFFF0000), F32),
            lax.bitcast_convert_type(u << 16, F32))


GATHER_CHUNK = 64


def _gather_rows(table, idx):
    n = idx.shape[0]
    width = table.shape[1]
    info = plsc.get_sparse_core_info()
    n_workers = info.num_cores * info.num_subcores
    per_worker = n // n_workers
    n_chunks = per_worker // GATHER_CHUNK
    assert per_worker * n_workers == n and n_chunks * GATHER_CHUNK == per_worker and n_chunks % 2 == 0
    mesh = plsc.VectorSubcoreMesh(core_axis_name="core", subcore_axis_name="subcore")

    @functools.partial(
        pl.kernel, out_type=jax.ShapeDtypeStruct((n, width), table.dtype), mesh=mesh,
        scratch_types=[pltpu.VMEM((per_worker,), jnp.int32),
                       pltpu.VMEM((2, GATHER_CHUNK, width), table.dtype),
                       pltpu.SemaphoreType.DMA((2,)), pltpu.SemaphoreType.DMA((2,))])
    def gather(table_hbm, idx_hbm, out_hbm, idx_v, rows_v, gather_sem, write_sem):
        worker = lax.axis_index("subcore") * info.num_cores + lax.axis_index("core")
        base = worker * per_worker
        pltpu.sync_copy(idx_hbm.at[pl.ds(base, per_worker)], idx_v)

        def fetch(chunk, slot):
            rows = idx_v.at[pl.ds(chunk * GATHER_CHUNK, GATHER_CHUNK)]
            return pltpu.make_async_copy(table_hbm.at[rows], rows_v.at[slot], gather_sem.at[slot])

        def write(chunk, slot):
            dst = out_hbm.at[pl.ds(base + chunk * GATHER_CHUNK, GATHER_CHUNK)]
            return pltpu.make_async_copy(rows_v.at[slot], dst, write_sem.at[slot])

        fetch(0, 0).start()

        @pl.loop(0, n_chunks, step=2)
        def _(c):
            @pl.when(c > 0)
            def _():
                write(c - 1, 1).wait()

            fetch(c + 1, 1).start()
            fetch(c, 0).wait()
            write(c, 0).start()
            write(c, 0).wait()

            @pl.when(c + 2 < n_chunks)
            def _():
                fetch(c + 2, 0).start()

            fetch(c + 1, 1).wait()
            write(c + 1, 1).start()

        write(n_chunks - 1, 1).wait()

    return gather(table, idx)


SC_WORKERS_V7X = 32
SCATTER_TOKENS_PER_WORKER = N_TOK // SC_WORKERS_V7X


def _scatter_rows(table, dest_sc, n_out):
    width = table.shape[1]
    info = plsc.get_sparse_core_info()
    assert info.num_cores * info.num_subcores == SC_WORKERS_V7X
    per_worker = SCATTER_TOKENS_PER_WORKER
    n_chunks = per_worker // GATHER_CHUNK
    assert n_chunks % 2 == 0 and dest_sc.shape == (SC_WORKERS_V7X, n_chunks * TOP_K, GATHER_CHUNK)
    mesh = plsc.VectorSubcoreMesh(core_axis_name="core", subcore_axis_name="subcore")

    @functools.partial(
        pl.kernel, out_type=jax.ShapeDtypeStruct((n_out, width), table.dtype), mesh=mesh,
        scratch_types=[pltpu.VMEM((n_chunks * TOP_K, GATHER_CHUNK), jnp.int32),
                       pltpu.VMEM((2, GATHER_CHUNK, width), table.dtype),
                       pltpu.SemaphoreType.DMA((2,)), pltpu.SemaphoreType.DMA((2,))])
    def scatter(table_hbm, dest_hbm, out_hbm, idx_v, rows_v, read_sem, write_sem):
        worker = lax.axis_index("subcore") * info.num_cores + lax.axis_index("core")
        base = worker * per_worker
        pltpu.sync_copy(dest_hbm.at[worker], idx_v)

        def read(chunk, slot):
            src = table_hbm.at[pl.ds(base + chunk * GATHER_CHUNK, GATHER_CHUNK)]
            return pltpu.make_async_copy(src, rows_v.at[slot], read_sem.at[slot])

        def writes(chunk, slot):
            return [pltpu.make_async_copy(rows_v.at[slot], out_hbm.at[idx_v.at[chunk * TOP_K + k]],
                                          write_sem.at[slot]) for k in range(TOP_K)]

        read(0, 0).start()

        @pl.loop(0, n_chunks, step=2)
        def _(c):
            @pl.when(c > 0)
            def _():
                for cp in writes(c - 1, 1):
                    cp.wait()

            read(c + 1, 1).start()
            read(c, 0).wait()
            for cp in writes(c, 0):
                cp.start()
            for cp in writes(c, 0):
                cp.wait()

            @pl.when(c + 2 < n_chunks)
            def _():
                read(c + 2, 0).start()

            read(c + 1, 1).wait()
            for cp in writes(c + 1, 1):
                cp.start()

        for cp in writes(n_chunks - 1, 1):
            cp.wait()

    return scatter(table, dest_sc)


def _adaln_kernel(c_ref, w_ref, b_ref, o_ref):
    c = c_ref[...]
    s = c / (1.0 + jnp.exp(-c))
    o_ref[...] = jnp.dot(s, w_ref[...], preferred_element_type=F32,
                         precision=lax.Precision.HIGHEST) + b_ref[...]


def _adaln(cond8, w_ada, b_ada3, l):
    tn = 1536
    return pl.pallas_call(
        _adaln_kernel,
        grid=(6 * D_MODEL // tn,),
        in_specs=[pl.BlockSpec((8, D_MODEL), lambda j: (0, 0)),
                  pl.BlockSpec((None, D_MODEL, tn), lambda j: (l, 0, j)),
                  pl.BlockSpec((None, 1, tn), lambda j: (l, 0, j))],
        out_specs=pl.BlockSpec((8, tn), lambda j: (0, j)),
        out_shape=jax.ShapeDtypeStruct((8, 6 * D_MODEL), F32),
        compiler_params=_cparams(("arbitrary",)),
        name="adaln",
    )(cond8, w_ada, b_ada3)


C_QA, C_QB, C_QC, C_KC, C_VC, C_KAB, C_GL = 0, 512, 1024, 1536, 2048, 2560, 3072


def _proj_kernel(h_ref, sh_ref, sc_ref, gpre_ref, w_ref, gq_ref, gk_ref, bd_ref,
                 cos_ref, s1_ref, s2_ref, *rest):
    q_ref, kvab_ref, kvc_ref, gates_ref, sta_ref, stb_ref, stc_ref = rest[-7:]
    i = pl.program_id(0)
    hn = _rms(h_ref[...]) * gpre_ref[...]
    hb = (hn * (1.0 + sc_ref[...]) + sh_ref[...]).astype(BF16)
    cos, s1, s2 = cos_ref[...], s1_ref[...], s2_ref[...]
    bd = bd_ref[...]

    def proj(c0, n):
        return jnp.dot(hb, w_ref[:, c0:c0 + n], preferred_element_type=F32)

    def rope(t):
        parts = []
        for g in range(t.shape[1] // 128):
            tg = t[:, g * 128:(g + 1) * 128]
            parts.append(tg * cos + pltpu.roll(tg, 112, 1) * s1 + pltpu.roll(tg, 16, 1) * s2)
        return parts[0] if len(parts) == 1 else jnp.concatenate(parts, axis=1)

    def headnorm(t, g):
        sq = (t * t).astype(BF16)
        n = t.shape[1]
        if n == 128:
            ms = jnp.dot(sq, bd[:128, :128], preferred_element_type=F32)
        else:
            ms = jnp.concatenate(
                [jnp.dot(sq[:, c:c + 256], bd, preferred_element_type=F32) for c in range(0, n, 256)],
                axis=1)
        return t * lax.rsqrt(ms + EPS) * g

    q_ref[:, 0:512] = rope(proj(C_QA, 512)).astype(BF16)
    q_ref[:, 512:1024] = rope(headnorm(proj(C_QB, 512), gq_ref[...])).astype(BF16)
    q_ref[:, 1024:1536] = proj(C_QC, 512).astype(BF16)
    kc = proj(C_KC, 512)
    vc = proj(C_VC, 512)
    kvc_ref[:, 0:512] = kc.astype(BF16)
    kvc_ref[:, 512:1024] = vc.astype(BF16)
    kab = proj(C_KAB, 512)
    ka = kab[:, 0:128]
    kb = headnorm(kab[:, 128:256], gk_ref[...])
    kvab_ref[:, 0:128] = rope(ka).astype(BF16)
    kvab_ref[:, 128:256] = rope(kb).astype(BF16)
    kvab_ref[:, 256:512] = kab[:, 256:512].astype(BF16)
    for j in range(6):
        gl = proj(C_GL + j * 512, 512)
        gates_ref[:, j * 512:(j + 1) * 512] = (1.0 / (1.0 + jnp.exp(-gl))).astype(BF16)

    @pl.when(i < CTX_TILES)
    def _():
        for st_ref, k, v in ((sta_ref, ka, kab[:, 256:384]), (stb_ref, kb, kab[:, 384:512]), (stc_ref, kc, vc)):
            if len(st_ref.shape) == 4:
                st_ref[0, 0] = k
                st_ref[0, 1] = v
                st_ref[1:] = jnp.zeros((DEPTH - 1,) + tuple(st_ref.shape[1:]), F32)
            else:
                st_ref[0] = k
                st_ref[1] = v


def _proj(h, mods3, g_pre3, w_in_p, gq, gk, bd, cos_t, s1_t, s2_t, prev_states, l):
    def rope_idx(i):
        return jnp.where(i < CTX_TILES, LAT_TILES_PER_BATCH, (i - CTX_TILES) % LAT_TILES_PER_BATCH)

    const = lambda shape: pl.BlockSpec(shape, lambda i: (0,) * len(shape), pipeline_mode=pl.Buffered(1))
    rope_spec = pl.BlockSpec((TM, 128), lambda i: (rope_idx(i), 0))
    row = lambda n: pl.BlockSpec((TM, n), lambda i: (i, 0))
    if l == 0:
        state_spec = lambda n: pl.BlockSpec((None, DEPTH, 2, SEQ, n),
                                            lambda i: (jnp.minimum(i, CTX_TILES - 1), 0, 0, 0, 0))
    else:
        state_spec = lambda n: pl.BlockSpec((None, None, 2, SEQ, n),
                                            lambda i: (jnp.minimum(i, CTX_TILES - 1), l, 0, 0, 0))
    state_shape = lambda n: jax.ShapeDtypeStruct((BATCH, DEPTH, 2, SEQ, n), F32)
    n_in = 11
    return pl.pallas_call(
        _proj_kernel,
        grid=(N_TILES,),
        in_specs=[row(D_MODEL),
                  pl.BlockSpec((None, 1, D_MODEL), lambda i: (_mod_index(i), 0, 0)),
                  pl.BlockSpec((None, 1, D_MODEL), lambda i: (_mod_index(i), 0, 1)),
                  pl.BlockSpec((None, 1, D_MODEL), lambda i: (l, 0, 0)),
                  const((D_MODEL, D_IN)), const((1, 512)), const((1, 128)), const((256, 256)),
                  rope_spec, rope_spec, rope_spec] + [pl.BlockSpec(memory_space=pl.ANY)] * len(prev_states),
        out_specs=[row(1536), row(512), row(1024), row(3072),
                   state_spec(W_KV), state_spec(W_KV), state_spec(W_HEADS)],
        out_shape=[jax.ShapeDtypeStruct((N_TOK, 1536), BF16),
                   jax.ShapeDtypeStruct((N_TOK, 512), BF16),
                   jax.ShapeDtypeStruct((N_TOK, 1024), BF16),
                   jax.ShapeDtypeStruct((N_TOK, 3072), BF16),
                   state_shape(W_KV), state_shape(W_KV), state_shape(W_HEADS)],
        input_output_aliases={n_in + j: 4 + j for j in range(len(prev_states))},
        compiler_params=_cparams(("arbitrary",)),
        name="proj",
    )(h, mods3, mods3, g_pre3, w_in_p, gq, gk, bd, cos_t, s1_t, s2_t, *prev_states)


def _qk(q, k):
    return lax.dot_general(q, k, (((1,), (1,)), ((), ())), preferred_element_type=F32)


def _softmax_pv(scores, values, sink=None):
    m = functools.reduce(jnp.maximum, [jnp.max(s, axis=-1, keepdims=True) for s in scores])
    if sink is not None:
        m = jnp.maximum(m, sink)
    ps = [jnp.exp(s - m) for s in scores]
    den = functools.reduce(jnp.add, [jnp.sum(p, axis=-1, keepdims=True) for p in ps])
    if sink is not None:
        den = den + jnp.exp(sink - m)
    o = functools.reduce(jnp.add, [jnp.dot(p.astype(BF16), v, preferred_element_type=F32)
                                   for p, v in zip(ps, values)])
    return o / den


def _lo_lanes(rows):
    return lax.broadcasted_iota(jnp.int32, (rows, 128), 1) < HEAD_DIM


def _stack_pairs(q, n_pairs):
    lo = _lo_lanes(q.shape[0])
    zero = jnp.zeros((q.shape[0], 128), q.dtype)
    pairs = [q[:, p * 128:(p + 1) * 128] for p in range(n_pairs)]
    return jnp.concatenate([jnp.where(lo, x, zero) for x in pairs] + [jnp.where(lo, zero, x) for x in pairs],
                           axis=0)


def _unstack_pairs(o, n_pairs):
    rows = o.shape[0] // (2 * n_pairs)
    lo = _lo_lanes(rows)
    return jnp.concatenate(
        [jnp.where(lo, o[p * rows:(p + 1) * rows], o[(n_pairs + p) * rows:(n_pairs + p + 1) * rows])
         for p in range(n_pairs)], axis=1)


def _sink_column(sink_ref, l, rows):
    return jnp.concatenate([jnp.full((rows, 1), sink_ref[l, h], F32) for h in range(H_A)], axis=0)


def _ctx_attn_kernel(l, sink_ref, q_ref, kvab_ref, kvc_ref, o_ref):
    qa = _stack_pairs(q_ref[:, 0:512], 4)
    o = _softmax_pv([_qk(qa, kvab_ref[:, 0:128])], [kvab_ref[:, 256:384]], _sink_column(sink_ref, l, SEQ))
    o_ref[:, 0:512] = _unstack_pairs(o, 4).astype(BF16)
    qb = _stack_pairs(q_ref[:, 512:1024], 4)
    o = _softmax_pv([_qk(qb, kvab_ref[:, 128:256])], [kvab_ref[:, 384:512]])
    o_ref[:, 512:1024] = _unstack_pairs(o, 4).astype(BF16)
    for hp in range(H_C // 2):
        cs = slice(hp * 128, (hp + 1) * 128)
        qc = _stack_pairs(q_ref[:, 1024 + hp * 128:1024 + (hp + 1) * 128], 1)
        o = _softmax_pv([_qk(qc, kvc_ref[:, cs])], [kvc_ref[:, 512 + hp * 128:512 + (hp + 1) * 128]])
        o_ref[:, 1024 + hp * 128:1024 + (hp + 1) * 128] = _unstack_pairs(o, 1).astype(BF16)


def _ctx_attn(sink_a, q_all, kvab, kvc, l):
    row = lambda n: pl.BlockSpec((SEQ, n), lambda b: (b, 0))
    return pl.pallas_call(
        functools.partial(_ctx_attn_kernel, l),
        grid=(BATCH,),
        in_specs=[pl.BlockSpec(memory_space=pltpu.SMEM), row(1536), row(512), row(1024)],
        out_specs=row(1536),
        out_shape=jax.ShapeDtypeStruct((N_CTX, 1536), BF16),
        compiler_params=_cparams(("arbitrary",)),
        name="ctx_attn",
    )(sink_a, q_all, kvab, kvc)


def _win_attn_kernel(l, sink_ref, q_ref, prev_ref, cur_ref, nxt_ref, ck_ref, cv_ref, o_ref):
    n = pl.program_id(1)
    nb = DEC_SEQ // TQ
    rows = H_A * TQ
    qpos = lax.broadcasted_iota(jnp.int32, (rows, TQ), 0) % TQ
    kpos = lax.broadcasted_iota(jnp.int32, (rows, TQ), 1)
    mask_prev = (kpos >= qpos) & (n > 0)
    mask_next = (kpos <= qpos) & (n < nb - 1)
    ks, vs = slice(0, 128), slice(256, 384)
    qs = _stack_pairs(q_ref[...], 4)
    s_prev = jnp.where(mask_prev, _qk(qs, prev_ref[:, ks]), NEG)
    s_cur = _qk(qs, cur_ref[:, ks])
    s_next = jnp.where(mask_next, _qk(qs, nxt_ref[:, ks]), NEG)
    s_ctx = _qk(qs, ck_ref[...])
    o = _softmax_pv([s_prev, s_cur, s_next, s_ctx],
                    [prev_ref[:, vs], cur_ref[:, vs], nxt_ref[:, vs], cv_ref[...]],
                    _sink_column(sink_ref, l, TQ))
    o_ref[...] = _unstack_pairs(o, 4).astype(BF16)


def _win_attn(sink_a, q_all, kvab, cache_a, l):
    nb = DEC_SEQ // TQ
    base = N_CTX // TQ
    kv_spec = lambda f: pl.BlockSpec((TQ, 512), lambda b, n: (base + b * nb + f(n), 0))
    cache_spec = lambda s: pl.BlockSpec((None, None, None, PAST_LEN, W_KV), lambda b, n: (b, l, s, 0, 0))
    return pl.pallas_call(
        functools.partial(_win_attn_kernel, l),
        grid=(DEC_BATCH, nb),
        in_specs=[pl.BlockSpec(memory_space=pltpu.SMEM),
                  pl.BlockSpec((TQ, 512), lambda b, n: (base + b * nb + n, 0)),
                  kv_spec(lambda n: jnp.maximum(n - 1, 0)), kv_spec(lambda n: n),
                  kv_spec(lambda n: jnp.minimum(n + 1, nb - 1)),
                  cache_spec(0), cache_spec(1)],
        out_specs=pl.BlockSpec((TQ, 512), lambda b, n: (b * nb + n, 0)),
        out_shape=jax.ShapeDtypeStruct((N_LAT, 512), BF16),
        compiler_params=_cparams(("arbitrary", "arbitrary")),
        name="win_attn",
    )(sink_a, q_all, kvab, kvab, kvab, cache_a, cache_a)


def _dense_attn_kernel(q_ref, kv_ref, ck_ref, cv_ref, o_ref):
    qs = _stack_pairs(q_ref[...], 4)
    half = qs.shape[0] // 2
    outs = []
    for g in range(KV_B):
        qg = qs[g * half:(g + 1) * half]
        outs.append(_softmax_pv([_qk(qg, kv_ref[:, 128:256]), _qk(qg, ck_ref[...])],
                                [kv_ref[:, 384:512], cv_ref[...]]))
    o_ref[...] = _unstack_pairs(jnp.concatenate(outs, axis=0), 4).astype(BF16)


def _dense_attn(q_all, kvab, cache_b, l):
    nb = DEC_SEQ // TQ
    base = N_CTX // TQ
    cache_spec = lambda s: pl.BlockSpec((None, None, None, PAST_LEN, W_KV), lambda b, n: (b, l, s, 0, 0))
    return pl.pallas_call(
        _dense_attn_kernel,
        grid=(DEC_BATCH, nb),
        in_specs=[pl.BlockSpec((TQ, 512), lambda b, n: (base + b * nb + n, 1)),
                  pl.BlockSpec((DEC_SEQ, 512), lambda b, n: (N_CTX // DEC_SEQ + b, 0)),
                  cache_spec(0), cache_spec(1)],
        out_specs=pl.BlockSpec((TQ, 512), lambda b, n: (b * nb + n, 0)),
        out_shape=jax.ShapeDtypeStruct((N_LAT, 512), BF16),
        compiler_params=_cparams(("arbitrary", "arbitrary")),
        name="dense_attn",
    )(q_all, kvab, cache_b, cache_b)


NBR_BAND = 4
NBR_Q = NBR_BAND * GRID_W
NBR_WIN_ROWS = 12
NBR_N_BANDS = GRID_ROWS // NBR_BAND
NBR_KBLK = NBR_Q
NBR_WIN_BLOCKS = NBR_WIN_ROWS * GRID_W // NBR_KBLK
NBR_LAST_KB = (GRID_ROWS - NBR_WIN_ROWS) * GRID_W // NBR_KBLK


def _nbr_window_block(band):
    return jnp.clip(band - 1, 0, NBR_LAST_KB)


def _nbr_attn_kernel(q_ref, k0_ref, k1_ref, k2_ref, ck_ref, cv_ref, bias_ref, o_ref):
    k_refs = (k0_ref, k1_ref, k2_ref)
    for hp in range(H_C // 2):
        cs = slice(hp * 128, (hp + 1) * 128)
        vs = slice(512 + hp * 128, 512 + (hp + 1) * 128)
        qs = _stack_pairs(q_ref[:, cs], 1)
        bias = jnp.concatenate([bias_ref[2 * hp], bias_ref[2 * hp + 1]], axis=0)
        scores = [_qk(qs, kr[:, cs]) + bias[:, j * NBR_KBLK:(j + 1) * NBR_KBLK] for j, kr in enumerate(k_refs)]
        scores.append(_qk(qs, ck_ref[:, cs]))
        o = _softmax_pv(scores, [kr[:, vs] for kr in k_refs] + [cv_ref[:, cs]])
        o_ref[:, cs] = _unstack_pairs(o, 1).astype(BF16)


def _nbr_attn(q_all, kvc, cache_c, bias_t, l):
    q_base = N_CTX // NBR_Q
    k_base = N_CTX // NBR_KBLK
    blocks_per_batch = DEC_SEQ // NBR_KBLK
    cache_spec = lambda s: pl.BlockSpec((None, None, None, PAST_LEN, W_HEADS), lambda band, b: (b, l, s, 0, 0))
    key_spec = lambda j: pl.BlockSpec(
        (NBR_KBLK, 1024), lambda band, b: (k_base + b * blocks_per_batch + _nbr_window_block(band) + j, 0))
    band_type = lambda band: jnp.where(band == 0, 0, jnp.where(band == NBR_N_BANDS - 1, 2, 1))
    return pl.pallas_call(
        _nbr_attn_kernel,
        grid=(NBR_N_BANDS, DEC_BATCH),
        in_specs=[pl.BlockSpec((NBR_Q, 512), lambda band, b: (q_base + b * NBR_N_BANDS + band, 2)),
                  key_spec(0), key_spec(1), key_spec(2), cache_spec(0), cache_spec(1),
                  pl.BlockSpec((None, H_C, NBR_Q, NBR_WIN_ROWS * GRID_W),
                               lambda band, b: (band_type(band), 0, 0, 0))],
        out_specs=pl.BlockSpec((NBR_Q, 512), lambda band, b: (b * NBR_N_BANDS + band, 0)),
        out_shape=jax.ShapeDtypeStruct((N_LAT, 512), BF16),
        compiler_params=_cparams(("arbitrary", "arbitrary")),
        name="nbr_attn",
    )(q_all, kvc, kvc, kvc, cache_c, cache_c, bias_t)


def _nbr_bias_table(rpb_l):
    c = jnp.arange(GRID_W)[:, None]
    kc = jnp.arange(GRID_W)[None, :]
    c_start = jnp.clip(c - NA_COLS // 2, 0, GRID_W - NA_COLS)
    valid = (kc >= c_start) & (kc < c_start + NA_COLS)
    dc = jnp.clip(kc - c + NA_COLS - 1, 0, 2 * NA_COLS - 2)
    t = jnp.where(valid[None, None], rpb_l[:, :, dc].astype(F32), NEG)
    neg = jnp.full((H_C, GRID_W, GRID_W), NEG, F32)
    tables = []
    for r0 in (0, NBR_BAND, GRID_ROWS - NBR_BAND):
        k0 = min(max(r0 - NA_ROWS // 2, 0), GRID_ROWS - NBR_WIN_ROWS)
        rows = []
        for dq in range(NBR_BAND):
            r = r0 + dq
            start = min(max(r - NA_ROWS // 2, 0), GRID_ROWS - NA_ROWS)
            cols = []
            for i in range(NBR_WIN_ROWS):
                kr = k0 + i
                cols.append(t[:, kr - r + NA_ROWS - 1] if start <= kr < start + NA_ROWS else neg)
            rows.append(jnp.concatenate(cols, axis=2))
        tables.append(jnp.concatenate(rows, axis=1))
    return jnp.stack(tables, axis=0)


def _merge_kernel(octx_ref, oa_ref, ob_ref, oc_ref, gates_ref, h_ref, g1_ref, sh2_ref, sc2_ref,
                  wbr_ref, wout_ref, gpost_ref, gpre_ref, wr_ref, br_ref,
                  h2_ref, hn2_ref, topk_ref, count_ref):
    i = pl.program_id(0)
    is_ctx = i < CTX_TILES
    merged = None
    for j, lat_ref in enumerate((oa_ref, ob_ref, oc_ref)):
        o = jnp.where(is_ctx, octx_ref[:, j * 512:(j + 1) * 512], lat_ref[...])
        br = jnp.dot(o, wbr_ref[j], preferred_element_type=F32)
        term = gates_ref[:, j * D_MODEL:(j + 1) * D_MODEL].astype(F32) * br
        merged = term if merged is None else merged + term
    t = jnp.dot(merged.astype(BF16), wout_ref[...], preferred_element_type=F32)
    h2 = h_ref[...] + g1_ref[...] * (_rms(t) * gpost_ref[...])
    h2_ref[...] = h2
    hn2 = _rms(h2) * gpre_ref[...] * (1.0 + sc2_ref[...]) + sh2_ref[...]
    hn2_ref[...] = _pack_halves(hn2)
    x_hi = hn2.astype(BF16)
    x_lo = (hn2 - x_hi.astype(F32)).astype(BF16)
    hi = jnp.dot(x_hi, wr_ref[...], preferred_element_type=F32)
    lo = jnp.dot(x_lo, wr_ref[:, :N_EXPERTS], preferred_element_type=F32)
    logits = hi[:, :N_EXPERTS] + (hi[:, N_EXPERTS:] + lo) + br_ref[...]
    lane = lax.broadcasted_iota(jnp.int32, (TM, N_EXPERTS), 1).astype(F32)
    idxs, vals = [], []
    chosen = jnp.zeros((TM, N_EXPERTS), F32)
    for _ in range(TOP_K):
        m = jnp.max(logits, axis=1, keepdims=True)
        idx = jnp.min(jnp.where(logits == m, lane, float(N_EXPERTS)), axis=1, keepdims=True)
        hot = lane == idx
        idxs.append(idx)
        vals.append(m)
        chosen = chosen + hot.astype(F32)
        logits = jnp.where(hot, -jnp.inf, logits)
    topk_ref[...] = jnp.concatenate(idxs + vals, axis=1)

    @pl.when(i == 0)
    def _():
        count_ref[...] = jnp.zeros_like(count_ref)

    count_ref[...] = count_ref[...] + jnp.sum(chosen, axis=0, keepdims=True)


def _merge(o_ctx, o_a, o_b, o_c, gates, h, mods3, w_br, w_out_b, g_post3, g_pre_ffn3, w_router, b_router3, l):
    lat = lambda: pl.BlockSpec((TM, 512), lambda i: (jnp.maximum(i - CTX_TILES, 0), 0))
    mod = lambda j: pl.BlockSpec((None, 1, D_MODEL), lambda i: (_mod_index(i), 0, j))
    lw = lambda: pl.BlockSpec((None, 1, D_MODEL), lambda i: (l, 0, 0))
    row = lambda n: pl.BlockSpec((TM, n), lambda i: (i, 0))
    const = lambda shape: pl.BlockSpec(shape, lambda i: (0,) * len(shape), pipeline_mode=pl.Buffered(1))
    return pl.pallas_call(
        _merge_kernel,
        grid=(N_TILES,),
        in_specs=[pl.BlockSpec((TM, 1536), lambda i: (jnp.minimum(i, CTX_TILES - 1), 0)),
                  lat(), lat(), lat(), row(3072), row(D_MODEL), mod(2), mod(3), mod(4),
                  const((3, 512, D_MODEL)), const((D_MODEL, D_MODEL)), lw(), lw(),
                  const((D_MODEL, 2 * N_EXPERTS)),
                  pl.BlockSpec((None, 1, N_EXPERTS), lambda i: (l, 0, 0))],
        out_specs=[row(D_MODEL), row(HALF), row(2 * TOP_K),
                   pl.BlockSpec((8, N_EXPERTS), lambda i: (0, 0))],
        out_shape=[jax.ShapeDtypeStruct((N_TOK, D_MODEL), F32),
                   jax.ShapeDtypeStruct((N_TOK, HALF), jnp.int32),
                   jax.ShapeDtypeStruct((N_TOK, 2 * TOP_K), F32),
                   jax.ShapeDtypeStruct((8, N_EXPERTS), F32)],
        compiler_params=_cparams(("arbitrary",)),
        name="merge",
    )(o_ctx, o_a, o_b, o_c, gates, h, mods3, mods3, mods3, w_br, w_out_b, g_post3, g_pre_ffn3,
      w_router, b_router3)


def _moe_kernel(l, be_ref, nv_ref, first_ref, slot_ref, nxt_ref, rows_ref, x_ref, wgu_hbm, bgu_ref, wd_hbm, bd_ref, y_ref,
                wgu_f32, wd_f32, wgu_bf, wd_bf, sem):
    i = pl.program_id(0)

    def fetch(e, s):
        return (pltpu.make_async_copy(wgu_hbm.at[l, e], wgu_f32.at[s], sem.at[0, s]),
                pltpu.make_async_copy(wd_hbm.at[l, e], wd_f32.at[s], sem.at[1, s]))

    @pl.when(first_ref[i] == 1)
    def _():
        s = slot_ref[i]

        @pl.when(i == 0)
        def _():
            for cp in fetch(be_ref[i], s):
                cp.start()

        for cp in fetch(be_ref[i], s):
            cp.wait()

        @pl.when(nxt_ref[i] >= 0)
        def _():
            for cp in fetch(nxt_ref[i], 1 - s):
                cp.start()

        wgu_bf[...] = wgu_f32[s].astype(BF16)
        wd_bf[...] = wd_f32[s].astype(BF16)

    @pl.when(i < nv_ref[0])
    def _():
        real = lax.broadcasted_iota(jnp.int32, (MOE_TILE, HALF), 0) < rows_ref[i]
        xa, xb = _unpack_halves(jnp.where(real, x_ref[...], 0))
        x = jnp.concatenate([xa.astype(BF16), xb.astype(BF16)], axis=1)
        b = bgu_ref[...]
        glu = jnp.dot(x, wgu_bf[:, :D_FF], preferred_element_type=F32) + b[:, :D_FF]
        lin = jnp.dot(x, wgu_bf[:, D_FF:], preferred_element_type=F32) + b[:, D_FF:]
        glu = jnp.minimum(glu, SWIGLU_LIMIT)
        lin = jnp.clip(lin, -SWIGLU_LIMIT, SWIGLU_LIMIT)
        act = glu * (1.0 / (1.0 + jnp.exp(-SWIGLU_ALPHA * glu))) * (lin + 1.0)
        y = jnp.dot(act.astype(BF16), wd_bf[...], preferred_element_type=F32) + bd_ref[...]
        y_ref[...] = _pack_halves(y)

    @pl.when(i >= nv_ref[0])
    def _():
        y_ref[...] = jnp.zeros_like(y_ref)


def _moe(plan, x_slots, w_gate_up, b_gate_up4, w_down, b_down4, l):
    grid_spec = pltpu.PrefetchScalarGridSpec(
        num_scalar_prefetch=6,
        grid=(N_MOE_BLOCKS,),
        in_specs=[pl.BlockSpec((MOE_TILE, HALF), lambda i, be, *_: (i, 0)),
                  pl.BlockSpec(memory_space=pl.ANY),
                  pl.BlockSpec((None, None, 1, 2 * D_FF), lambda i, be, *_: (l, be[i], 0, 0)),
                  pl.BlockSpec(memory_space=pl.ANY),
                  pl.BlockSpec((None, None, 1, D_MODEL), lambda i, be, *_: (l, be[i], 0, 0))],
        out_specs=pl.BlockSpec((MOE_TILE, HALF), lambda i, be, *_: (i, 0)),
        scratch_shapes=[pltpu.VMEM((2, D_MODEL, 2 * D_FF), F32), pltpu.VMEM((2, D_FF, D_MODEL), F32),
                        pltpu.VMEM((D_MODEL, 2 * D_FF), BF16), pltpu.VMEM((D_FF, D_MODEL), BF16),
                        pltpu.SemaphoreType.DMA((2, 2))])
    return pl.pallas_call(
        functools.partial(_moe_kernel, l),
        grid_spec=grid_spec,
        out_shape=jax.ShapeDtypeStruct((N_SLOTS, HALF), jnp.int32),
        compiler_params=_cparams(("arbitrary",)),
        name="moe",
    )(*plan, x_slots, w_gate_up, b_gate_up4, w_down, b_down4)


def _combine_kernel(y0_ref, y1_ref, y2_ref, y3_ref, gate_ref, h_ref, g2_ref, gpost_ref, o_ref):
    gate = gate_ref[...]
    ffn = None
    for k, y_ref in enumerate((y0_ref, y1_ref, y2_ref, y3_ref)):
        ya, yb = _unpack_halves(y_ref[...])
        term = gate[:, k:k + 1] * jnp.concatenate([ya, yb], axis=1)
        ffn = term if ffn is None else ffn + term
    o_ref[...] = h_ref[...] + g2_ref[...] * (_rms(ffn) * gpost_ref[...])


def _combine(y_tok, gate, h2, mods3, g_post_ffn3, l):
    row = lambda n: pl.BlockSpec((TM, n), lambda i: (i, 0))
    choice = lambda k: pl.BlockSpec((TM, HALF), lambda i: (k * N_TILES + i, 0))
    return pl.pallas_call(
        _combine_kernel,
        grid=(N_TILES,),
        in_specs=[choice(0), choice(1), choice(2), choice(3), row(TOP_K), row(D_MODEL),
                  pl.BlockSpec((None, 1, D_MODEL), lambda i: (_mod_index(i), 0, 5)),
                  pl.BlockSpec((None, 1, D_MODEL), lambda i: (l, 0, 0))],
        out_specs=row(D_MODEL),
        out_shape=jax.ShapeDtypeStruct((N_TOK, D_MODEL), F32),
        compiler_params=_cparams(("arbitrary",)),
        name="combine",
    )(y_tok, y_tok, y_tok, y_tok, gate, h2, mods3, g_post_ffn3)


def _route_kernel(topk_ref, count_ref, dest_ref, gate_ref, base_ref):
    i = pl.program_id(0)

    @pl.when(i == 0)
    def _():
        total = count_ref[...]
        padded = jnp.floor((total + (MOE_TILE - 1.0)) * (1.0 / MOE_TILE)) * MOE_TILE
        before = (lax.broadcasted_iota(jnp.int32, (N_EXPERTS, N_EXPERTS), 0)
                  < lax.broadcasted_iota(jnp.int32, (N_EXPERTS, N_EXPERTS), 1)).astype(F32)
        base_ref[...] = jnp.dot(padded, before, preferred_element_type=F32, precision=lax.Precision.HIGHEST)

    topk = topk_ref[...]
    lane = lax.broadcasted_iota(jnp.int32, (TM, N_EXPERTS), 1).astype(F32)
    hots = [lane == topk[:, k:k + 1] for k in range(TOP_K)]
    vals = [topk[:, TOP_K + k:TOP_K + k + 1] for k in range(TOP_K)]
    chosen = functools.reduce(jnp.add, [h.astype(F32) for h in hots])
    earlier_row = (lax.broadcasted_iota(jnp.int32, (TM, TM), 1)
                   < lax.broadcasted_iota(jnp.int32, (TM, TM), 0)).astype(BF16)
    earlier = jnp.dot(earlier_row, chosen.astype(BF16), preferred_element_type=F32)
    offs = base_ref[0:1, :] + earlier
    dest = [jnp.sum(jnp.where(h, offs, 0.0), axis=1, keepdims=True) for h in hots]
    dest_ref[...] = jnp.concatenate(dest, axis=1).astype(jnp.int32)
    e = [jnp.exp(v - vals[0]) for v in vals]
    den = functools.reduce(jnp.add, e)
    gate_ref[...] = jnp.concatenate(e, axis=1) / den
    base_ref[...] = base_ref[...] + jnp.sum(chosen, axis=0, keepdims=True)


def _route(topk, counts):
    tile = lambda n: pl.BlockSpec((TM, n), lambda i: (i, 0))
    dest, gate = pl.pallas_call(
        _route_kernel,
        grid=(N_TILES,),
        in_specs=[tile(2 * TOP_K), pl.BlockSpec((8, N_EXPERTS), lambda i: (0, 0))],
        out_specs=[tile(TOP_K), tile(TOP_K)],
        out_shape=[jax.ShapeDtypeStruct((N_TOK, TOP_K), jnp.int32),
                   jax.ShapeDtypeStruct((N_TOK, TOP_K), F32)],
        scratch_shapes=[pltpu.VMEM((8, N_EXPERTS), F32)],
        compiler_params=_cparams(("arbitrary",)),
        name="route",
    )(topk, counts)
    counts = counts[0].astype(jnp.int32)
    padded = (counts + MOE_TILE - 1) // MOE_TILE * MOE_TILE
    pad_end = jnp.cumsum(padded)
    block = jnp.arange(N_MOE_BLOCKS, dtype=jnp.int32)
    block_e = jnp.minimum(jnp.sum((pad_end[None, :] <= block[:, None] * MOE_TILE).astype(jnp.int32), axis=1),
                          N_EXPERTS - 1)
    n_valid = pad_end[-1] // MOE_TILE
    first = (block < n_valid) & ((block == 0) | (block_e != jnp.roll(block_e, 1)))
    slot = (jnp.cumsum(first.astype(jnp.int32)) - 1) % 2
    next_block = pad_end[block_e] // MOE_TILE
    nxt = jnp.where(next_block < n_valid, block_e[jnp.minimum(next_block, N_MOE_BLOCKS - 1)], -1)
    rows = jnp.clip(counts[block_e] - (block * MOE_TILE - (pad_end - padded)[block_e]), 0, MOE_TILE)
    rows = jnp.where(block < n_valid, rows, 0)
    plan = tuple(a.astype(jnp.int32) for a in (block_e, n_valid[None], first, slot, nxt, rows))
    n_workers = N_TOK // SCATTER_TOKENS_PER_WORKER
    dest_sc = dest.reshape(n_workers, SCATTER_TOKENS_PER_WORKER // GATHER_CHUNK, GATHER_CHUNK, TOP_K)
    dest_sc = dest_sc.transpose(0, 1, 3, 2).reshape(n_workers, -1, GATHER_CHUNK)
    return gate, dest.T.reshape(-1), dest_sc, plan


def _rope_tables():
    t = jnp.arange(DEC_SEQ)
    inv = ROPE_THETA ** (-jnp.arange(ROPE_PAIRS, dtype=F32) / ROPE_PAIRS)
    row = (t // GRID_W).astype(F32)[:, None] * inv
    col = (t % GRID_W).astype(F32)[:, None] * inv
    zeros = jnp.zeros_like(row)
    cos = jnp.concatenate([jnp.cos(row), jnp.cos(row), jnp.cos(col), jnp.cos(col)], axis=1)
    s1 = jnp.concatenate([-jnp.sin(row), zeros, -jnp.sin(col), zeros], axis=1)
    s2 = jnp.concatenate([zeros, jnp.sin(row), zeros, jnp.sin(col)], axis=1)
    ident = lambda v: jnp.full((TM, HEAD_DIM), v, F32)
    tables = [jnp.concatenate([x, ident(v)], axis=0) for x, v in ((cos, 1.0), (s1, 0.0), (s2, 0.0))]
    return [jnp.tile(x, (1, 2)) for x in tables]


def kernel(x_prompt, x_sample, cache_a, cache_b, cache_c, c, c_ctx, w_ada, b_ada, g_pre_mix, g_post_mix,
           g_pre_ffn, g_post_ffn, w_in, g_q_b, g_k_b, sink_a, rpb_c, w_br_a, w_br_b, w_br_c, w_out,
           w_router, b_router, w_gate_up, b_gate_up, w_down, b_down):
    h = jnp.concatenate([x_prompt.reshape(N_CTX, D_MODEL), x_sample.reshape(N_LAT, D_MODEL)], axis=0)
    cond8 = jnp.concatenate([c_ctx[None], c, jnp.zeros((3, D_MODEL), F32)], axis=0)
    cache_a = cache_a.astype(BF16).reshape(DEC_BATCH, DEPTH, 2, PAST_LEN, W_KV)
    cache_b = cache_b.astype(BF16).reshape(DEC_BATCH, DEPTH, 2, PAST_LEN, W_KV)
    cache_c = cache_c.astype(BF16).reshape(DEC_BATCH, DEPTH, 2, PAST_LEN, W_HEADS)
    cos_t, s1_t, s2_t = _rope_tables()
    bd = jnp.kron(jnp.eye(256 // HEAD_DIM, dtype=F32),
                  jnp.full((HEAD_DIM, HEAD_DIM), 1.0 / HEAD_DIM, F32)).astype(BF16)
    vec3 = lambda a: a.reshape(DEPTH, 1, a.shape[-1])
    scale = HEAD_DIM ** -0.5
    states = []
    for l in range(DEPTH):
        w = w_in[l]
        pair_cols = lambda a: a.reshape(D_MODEL, 2, 4, HEAD_DIM).transpose(0, 2, 1, 3).reshape(D_MODEL, 512)
        pair_rows = lambda a: a.reshape(2, 4, HEAD_DIM, D_MODEL).transpose(1, 0, 2, 3).reshape(512, D_MODEL)
        w_in_p = jnp.concatenate(
            [pair_cols(w[:, 0:512]) * scale, pair_cols(w[:, 768:1280]), w[:, 1536:2048] * scale,
             w[:, 2048:2560], w[:, 2560:3072],
             w[:, 512:640], w[:, 1280:1408], w[:, 640:768], w[:, 1408:1536], w[:, 3072:]], axis=1).astype(BF16)
        gq = (jnp.tile(g_q_b[l], H_B) * scale)[None]
        gk = jnp.tile(g_k_b[l], KV_B)[None]
        w_br = jnp.stack([pair_rows(w_br_a[l]), pair_rows(w_br_b[l]), w_br_c[l]], axis=0).astype(BF16)
        w_out_b = w_out[l].astype(BF16)
        w_r_hi = w_router[l].astype(BF16)
        w_r_lo = (w_router[l] - w_r_hi.astype(F32)).astype(BF16)
        w_router2 = jnp.concatenate([w_r_hi, w_r_lo], axis=1)

        mods3 = _adaln(cond8, w_ada, vec3(b_ada), l).reshape(8, 1, 6 * D_MODEL)
        q_all, kvab, kvc, gates, *states = _proj(h, mods3, vec3(g_pre_mix), w_in_p, gq, gk, bd,
                                                 cos_t, s1_t, s2_t, tuple(states), l)
        o_ctx = _ctx_attn(sink_a, q_all, kvab, kvc, l)
        o_a = _win_attn(sink_a, q_all, kvab, cache_a, l)
        o_b = _dense_attn(q_all, kvab, cache_b, l)
        o_c = _nbr_attn(q_all, kvc, cache_c, _nbr_bias_table(rpb_c[l]), l)
        h2, hn2, topk, counts = _merge(o_ctx, o_a, o_b, o_c, gates, h, mods3, w_br, w_out_b,
                                       vec3(g_post_mix), vec3(g_pre_ffn), w_router2, vec3(b_router), l)
        gate, dest, dest_sc, plan = _route(topk, counts)
        x_slots = _scatter_rows(hn2, dest_sc, N_SLOTS)
        y_slots = _moe(plan, x_slots, w_gate_up,
                       b_gate_up.reshape(DEPTH, N_EXPERTS, 1, 2 * D_FF), w_down,
                       b_down.reshape(DEPTH, N_EXPERTS, 1, D_MODEL), l)
        y_tok = _gather_rows(y_slots, dest)
        h = _combine(y_tok, gate, h2, mods3, vec3(g_post_ffn), l)

    state_a, state_b, state_c = states
    heads = lambda s, n: s.reshape(BATCH, DEPTH, 2, SEQ, n, HEAD_DIM)
    return (h[:N_CTX].reshape(BATCH, SEQ, D_MODEL), h[N_CTX:].reshape(DEC_BATCH, DEC_SEQ, D_MODEL),
            heads(state_a, KV_A), heads(state_b, KV_B), heads(state_c, H_C))
```

```python
import functools

import jax
import jax.numpy as jnp
from jax import lax
from jax.experimental import pallas as pl
from jax.experimental.pallas import tpu as pltpu
from jax.experimental.pallas import tpu_sc as plsc

D_MODEL = 1024
BATCH = 32
SEQ = 256
DEPTH = 2
DEC_BATCH = 4
DEC_SEQ = 2048
PAST_LEN = 512
GRID_W = 64
HEAD_DIM = 64
H_A = 8
KV_A = 2
H_B = 8
KV_B = 2
H_C = 8
WINDOW_A = 128
NA_ROWS = 8
NA_COLS = 16
ROPE_THETA = 10000.0
ROPE_PAIRS = HEAD_DIM // 4
N_EXPERTS = 32
TOP_K = 4
D_FF = D_MODEL
SWIGLU_ALPHA = 1.702
SWIGLU_LIMIT = 7.0
EPS = 1e-6

W_HEADS = H_A * HEAD_DIM
W_KV = KV_A * HEAD_DIM
N_CTX = BATCH * SEQ
N_LAT = DEC_BATCH * DEC_SEQ
N_TOK = N_CTX + N_LAT
GRID_ROWS = DEC_SEQ // GRID_W
D_IN = 3 * W_HEADS + 4 * W_KV + 2 * W_HEADS + 3 * D_MODEL

TM = 256
N_TILES = N_TOK // TM
CTX_TILES = N_CTX // TM
LAT_TILES_PER_BATCH = DEC_SEQ // TM
TQ = 128
MOE_TILE = 256
N_SLOTS = N_TOK * TOP_K + N_EXPERTS * MOE_TILE
N_MOE_BLOCKS = N_SLOTS // MOE_TILE
NEG = -1e30
VMEM_LIMIT = 56 * 1024 * 1024

BF16 = jnp.bfloat16
F32 = jnp.float32


def _cparams(sem):
    return pltpu.CompilerParams(dimension_semantics=sem, vmem_limit_bytes=VMEM_LIMIT)


def _mod_index(i):
    return jnp.where(i < CTX_TILES, 0, 1 + (i - CTX_TILES) // LAT_TILES_PER_BATCH)


def _rms(x):
    return x * lax.rsqrt(jnp.mean(x * x, axis=-1, keepdims=True) + EPS)


HALF = D_MODEL // 2


def _pack_halves(x):
    hi = lax.bitcast_convert_type(x[:, :HALF].astype(BF16).astype(F32), jnp.uint32)
    lo = lax.bitcast_convert_type(x[:, HALF:].astype(BF16).astype(F32), jnp.uint32)
    return lax.bitcast_convert_type(hi | (lo >> 16), jnp.int32)


def _unpack_halves(w):
    u = lax.bitcast_convert_type(w, jnp.uint32)
    return (lax.bitcast_convert_type(u & jnp.uint32(0xFFFF0000), F32),
            lax.bitcast_convert_type(u << 16, F32))


GATHER_CHUNK = 64


def _gather_rows(table, idx):
    n = idx.shape[0]
    width = table.shape[1]
    info = plsc.get_sparse_core_info()
    n_workers = info.num_cores * info.num_subcores
    per_worker = n // n_workers
    n_chunks = per_worker // GATHER_CHUNK
    assert per_worker * n_workers == n and n_chunks * GATHER_CHUNK == per_worker and n_chunks % 2 == 0
    mesh = plsc.VectorSubcoreMesh(core_axis_name="core", subcore_axis_name="subcore")

    @functools.partial(
        pl.kernel, out_type=jax.ShapeDtypeStruct((n, width), table.dtype), mesh=mesh,
        scratch_types=[pltpu.VMEM((per_worker,), jnp.int32),
                       pltpu.VMEM((2, GATHER_CHUNK, width), table.dtype),
                       pltpu.SemaphoreType.DMA((2,)), pltpu.SemaphoreType.DMA((2,))])
    def gather(table_hbm, idx_hbm, out_hbm, idx_v, rows_v, gather_sem, write_sem):
        worker = lax.axis_index("subcore") * info.num_cores + lax.axis_index("core")
        base = worker * per_worker
        pltpu.sync_copy(idx_hbm.at[pl.ds(base, per_worker)], idx_v)

        def fetch(chunk, slot):
            rows = idx_v.at[pl.ds(chunk * GATHER_CHUNK, GATHER_CHUNK)]
            return pltpu.make_async_copy(table_hbm.at[rows], rows_v.at[slot], gather_sem.at[slot])

        def write(chunk, slot):
            dst = out_hbm.at[pl.ds(base + chunk * GATHER_CHUNK, GATHER_CHUNK)]
            return pltpu.make_async_copy(rows_v.at[slot], dst, write_sem.at[slot])

        fetch(0, 0).start()

        @pl.loop(0, n_chunks, step=2)
        def _(c):
            @pl.when(c > 0)
            def _():
                write(c - 1, 1).wait()

            fetch(c + 1, 1).start()
            fetch(c, 0).wait()
            write(c, 0).start()
            write(c, 0).wait()

            @pl.when(c + 2 < n_chunks)
            def _():
                fetch(c + 2, 0).start()

            fetch(c + 1, 1).wait()
            write(c + 1, 1).start()

        write(n_chunks - 1, 1).wait()

    return gather(table, idx)


SC_WORKERS_V7X = 32
SCATTER_TOKENS_PER_WORKER = N_TOK // SC_WORKERS_V7X


def _scatter_rows(table, dest_sc, n_out):
    width = table.shape[1]
    info = plsc.get_sparse_core_info()
    assert info.num_cores * info.num_subcores == SC_WORKERS_V7X
    per_worker = SCATTER_TOKENS_PER_WORKER
    n_chunks = per_worker // GATHER_CHUNK
    assert n_chunks % 2 == 0 and dest_sc.shape == (SC_WORKERS_V7X, n_chunks * TOP_K, GATHER_CHUNK)
    mesh = plsc.VectorSubcoreMesh(core_axis_name="core", subcore_axis_name="subcore")

    @functools.partial(
        pl.kernel, out_type=jax.ShapeDtypeStruct((n_out, width), table.dtype), mesh=mesh,
        scratch_types=[pltpu.VMEM((n_chunks * TOP_K, GATHER_CHUNK), jnp.int32),
                       pltpu.VMEM((2, GATHER_CHUNK, width), table.dtype),
                       pltpu.SemaphoreType.DMA((2,)), pltpu.SemaphoreType.DMA((2,))])
    def scatter(table_hbm, dest_hbm, out_hbm, idx_v, rows_v, read_sem, write_sem):
        worker = lax.axis_index("subcore") * info.num_cores + lax.axis_index("core")
        base = worker * per_worker
        pltpu.sync_copy(dest_hbm.at[worker], idx_v)

        def read(chunk, slot):
            src = table_hbm.at[pl.ds(base + chunk * GATHER_CHUNK, GATHER_CHUNK)]
            return pltpu.make_async_copy(src, rows_v.at[slot], read_sem.at[slot])

        def writes(chunk, slot):
            return [pltpu.make_async_copy(rows_v.at[slot], out_hbm.at[idx_v.at[chunk * TOP_K + k]],
                                          write_sem.at[slot]) for k in range(TOP_K)]

        read(0, 0).start()

        @pl.loop(0, n_chunks, step=2)
        def _(c):
            @pl.when(c > 0)
            def _():
                for cp in writes(c - 1, 1):
                    cp.wait()

            read(c + 1, 1).start()
            read(c, 0).wait()
            for cp in writes(c, 0):
                cp.start()
            for cp in writes(c, 0):
                cp.wait()

            @pl.when(c + 2 < n_chunks)
            def _():
                read(c + 2, 0).start()

            read(c + 1, 1).wait()
            for cp in writes(c + 1, 1):
                cp.start()

        for cp in writes(n_chunks - 1, 1):
            cp.wait()

    return scatter(table, dest_sc)


def _adaln_kernel(c_ref, w_ref, b_ref, o_ref):
    c = c_ref[...]
    s = c / (1.0 + jnp.exp(-c))
    o_ref[...] = jnp.dot(s, w_ref[...], preferred_element_type=F32,
                         precision=lax.Precision.HIGHEST) + b_ref[...]


def _adaln(cond8, w_ada, b_ada3, l):
    tn = 1536
    return pl.pallas_call(
        _adaln_kernel,
        grid=(6 * D_MODEL // tn,),
        in_specs=[pl.BlockSpec((8, D_MODEL), lambda j: (0, 0)),
                  pl.BlockSpec((None, D_MODEL, tn), lambda j: (l, 0, j)),
                  pl.BlockSpec((None, 1, tn), lambda j: (l, 0, j))],
        out_specs=pl.BlockSpec((8, tn), lambda j: (0, j)),
        out_shape=jax.ShapeDtypeStruct((8, 6 * D_MODEL), F32),
        compiler_params=_cparams(("arbitrary",)),
        name="adaln",
    )(cond8, w_ada, b_ada3)


C_QA, C_QB, C_QC, C_KC, C_VC, C_KAB, C_GL = 0, 512, 1024, 1536, 2048, 2560, 3072


def _ctx_spec(n):
    return pl.BlockSpec((TM, n), lambda i: (jnp.minimum(i, CTX_TILES - 1), 0))


def _lat_spec(n):
    return pl.BlockSpec((TM, n), lambda i: (jnp.maximum(i - CTX_TILES, 0), 0))


def _proj_kernel(hc_ref, hl_ref, sh_ref, sc_ref, gpre_ref, w_ref, gq_ref, gk_ref, bd_ref,
                 cos_ref, s1_ref, s2_ref, *rest):
    q_ref, kvab_ref, kvc_ref, gates_ref, sta_ref, stb_ref, stc_ref = rest[-7:]
    i = pl.program_id(0)
    hn = _rms(jnp.where(i < CTX_TILES, hc_ref[...], hl_ref[...])) * gpre_ref[...]
    hb = (hn * (1.0 + sc_ref[...]) + sh_ref[...]).astype(BF16)
    cos, s1, s2 = cos_ref[...], s1_ref[...], s2_ref[...]
    bd = bd_ref[...]

    def proj(c0, n):
        return jnp.dot(hb, w_ref[:, c0:c0 + n], preferred_element_type=F32)

    def rope(t):
        parts = []
        for g in range(t.shape[1] // 128):
            tg = t[:, g * 128:(g + 1) * 128]
            parts.append(tg * cos + pltpu.roll(tg, 112, 1) * s1 + pltpu.roll(tg, 16, 1) * s2)
        return parts[0] if len(parts) == 1 else jnp.concatenate(parts, axis=1)

    def headnorm(t, g):
        sq = (t * t).astype(BF16)
        n = t.shape[1]
        if n == 128:
            ms = jnp.dot(sq, bd[:128, :128], preferred_element_type=F32)
        else:
            ms = jnp.concatenate(
                [jnp.dot(sq[:, c:c + 256], bd, preferred_element_type=F32) for c in range(0, n, 256)],
                axis=1)
        return t * lax.rsqrt(ms + EPS) * g

    q_ref[:, 0:512] = rope(proj(C_QA, 512)).astype(BF16)
    q_ref[:, 512:1024] = rope(headnorm(proj(C_QB, 512), gq_ref[...])).astype(BF16)
    q_ref[:, 1024:1536] = proj(C_QC, 512).astype(BF16)
    kc = proj(C_KC, 512)
    vc = proj(C_VC, 512)
    kvc_ref[:, 0:512] = kc.astype(BF16)
    kvc_ref[:, 512:1024] = vc.astype(BF16)
    kab = proj(C_KAB, 512)
    ka = kab[:, 0:128]
    kb = headnorm(kab[:, 128:256], gk_ref[...])
    kvab_ref[:, 0:128] = rope(ka).astype(BF16)
    kvab_ref[:, 128:256] = rope(kb).astype(BF16)
    kvab_ref[:, 256:512] = kab[:, 256:512].astype(BF16)
    for j in range(6):
        gl = proj(C_GL + j * 512, 512)
        gates_ref[:, j * 512:(j + 1) * 512] = (1.0 / (1.0 + jnp.exp(-gl))).astype(BF16)

    @pl.when(i < CTX_TILES)
    def _():
        for st_ref, k, v in ((sta_ref, ka, kab[:, 256:384]), (stb_ref, kb, kab[:, 384:512]), (stc_ref, kc, vc)):
            if len(st_ref.shape) == 4:
                st_ref[0, 0] = k
                st_ref[0, 1] = v
                st_ref[1:] = jnp.zeros((DEPTH - 1,) + tuple(st_ref.shape[1:]), F32)
            else:
                st_ref[0] = k
                st_ref[1] = v


def _proj(h, mods3, g_pre3, w_in_p, gq, gk, bd, cos_t, s1_t, s2_t, prev_states, l):
    def rope_idx(i):
        return jnp.where(i < CTX_TILES, LAT_TILES_PER_BATCH, (i - CTX_TILES) % LAT_TILES_PER_BATCH)

    const = lambda shape: pl.BlockSpec(shape, lambda i: (0,) * len(shape), pipeline_mode=pl.Buffered(1))
    rope_spec = pl.BlockSpec((TM, 128), lambda i: (rope_idx(i), 0))
    row = lambda n: pl.BlockSpec((TM, n), lambda i: (i, 0))
    if l == 0:
        state_spec = lambda n: pl.BlockSpec((None, DEPTH, 2, SEQ, n),
                                            lambda i: (jnp.minimum(i, CTX_TILES - 1), 0, 0, 0, 0))
    else:
        state_spec = lambda n: pl.BlockSpec((None, None, 2, SEQ, n),
                                            lambda i: (jnp.minimum(i, CTX_TILES - 1), l, 0, 0, 0))
    state_shape = lambda n: jax.ShapeDtypeStruct((BATCH, DEPTH, 2, SEQ, n), F32)
    n_in = 12
    return pl.pallas_call(
        _proj_kernel,
        grid=(N_TILES,),
        in_specs=[_ctx_spec(D_MODEL), _lat_spec(D_MODEL),
                  pl.BlockSpec((None, 1, D_MODEL), lambda i: (_mod_index(i), 0, 0)),
                  pl.BlockSpec((None, 1, D_MODEL), lambda i: (_mod_index(i), 0, 1)),
                  pl.BlockSpec((None, 1, D_MODEL), lambda i: (l, 0, 0)),
                  const((D_MODEL, D_IN)), const((1, 512)), const((1, 128)), const((256, 256)),
                  rope_spec, rope_spec, rope_spec] + [pl.BlockSpec(memory_space=pl.ANY)] * len(prev_states),
        out_specs=[row(1536), row(512), row(1024), row(3072),
                   state_spec(W_KV), state_spec(W_KV), state_spec(W_HEADS)],
        out_shape=[jax.ShapeDtypeStruct((N_TOK, 1536), BF16),
                   jax.ShapeDtypeStruct((N_TOK, 512), BF16),
                   jax.ShapeDtypeStruct((N_TOK, 1024), BF16),
                   jax.ShapeDtypeStruct((N_TOK, 3072), BF16),
                   state_shape(W_KV), state_shape(W_KV), state_shape(W_HEADS)],
        input_output_aliases={n_in + j: 4 + j for j in range(len(prev_states))},
        compiler_params=_cparams(("arbitrary",)),
        name="proj",
    )(*h, mods3, mods3, g_pre3, w_in_p, gq, gk, bd, cos_t, s1_t, s2_t, *prev_states)


def _qk(q, k):
    return lax.dot_general(q, k, (((1,), (1,)), ((), ())), preferred_element_type=F32)


def _softmax_pv(scores, values, sink=None, halves=None):
    m = functools.reduce(jnp.maximum, [jnp.max(s, axis=-1, keepdims=True) for s in scores])
    if sink is not None:
        m = jnp.maximum(m, sink)
    if halves is None:
        ps = [jnp.exp(s - m) for s in scores]
        den = functools.reduce(jnp.add, [jnp.sum(p, axis=-1, keepdims=True) for p in ps])
        if sink is not None:
            den = den + jnp.exp(sink - m)
        o = functools.reduce(jnp.add, [jnp.dot(p.astype(BF16), v, preferred_element_type=F32)
                                       for p, v in zip(ps, values)])
        return o / den
    ps = [jnp.exp(s - m).astype(BF16) for s in scores]
    rows = ps[0].shape[0]
    split = {"both": rows // 2, "lo": rows, "hi": 0}[halves]
    one = jnp.ones((), BF16)

    def pv(r0, r1, keep_lo):
        acc = None
        for p, v in zip(ps, values):
            lo = _lo_lanes(v.shape[0])
            v1 = jnp.where(lo, v, one) if keep_lo else jnp.where(lo, one, v)
            t = jnp.dot(p[r0:r1], v1, preferred_element_type=F32)
            acc = t if acc is None else acc + t
        return acc

    parts = ([pv(0, split, True)] if split > 0 else []) + ([pv(split, rows, False)] if split < rows else [])
    o = parts[0] if len(parts) == 1 else jnp.concatenate(parts, axis=0)
    if sink is not None:
        is_lo_row = lax.broadcasted_iota(jnp.int32, (rows, 128), 0) < split
        o = o + jnp.where(is_lo_row != _lo_lanes(rows), jnp.exp(sink - m), 0.0)
    return o * pltpu.roll(1.0 / o, HEAD_DIM, 1)


def _lo_lanes(rows):
    return lax.broadcasted_iota(jnp.int32, (rows, 128), 1) < HEAD_DIM


def _stack_pairs(q, n_pairs):
    lo = _lo_lanes(q.shape[0])
    zero = jnp.zeros((q.shape[0], 128), q.dtype)
    pairs = [q[:, p * 128:(p + 1) * 128] for p in range(n_pairs)]
    return jnp.concatenate([jnp.where(lo, x, zero) for x in pairs] + [jnp.where(lo, zero, x) for x in pairs],
                           axis=0)


def _unstack_pairs(o, n_pairs):
    rows = o.shape[0] // (2 * n_pairs)
    lo = _lo_lanes(rows)
    return jnp.concatenate(
        [jnp.where(lo, o[p * rows:(p + 1) * rows], o[(n_pairs + p) * rows:(n_pairs + p + 1) * rows])
         for p in range(n_pairs)], axis=1)


def _sink_column(sink_ref, l, rows):
    return jnp.concatenate([jnp.full((rows, 1), sink_ref[l, h], F32) for h in range(H_A)], axis=0)


def _ctx_attn_kernel(l, sink_ref, q_ref, kvab_ref, kvc_ref, o_ref):
    qa = _stack_pairs(q_ref[:, 0:512], 4)
    o = _softmax_pv([_qk(qa, kvab_ref[:, 0:128])], [kvab_ref[:, 256:384]], _sink_column(sink_ref, l, SEQ))
    o_ref[:, 0:512] = _unstack_pairs(o, 4).astype(BF16)
    qb = _stack_pairs(q_ref[:, 512:1024], 4)
    o = _softmax_pv([_qk(qb, kvab_ref[:, 128:256])], [kvab_ref[:, 384:512]])
    o_ref[:, 512:1024] = _unstack_pairs(o, 4).astype(BF16)
    for hp in range(H_C // 2):
        cs = slice(hp * 128, (hp + 1) * 128)
        qc = _stack_pairs(q_ref[:, 1024 + hp * 128:1024 + (hp + 1) * 128], 1)
        o = _softmax_pv([_qk(qc, kvc_ref[:, cs])], [kvc_ref[:, 512 + hp * 128:512 + (hp + 1) * 128]])
        o_ref[:, 1024 + hp * 128:1024 + (hp + 1) * 128] = _unstack_pairs(o, 1).astype(BF16)


def _ctx_attn(sink_a, q_all, kvab, kvc, l):
    row = lambda n: pl.BlockSpec((SEQ, n), lambda b: (b, 0))
    return pl.pallas_call(
        functools.partial(_ctx_attn_kernel, l),
        grid=(BATCH,),
        in_specs=[pl.BlockSpec(memory_space=pltpu.SMEM), row(1536), row(512), row(1024)],
        out_specs=row(1536),
        out_shape=jax.ShapeDtypeStruct((N_CTX, 1536), BF16),
        compiler_params=_cparams(("arbitrary",)),
        name="ctx_attn",
    )(sink_a, q_all, kvab, kvc)


def _win_attn_kernel(l, sink_ref, q_ref, prev_ref, cur_ref, nxt_ref, ck_ref, cv_ref, o_ref):
    n = pl.program_id(1)
    nb = DEC_SEQ // TQ
    rows = H_A * TQ
    qpos = lax.broadcasted_iota(jnp.int32, (rows, TQ), 0) % TQ
    kpos = lax.broadcasted_iota(jnp.int32, (rows, TQ), 1)
    mask_prev = (kpos >= qpos) & (n > 0)
    mask_next = (kpos <= qpos) & (n < nb - 1)
    ks, vs = slice(0, 128), slice(256, 384)
    qs = _stack_pairs(q_ref[...], 4)
    s_prev = jnp.where(mask_prev, _qk(qs, prev_ref[:, ks]), NEG)
    s_cur = _qk(qs, cur_ref[:, ks])
    s_next = jnp.where(mask_next, _qk(qs, nxt_ref[:, ks]), NEG)
    s_ctx = _qk(qs, ck_ref[...])
    o = _softmax_pv([s_prev, s_cur, s_next, s_ctx],
                    [prev_ref[:, vs], cur_ref[:, vs], nxt_ref[:, vs], cv_ref[...]],
                    _sink_column(sink_ref, l, TQ), halves="both")
    o_ref[...] = _unstack_pairs(o, 4).astype(BF16)


def _win_attn(sink_a, q_all, kvab, cache_a, l):
    nb = DEC_SEQ // TQ
    base = N_CTX // TQ
    kv_spec = lambda f: pl.BlockSpec((TQ, 512), lambda b, n: (base + b * nb + f(n), 0))
    cache_spec = lambda s: pl.BlockSpec((None, None, None, PAST_LEN, W_KV), lambda b, n: (b, l, s, 0, 0))
    return pl.pallas_call(
        functools.partial(_win_attn_kernel, l),
        grid=(DEC_BATCH, nb),
        in_specs=[pl.BlockSpec(memory_space=pltpu.SMEM),
                  pl.BlockSpec((TQ, 512), lambda b, n: (base + b * nb + n, 0)),
                  kv_spec(lambda n: jnp.maximum(n - 1, 0)), kv_spec(lambda n: n),
                  kv_spec(lambda n: jnp.minimum(n + 1, nb - 1)),
                  cache_spec(0), cache_spec(1)],
        out_specs=pl.BlockSpec((TQ, 512), lambda b, n: (b * nb + n, 0)),
        out_shape=jax.ShapeDtypeStruct((N_LAT, 512), BF16),
        compiler_params=_cparams(("arbitrary", "arbitrary")),
        name="win_attn",
    )(sink_a, q_all, kvab, kvab, kvab, cache_a, cache_a)


def _dense_attn_kernel(q_ref, kv_ref, ck_ref, cv_ref, o_ref):
    qs = _stack_pairs(q_ref[...], 4)
    half = qs.shape[0] // 2
    outs = []
    for g in range(KV_B):
        qg = qs[g * half:(g + 1) * half]
        outs.append(_softmax_pv([_qk(qg, kv_ref[:, 128:256]), _qk(qg, ck_ref[...])],
                                [kv_ref[:, 384:512], cv_ref[...]], halves=("lo", "hi")[g]))
    o_ref[...] = _unstack_pairs(jnp.concatenate(outs, axis=0), 4).astype(BF16)


def _dense_attn(q_all, kvab, cache_b, l):
    nb = DEC_SEQ // TQ
    base = N_CTX // TQ
    cache_spec = lambda s: pl.BlockSpec((None, None, None, PAST_LEN, W_KV), lambda b, n: (b, l, s, 0, 0))
    return pl.pallas_call(
        _dense_attn_kernel,
        grid=(DEC_BATCH, nb),
        in_specs=[pl.BlockSpec((TQ, 512), lambda b, n: (base + b * nb + n, 1)),
                  pl.BlockSpec((DEC_SEQ, 512), lambda b, n: (N_CTX // DEC_SEQ + b, 0)),
                  cache_spec(0), cache_spec(1)],
        out_specs=pl.BlockSpec((TQ, 512), lambda b, n: (b * nb + n, 0)),
        out_shape=jax.ShapeDtypeStruct((N_LAT, 512), BF16),
        compiler_params=_cparams(("arbitrary", "arbitrary")),
        name="dense_attn",
    )(q_all, kvab, cache_b, cache_b)


NBR_BAND = 4
NBR_Q = NBR_BAND * GRID_W
NBR_WIN_ROWS = 12
NBR_N_BANDS = GRID_ROWS // NBR_BAND
NBR_KBLK = NBR_Q
NBR_WIN_BLOCKS = NBR_WIN_ROWS * GRID_W // NBR_KBLK
NBR_LAST_KB = (GRID_ROWS - NBR_WIN_ROWS) * GRID_W // NBR_KBLK


def _nbr_window_block(band):
    return jnp.clip(band - 1, 0, NBR_LAST_KB)


def _nbr_attn_kernel(q_ref, k0_ref, k1_ref, k2_ref, ck_ref, cv_ref, bias_ref, o_ref):
    k_refs = (k0_ref, k1_ref, k2_ref)
    for hp in range(H_C // 2):
        cs = slice(hp * 128, (hp + 1) * 128)
        vs = slice(512 + hp * 128, 512 + (hp + 1) * 128)
        qs = _stack_pairs(q_ref[:, cs], 1)
        bias = jnp.concatenate([bias_ref[2 * hp], bias_ref[2 * hp + 1]], axis=0)
        scores = [_qk(qs, kr[:, cs]) + bias[:, j * NBR_KBLK:(j + 1) * NBR_KBLK] for j, kr in enumerate(k_refs)]
        scores.append(_qk(qs, ck_ref[:, cs]))
        o = _softmax_pv(scores, [kr[:, vs] for kr in k_refs] + [cv_ref[:, cs]])
        o_ref[:, cs] = _unstack_pairs(o, 1).astype(BF16)


def _nbr_attn(q_all, kvc, cache_c, bias_t, l):
    q_base = N_CTX // NBR_Q
    k_base = N_CTX // NBR_KBLK
    blocks_per_batch = DEC_SEQ // NBR_KBLK
    cache_spec = lambda s: pl.BlockSpec((None, None, None, PAST_LEN, W_HEADS), lambda band, b: (b, l, s, 0, 0))
    key_spec = lambda j: pl.BlockSpec(
        (NBR_KBLK, 1024), lambda band, b: (k_base + b * blocks_per_batch + _nbr_window_block(band) + j, 0))
    band_type = lambda band: jnp.where(band == 0, 0, jnp.where(band == NBR_N_BANDS - 1, 2, 1))
    return pl.pallas_call(
        _nbr_attn_kernel,
        grid=(NBR_N_BANDS, DEC_BATCH),
        in_specs=[pl.BlockSpec((NBR_Q, 512), lambda band, b: (q_base + b * NBR_N_BANDS + band, 2)),
                  key_spec(0), key_spec(1), key_spec(2), cache_spec(0), cache_spec(1),
                  pl.BlockSpec((None, H_C, NBR_Q, NBR_WIN_ROWS * GRID_W),
                               lambda band, b: (band_type(band), 0, 0, 0))],
        out_specs=pl.BlockSpec((NBR_Q, 512), lambda band, b: (b * NBR_N_BANDS + band, 0)),
        out_shape=jax.ShapeDtypeStruct((N_LAT, 512), BF16),
        compiler_params=_cparams(("arbitrary", "arbitrary")),
        name="nbr_attn",
    )(q_all, kvc, kvc, kvc, cache_c, cache_c, bias_t)


def _nbr_bias_table(rpb_l):
    c = jnp.arange(GRID_W)[:, None]
    kc = jnp.arange(GRID_W)[None, :]
    c_start = jnp.clip(c - NA_COLS // 2, 0, GRID_W - NA_COLS)
    valid = (kc >= c_start) & (kc < c_start + NA_COLS)
    pad = GRID_W - NA_COLS
    rpb_pad = jnp.pad(rpb_l.astype(F32), ((0, 0), (0, 0), (pad, pad)))
    toeplitz = jnp.stack([rpb_pad[:, :, GRID_W - 1 - q:2 * GRID_W - 1 - q] for q in range(GRID_W)], axis=2)
    t = jnp.where(valid[None, None], toeplitz, NEG)
    neg = jnp.full((H_C, GRID_W, GRID_W), NEG, F32)
    tables = []
    for r0 in (0, NBR_BAND, GRID_ROWS - NBR_BAND):
        k0 = min(max(r0 - NA_ROWS // 2, 0), GRID_ROWS - NBR_WIN_ROWS)
        rows = []
        for dq in range(NBR_BAND):
            r = r0 + dq
            start = min(max(r - NA_ROWS // 2, 0), GRID_ROWS - NA_ROWS)
            cols = []
            for i in range(NBR_WIN_ROWS):
                kr = k0 + i
                cols.append(t[:, kr - r + NA_ROWS - 1] if start <= kr < start + NA_ROWS else neg)
            rows.append(jnp.concatenate(cols, axis=2))
        tables.append(jnp.concatenate(rows, axis=1))
    return jnp.stack(tables, axis=0)


def _merge_kernel(octx_ref, oa_ref, ob_ref, oc_ref, gates_ref, hc_ref, hl_ref, g1_ref, sh2_ref, sc2_ref,
                  wbr_ref, wout_ref, gpost_ref, gpre_ref, wr_ref, br_ref,
                  h2_ref, hn2_ref, topk_ref, count_ref):
    i = pl.program_id(0)
    is_ctx = i < CTX_TILES
    merged = None
    for j, lat_ref in enumerate((oa_ref, ob_ref, oc_ref)):
        o = jnp.where(is_ctx, octx_ref[:, j * 512:(j + 1) * 512], lat_ref[...])
        br = jnp.dot(o, wbr_ref[j], preferred_element_type=F32)
        term = gates_ref[:, j * D_MODEL:(j + 1) * D_MODEL].astype(F32) * br
        merged = term if merged is None else merged + term
    t = jnp.dot(merged.astype(BF16), wout_ref[...], preferred_element_type=F32)
    h2 = jnp.where(is_ctx, hc_ref[...], hl_ref[...]) + g1_ref[...] * (_rms(t) * gpost_ref[...])
    h2_ref[...] = h2
    hn2 = _rms(h2) * gpre_ref[...] * (1.0 + sc2_ref[...]) + sh2_ref[...]
    hn2_ref[...] = _pack_halves(hn2)
    x_hi = hn2.astype(BF16)
    x_lo = (hn2 - x_hi.astype(F32)).astype(BF16)
    hi = jnp.dot(x_hi, wr_ref[...], preferred_element_type=F32)
    lo = jnp.dot(x_lo, wr_ref[:, :N_EXPERTS], preferred_element_type=F32)
    logits = hi[:, :N_EXPERTS] + (hi[:, N_EXPERTS:] + lo) + br_ref[...]
    lane = lax.broadcasted_iota(jnp.int32, (TM, N_EXPERTS), 1).astype(F32)
    idxs, vals = [], []
    chosen = jnp.zeros((TM, N_EXPERTS), F32)
    for _ in range(TOP_K):
        m = jnp.max(logits, axis=1, keepdims=True)
        idx = jnp.min(jnp.where(logits == m, lane, float(N_EXPERTS)), axis=1, keepdims=True)
        hot = lane == idx
        idxs.append(idx)
        vals.append(m)
        chosen = chosen + hot.astype(F32)
        logits = jnp.where(hot, -jnp.inf, logits)
    topk_ref[...] = jnp.concatenate(idxs + vals, axis=1)

    @pl.when(i == 0)
    def _():
        count_ref[...] = jnp.zeros_like(count_ref)

    count_ref[...] = count_ref[...] + jnp.sum(chosen, axis=0, keepdims=True)


def _merge(o_ctx, o_a, o_b, o_c, gates, h, mods3, w_br, w_out_b, g_post3, g_pre_ffn3, w_router, b_router3, l):
    lat = lambda: pl.BlockSpec((TM, 512), lambda i: (jnp.maximum(i - CTX_TILES, 0), 0))
    mod = lambda j: pl.BlockSpec((None, 1, D_MODEL), lambda i: (_mod_index(i), 0, j))
    lw = lambda: pl.BlockSpec((None, 1, D_MODEL), lambda i: (l, 0, 0))
    row = lambda n: pl.BlockSpec((TM, n), lambda i: (i, 0))
    const = lambda shape: pl.BlockSpec(shape, lambda i: (0,) * len(shape), pipeline_mode=pl.Buffered(1))
    return pl.pallas_call(
        _merge_kernel,
        grid=(N_TILES,),
        in_specs=[pl.BlockSpec((TM, 1536), lambda i: (jnp.minimum(i, CTX_TILES - 1), 0)),
                  lat(), lat(), lat(), row(3072), _ctx_spec(D_MODEL), _lat_spec(D_MODEL),
                  mod(2), mod(3), mod(4),
                  const((3, 512, D_MODEL)), const((D_MODEL, D_MODEL)), lw(), lw(),
                  const((D_MODEL, 2 * N_EXPERTS)),
                  pl.BlockSpec((None, 1, N_EXPERTS), lambda i: (l, 0, 0))],
        out_specs=[row(D_MODEL), row(HALF), row(2 * TOP_K),
                   pl.BlockSpec((8, N_EXPERTS), lambda i: (0, 0))],
        out_shape=[jax.ShapeDtypeStruct((N_TOK, D_MODEL), F32),
                   jax.ShapeDtypeStruct((N_TOK, HALF), jnp.int32),
                   jax.ShapeDtypeStruct((N_TOK, 2 * TOP_K), F32),
                   jax.ShapeDtypeStruct((8, N_EXPERTS), F32)],
        compiler_params=_cparams(("arbitrary",)),
        name="merge",
    )(o_ctx, o_a, o_b, o_c, gates, *h, mods3, mods3, mods3, w_br, w_out_b, g_post3, g_pre_ffn3,
      w_router, b_router3)


def _moe_kernel(l, be_ref, nv_ref, first_ref, slot_ref, nxt_ref, rows_ref, x_ref, wgu_hbm, bgu_ref, wd_hbm, bd_ref, y_ref,
                wgu_f32, wd_f32, wgu_bf, wd_bf, sem):
    i = pl.program_id(0)

    def fetch(e, s):
        return (pltpu.make_async_copy(wgu_hbm.at[l, e], wgu_f32.at[s], sem.at[0, s]),
                pltpu.make_async_copy(wd_hbm.at[l, e], wd_f32.at[s], sem.at[1, s]))

    @pl.when(first_ref[i] == 1)
    def _():
        s = slot_ref[i]

        @pl.when(i == 0)
        def _():
            for cp in fetch(be_ref[i], s):
                cp.start()

        for cp in fetch(be_ref[i], s):
            cp.wait()

        @pl.when(nxt_ref[i] >= 0)
        def _():
            for cp in fetch(nxt_ref[i], 1 - s):
                cp.start()

        wgu_bf[...] = wgu_f32[s].astype(BF16)
        wd_bf[...] = wd_f32[s].astype(BF16)

    @pl.when(i < nv_ref[0])
    def _():
        real = lax.broadcasted_iota(jnp.int32, (MOE_TILE, HALF), 0) < rows_ref[i]
        xa, xb = _unpack_halves(jnp.where(real, x_ref[...], 0))
        x = jnp.concatenate([xa.astype(BF16), xb.astype(BF16)], axis=1)
        b = bgu_ref[...]
        glu = jnp.dot(x, wgu_bf[:, :D_FF], preferred_element_type=F32) + b[:, :D_FF]
        lin = jnp.dot(x, wgu_bf[:, D_FF:], preferred_element_type=F32) + b[:, D_FF:]
        glu = jnp.minimum(glu, SWIGLU_LIMIT)
        lin = jnp.clip(lin, -SWIGLU_LIMIT, SWIGLU_LIMIT)
        act = glu * (1.0 / (1.0 + jnp.exp(-SWIGLU_ALPHA * glu))) * (lin + 1.0)
        y = jnp.dot(act.astype(BF16), wd_bf[...], preferred_element_type=F32) + bd_ref[...]
        y_ref[...] = _pack_halves(y)

    @pl.when(i >= nv_ref[0])
    def _():
        y_ref[...] = jnp.zeros_like(y_ref)


def _moe(plan, x_slots, w_gate_up, b_gate_up4, w_down, b_down4, l):
    grid_spec = pltpu.PrefetchScalarGridSpec(
        num_scalar_prefetch=6,
        grid=(N_MOE_BLOCKS,),
        in_specs=[pl.BlockSpec((MOE_TILE, HALF), lambda i, be, *_: (i, 0)),
                  pl.BlockSpec(memory_space=pl.ANY),
                  pl.BlockSpec((None, None, 1, 2 * D_FF), lambda i, be, *_: (l, be[i], 0, 0)),
                  pl.BlockSpec(memory_space=pl.ANY),
                  pl.BlockSpec((None, None, 1, D_MODEL), lambda i, be, *_: (l, be[i], 0, 0))],
        out_specs=pl.BlockSpec((MOE_TILE, HALF), lambda i, be, *_: (i, 0)),
        scratch_shapes=[pltpu.VMEM((2, D_MODEL, 2 * D_FF), F32), pltpu.VMEM((2, D_FF, D_MODEL), F32),
                        pltpu.VMEM((D_MODEL, 2 * D_FF), BF16), pltpu.VMEM((D_FF, D_MODEL), BF16),
                        pltpu.SemaphoreType.DMA((2, 2))])
    return pl.pallas_call(
        functools.partial(_moe_kernel, l),
        grid_spec=grid_spec,
        out_shape=jax.ShapeDtypeStruct((N_SLOTS, HALF), jnp.int32),
        compiler_params=_cparams(("arbitrary",)),
        name="moe",
    )(*plan, x_slots, w_gate_up, b_gate_up4, w_down, b_down4)


def _combine_kernel(y0_ref, y1_ref, y2_ref, y3_ref, gate_ref, h_ref, g2_ref, gpost_ref, oc_ref, ol_ref):
    i = pl.program_id(0)
    gate = gate_ref[...]
    ffn = None
    for k, y_ref in enumerate((y0_ref, y1_ref, y2_ref, y3_ref)):
        ya, yb = _unpack_halves(y_ref[...])
        term = gate[:, k:k + 1] * jnp.concatenate([ya, yb], axis=1)
        ffn = term if ffn is None else ffn + term
    out = h_ref[...] + g2_ref[...] * (_rms(ffn) * gpost_ref[...])

    @pl.when(i < CTX_TILES)
    def _():
        oc_ref[...] = out

    @pl.when(i >= CTX_TILES)
    def _():
        ol_ref[...] = out


def _combine(y_tok, gate, h2, mods3, g_post_ffn3, l):
    row = lambda n: pl.BlockSpec((TM, n), lambda i: (i, 0))
    choice = lambda k: pl.BlockSpec((TM, HALF), lambda i: (k * N_TILES + i, 0))
    return pl.pallas_call(
        _combine_kernel,
        grid=(N_TILES,),
        in_specs=[choice(0), choice(1), choice(2), choice(3), row(TOP_K), row(D_MODEL),
                  pl.BlockSpec((None, 1, D_MODEL), lambda i: (_mod_index(i), 0, 5)),
                  pl.BlockSpec((None, 1, D_MODEL), lambda i: (l, 0, 0))],
        out_specs=[_ctx_spec(D_MODEL), _lat_spec(D_MODEL)],
        out_shape=[jax.ShapeDtypeStruct((N_CTX, D_MODEL), F32), jax.ShapeDtypeStruct((N_LAT, D_MODEL), F32)],
        compiler_params=_cparams(("arbitrary",)),
        name="combine",
    )(y_tok, y_tok, y_tok, y_tok, gate, h2, mods3, g_post_ffn3)


def _route_kernel(topk_ref, count_ref, dest_ref, gate_ref, base_ref):
    i = pl.program_id(0)

    @pl.when(i == 0)
    def _():
        total = count_ref[...]
        padded = jnp.floor((total + (MOE_TILE - 1.0)) * (1.0 / MOE_TILE)) * MOE_TILE
        before = (lax.broadcasted_iota(jnp.int32, (N_EXPERTS, N_EXPERTS), 0)
                  < lax.broadcasted_iota(jnp.int32, (N_EXPERTS, N_EXPERTS), 1)).astype(F32)
        base_ref[...] = jnp.dot(padded, before, preferred_element_type=F32, precision=lax.Precision.HIGHEST)

    topk = topk_ref[...]
    lane = lax.broadcasted_iota(jnp.int32, (TM, N_EXPERTS), 1).astype(F32)
    hots = [lane == topk[:, k:k + 1] for k in range(TOP_K)]
    vals = [topk[:, TOP_K + k:TOP_K + k + 1] for k in range(TOP_K)]
    chosen = functools.reduce(jnp.add, [h.astype(F32) for h in hots])
    earlier_row = (lax.broadcasted_iota(jnp.int32, (TM, TM), 1)
                   < lax.broadcasted_iota(jnp.int32, (TM, TM), 0)).astype(BF16)
    earlier = jnp.dot(earlier_row, chosen.astype(BF16), preferred_element_type=F32)
    offs = base_ref[0:1, :] + earlier
    dest = [jnp.sum(jnp.where(h, offs, 0.0), axis=1, keepdims=True) for h in hots]
    dest_ref[...] = jnp.concatenate(dest, axis=1).astype(jnp.int32)
    e = [jnp.exp(v - vals[0]) for v in vals]
    den = functools.reduce(jnp.add, e)
    gate_ref[...] = jnp.concatenate(e, axis=1) / den
    base_ref[...] = base_ref[...] + jnp.sum(chosen, axis=0, keepdims=True)


def _route(topk, counts):
    tile = lambda n: pl.BlockSpec((TM, n), lambda i: (i, 0))
    dest, gate = pl.pallas_call(
        _route_kernel,
        grid=(N_TILES,),
        in_specs=[tile(2 * TOP_K), pl.BlockSpec((8, N_EXPERTS), lambda i: (0, 0))],
        out_specs=[tile(TOP_K), tile(TOP_K)],
        out_shape=[jax.ShapeDtypeStruct((N_TOK, TOP_K), jnp.int32),
                   jax.ShapeDtypeStruct((N_TOK, TOP_K), F32)],
        scratch_shapes=[pltpu.VMEM((8, N_EXPERTS), F32)],
        compiler_params=_cparams(("arbitrary",)),
        name="route",
    )(topk, counts)
    counts = counts[0].astype(jnp.int32)
    expert = jnp.arange(N_EXPERTS, dtype=jnp.int32)
    padded = (counts + MOE_TILE - 1) // MOE_TILE * MOE_TILE
    pad_end = jnp.sum(jnp.where(expert[None, :] <= expert[:, None], padded[None, :], 0), axis=1)
    block = jnp.arange(N_MOE_BLOCKS, dtype=jnp.int32)
    block_e = jnp.minimum(jnp.sum((pad_end[None, :] <= block[:, None] * MOE_TILE).astype(jnp.int32), axis=1),
                          N_EXPERTS - 1)
    n_valid = pad_end[-1] // MOE_TILE
    mine = expert[None, :] == block_e[:, None]
    pick = lambda v: jnp.sum(jnp.where(mine, v[None, :], 0), axis=1)
    offset = block * MOE_TILE - pick(pad_end - padded)
    valid = block < n_valid
    first = valid & (offset == 0)
    used = counts > 0
    slot = jnp.sum((used[None, :] & (expert[None, :] < block_e[:, None])).astype(jnp.int32), axis=1) % 2
    nxt = jnp.min(jnp.where(used[None, :] & (expert[None, :] > block_e[:, None]), expert[None, :], N_EXPERTS),
                  axis=1)
    nxt = jnp.where(nxt < N_EXPERTS, nxt, -1)
    rows = jnp.where(valid, jnp.clip(pick(counts) - offset, 0, MOE_TILE), 0)
    plan = tuple(a.astype(jnp.int32) for a in (block_e, n_valid[None], first, slot, nxt, rows))
    n_workers = N_TOK // SCATTER_TOKENS_PER_WORKER
    dest_sc = dest.reshape(n_workers, SCATTER_TOKENS_PER_WORKER // GATHER_CHUNK, GATHER_CHUNK, TOP_K)
    dest_sc = dest_sc.transpose(0, 1, 3, 2).reshape(n_workers, -1, GATHER_CHUNK)
    return gate, dest.T.reshape(-1), dest_sc, plan


def _rope_tables():
    t = jnp.arange(DEC_SEQ)
    inv = ROPE_THETA ** (-jnp.arange(ROPE_PAIRS, dtype=F32) / ROPE_PAIRS)
    row = (t // GRID_W).astype(F32)[:, None] * inv
    col = (t % GRID_W).astype(F32)[:, None] * inv
    zeros = jnp.zeros_like(row)
    cos = jnp.concatenate([jnp.cos(row), jnp.cos(row), jnp.cos(col), jnp.cos(col)], axis=1)
    s1 = jnp.concatenate([-jnp.sin(row), zeros, -jnp.sin(col), zeros], axis=1)
    s2 = jnp.concatenate([zeros, jnp.sin(row), zeros, jnp.sin(col)], axis=1)
    ident = lambda v: jnp.full((TM, HEAD_DIM), v, F32)
    tables = [jnp.concatenate([x, ident(v)], axis=0) for x, v in ((cos, 1.0), (s1, 0.0), (s2, 0.0))]
    return [jnp.tile(x, (1, 2)) for x in tables]


def kernel(x_prompt, x_sample, cache_a, cache_b, cache_c, c, c_ctx, w_ada, b_ada, g_pre_mix, g_post_mix,
           g_pre_ffn, g_post_ffn, w_in, g_q_b, g_k_b, sink_a, rpb_c, w_br_a, w_br_b, w_br_c, w_out,
           w_router, b_router, w_gate_up, b_gate_up, w_down, b_down):
    h = (x_prompt.reshape(N_CTX, D_MODEL), x_sample.reshape(N_LAT, D_MODEL))
    cond8 = jnp.concatenate([c_ctx[None], c, jnp.zeros((3, D_MODEL), F32)], axis=0)
    cache_a = cache_a.astype(BF16).reshape(DEC_BATCH, DEPTH, 2, PAST_LEN, W_KV)
    cache_b = cache_b.astype(BF16).reshape(DEC_BATCH, DEPTH, 2, PAST_LEN, W_KV)
    cache_c = cache_c.astype(BF16).reshape(DEC_BATCH, DEPTH, 2, PAST_LEN, W_HEADS)
    cos_t, s1_t, s2_t = _rope_tables()
    bd = jnp.kron(jnp.eye(256 // HEAD_DIM, dtype=F32),
                  jnp.full((HEAD_DIM, HEAD_DIM), 1.0 / HEAD_DIM, F32)).astype(BF16)
    vec3 = lambda a: a.reshape(DEPTH, 1, a.shape[-1])
    scale = HEAD_DIM ** -0.5
    states = []
    for l in range(DEPTH):
        w = w_in[l]
        pair_cols = lambda a: a.reshape(D_MODEL, 2, 4, HEAD_DIM).transpose(0, 2, 1, 3).reshape(D_MODEL, 512)
        pair_rows = lambda a: a.reshape(2, 4, HEAD_DIM, D_MODEL).transpose(1, 0, 2, 3).reshape(512, D_MODEL)
        w_in_p = jnp.concatenate(
            [pair_cols(w[:, 0:512]) * scale, pair_cols(w[:, 768:1280]), w[:, 1536:2048] * scale,
             w[:, 2048:2560], w[:, 2560:3072],
             w[:, 512:640], w[:, 1280:1408], w[:, 640:768], w[:, 1408:1536], w[:, 3072:]], axis=1).astype(BF16)
        gq = (jnp.tile(g_q_b[l], H_B) * scale)[None]
        gk = jnp.tile(g_k_b[l], KV_B)[None]
        w_br = jnp.stack([pair_rows(w_br_a[l]), pair_rows(w_br_b[l]), w_br_c[l]], axis=0).astype(BF16)
        w_out_b = w_out[l].astype(BF16)
        w_r_hi = w_router[l].astype(BF16)
        w_r_lo = (w_router[l] - w_r_hi.astype(F32)).astype(BF16)
        w_router2 = jnp.concatenate([w_r_hi, w_r_lo], axis=1)

        mods3 = _adaln(cond8, w_ada, vec3(b_ada), l).reshape(8, 1, 6 * D_MODEL)
        q_all, kvab, kvc, gates, *states = _proj(h, mods3, vec3(g_pre_mix), w_in_p, gq, gk, bd,
                                                 cos_t, s1_t, s2_t, tuple(states), l)
        o_ctx = _ctx_attn(sink_a, q_all, kvab, kvc, l)
        o_a = _win_attn(sink_a, q_all, kvab, cache_a, l)
        o_b = _dense_attn(q_all, kvab, cache_b, l)
        o_c = _nbr_attn(q_all, kvc, cache_c, _nbr_bias_table(rpb_c[l]), l)
        h2, hn2, topk, counts = _merge(o_ctx, o_a, o_b, o_c, gates, h, mods3, w_br, w_out_b,
                                       vec3(g_post_mix), vec3(g_pre_ffn), w_router2, vec3(b_router), l)
        gate, dest, dest_sc, plan = _route(topk, counts)
        x_slots = _scatter_rows(hn2, dest_sc, N_SLOTS)
        y_slots = _moe(plan, x_slots, w_gate_up,
                       b_gate_up.reshape(DEPTH, N_EXPERTS, 1, 2 * D_FF), w_down,
                       b_down.reshape(DEPTH, N_EXPERTS, 1, D_MODEL), l)
        y_tok = _gather_rows(y_slots, dest)
        h = _combine(y_tok, gate, h2, mods3, vec3(g_post_ffn), l)

    state_a, state_b, state_c = states
    heads = lambda s, n: s.reshape(BATCH, DEPTH, 2, SEQ, n, HEAD_DIM)
    return (h[0].reshape(BATCH, SEQ, D_MODEL), h[1].reshape(DEC_BATCH, DEC_SEQ, D_MODEL),
            heads(state_a, KV_A), heads(state_b, KV_B), heads(state_c, H_C))
```

```python
import functools

import jax
import jax.numpy as jnp
from jax import lax
from jax.experimental import pallas as pl
from jax.experimental.pallas import tpu as pltpu
from jax.experimental.pallas import tpu_sc as plsc

D_MODEL = 1024
BATCH = 32
SEQ = 256
DEPTH = 2
DEC_BATCH = 4
DEC_SEQ = 2048
PAST_LEN = 512
GRID_W = 64
HEAD_DIM = 64
H_A = 8
KV_A = 2
H_B = 8
KV_B = 2
H_C = 8
WINDOW_A = 128
NA_ROWS = 8
NA_COLS = 16
ROPE_THETA = 10000.0
ROPE_PAIRS = HEAD_DIM // 4
N_EXPERTS = 32
TOP_K = 4
D_FF = D_MODEL
SWIGLU_ALPHA = 1.702
SWIGLU_LIMIT = 7.0
EPS = 1e-6

W_HEADS = H_A * HEAD_DIM
W_KV = KV_A * HEAD_DIM
N_CTX = BATCH * SEQ
N_LAT = DEC_BATCH * DEC_SEQ
N_TOK = N_CTX + N_LAT
GRID_ROWS = DEC_SEQ // GRID_W
D_IN = 3 * W_HEADS + 4 * W_KV + 2 * W_HEADS + 3 * D_MODEL

TM = 256
N_TILES = N_TOK // TM
CTX_TILES = N_CTX // TM
LAT_TILES_PER_BATCH = DEC_SEQ // TM
TQ = 128
MOE_TILE = 256
N_SLOTS = N_TOK * TOP_K + N_EXPERTS * MOE_TILE
N_MOE_BLOCKS = N_SLOTS // MOE_TILE
NEG = -1e30
VMEM_LIMIT = 56 * 1024 * 1024

BF16 = jnp.bfloat16
F32 = jnp.float32


def _cparams(sem):
    return pltpu.CompilerParams(dimension_semantics=sem, vmem_limit_bytes=VMEM_LIMIT)


def _mod_index(i):
    return jnp.where(i < CTX_TILES, 0, 1 + (i - CTX_TILES) // LAT_TILES_PER_BATCH)


def _rms(x):
    return x * lax.rsqrt(jnp.mean(x * x, axis=-1, keepdims=True) + EPS)


HALF = D_MODEL // 2


def _pack_halves(x):
    hi = lax.bitcast_convert_type(x[:, :HALF].astype(BF16).astype(F32), jnp.uint32)
    lo = lax.bitcast_convert_type(x[:, HALF:].astype(BF16).astype(F32), jnp.uint32)
    return lax.bitcast_convert_type(hi | (lo >> 16), jnp.int32)


def _unpack_halves(w):
    u = lax.bitcast_convert_type(w, jnp.uint32)
    return (lax.bitcast_convert_type(u & jnp.uint32(0xFFFF0000), F32),
            lax.bitcast_convert_type(u << 16, F32))


GATHER_CHUNK = 64


def _gather_rows(table, idx):
    n = idx.shape[0]
    width = table.shape[1]
    info = plsc.get_sparse_core_info()
    n_workers = info.num_cores * info.num_subcores
    per_worker = n // n_workers
    n_chunks = per_worker // GATHER_CHUNK
    assert per_worker * n_workers == n and n_chunks * GATHER_CHUNK == per_worker and n_chunks % 2 == 0
    mesh = plsc.VectorSubcoreMesh(core_axis_name="core", subcore_axis_name="subcore")

    @functools.partial(
        pl.kernel, out_type=jax.ShapeDtypeStruct((n, width), table.dtype), mesh=mesh,
        scratch_types=[pltpu.VMEM((per_worker,), jnp.int32),
                       pltpu.VMEM((2, GATHER_CHUNK, width), table.dtype),
                       pltpu.SemaphoreType.DMA((2,)), pltpu.SemaphoreType.DMA((2,))])
    def gather(table_hbm, idx_hbm, out_hbm, idx_v, rows_v, gather_sem, write_sem):
        worker = lax.axis_index("subcore") * info.num_cores + lax.axis_index("core")
        base = worker * per_worker
        pltpu.sync_copy(idx_hbm.at[pl.ds(base, per_worker)], idx_v)

        def fetch(chunk, slot):
            rows = idx_v.at[pl.ds(chunk * GATHER_CHUNK, GATHER_CHUNK)]
            return pltpu.make_async_copy(table_hbm.at[rows], rows_v.at[slot], gather_sem.at[slot])

        def write(chunk, slot):
            dst = out_hbm.at[pl.ds(base + chunk * GATHER_CHUNK, GATHER_CHUNK)]
            return pltpu.make_async_copy(rows_v.at[slot], dst, write_sem.at[slot])

        fetch(0, 0).start()

        @pl.loop(0, n_chunks, step=2)
        def _(c):
            @pl.when(c > 0)
            def _():
                write(c - 1, 1).wait()

            fetch(c + 1, 1).start()
            fetch(c, 0).wait()
            write(c, 0).start()
            write(c, 0).wait()

            @pl.when(c + 2 < n_chunks)
            def _():
                fetch(c + 2, 0).start()

            fetch(c + 1, 1).wait()
            write(c + 1, 1).start()

        write(n_chunks - 1, 1).wait()

    return gather(table, idx)


SC_WORKERS_V7X = 32
SCATTER_TOKENS_PER_WORKER = N_TOK // SC_WORKERS_V7X


def _scatter_rows(table, dest_sc, n_out):
    width = table.shape[1]
    info = plsc.get_sparse_core_info()
    assert info.num_cores * info.num_subcores == SC_WORKERS_V7X
    per_worker = SCATTER_TOKENS_PER_WORKER
    n_chunks = per_worker // GATHER_CHUNK
    assert n_chunks % 2 == 0 and dest_sc.shape == (SC_WORKERS_V7X, n_chunks * TOP_K, GATHER_CHUNK)
    mesh = plsc.VectorSubcoreMesh(core_axis_name="core", subcore_axis_name="subcore")

    @functools.partial(
        pl.kernel, out_type=jax.ShapeDtypeStruct((n_out, width), table.dtype), mesh=mesh,
        scratch_types=[pltpu.VMEM((n_chunks * TOP_K, GATHER_CHUNK), jnp.int32),
                       pltpu.VMEM((2, GATHER_CHUNK, width), table.dtype),
                       pltpu.SemaphoreType.DMA((2,)), pltpu.SemaphoreType.DMA((2,))])
    def scatter(table_hbm, dest_hbm, out_hbm, idx_v, rows_v, read_sem, write_sem):
        worker = lax.axis_index("subcore") * info.num_cores + lax.axis_index("core")
        base = worker * per_worker
        pltpu.sync_copy(dest_hbm.at[worker], idx_v)

        def read(chunk, slot):
            src = table_hbm.at[pl.ds(base + chunk * GATHER_CHUNK, GATHER_CHUNK)]
            return pltpu.make_async_copy(src, rows_v.at[slot], read_sem.at[slot])

        def writes(chunk, slot):
            return [pltpu.make_async_copy(rows_v.at[slot], out_hbm.at[idx_v.at[chunk * TOP_K + k]],
                                          write_sem.at[slot]) for k in range(TOP_K)]

        read(0, 0).start()

        @pl.loop(0, n_chunks, step=2)
        def _(c):
            @pl.when(c > 0)
            def _():
                for cp in writes(c - 1, 1):
                    cp.wait()

            read(c + 1, 1).start()
            read(c, 0).wait()
            for cp in writes(c, 0):
                cp.start()
            for cp in writes(c, 0):
                cp.wait()

            @pl.when(c + 2 < n_chunks)
            def _():
                read(c + 2, 0).start()

            read(c + 1, 1).wait()
            for cp in writes(c + 1, 1):
                cp.start()

        for cp in writes(n_chunks - 1, 1):
            cp.wait()

    return scatter(table, dest_sc)


def _adaln_kernel(c_ref, w_ref, b_ref, o_ref):
    c = c_ref[...]
    s = c / (1.0 + jnp.exp(-c))
    o_ref[...] = jnp.dot(s, w_ref[...], preferred_element_type=F32,
                         precision=lax.Precision.HIGHEST) + b_ref[...]


def _adaln(cond8, w_ada, b_ada3):
    tn = 768
    return pl.pallas_call(
        _adaln_kernel,
        grid=(DEPTH, 6 * D_MODEL // tn),
        in_specs=[pl.BlockSpec((8, D_MODEL), lambda l, j: (0, 0)),
                  pl.BlockSpec((None, D_MODEL, tn), lambda l, j: (l, 0, j)),
                  pl.BlockSpec((None, 1, tn), lambda l, j: (l, 0, j))],
        out_specs=pl.BlockSpec((None, 8, tn), lambda l, j: (l, 0, j)),
        out_shape=jax.ShapeDtypeStruct((DEPTH, 8, 6 * D_MODEL), F32),
        compiler_params=_cparams(("arbitrary", "arbitrary")),
        name="adaln",
    )(cond8, w_ada, b_ada3)


C_QA, C_QB, C_QC, C_KC, C_VC, C_KAB, C_GL = 0, 512, 1024, 1536, 2048, 2560, 3072


def _ctx_spec(n):
    return pl.BlockSpec((TM, n), lambda i: (jnp.minimum(i, CTX_TILES - 1), 0))


def _lat_spec(n):
    return pl.BlockSpec((TM, n), lambda i: (jnp.maximum(i - CTX_TILES, 0), 0))


def _proj_kernel(hc_ref, hl_ref, sh_ref, sc_ref, gpre_ref, w_ref, gq_ref, gk_ref, bd_ref,
                 cos_ref, s1_ref, s2_ref, *rest):
    q_ref, kvab_ref, kvc_ref, gates_ref, sta_ref, stb_ref, stc_ref = rest[-7:]
    i = pl.program_id(0)
    hn = _rms(jnp.where(i < CTX_TILES, hc_ref[...], hl_ref[...])) * gpre_ref[...]
    hb = (hn * (1.0 + sc_ref[...]) + sh_ref[...]).astype(BF16)
    cos, s1, s2 = cos_ref[...], s1_ref[...], s2_ref[...]
    bd = bd_ref[...]

    def proj(c0, n):
        return jnp.dot(hb, w_ref[:, c0:c0 + n], preferred_element_type=F32)

    def rope(t):
        parts = []
        for g in range(t.shape[1] // 128):
            tg = t[:, g * 128:(g + 1) * 128]
            parts.append(tg * cos + pltpu.roll(tg, 112, 1) * s1 + pltpu.roll(tg, 16, 1) * s2)
        return parts[0] if len(parts) == 1 else jnp.concatenate(parts, axis=1)

    def headnorm(t, g):
        sq = (t * t).astype(BF16)
        n = t.shape[1]
        if n == 128:
            ms = jnp.dot(sq, bd[:128, :128], preferred_element_type=F32)
        else:
            ms = jnp.concatenate(
                [jnp.dot(sq[:, c:c + 256], bd, preferred_element_type=F32) for c in range(0, n, 256)],
                axis=1)
        return t * lax.rsqrt(ms + EPS) * g

    q_ref[:, 0:512] = rope(proj(C_QA, 512)).astype(BF16)
    q_ref[:, 512:1024] = rope(headnorm(proj(C_QB, 512), gq_ref[...])).astype(BF16)
    q_ref[:, 1024:1536] = proj(C_QC, 512).astype(BF16)
    kc = proj(C_KC, 512)
    vc = proj(C_VC, 512)
    kvc_ref[:, 0:512] = kc.astype(BF16)
    kvc_ref[:, 512:1024] = vc.astype(BF16)
    kab = proj(C_KAB, 512)
    ka = kab[:, 0:128]
    kb = headnorm(kab[:, 128:256], gk_ref[...])
    kvab_ref[:, 0:128] = rope(ka).astype(BF16)
    kvab_ref[:, 128:256] = rope(kb).astype(BF16)
    kvab_ref[:, 256:512] = kab[:, 256:512].astype(BF16)
    for j in range(6):
        gl = proj(C_GL + j * 512, 512)
        gates_ref[:, j * 512:(j + 1) * 512] = (1.0 / (1.0 + jnp.exp(-gl))).astype(BF16)

    @pl.when(i < CTX_TILES)
    def _():
        for st_ref, k, v in ((sta_ref, ka, kab[:, 256:384]), (stb_ref, kb, kab[:, 384:512]), (stc_ref, kc, vc)):
            if len(st_ref.shape) == 4:
                st_ref[0, 0] = k
                st_ref[0, 1] = v
                st_ref[1:] = jnp.zeros((DEPTH - 1,) + tuple(st_ref.shape[1:]), F32)
            else:
                st_ref[0] = k
                st_ref[1] = v


def _proj(h, mods3, g_pre3, w_in_p, gq, gk, bd, cos_t, s1_t, s2_t, prev_states, l):
    def rope_idx(i):
        return jnp.where(i < CTX_TILES, LAT_TILES_PER_BATCH, (i - CTX_TILES) % LAT_TILES_PER_BATCH)

    const = lambda shape: pl.BlockSpec(shape, lambda i: (0,) * len(shape), pipeline_mode=pl.Buffered(1))
    rope_spec = pl.BlockSpec((TM, 128), lambda i: (rope_idx(i), 0))
    row = lambda n: pl.BlockSpec((TM, n), lambda i: (i, 0))
    if l == 0:
        state_spec = lambda n: pl.BlockSpec((None, DEPTH, 2, SEQ, n),
                                            lambda i: (jnp.minimum(i, CTX_TILES - 1), 0, 0, 0, 0))
    else:
        state_spec = lambda n: pl.BlockSpec((None, None, 2, SEQ, n),
                                            lambda i: (jnp.minimum(i, CTX_TILES - 1), l, 0, 0, 0))
    state_shape = lambda n: jax.ShapeDtypeStruct((BATCH, DEPTH, 2, SEQ, n), F32)
    n_in = 12
    return pl.pallas_call(
        _proj_kernel,
        grid=(N_TILES,),
        in_specs=[_ctx_spec(D_MODEL), _lat_spec(D_MODEL),
                  pl.BlockSpec((None, 1, D_MODEL), lambda i: (_mod_index(i), 0, 0)),
                  pl.BlockSpec((None, 1, D_MODEL), lambda i: (_mod_index(i), 0, 1)),
                  pl.BlockSpec((None, 1, D_MODEL), lambda i: (l, 0, 0)),
                  const((D_MODEL, D_IN)), const((1, 512)), const((1, 128)), const((256, 256)),
                  rope_spec, rope_spec, rope_spec] + [pl.BlockSpec(memory_space=pl.ANY)] * len(prev_states),
        out_specs=[row(1536), row(512), row(1024), row(3072),
                   state_spec(W_KV), state_spec(W_KV), state_spec(W_HEADS)],
        out_shape=[jax.ShapeDtypeStruct((N_TOK, 1536), BF16),
                   jax.ShapeDtypeStruct((N_TOK, 512), BF16),
                   jax.ShapeDtypeStruct((N_TOK, 1024), BF16),
                   jax.ShapeDtypeStruct((N_TOK, 3072), BF16),
                   state_shape(W_KV), state_shape(W_KV), state_shape(W_HEADS)],
        input_output_aliases={n_in + j: 4 + j for j in range(len(prev_states))},
        compiler_params=_cparams(("arbitrary",)),
        name="proj",
    )(*h, mods3, mods3, g_pre3, w_in_p, gq, gk, bd, cos_t, s1_t, s2_t, *prev_states)


def _qk(q, k):
    return lax.dot_general(q, k, (((1,), (1,)), ((), ())), preferred_element_type=F32)


def _softmax_pv(scores, values, sink=None, halves=None):
    m = functools.reduce(jnp.maximum, [jnp.max(s, axis=-1, keepdims=True) for s in scores])
    if sink is not None:
        m = jnp.maximum(m, sink)
    if halves is None:
        ps = [jnp.exp(s - m) for s in scores]
        den = functools.reduce(jnp.add, [jnp.sum(p, axis=-1, keepdims=True) for p in ps])
        if sink is not None:
            den = den + jnp.exp(sink - m)
        o = functools.reduce(jnp.add, [jnp.dot(p.astype(BF16), v, preferred_element_type=F32)
                                       for p, v in zip(ps, values)])
        return o / den
    ps = [jnp.exp(s - m).astype(BF16) for s in scores]
    rows = ps[0].shape[0]
    split = {"both": rows // 2, "lo": rows, "hi": 0}[halves]
    one = jnp.ones((), BF16)

    def pv(r0, r1, keep_lo):
        acc = None
        for p, v in zip(ps, values):
            lo = _lo_lanes(v.shape[0])
            v1 = jnp.where(lo, v, one) if keep_lo else jnp.where(lo, one, v)
            t = jnp.dot(p[r0:r1], v1, preferred_element_type=F32)
            acc = t if acc is None else acc + t
        return acc

    parts = ([pv(0, split, True)] if split > 0 else []) + ([pv(split, rows, False)] if split < rows else [])
    o = parts[0] if len(parts) == 1 else jnp.concatenate(parts, axis=0)
    if sink is not None:
        is_lo_row = lax.broadcasted_iota(jnp.int32, (rows, 128), 0) < split
        o = o + jnp.where(is_lo_row != _lo_lanes(rows), jnp.exp(sink - m), 0.0)
    return o * pltpu.roll(1.0 / o, HEAD_DIM, 1)


def _lo_lanes(rows):
    return lax.broadcasted_iota(jnp.int32, (rows, 128), 1) < HEAD_DIM


def _stack_pairs(q, n_pairs):
    lo = _lo_lanes(q.shape[0])
    zero = jnp.zeros((q.shape[0], 128), q.dtype)
    pairs = [q[:, p * 128:(p + 1) * 128] for p in range(n_pairs)]
    return jnp.concatenate([jnp.where(lo, x, zero) for x in pairs] + [jnp.where(lo, zero, x) for x in pairs],
                           axis=0)


def _unstack_pairs(o, n_pairs):
    rows = o.shape[0] // (2 * n_pairs)
    lo = _lo_lanes(rows)
    return jnp.concatenate(
        [jnp.where(lo, o[p * rows:(p + 1) * rows], o[(n_pairs + p) * rows:(n_pairs + p + 1) * rows])
         for p in range(n_pairs)], axis=1)


def _sink_column(sink_ref, l, rows):
    return jnp.concatenate([jnp.full((rows, 1), sink_ref[l, h], F32) for h in range(H_A)], axis=0)


def _ctx_attn_kernel(l, sink_ref, q_ref, kvab_ref, kvc_ref, o_ref):
    qa = _stack_pairs(q_ref[:, 0:512], 4)
    o = _softmax_pv([_qk(qa, kvab_ref[:, 0:128])], [kvab_ref[:, 256:384]], _sink_column(sink_ref, l, SEQ))
    o_ref[:, 0:512] = _unstack_pairs(o, 4).astype(BF16)
    qb = _stack_pairs(q_ref[:, 512:1024], 4)
    o = _softmax_pv([_qk(qb, kvab_ref[:, 128:256])], [kvab_ref[:, 384:512]])
    o_ref[:, 512:1024] = _unstack_pairs(o, 4).astype(BF16)
    for hp in range(H_C // 2):
        cs = slice(hp * 128, (hp + 1) * 128)
        qc = _stack_pairs(q_ref[:, 1024 + hp * 128:1024 + (hp + 1) * 128], 1)
        o = _softmax_pv([_qk(qc, kvc_ref[:, cs])], [kvc_ref[:, 512 + hp * 128:512 + (hp + 1) * 128]])
        o_ref[:, 1024 + hp * 128:1024 + (hp + 1) * 128] = _unstack_pairs(o, 1).astype(BF16)


def _ctx_attn(sink_a, q_all, kvab, kvc, l):
    row = lambda n: pl.BlockSpec((SEQ, n), lambda b: (b, 0))
    return pl.pallas_call(
        functools.partial(_ctx_attn_kernel, l),
        grid=(BATCH,),
        in_specs=[pl.BlockSpec(memory_space=pltpu.SMEM), row(1536), row(512), row(1024)],
        out_specs=row(1536),
        out_shape=jax.ShapeDtypeStruct((N_CTX, 1536), BF16),
        compiler_params=_cparams(("arbitrary",)),
        name="ctx_attn",
    )(sink_a, q_all, kvab, kvc)


def _win_attn_kernel(l, sink_ref, q_ref, prev_ref, cur_ref, nxt_ref, ck_ref, cv_ref, o_ref):
    n = pl.program_id(1)
    nb = DEC_SEQ // TQ
    rows = H_A * TQ
    qpos = lax.broadcasted_iota(jnp.int32, (rows, TQ), 0) % TQ
    kpos = lax.broadcasted_iota(jnp.int32, (rows, TQ), 1)
    mask_prev = (kpos >= qpos) & (n > 0)
    mask_next = (kpos <= qpos) & (n < nb - 1)
    ks, vs = slice(0, 128), slice(256, 384)
    qs = _stack_pairs(q_ref[...], 4)
    s_prev = jnp.where(mask_prev, _qk(qs, prev_ref[:, ks]), NEG)
    s_cur = _qk(qs, cur_ref[:, ks])
    s_next = jnp.where(mask_next, _qk(qs, nxt_ref[:, ks]), NEG)
    s_ctx = _qk(qs, ck_ref[...])
    o = _softmax_pv([s_prev, s_cur, s_next, s_ctx],
                    [prev_ref[:, vs], cur_ref[:, vs], nxt_ref[:, vs], cv_ref[...]],
                    _sink_column(sink_ref, l, TQ), halves="both")
    o_ref[...] = _unstack_pairs(o, 4).astype(BF16)


def _win_attn(sink_a, q_all, kvab, cache_a, l):
    nb = DEC_SEQ // TQ
    base = N_CTX // TQ
    kv_spec = lambda f: pl.BlockSpec((TQ, 512), lambda b, n: (base + b * nb + f(n), 0))
    cache_spec = lambda s: pl.BlockSpec((None, None, None, PAST_LEN, W_KV), lambda b, n: (b, l, s, 0, 0))
    return pl.pallas_call(
        functools.partial(_win_attn_kernel, l),
        grid=(DEC_BATCH, nb),
        in_specs=[pl.BlockSpec(memory_space=pltpu.SMEM),
                  pl.BlockSpec((TQ, 512), lambda b, n: (base + b * nb + n, 0)),
                  kv_spec(lambda n: jnp.maximum(n - 1, 0)), kv_spec(lambda n: n),
                  kv_spec(lambda n: jnp.minimum(n + 1, nb - 1)),
                  cache_spec(0), cache_spec(1)],
        out_specs=pl.BlockSpec((TQ, 512), lambda b, n: (b * nb + n, 0)),
        out_shape=jax.ShapeDtypeStruct((N_LAT, 512), BF16),
        compiler_params=_cparams(("arbitrary", "arbitrary")),
        name="win_attn",
    )(sink_a, q_all, kvab, kvab, kvab, cache_a, cache_a)


def _dense_attn_kernel(q_ref, kv_ref, ck_ref, cv_ref, o_ref):
    qs = _stack_pairs(q_ref[...], 4)
    half = qs.shape[0] // 2
    outs = []
    for g in range(KV_B):
        qg = qs[g * half:(g + 1) * half]
        outs.append(_softmax_pv([_qk(qg, kv_ref[:, 128:256]), _qk(qg, ck_ref[...])],
                                [kv_ref[:, 384:512], cv_ref[...]], halves=("lo", "hi")[g]))
    o_ref[...] = _unstack_pairs(jnp.concatenate(outs, axis=0), 4).astype(BF16)


def _dense_attn(q_all, kvab, cache_b, l):
    nb = DEC_SEQ // TQ
    base = N_CTX // TQ
    cache_spec = lambda s: pl.BlockSpec((None, None, None, PAST_LEN, W_KV), lambda b, n: (b, l, s, 0, 0))
    return pl.pallas_call(
        _dense_attn_kernel,
        grid=(DEC_BATCH, nb),
        in_specs=[pl.BlockSpec((TQ, 512), lambda b, n: (base + b * nb + n, 1)),
                  pl.BlockSpec((DEC_SEQ, 512), lambda b, n: (N_CTX // DEC_SEQ + b, 0)),
                  cache_spec(0), cache_spec(1)],
        out_specs=pl.BlockSpec((TQ, 512), lambda b, n: (b * nb + n, 0)),
        out_shape=jax.ShapeDtypeStruct((N_LAT, 512), BF16),
        compiler_params=_cparams(("arbitrary", "arbitrary")),
        name="dense_attn",
    )(q_all, kvab, cache_b, cache_b)


NBR_BAND = 4
NBR_Q = NBR_BAND * GRID_W
NBR_WIN_ROWS = 12
NBR_N_BANDS = GRID_ROWS // NBR_BAND
NBR_KBLK = NBR_Q
NBR_WIN_BLOCKS = NBR_WIN_ROWS * GRID_W // NBR_KBLK
NBR_LAST_KB = (GRID_ROWS - NBR_WIN_ROWS) * GRID_W // NBR_KBLK


def _nbr_window_block(band):
    return jnp.clip(band - 1, 0, NBR_LAST_KB)


def _nbr_attn_kernel(q_ref, k0_ref, k1_ref, k2_ref, ck_ref, cv_ref, bias_ref, o_ref):
    k_refs = (k0_ref, k1_ref, k2_ref)
    for hp in range(H_C // 2):
        cs = slice(hp * 128, (hp + 1) * 128)
        vs = slice(512 + hp * 128, 512 + (hp + 1) * 128)
        qs = _stack_pairs(q_ref[:, cs], 1)
        bias = jnp.concatenate([bias_ref[2 * hp], bias_ref[2 * hp + 1]], axis=0)
        scores = [_qk(qs, kr[:, cs]) + bias[:, j * NBR_KBLK:(j + 1) * NBR_KBLK] for j, kr in enumerate(k_refs)]
        scores.append(_qk(qs, ck_ref[:, cs]))
        o = _softmax_pv(scores, [kr[:, vs] for kr in k_refs] + [cv_ref[:, cs]])
        o_ref[:, cs] = _unstack_pairs(o, 1).astype(BF16)


def _nbr_attn(q_all, kvc, cache_c, bias_t, l):
    q_base = N_CTX // NBR_Q
    k_base = N_CTX // NBR_KBLK
    blocks_per_batch = DEC_SEQ // NBR_KBLK
    cache_spec = lambda s: pl.BlockSpec((None, None, None, PAST_LEN, W_HEADS), lambda band, b: (b, l, s, 0, 0))
    key_spec = lambda j: pl.BlockSpec(
        (NBR_KBLK, 1024), lambda band, b: (k_base + b * blocks_per_batch + _nbr_window_block(band) + j, 0))
    band_type = lambda band: jnp.where(band == 0, 0, jnp.where(band == NBR_N_BANDS - 1, 2, 1))
    return pl.pallas_call(
        _nbr_attn_kernel,
        grid=(NBR_N_BANDS, DEC_BATCH),
        in_specs=[pl.BlockSpec((NBR_Q, 512), lambda band, b: (q_base + b * NBR_N_BANDS + band, 2)),
                  key_spec(0), key_spec(1), key_spec(2), cache_spec(0), cache_spec(1),
                  pl.BlockSpec((None, H_C, NBR_Q, NBR_WIN_ROWS * GRID_W),
                               lambda band, b: (band_type(band), 0, 0, 0))],
        out_specs=pl.BlockSpec((NBR_Q, 512), lambda band, b: (b * NBR_N_BANDS + band, 0)),
        out_shape=jax.ShapeDtypeStruct((N_LAT, 512), BF16),
        compiler_params=_cparams(("arbitrary", "arbitrary")),
        name="nbr_attn",
    )(q_all, kvc, kvc, kvc, cache_c, cache_c, bias_t)


def _nbr_bias_table(rpb_l):
    c = jnp.arange(GRID_W)[:, None]
    kc = jnp.arange(GRID_W)[None, :]
    c_start = jnp.clip(c - NA_COLS // 2, 0, GRID_W - NA_COLS)
    valid = (kc >= c_start) & (kc < c_start + NA_COLS)
    pad = GRID_W - NA_COLS
    rpb_pad = jnp.pad(rpb_l.astype(F32), ((0, 0), (0, 0), (pad, pad)))
    toeplitz = jnp.stack([rpb_pad[:, :, GRID_W - 1 - q:2 * GRID_W - 1 - q] for q in range(GRID_W)], axis=2)
    t = jnp.where(valid[None, None], toeplitz, NEG)
    neg = jnp.full((H_C, GRID_W, GRID_W), NEG, F32)
    tables = []
    for r0 in (0, NBR_BAND, GRID_ROWS - NBR_BAND):
        k0 = min(max(r0 - NA_ROWS // 2, 0), GRID_ROWS - NBR_WIN_ROWS)
        rows = []
        for dq in range(NBR_BAND):
            r = r0 + dq
            start = min(max(r - NA_ROWS // 2, 0), GRID_ROWS - NA_ROWS)
            cols = []
            for i in range(NBR_WIN_ROWS):
                kr = k0 + i
                cols.append(t[:, kr - r + NA_ROWS - 1] if start <= kr < start + NA_ROWS else neg)
            rows.append(jnp.concatenate(cols, axis=2))
        tables.append(jnp.concatenate(rows, axis=1))
    return jnp.stack(tables, axis=0)


def _merge_kernel(octx_ref, oa_ref, ob_ref, oc_ref, gates_ref, hc_ref, hl_ref, g1_ref, sh2_ref, sc2_ref,
                  wbr_ref, wout_ref, gpost_ref, gpre_ref, wr_ref, br_ref,
                  h2_ref, hn2_ref, topk_ref, count_ref):
    i = pl.program_id(0)
    is_ctx = i < MERGE_CTX_TILES
    chosen = None
    for r0 in range(0, MERGE_TM, TM):
        rs = slice(r0, r0 + TM)
        merged = None
        for j, lat_ref in enumerate((oa_ref, ob_ref, oc_ref)):
            o = jnp.where(is_ctx, octx_ref[rs, j * 512:(j + 1) * 512], lat_ref[rs, :])
            br = jnp.dot(o, wbr_ref[j], preferred_element_type=F32)
            term = gates_ref[rs, j * D_MODEL:(j + 1) * D_MODEL].astype(F32) * br
            merged = term if merged is None else merged + term
        t = jnp.dot(merged.astype(BF16), wout_ref[...], preferred_element_type=F32)
        h2 = jnp.where(is_ctx, hc_ref[rs, :], hl_ref[rs, :]) + g1_ref[...] * (_rms(t) * gpost_ref[...])
        h2_ref[rs, :] = h2
        hn2 = _rms(h2) * gpre_ref[...] * (1.0 + sc2_ref[...]) + sh2_ref[...]
        hn2_ref[rs, :] = _pack_halves(hn2)
        x_hi = hn2.astype(BF16)
        x_lo = (hn2 - x_hi.astype(F32)).astype(BF16)
        hi = jnp.dot(x_hi, wr_ref[...], preferred_element_type=F32)
        lo = jnp.dot(x_lo, wr_ref[:, :N_EXPERTS], preferred_element_type=F32)
        logits = hi[:, :N_EXPERTS] + (hi[:, N_EXPERTS:] + lo) + br_ref[...]
        lane = lax.broadcasted_iota(jnp.int32, (TM, N_EXPERTS), 1)
        idxs, vals = [], []
        for _ in range(TOP_K):
            idx = jnp.argmax(logits, axis=1, keepdims=True).astype(jnp.int32)
            hot = lane == idx
            idxs.append(idx.astype(F32))
            vals.append(jnp.max(logits, axis=1, keepdims=True))
            chosen = hot.astype(F32) if chosen is None else chosen + hot.astype(F32)
            logits = jnp.where(hot, -jnp.inf, logits)
        topk_ref[rs, :] = jnp.concatenate(idxs + vals, axis=1)

    @pl.when(i == 0)
    def _():
        count_ref[...] = jnp.zeros_like(count_ref)

    count_ref[...] = count_ref[...] + jnp.sum(chosen, axis=0, keepdims=True)


MERGE_TM = 2 * TM
MERGE_TILES = N_TOK // MERGE_TM
MERGE_CTX_TILES = N_CTX // MERGE_TM
MERGE_LAT_TILES_PER_BATCH = DEC_SEQ // MERGE_TM


def _merge(o_ctx, o_a, o_b, o_c, gates, h, mods3, w_br, w_out_b, g_post3, g_pre_ffn3, w_router, b_router3, l):
    ctx = lambda n: pl.BlockSpec((MERGE_TM, n), lambda i: (jnp.minimum(i, MERGE_CTX_TILES - 1), 0))
    lat_n = lambda n: pl.BlockSpec((MERGE_TM, n), lambda i: (jnp.maximum(i - MERGE_CTX_TILES, 0), 0))
    lat = lambda: lat_n(512)
    mod_index = lambda i: jnp.where(i < MERGE_CTX_TILES, 0,
                                    1 + (i - MERGE_CTX_TILES) // MERGE_LAT_TILES_PER_BATCH)
    mod = lambda j: pl.BlockSpec((None, 1, D_MODEL), lambda i: (mod_index(i), 0, j))
    lw = lambda: pl.BlockSpec((None, 1, D_MODEL), lambda i: (l, 0, 0))
    row = lambda n: pl.BlockSpec((MERGE_TM, n), lambda i: (i, 0))
    const = lambda shape: pl.BlockSpec(shape, lambda i: (0,) * len(shape), pipeline_mode=pl.Buffered(1))
    return pl.pallas_call(
        _merge_kernel,
        grid=(MERGE_TILES,),
        in_specs=[ctx(1536),
                  lat(), lat(), lat(), row(3072), ctx(D_MODEL), lat_n(D_MODEL),
                  mod(2), mod(3), mod(4),
                  const((3, 512, D_MODEL)), const((D_MODEL, D_MODEL)), lw(), lw(),
                  const((D_MODEL, 2 * N_EXPERTS)),
                  pl.BlockSpec((None, 1, N_EXPERTS), lambda i: (l, 0, 0))],
        out_specs=[row(D_MODEL), row(HALF), row(2 * TOP_K),
                   pl.BlockSpec((8, N_EXPERTS), lambda i: (0, 0))],
        out_shape=[jax.ShapeDtypeStruct((N_TOK, D_MODEL), F32),
                   jax.ShapeDtypeStruct((N_TOK, HALF), jnp.int32),
                   jax.ShapeDtypeStruct((N_TOK, 2 * TOP_K), F32),
                   jax.ShapeDtypeStruct((8, N_EXPERTS), F32)],
        compiler_params=_cparams(("arbitrary",)),
        name="merge",
    )(o_ctx, o_a, o_b, o_c, gates, *h, mods3, mods3, mods3, w_br, w_out_b, g_post3, g_pre_ffn3,
      w_router, b_router3)


def _moe_kernel(l, be_ref, nv_ref, first_ref, slot_ref, nxt_ref, rows_ref, x_ref, wgu_hbm, bgu_ref, wd_hbm, bd_ref, y_ref,
                wgu_f32, wd_f32, wgu_bf, wd_bf, sem):
    i = pl.program_id(0)

    def fetch(e, s):
        return (pltpu.make_async_copy(wgu_hbm.at[l, e], wgu_f32.at[s], sem.at[0, s]),
                pltpu.make_async_copy(wd_hbm.at[l, e], wd_f32.at[s], sem.at[1, s]))

    @pl.when(first_ref[i] == 1)
    def _():
        s = slot_ref[i]

        @pl.when(i == 0)
        def _():
            for cp in fetch(be_ref[i], s):
                cp.start()

        for cp in fetch(be_ref[i], s):
            cp.wait()

        @pl.when(nxt_ref[i] >= 0)
        def _():
            for cp in fetch(nxt_ref[i], 1 - s):
                cp.start()

        wgu_bf[...] = wgu_f32[s].astype(BF16)
        wd_bf[...] = wd_f32[s].astype(BF16)

    @pl.when(i < nv_ref[0])
    def _():
        real = lax.broadcasted_iota(jnp.int32, (MOE_TILE, HALF), 0) < rows_ref[i]
        xa, xb = _unpack_halves(jnp.where(real, x_ref[...], 0))
        x = jnp.concatenate([xa.astype(BF16), xb.astype(BF16)], axis=1)
        b = bgu_ref[...]
        glu = jnp.dot(x, wgu_bf[:, :D_FF], preferred_element_type=F32) + b[:, :D_FF]
        lin = jnp.dot(x, wgu_bf[:, D_FF:], preferred_element_type=F32) + b[:, D_FF:]
        glu = jnp.minimum(glu, SWIGLU_LIMIT)
        lin = jnp.clip(lin, -SWIGLU_LIMIT, SWIGLU_LIMIT)
        act = glu * (1.0 / (1.0 + jnp.exp(-SWIGLU_ALPHA * glu))) * (lin + 1.0)
        y = jnp.dot(act.astype(BF16), wd_bf[...], preferred_element_type=F32) + bd_ref[...]
        y_ref[...] = _pack_halves(y)

    @pl.when(i >= nv_ref[0])
    def _():
        y_ref[...] = jnp.zeros_like(y_ref)


def _moe(plan, x_slots, w_gate_up, b_gate_up4, w_down, b_down4, l):
    grid_spec = pltpu.PrefetchScalarGridSpec(
        num_scalar_prefetch=6,
        grid=(N_MOE_BLOCKS,),
        in_specs=[pl.BlockSpec((MOE_TILE, HALF), lambda i, be, *_: (i, 0)),
                  pl.BlockSpec(memory_space=pl.ANY),
                  pl.BlockSpec((None, None, 1, 2 * D_FF), lambda i, be, *_: (l, be[i], 0, 0)),
                  pl.BlockSpec(memory_space=pl.ANY),
                  pl.BlockSpec((None, None, 1, D_MODEL), lambda i, be, *_: (l, be[i], 0, 0))],
        out_specs=pl.BlockSpec((MOE_TILE, HALF), lambda i, be, *_: (i, 0)),
        scratch_shapes=[pltpu.VMEM((2, D_MODEL, 2 * D_FF), F32), pltpu.VMEM((2, D_FF, D_MODEL), F32),
                        pltpu.VMEM((D_MODEL, 2 * D_FF), BF16), pltpu.VMEM((D_FF, D_MODEL), BF16),
                        pltpu.SemaphoreType.DMA((2, 2))])
    return pl.pallas_call(
        functools.partial(_moe_kernel, l),
        grid_spec=grid_spec,
        out_shape=jax.ShapeDtypeStruct((N_SLOTS, HALF), jnp.int32),
        compiler_params=_cparams(("arbitrary",)),
        name="moe",
    )(*plan, x_slots, w_gate_up, b_gate_up4, w_down, b_down4)


def _combine_kernel(y0_ref, y1_ref, y2_ref, y3_ref, gate_ref, h_ref, g2_ref, gpost_ref, oc_ref, ol_ref):
    i = pl.program_id(0)
    gate = gate_ref[...]
    ffn = None
    for k, y_ref in enumerate((y0_ref, y1_ref, y2_ref, y3_ref)):
        ya, yb = _unpack_halves(y_ref[...])
        term = gate[:, k:k + 1] * jnp.concatenate([ya, yb], axis=1)
        ffn = term if ffn is None else ffn + term
    out = h_ref[...] + g2_ref[...] * (_rms(ffn) * gpost_ref[...])

    @pl.when(i < CTX_TILES)
    def _():
        oc_ref[...] = out

    @pl.when(i >= CTX_TILES)
    def _():
        ol_ref[...] = out


def _combine(y_tok, gate, h2, mods3, g_post_ffn3, l):
    row = lambda n: pl.BlockSpec((TM, n), lambda i: (i, 0))
    choice = lambda k: pl.BlockSpec((TM, HALF), lambda i: (k * N_TILES + i, 0))
    return pl.pallas_call(
        _combine_kernel,
        grid=(N_TILES,),
        in_specs=[choice(0), choice(1), choice(2), choice(3), row(TOP_K), row(D_MODEL),
                  pl.BlockSpec((None, 1, D_MODEL), lambda i: (_mod_index(i), 0, 5)),
                  pl.BlockSpec((None, 1, D_MODEL), lambda i: (l, 0, 0))],
        out_specs=[_ctx_spec(D_MODEL), _lat_spec(D_MODEL)],
        out_shape=[jax.ShapeDtypeStruct((N_CTX, D_MODEL), F32), jax.ShapeDtypeStruct((N_LAT, D_MODEL), F32)],
        compiler_params=_cparams(("arbitrary",)),
        name="combine",
    )(y_tok, y_tok, y_tok, y_tok, gate, h2, mods3, g_post_ffn3)


def _route_kernel(topk_ref, count_ref, dest_ref, gate_ref, base_ref):
    i = pl.program_id(0)

    @pl.when(i == 0)
    def _():
        total = count_ref[...]
        padded = jnp.floor((total + (MOE_TILE - 1.0)) * (1.0 / MOE_TILE)) * MOE_TILE
        before = (lax.broadcasted_iota(jnp.int32, (N_EXPERTS, N_EXPERTS), 0)
                  < lax.broadcasted_iota(jnp.int32, (N_EXPERTS, N_EXPERTS), 1)).astype(F32)
        base_ref[...] = jnp.dot(padded, before, preferred_element_type=F32, precision=lax.Precision.HIGHEST)

    topk = topk_ref[...]
    lane = lax.broadcasted_iota(jnp.int32, (TM, N_EXPERTS), 1).astype(F32)
    hots = [lane == topk[:, k:k + 1] for k in range(TOP_K)]
    vals = [topk[:, TOP_K + k:TOP_K + k + 1] for k in range(TOP_K)]
    chosen = functools.reduce(jnp.add, [h.astype(F32) for h in hots])
    earlier_row = (lax.broadcasted_iota(jnp.int32, (TM, TM), 1)
                   < lax.broadcasted_iota(jnp.int32, (TM, TM), 0)).astype(BF16)
    earlier = jnp.dot(earlier_row, chosen.astype(BF16), preferred_element_type=F32)
    offs = base_ref[0:1, :] + earlier
    dest = [jnp.sum(jnp.where(h, offs, 0.0), axis=1, keepdims=True) for h in hots]
    dest_ref[...] = jnp.concatenate(dest, axis=1).astype(jnp.int32)
    e = [jnp.exp(v - vals[0]) for v in vals]
    den = functools.reduce(jnp.add, e)
    gate_ref[...] = jnp.concatenate(e, axis=1) / den
    base_ref[...] = base_ref[...] + jnp.sum(chosen, axis=0, keepdims=True)


def _route(topk, counts):
    tile = lambda n: pl.BlockSpec((TM, n), lambda i: (i, 0))
    dest, gate = pl.pallas_call(
        _route_kernel,
        grid=(N_TILES,),
        in_specs=[tile(2 * TOP_K), pl.BlockSpec((8, N_EXPERTS), lambda i: (0, 0))],
        out_specs=[tile(TOP_K), tile(TOP_K)],
        out_shape=[jax.ShapeDtypeStruct((N_TOK, TOP_K), jnp.int32),
                   jax.ShapeDtypeStruct((N_TOK, TOP_K), F32)],
        scratch_shapes=[pltpu.VMEM((8, N_EXPERTS), F32)],
        compiler_params=_cparams(("arbitrary",)),
        name="route",
    )(topk, counts)
    counts = counts[0].astype(jnp.int32)
    expert = jnp.arange(N_EXPERTS, dtype=jnp.int32)
    padded = (counts + MOE_TILE - 1) // MOE_TILE * MOE_TILE
    pad_end = jnp.sum(jnp.where(expert[None, :] <= expert[:, None], padded[None, :], 0), axis=1)
    block = jnp.arange(N_MOE_BLOCKS, dtype=jnp.int32)
    block_e = jnp.minimum(jnp.sum((pad_end[None, :] <= block[:, None] * MOE_TILE).astype(jnp.int32), axis=1),
                          N_EXPERTS - 1)
    n_valid = pad_end[-1] // MOE_TILE
    mine = expert[None, :] == block_e[:, None]
    pick = lambda v: jnp.sum(jnp.where(mine, v[None, :], 0), axis=1)
    offset = block * MOE_TILE - pick(pad_end - padded)
    valid = block < n_valid
    first = valid & (offset == 0)
    used = counts > 0
    slot = jnp.sum((used[None, :] & (expert[None, :] < block_e[:, None])).astype(jnp.int32), axis=1) % 2
    nxt = jnp.min(jnp.where(used[None, :] & (expert[None, :] > block_e[:, None]), expert[None, :], N_EXPERTS),
                  axis=1)
    nxt = jnp.where(nxt < N_EXPERTS, nxt, -1)
    rows = jnp.where(valid, jnp.clip(pick(counts) - offset, 0, MOE_TILE), 0)
    plan = tuple(a.astype(jnp.int32) for a in (block_e, n_valid[None], first, slot, nxt, rows))
    n_workers = N_TOK // SCATTER_TOKENS_PER_WORKER
    dest_sc = dest.reshape(n_workers, SCATTER_TOKENS_PER_WORKER // GATHER_CHUNK, GATHER_CHUNK, TOP_K)
    dest_sc = dest_sc.transpose(0, 1, 3, 2).reshape(n_workers, -1, GATHER_CHUNK)
    return gate, dest.T.reshape(-1), dest_sc, plan


def _rope_tables():
    t = jnp.arange(DEC_SEQ)
    inv = ROPE_THETA ** (-jnp.arange(ROPE_PAIRS, dtype=F32) / ROPE_PAIRS)
    row = (t // GRID_W).astype(F32)[:, None] * inv
    col = (t % GRID_W).astype(F32)[:, None] * inv
    zeros = jnp.zeros_like(row)
    cos = jnp.concatenate([jnp.cos(row), jnp.cos(row), jnp.cos(col), jnp.cos(col)], axis=1)
    s1 = jnp.concatenate([-jnp.sin(row), zeros, -jnp.sin(col), zeros], axis=1)
    s2 = jnp.concatenate([zeros, jnp.sin(row), zeros, jnp.sin(col)], axis=1)
    ident = lambda v: jnp.full((TM, HEAD_DIM), v, F32)
    tables = [jnp.concatenate([x, ident(v)], axis=0) for x, v in ((cos, 1.0), (s1, 0.0), (s2, 0.0))]
    return [jnp.tile(x, (1, 2)) for x in tables]


def kernel(x_prompt, x_sample, cache_a, cache_b, cache_c, c, c_ctx, w_ada, b_ada, g_pre_mix, g_post_mix,
           g_pre_ffn, g_post_ffn, w_in, g_q_b, g_k_b, sink_a, rpb_c, w_br_a, w_br_b, w_br_c, w_out,
           w_router, b_router, w_gate_up, b_gate_up, w_down, b_down):
    h = (x_prompt.reshape(N_CTX, D_MODEL), x_sample.reshape(N_LAT, D_MODEL))
    cond8 = jnp.concatenate([c_ctx[None], c, jnp.zeros((3, D_MODEL), F32)], axis=0)
    cache_a = cache_a.astype(BF16).reshape(DEC_BATCH, DEPTH, 2, PAST_LEN, W_KV)
    cache_b = cache_b.astype(BF16).reshape(DEC_BATCH, DEPTH, 2, PAST_LEN, W_KV)
    cache_c = cache_c.astype(BF16).reshape(DEC_BATCH, DEPTH, 2, PAST_LEN, W_HEADS)
    cos_t, s1_t, s2_t = _rope_tables()
    bd = jnp.kron(jnp.eye(256 // HEAD_DIM, dtype=F32),
                  jnp.full((HEAD_DIM, HEAD_DIM), 1.0 / HEAD_DIM, F32)).astype(BF16)
    vec3 = lambda a: a.reshape(DEPTH, 1, a.shape[-1])
    scale = HEAD_DIM ** -0.5
    states = []
    mods_all = _adaln(cond8, w_ada, vec3(b_ada))
    for l in range(DEPTH):
        w = w_in[l]
        pair_cols = lambda a: a.reshape(D_MODEL, 2, 4, HEAD_DIM).transpose(0, 2, 1, 3).reshape(D_MODEL, 512)
        pair_rows = lambda a: a.reshape(2, 4, HEAD_DIM, D_MODEL).transpose(1, 0, 2, 3).reshape(512, D_MODEL)
        w_in_p = jnp.concatenate(
            [pair_cols(w[:, 0:512]) * scale, pair_cols(w[:, 768:1280]), w[:, 1536:2048] * scale,
             w[:, 2048:2560], w[:, 2560:3072],
             w[:, 512:640], w[:, 1280:1408], w[:, 640:768], w[:, 1408:1536], w[:, 3072:]], axis=1).astype(BF16)
        gq = (jnp.tile(g_q_b[l], H_B) * scale)[None]
        gk = jnp.tile(g_k_b[l], KV_B)[None]
        w_br = jnp.stack([pair_rows(w_br_a[l]), pair_rows(w_br_b[l]), w_br_c[l]], axis=0).astype(BF16)
        w_out_b = w_out[l].astype(BF16)
        w_r_hi = w_router[l].astype(BF16)
        w_r_lo = (w_router[l] - w_r_hi.astype(F32)).astype(BF16)
        w_router2 = jnp.concatenate([w_r_hi, w_r_lo], axis=1)

        mods3 = mods_all[l].reshape(8, 1, 6 * D_MODEL)
        q_all, kvab, kvc, gates, *states = _proj(h, mods3, vec3(g_pre_mix), w_in_p, gq, gk, bd,
                                                 cos_t, s1_t, s2_t, tuple(states), l)
        o_ctx = _ctx_attn(sink_a, q_all, kvab, kvc, l)
        o_a = _win_attn(sink_a, q_all, kvab, cache_a, l)
        o_b = _dense_attn(q_all, kvab, cache_b, l)
        o_c = _nbr_attn(q_all, kvc, cache_c, _nbr_bias_table(rpb_c[l]), l)
        h2, hn2, topk, counts = _merge(o_ctx, o_a, o_b, o_c, gates, h, mods3, w_br, w_out_b,
                                       vec3(g_post_mix), vec3(g_pre_ffn), w_router2, vec3(b_router), l)
        gate, dest, dest_sc, plan = _route(topk, counts)
        x_slots = _scatter_rows(hn2, dest_sc, N_SLOTS)
        y_slots = _moe(plan, x_slots, w_gate_up,
                       b_gate_up.reshape(DEPTH, N_EXPERTS, 1, 2 * D_FF), w_down,
                       b_down.reshape(DEPTH, N_EXPERTS, 1, D_MODEL), l)
        y_tok = _gather_rows(y_slots, dest)
        h = _combine(y_tok, gate, h2, mods3, vec3(g_post_ffn), l)

    state_a, state_b, state_c = states
    heads = lambda s, n: s.reshape(BATCH, DEPTH, 2, SEQ, n, HEAD_DIM)
    return (h[0].reshape(BATCH, SEQ, D_MODEL), h[1].reshape(DEC_BATCH, DEC_SEQ, D_MODEL),
            heads(state_a, KV_A), heads(state_b, KV_B), heads(state_c, H_C))
```

```python
import functools

import jax
import jax.numpy as jnp
from jax import lax
from jax.experimental import pallas as pl
from jax.experimental.pallas import tpu as pltpu
from jax.experimental.pallas import tpu_sc as plsc

D_MODEL = 1024
BATCH = 32
SEQ = 256
DEPTH = 2
DEC_BATCH = 4
DEC_SEQ = 2048
PAST_LEN = 512
GRID_W = 64
HEAD_DIM = 64
H_A = 8
KV_A = 2
H_B = 8
KV_B = 2
H_C = 8
WINDOW_A = 128
NA_ROWS = 8
NA_COLS = 16
ROPE_THETA = 10000.0
ROPE_PAIRS = HEAD_DIM // 4
N_EXPERTS = 32
TOP_K = 4
D_FF = D_MODEL
SWIGLU_ALPHA = 1.702
SWIGLU_LIMIT = 7.0
EPS = 1e-6

W_HEADS = H_A * HEAD_DIM
W_KV = KV_A * HEAD_DIM
N_CTX = BATCH * SEQ
N_LAT = DEC_BATCH * DEC_SEQ
N_TOK = N_CTX + N_LAT
GRID_ROWS = DEC_SEQ // GRID_W
D_IN = 3 * W_HEADS + 4 * W_KV + 2 * W_HEADS + 3 * D_MODEL

TM = 256
N_TILES = N_TOK // TM
CTX_TILES = N_CTX // TM
LAT_TILES_PER_BATCH = DEC_SEQ // TM
TQ = 128
MOE_TILE = 256
N_SLOTS = N_TOK * TOP_K + N_EXPERTS * MOE_TILE
N_MOE_BLOCKS = N_SLOTS // MOE_TILE
NEG = -1e30
VMEM_LIMIT = 56 * 1024 * 1024

BF16 = jnp.bfloat16
F32 = jnp.float32


def _cparams(sem):
    return pltpu.CompilerParams(dimension_semantics=sem, vmem_limit_bytes=VMEM_LIMIT)


def _mod_index(i):
    return jnp.where(i < CTX_TILES, 0, 1 + (i - CTX_TILES) // LAT_TILES_PER_BATCH)


def _rms(x):
    return x * lax.rsqrt(jnp.mean(x * x, axis=-1, keepdims=True) + EPS)


HALF = D_MODEL // 2


def _pack_halves(x):
    hi = lax.bitcast_convert_type(x[:, :HALF].astype(BF16).astype(F32), jnp.uint32)
    lo = lax.bitcast_convert_type(x[:, HALF:].astype(BF16).astype(F32), jnp.uint32)
    return lax.bitcast_convert_type(hi | (lo >> 16), jnp.int32)


def _unpack_halves(w):
    u = lax.bitcast_convert_type(w, jnp.uint32)
    return (lax.bitcast_convert_type(u & jnp.uint32(0xFFFF0000), F32),
            lax.bitcast_convert_type(u << 16, F32))


GATHER_CHUNK = 64


def _gather_rows(table, idx):
    n = idx.shape[0]
    width = table.shape[1]
    info = plsc.get_sparse_core_info()
    n_workers = info.num_cores * info.num_subcores
    per_worker = n // n_workers
    n_chunks = per_worker // GATHER_CHUNK
    assert per_worker * n_workers == n and n_chunks * GATHER_CHUNK == per_worker and n_chunks % 2 == 0
    mesh = plsc.VectorSubcoreMesh(core_axis_name="core", subcore_axis_name="subcore")

    @functools.partial(
        pl.kernel, out_type=jax.ShapeDtypeStruct((n, width), table.dtype), mesh=mesh,
        scratch_types=[pltpu.VMEM((per_worker,), jnp.int32),
                       pltpu.VMEM((2, GATHER_CHUNK, width), table.dtype),
                       pltpu.SemaphoreType.DMA((2,)), pltpu.SemaphoreType.DMA((2,))])
    def gather(table_hbm, idx_hbm, out_hbm, idx_v, rows_v, gather_sem, write_sem):
        worker = lax.axis_index("subcore") * info.num_cores + lax.axis_index("core")
        base = worker * per_worker
        pltpu.sync_copy(idx_hbm.at[pl.ds(base, per_worker)], idx_v)

        def fetch(chunk, slot):
            rows = idx_v.at[pl.ds(chunk * GATHER_CHUNK, GATHER_CHUNK)]
            return pltpu.make_async_copy(table_hbm.at[rows], rows_v.at[slot], gather_sem.at[slot])

        def write(chunk, slot):
            dst = out_hbm.at[pl.ds(base + chunk * GATHER_CHUNK, GATHER_CHUNK)]
            return pltpu.make_async_copy(rows_v.at[slot], dst, write_sem.at[slot])

        fetch(0, 0).start()

        @pl.loop(0, n_chunks, step=2)
        def _(c):
            @pl.when(c > 0)
            def _():
                write(c - 1, 1).wait()

            fetch(c + 1, 1).start()
            fetch(c, 0).wait()
            write(c, 0).start()
            write(c, 0).wait()

            @pl.when(c + 2 < n_chunks)
            def _():
                fetch(c + 2, 0).start()

            fetch(c + 1, 1).wait()
            write(c + 1, 1).start()

        write(n_chunks - 1, 1).wait()

    return gather(table, idx)


SC_WORKERS_V7X = 32
SCATTER_TOKENS_PER_WORKER = N_TOK // SC_WORKERS_V7X


def _scatter_rows(table, dest_sc, n_out):
    width = table.shape[1]
    info = plsc.get_sparse_core_info()
    assert info.num_cores * info.num_subcores == SC_WORKERS_V7X
    per_worker = SCATTER_TOKENS_PER_WORKER
    n_chunks = per_worker // GATHER_CHUNK
    assert n_chunks % 2 == 0 and dest_sc.shape == (SC_WORKERS_V7X, n_chunks * TOP_K, GATHER_CHUNK)
    mesh = plsc.VectorSubcoreMesh(core_axis_name="core", subcore_axis_name="subcore")

    @functools.partial(
        pl.kernel, out_type=jax.ShapeDtypeStruct((n_out, width), table.dtype), mesh=mesh,
        scratch_types=[pltpu.VMEM((n_chunks * TOP_K, GATHER_CHUNK), jnp.int32),
                       pltpu.VMEM((2, GATHER_CHUNK, width), table.dtype),
                       pltpu.SemaphoreType.DMA((2,)), pltpu.SemaphoreType.DMA((2,))])
    def scatter(table_hbm, dest_hbm, out_hbm, idx_v, rows_v, read_sem, write_sem):
        worker = lax.axis_index("subcore") * info.num_cores + lax.axis_index("core")
        base = worker * per_worker
        pltpu.sync_copy(dest_hbm.at[worker], idx_v)

        def read(chunk, slot):
            src = table_hbm.at[pl.ds(base + chunk * GATHER_CHUNK, GATHER_CHUNK)]
            return pltpu.make_async_copy(src, rows_v.at[slot], read_sem.at[slot])

        def writes(chunk, slot):
            return [pltpu.make_async_copy(rows_v.at[slot], out_hbm.at[idx_v.at[chunk * TOP_K + k]],
                                          write_sem.at[slot]) for k in range(TOP_K)]

        read(0, 0).start()

        @pl.loop(0, n_chunks, step=2)
        def _(c):
            @pl.when(c > 0)
            def _():
                for cp in writes(c - 1, 1):
                    cp.wait()

            read(c + 1, 1).start()
            read(c, 0).wait()
            for cp in writes(c, 0):
                cp.start()
            for cp in writes(c, 0):
                cp.wait()

            @pl.when(c + 2 < n_chunks)
            def _():
                read(c + 2, 0).start()

            read(c + 1, 1).wait()
            for cp in writes(c + 1, 1):
                cp.start()

        for cp in writes(n_chunks - 1, 1):
            cp.wait()

    return scatter(table, dest_sc)


def _adaln_kernel(c_ref, w_ref, b_ref, o_ref):
    c = c_ref[...]
    s = c / (1.0 + jnp.exp(-c))
    o_ref[...] = jnp.dot(s, w_ref[...], preferred_element_type=F32,
                         precision=lax.Precision.HIGHEST) + b_ref[...]


def _adaln(cond8, w_ada, b_ada3):
    tn = 768
    return pl.pallas_call(
        _adaln_kernel,
        grid=(DEPTH, 6 * D_MODEL // tn),
        in_specs=[pl.BlockSpec((8, D_MODEL), lambda l, j: (0, 0)),
                  pl.BlockSpec((None, D_MODEL, tn), lambda l, j: (l, 0, j)),
                  pl.BlockSpec((None, 1, tn), lambda l, j: (l, 0, j))],
        out_specs=pl.BlockSpec((None, 8, tn), lambda l, j: (l, 0, j)),
        out_shape=jax.ShapeDtypeStruct((DEPTH, 8, 6 * D_MODEL), F32),
        compiler_params=_cparams(("arbitrary", "arbitrary")),
        name="adaln",
    )(cond8, w_ada, b_ada3)


C_QA, C_QB, C_QC, C_KC, C_VC, C_KAB, C_GL = 0, 512, 1024, 1536, 2048, 2560, 3072


def _ctx_spec(n):
    return pl.BlockSpec((TM, n), lambda i: (jnp.minimum(i, CTX_TILES - 1), 0))


def _lat_spec(n):
    return pl.BlockSpec((TM, n), lambda i: (jnp.maximum(i - CTX_TILES, 0), 0))


def _proj_kernel(hc_ref, hl_ref, sh_ref, sc_ref, gpre_ref, w_ref, gq_ref, gk_ref, bd_ref,
                 cos_ref, s1_ref, s2_ref, *rest):
    q_ref, kvab_ref, kvc_ref, gates_ref, sta_ref, stb_ref, stc_ref = rest[-7:]
    i = pl.program_id(0)
    hn = _rms(jnp.where(i < CTX_TILES, hc_ref[...], hl_ref[...])) * gpre_ref[...]
    hb = (hn * (1.0 + sc_ref[...]) + sh_ref[...]).astype(BF16)
    cos, s1, s2 = cos_ref[...], s1_ref[...], s2_ref[...]
    bd = bd_ref[...]

    def proj(c0, n):
        return jnp.dot(hb, w_ref[:, c0:c0 + n], preferred_element_type=F32)

    def rope(t):
        parts = []
        for g in range(t.shape[1] // 128):
            tg = t[:, g * 128:(g + 1) * 128]
            parts.append(tg * cos + pltpu.roll(tg, 112, 1) * s1 + pltpu.roll(tg, 16, 1) * s2)
        return parts[0] if len(parts) == 1 else jnp.concatenate(parts, axis=1)

    def headnorm(t, g):
        sq = (t * t).astype(BF16)
        n = t.shape[1]
        if n == 128:
            ms = jnp.dot(sq, bd[:128, :128], preferred_element_type=F32)
        else:
            ms = jnp.concatenate(
                [jnp.dot(sq[:, c:c + 256], bd, preferred_element_type=F32) for c in range(0, n, 256)],
                axis=1)
        return t * lax.rsqrt(ms + EPS) * g

    q_ref[:, 0:512] = rope(proj(C_QA, 512)).astype(BF16)
    q_ref[:, 512:1024] = rope(headnorm(proj(C_QB, 512), gq_ref[...])).astype(BF16)
    q_ref[:, 1024:1536] = proj(C_QC, 512).astype(BF16)
    kc = proj(C_KC, 512)
    vc = proj(C_VC, 512)
    kvc_ref[:, 0:512] = kc.astype(BF16)
    kvc_ref[:, 512:1024] = vc.astype(BF16)
    kab = proj(C_KAB, 512)
    ka = kab[:, 0:128]
    kb = headnorm(kab[:, 128:256], gk_ref[...])
    kvab_ref[:, 0:128] = rope(ka).astype(BF16)
    kvab_ref[:, 128:256] = rope(kb).astype(BF16)
    kvab_ref[:, 256:512] = kab[:, 256:512].astype(BF16)
    for j in range(6):
        gl = proj(C_GL + j * 512, 512)
        gates_ref[:, j * 512:(j + 1) * 512] = (1.0 / (1.0 + jnp.exp(-gl))).astype(BF16)

    @pl.when(i < CTX_TILES)
    def _():
        for st_ref, k, v in ((sta_ref, ka, kab[:, 256:384]), (stb_ref, kb, kab[:, 384:512]), (stc_ref, kc, vc)):
            if len(st_ref.shape) == 4:
                st_ref[0, 0] = k
                st_ref[0, 1] = v
                st_ref[1:] = jnp.zeros((DEPTH - 1,) + tuple(st_ref.shape[1:]), F32)
            else:
                st_ref[0] = k
                st_ref[1] = v


def _proj(h, mods3, g_pre3, w_in_p, gq, gk, bd, cos_t, s1_t, s2_t, prev_states, l):
    def rope_idx(i):
        return jnp.where(i < CTX_TILES, LAT_TILES_PER_BATCH, (i - CTX_TILES) % LAT_TILES_PER_BATCH)

    const = lambda shape: pl.BlockSpec(shape, lambda i: (0,) * len(shape), pipeline_mode=pl.Buffered(1))
    rope_spec = pl.BlockSpec((TM, 128), lambda i: (rope_idx(i), 0))
    row = lambda n: pl.BlockSpec((TM, n), lambda i: (i, 0))
    if l == 0:
        state_spec = lambda n: pl.BlockSpec((None, DEPTH, 2, SEQ, n),
                                            lambda i: (jnp.minimum(i, CTX_TILES - 1), 0, 0, 0, 0))
    else:
        state_spec = lambda n: pl.BlockSpec((None, None, 2, SEQ, n),
                                            lambda i: (jnp.minimum(i, CTX_TILES - 1), l, 0, 0, 0))
    state_shape = lambda n: jax.ShapeDtypeStruct((BATCH, DEPTH, 2, SEQ, n), F32)
    n_in = 12
    return pl.pallas_call(
        _proj_kernel,
        grid=(N_TILES,),
        in_specs=[_ctx_spec(D_MODEL), _lat_spec(D_MODEL),
                  pl.BlockSpec((None, 1, D_MODEL), lambda i: (_mod_index(i), 0, 0)),
                  pl.BlockSpec((None, 1, D_MODEL), lambda i: (_mod_index(i), 0, 1)),
                  pl.BlockSpec((None, 1, D_MODEL), lambda i: (l, 0, 0)),
                  const((D_MODEL, D_IN)), const((1, 512)), const((1, 128)), const((256, 256)),
                  rope_spec, rope_spec, rope_spec] + [pl.BlockSpec(memory_space=pl.ANY)] * len(prev_states),
        out_specs=[row(1536), row(512), row(1024), row(3072),
                   state_spec(W_KV), state_spec(W_KV), state_spec(W_HEADS)],
        out_shape=[jax.ShapeDtypeStruct((N_TOK, 1536), BF16),
                   jax.ShapeDtypeStruct((N_TOK, 512), BF16),
                   jax.ShapeDtypeStruct((N_TOK, 1024), BF16),
                   jax.ShapeDtypeStruct((N_TOK, 3072), BF16),
                   state_shape(W_KV), state_shape(W_KV), state_shape(W_HEADS)],
        input_output_aliases={n_in + j: 4 + j for j in range(len(prev_states))},
        compiler_params=_cparams(("arbitrary",)),
        name="proj",
    )(*h, mods3, mods3, g_pre3, w_in_p, gq, gk, bd, cos_t, s1_t, s2_t, *prev_states)


def _qk(q, k):
    return lax.dot_general(q, k, (((1,), (1,)), ((), ())), preferred_element_type=F32)


def _softmax_pv(scores, values, sink=None, halves=None):
    m = functools.reduce(jnp.maximum, [jnp.max(s, axis=-1, keepdims=True) for s in scores])
    if sink is not None:
        m = jnp.maximum(m, sink)
    if halves is None:
        ps = [jnp.exp(s - m) for s in scores]
        den = functools.reduce(jnp.add, [jnp.sum(p, axis=-1, keepdims=True) for p in ps])
        if sink is not None:
            den = den + jnp.exp(sink - m)
        o = functools.reduce(jnp.add, [jnp.dot(p.astype(BF16), v, preferred_element_type=F32)
                                       for p, v in zip(ps, values)])
        return o / den
    ps = [jnp.exp(s - m).astype(BF16) for s in scores]
    rows = ps[0].shape[0]
    split = {"both": rows // 2, "lo": rows, "hi": 0}[halves]
    one = jnp.ones((), BF16)

    def pv(r0, r1, keep_lo):
        acc = None
        for p, v in zip(ps, values):
            lo = _lo_lanes(v.shape[0])
            v1 = jnp.where(lo, v, one) if keep_lo else jnp.where(lo, one, v)
            t = jnp.dot(p[r0:r1], v1, preferred_element_type=F32)
            acc = t if acc is None else acc + t
        return acc

    parts = ([pv(0, split, True)] if split > 0 else []) + ([pv(split, rows, False)] if split < rows else [])
    o = parts[0] if len(parts) == 1 else jnp.concatenate(parts, axis=0)
    if sink is not None:
        is_lo_row = lax.broadcasted_iota(jnp.int32, (rows, 128), 0) < split
        o = o + jnp.where(is_lo_row != _lo_lanes(rows), jnp.exp(sink - m), 0.0)
    return o * pltpu.roll(1.0 / o, HEAD_DIM, 1)


def _lo_lanes(rows):
    return lax.broadcasted_iota(jnp.int32, (rows, 128), 1) < HEAD_DIM


def _stack_pairs(q, n_pairs):
    lo = _lo_lanes(q.shape[0])
    zero = jnp.zeros((q.shape[0], 128), q.dtype)
    pairs = [q[:, p * 128:(p + 1) * 128] for p in range(n_pairs)]
    return jnp.concatenate([jnp.where(lo, x, zero) for x in pairs] + [jnp.where(lo, zero, x) for x in pairs],
                           axis=0)


def _unstack_pairs(o, n_pairs):
    rows = o.shape[0] // (2 * n_pairs)
    lo = _lo_lanes(rows)
    return jnp.concatenate(
        [jnp.where(lo, o[p * rows:(p + 1) * rows], o[(n_pairs + p) * rows:(n_pairs + p + 1) * rows])
         for p in range(n_pairs)], axis=1)


def _sink_column(sink_ref, l, rows):
    return jnp.concatenate([jnp.full((rows, 1), sink_ref[l, h], F32) for h in range(H_A)], axis=0)


def _ctx_attn_kernel(l, sink_ref, q_ref, kvab_ref, kvc_ref, o_ref):
    qa = _stack_pairs(q_ref[:, 0:512], 4)
    o = _softmax_pv([_qk(qa, kvab_ref[:, 0:128])], [kvab_ref[:, 256:384]], _sink_column(sink_ref, l, SEQ))
    o_ref[:, 0:512] = _unstack_pairs(o, 4).astype(BF16)
    qb = _stack_pairs(q_ref[:, 512:1024], 4)
    o = _softmax_pv([_qk(qb, kvab_ref[:, 128:256])], [kvab_ref[:, 384:512]])
    o_ref[:, 512:1024] = _unstack_pairs(o, 4).astype(BF16)
    for hp in range(H_C // 2):
        cs = slice(hp * 128, (hp + 1) * 128)
        qc = _stack_pairs(q_ref[:, 1024 + hp * 128:1024 + (hp + 1) * 128], 1)
        o = _softmax_pv([_qk(qc, kvc_ref[:, cs])], [kvc_ref[:, 512 + hp * 128:512 + (hp + 1) * 128]])
        o_ref[:, 1024 + hp * 128:1024 + (hp + 1) * 128] = _unstack_pairs(o, 1).astype(BF16)


def _ctx_attn(sink_a, q_all, kvab, kvc, l):
    row = lambda n: pl.BlockSpec((SEQ, n), lambda b: (b, 0))
    return pl.pallas_call(
        functools.partial(_ctx_attn_kernel, l),
        grid=(BATCH,),
        in_specs=[pl.BlockSpec(memory_space=pltpu.SMEM), row(1536), row(512), row(1024)],
        out_specs=row(1536),
        out_shape=jax.ShapeDtypeStruct((N_CTX, 1536), BF16),
        compiler_params=_cparams(("arbitrary",)),
        name="ctx_attn",
    )(sink_a, q_all, kvab, kvc)


def _win_attn_kernel(l, sink_ref, q_ref, prev_ref, cur_ref, nxt_ref, ck_ref, cv_ref, o_ref):
    n = pl.program_id(1)
    nb = DEC_SEQ // TQ
    rows = H_A * TQ
    qpos = lax.broadcasted_iota(jnp.int32, (rows, TQ), 0) % TQ
    kpos = lax.broadcasted_iota(jnp.int32, (rows, TQ), 1)
    mask_prev = (kpos >= qpos) & (n > 0)
    mask_next = (kpos <= qpos) & (n < nb - 1)
    ks, vs = slice(0, 128), slice(256, 384)
    qs = _stack_pairs(q_ref[...], 4)
    s_prev = jnp.where(mask_prev, _qk(qs, prev_ref[:, ks]), NEG)
    s_cur = _qk(qs, cur_ref[:, ks])
    s_next = jnp.where(mask_next, _qk(qs, nxt_ref[:, ks]), NEG)
    s_ctx = _qk(qs, ck_ref[...])
    o = _softmax_pv([s_prev, s_cur, s_next, s_ctx],
                    [prev_ref[:, vs], cur_ref[:, vs], nxt_ref[:, vs], cv_ref[...]],
                    _sink_column(sink_ref, l, TQ), halves="both")
    o_ref[...] = _unstack_pairs(o, 4).astype(BF16)


def _win_attn(sink_a, q_all, kvab, cache_a, l):
    nb = DEC_SEQ // TQ
    base = N_CTX // TQ
    kv_spec = lambda f: pl.BlockSpec((TQ, 512), lambda b, n: (base + b * nb + f(n), 0))
    cache_spec = lambda s: pl.BlockSpec((None, None, None, PAST_LEN, W_KV), lambda b, n: (b, l, s, 0, 0))
    return pl.pallas_call(
        functools.partial(_win_attn_kernel, l),
        grid=(DEC_BATCH, nb),
        in_specs=[pl.BlockSpec(memory_space=pltpu.SMEM),
                  pl.BlockSpec((TQ, 512), lambda b, n: (base + b * nb + n, 0)),
                  kv_spec(lambda n: jnp.maximum(n - 1, 0)), kv_spec(lambda n: n),
                  kv_spec(lambda n: jnp.minimum(n + 1, nb - 1)),
                  cache_spec(0), cache_spec(1)],
        out_specs=pl.BlockSpec((TQ, 512), lambda b, n: (b * nb + n, 0)),
        out_shape=jax.ShapeDtypeStruct((N_LAT, 512), BF16),
        compiler_params=_cparams(("arbitrary", "arbitrary")),
        name="win_attn",
    )(sink_a, q_all, kvab, kvab, kvab, cache_a, cache_a)


DENSE_KEY_CHUNK = 1024
DENSE_TQ = 256


def _online_softmax_pv(q, key_chunks, value_chunks, keep_lo):
    one = jnp.ones((), BF16)
    m = acc = None
    for k, v in zip(key_chunks, value_chunks):
        s = _qk(q, k)
        m_new = jnp.max(s, axis=-1, keepdims=True)
        if m is not None:
            m_new = jnp.maximum(m, m_new)
        p = jnp.exp(s - m_new).astype(BF16)
        lo = _lo_lanes(v.shape[0])
        pv = jnp.dot(p, jnp.where(lo, v, one) if keep_lo else jnp.where(lo, one, v), preferred_element_type=F32)
        acc = pv if acc is None else acc * jnp.exp(m - m_new) + pv
        m = m_new
    return acc * pltpu.roll(1.0 / acc, HEAD_DIM, 1)


def _dense_attn_kernel(q_ref, kv_ref, ck_ref, cv_ref, o_ref):
    qs = _stack_pairs(q_ref[...], 4)
    half = qs.shape[0] // 2
    starts = range(0, DEC_SEQ, DENSE_KEY_CHUNK)
    keys = [kv_ref[c:c + DENSE_KEY_CHUNK, 128:256] for c in starts] + [ck_ref[...]]
    values = [kv_ref[c:c + DENSE_KEY_CHUNK, 384:512] for c in starts] + [cv_ref[...]]
    outs = [_online_softmax_pv(qs[g * half:(g + 1) * half], keys, values, g == 0) for g in range(KV_B)]
    o_ref[...] = _unstack_pairs(jnp.concatenate(outs, axis=0), 4).astype(BF16)


def _dense_attn(q_all, kvab, cache_b, l):
    nb = DEC_SEQ // DENSE_TQ
    base = N_CTX // DENSE_TQ
    cache_spec = lambda s: pl.BlockSpec((None, None, None, PAST_LEN, W_KV), lambda b, n: (b, l, s, 0, 0))
    return pl.pallas_call(
        _dense_attn_kernel,
        grid=(DEC_BATCH, nb),
        in_specs=[pl.BlockSpec((DENSE_TQ, 512), lambda b, n: (base + b * nb + n, 1)),
                  pl.BlockSpec((DEC_SEQ, 512), lambda b, n: (N_CTX // DEC_SEQ + b, 0)),
                  cache_spec(0), cache_spec(1)],
        out_specs=pl.BlockSpec((DENSE_TQ, 512), lambda b, n: (b * nb + n, 0)),
        out_shape=jax.ShapeDtypeStruct((N_LAT, 512), BF16),
        compiler_params=_cparams(("arbitrary", "arbitrary")),
        name="dense_attn",
    )(q_all, kvab, cache_b, cache_b)


NBR_BAND = 4
NBR_Q = NBR_BAND * GRID_W
NBR_WIN_ROWS = 12
NBR_N_BANDS = GRID_ROWS // NBR_BAND
NBR_KBLK = NBR_Q
NBR_WIN_BLOCKS = NBR_WIN_ROWS * GRID_W // NBR_KBLK
NBR_LAST_KB = (GRID_ROWS - NBR_WIN_ROWS) * GRID_W // NBR_KBLK


def _nbr_window_block(band):
    return jnp.clip(band - 1, 0, NBR_LAST_KB)


def _nbr_attn_kernel(q_ref, k0_ref, k1_ref, k2_ref, ck_ref, cv_ref, bias_ref, o_ref):
    k_refs = (k0_ref, k1_ref, k2_ref)
    for hp in range(H_C // 2):
        cs = slice(hp * 128, (hp + 1) * 128)
        vs = slice(512 + hp * 128, 512 + (hp + 1) * 128)
        qs = _stack_pairs(q_ref[:, cs], 1)
        bias = jnp.concatenate([bias_ref[2 * hp], bias_ref[2 * hp + 1]], axis=0)
        scores = [_qk(qs, kr[:, cs]) + bias[:, j * NBR_KBLK:(j + 1) * NBR_KBLK] for j, kr in enumerate(k_refs)]
        scores.append(_qk(qs, ck_ref[:, cs]))
        o = _softmax_pv(scores, [kr[:, vs] for kr in k_refs] + [cv_ref[:, cs]])
        o_ref[:, cs] = _unstack_pairs(o, 1).astype(BF16)


def _nbr_attn(q_all, kvc, cache_c, bias_t, l):
    q_base = N_CTX // NBR_Q
    k_base = N_CTX // NBR_KBLK
    blocks_per_batch = DEC_SEQ // NBR_KBLK
    cache_spec = lambda s: pl.BlockSpec((None, None, None, PAST_LEN, W_HEADS), lambda band, b: (b, l, s, 0, 0))
    key_spec = lambda j: pl.BlockSpec(
        (NBR_KBLK, 1024), lambda band, b: (k_base + b * blocks_per_batch + _nbr_window_block(band) + j, 0))
    band_type = lambda band: jnp.where(band == 0, 0, jnp.where(band == NBR_N_BANDS - 1, 2, 1))
    return pl.pallas_call(
        _nbr_attn_kernel,
        grid=(NBR_N_BANDS, DEC_BATCH),
        in_specs=[pl.BlockSpec((NBR_Q, 512), lambda band, b: (q_base + b * NBR_N_BANDS + band, 2)),
                  key_spec(0), key_spec(1), key_spec(2), cache_spec(0), cache_spec(1),
                  pl.BlockSpec((None, H_C, NBR_Q, NBR_WIN_ROWS * GRID_W),
                               lambda band, b: (band_type(band), 0, 0, 0))],
        out_specs=pl.BlockSpec((NBR_Q, 512), lambda band, b: (b * NBR_N_BANDS + band, 0)),
        out_shape=jax.ShapeDtypeStruct((N_LAT, 512), BF16),
        compiler_params=_cparams(("arbitrary", "arbitrary")),
        name="nbr_attn",
    )(q_all, kvc, kvc, kvc, cache_c, cache_c, bias_t)


def _nbr_bias_table(rpb_l):
    c = jnp.arange(GRID_W)[:, None]
    kc = jnp.arange(GRID_W)[None, :]
    c_start = jnp.clip(c - NA_COLS // 2, 0, GRID_W - NA_COLS)
    valid = (kc >= c_start) & (kc < c_start + NA_COLS)
    pad = GRID_W - NA_COLS
    rpb_pad = jnp.pad(rpb_l.astype(F32), ((0, 0), (0, 0), (pad, pad)))
    toeplitz = jnp.stack([rpb_pad[:, :, GRID_W - 1 - q:2 * GRID_W - 1 - q] for q in range(GRID_W)], axis=2)
    t = jnp.where(valid[None, None], toeplitz, NEG)
    neg = jnp.full((H_C, GRID_W, GRID_W), NEG, F32)
    tables = []
    for r0 in (0, NBR_BAND, GRID_ROWS - NBR_BAND):
        k0 = min(max(r0 - NA_ROWS // 2, 0), GRID_ROWS - NBR_WIN_ROWS)
        rows = []
        for dq in range(NBR_BAND):
            r = r0 + dq
            start = min(max(r - NA_ROWS // 2, 0), GRID_ROWS - NA_ROWS)
            cols = []
            for i in range(NBR_WIN_ROWS):
                kr = k0 + i
                cols.append(t[:, kr - r + NA_ROWS - 1] if start <= kr < start + NA_ROWS else neg)
            rows.append(jnp.concatenate(cols, axis=2))
        tables.append(jnp.concatenate(rows, axis=1))
    return jnp.stack(tables, axis=0)


def _merge_kernel(octx_ref, oa_ref, ob_ref, oc_ref, gates_ref, hc_ref, hl_ref, g1_ref, sh2_ref, sc2_ref,
                  wbr_ref, wout_ref, gpost_ref, gpre_ref, wr_ref, br_ref,
                  h2_ref, hn2_ref, topk_ref, count_ref):
    i = pl.program_id(0)
    is_ctx = i < MERGE_CTX_TILES
    chosen = None
    for r0 in range(0, MERGE_TM, TM):
        rs = slice(r0, r0 + TM)
        merged = None
        for j, lat_ref in enumerate((oa_ref, ob_ref, oc_ref)):
            o = jnp.where(is_ctx, octx_ref[rs, j * 512:(j + 1) * 512], lat_ref[rs, :])
            br = jnp.dot(o, wbr_ref[j], preferred_element_type=F32)
            term = gates_ref[rs, j * D_MODEL:(j + 1) * D_MODEL].astype(F32) * br
            merged = term if merged is None else merged + term
        t = jnp.dot(merged.astype(BF16), wout_ref[...], preferred_element_type=F32)
        h2 = jnp.where(is_ctx, hc_ref[rs, :], hl_ref[rs, :]) + g1_ref[...] * (_rms(t) * gpost_ref[...])
        h2_ref[rs, :] = h2
        hn2 = _rms(h2) * gpre_ref[...] * (1.0 + sc2_ref[...]) + sh2_ref[...]
        hn2_ref[rs, :] = _pack_halves(hn2)
        x_hi = hn2.astype(BF16)
        x_lo = (hn2 - x_hi.astype(F32)).astype(BF16)
        hi = jnp.dot(x_hi, wr_ref[...], preferred_element_type=F32)
        lo = jnp.dot(x_lo, wr_ref[:, :N_EXPERTS], preferred_element_type=F32)
        logits = hi[:, :N_EXPERTS] + (hi[:, N_EXPERTS:] + lo) + br_ref[...]
        lane = lax.broadcasted_iota(jnp.int32, (TM, N_EXPERTS), 1)
        idxs, vals = [], []
        for _ in range(TOP_K):
            idx = jnp.argmax(logits, axis=1, keepdims=True).astype(jnp.int32)
            hot = lane == idx
            idxs.append(idx.astype(F32))
            vals.append(jnp.max(logits, axis=1, keepdims=True))
            chosen = hot.astype(F32) if chosen is None else chosen + hot.astype(F32)
            logits = jnp.where(hot, -jnp.inf, logits)
        topk_ref[rs, :] = jnp.concatenate(idxs + vals, axis=1)

    @pl.when(i == 0)
    def _():
        count_ref[...] = jnp.zeros_like(count_ref)

    count_ref[...] = count_ref[...] + jnp.sum(chosen, axis=0, keepdims=True)


MERGE_TM = 2 * TM
MERGE_TILES = N_TOK // MERGE_TM
MERGE_CTX_TILES = N_CTX // MERGE_TM
MERGE_LAT_TILES_PER_BATCH = DEC_SEQ // MERGE_TM


def _merge(o_ctx, o_a, o_b, o_c, gates, h, mods3, w_br, w_out_b, g_post3, g_pre_ffn3, w_router, b_router3, l):
    ctx = lambda n: pl.BlockSpec((MERGE_TM, n), lambda i: (jnp.minimum(i, MERGE_CTX_TILES - 1), 0))
    lat_n = lambda n: pl.BlockSpec((MERGE_TM, n), lambda i: (jnp.maximum(i - MERGE_CTX_TILES, 0), 0))
    lat = lambda: lat_n(512)
    mod_index = lambda i: jnp.where(i < MERGE_CTX_TILES, 0,
                                    1 + (i - MERGE_CTX_TILES) // MERGE_LAT_TILES_PER_BATCH)
    mod = lambda j: pl.BlockSpec((None, 1, D_MODEL), lambda i: (mod_index(i), 0, j))
    lw = lambda: pl.BlockSpec((None, 1, D_MODEL), lambda i: (l, 0, 0))
    row = lambda n: pl.BlockSpec((MERGE_TM, n), lambda i: (i, 0))
    const = lambda shape: pl.BlockSpec(shape, lambda i: (0,) * len(shape), pipeline_mode=pl.Buffered(1))
    return pl.pallas_call(
        _merge_kernel,
        grid=(MERGE_TILES,),
        in_specs=[ctx(1536),
                  lat(), lat(), lat(), row(3072), ctx(D_MODEL), lat_n(D_MODEL),
                  mod(2), mod(3), mod(4),
                  const((3, 512, D_MODEL)), const((D_MODEL, D_MODEL)), lw(), lw(),
                  const((D_MODEL, 2 * N_EXPERTS)),
                  pl.BlockSpec((None, 1, N_EXPERTS), lambda i: (l, 0, 0))],
        out_specs=[row(D_MODEL), row(HALF), row(2 * TOP_K),
                   pl.BlockSpec((8, N_EXPERTS), lambda i: (0, 0))],
        out_shape=[jax.ShapeDtypeStruct((N_TOK, D_MODEL), F32),
                   jax.ShapeDtypeStruct((N_TOK, HALF), jnp.int32),
                   jax.ShapeDtypeStruct((N_TOK, 2 * TOP_K), F32),
                   jax.ShapeDtypeStruct((8, N_EXPERTS), F32)],
        compiler_params=_cparams(("arbitrary",)),
        name="merge",
    )(o_ctx, o_a, o_b, o_c, gates, *h, mods3, mods3, mods3, w_br, w_out_b, g_post3, g_pre_ffn3,
      w_router, b_router3)


def _moe_kernel(l, be_ref, nv_ref, first_ref, slot_ref, nxt_ref, rows_ref, x_ref, wgu_hbm, bgu_ref, wd_hbm, bd_ref, y_ref,
                wgu_f32, wd_f32, wgu_bf, wd_bf, sem):
    i = pl.program_id(0)

    def fetch(e, s):
        return (pltpu.make_async_copy(wgu_hbm.at[l, e], wgu_f32.at[s], sem.at[0, s]),
                pltpu.make_async_copy(wd_hbm.at[l, e], wd_f32.at[s], sem.at[1, s]))

    @pl.when(first_ref[i] == 1)
    def _():
        s = slot_ref[i]

        @pl.when(i == 0)
        def _():
            for cp in fetch(be_ref[i], s):
                cp.start()

        for cp in fetch(be_ref[i], s):
            cp.wait()

        @pl.when(nxt_ref[i] >= 0)
        def _():
            for cp in fetch(nxt_ref[i], 1 - s):
                cp.start()

        wgu_bf[...] = wgu_f32[s].astype(BF16)
        wd_bf[...] = wd_f32[s].astype(BF16)

    @pl.when(i < nv_ref[0])
    def _():
        real = lax.broadcasted_iota(jnp.int32, (MOE_TILE, HALF), 0) < rows_ref[i]
        xa, xb = _unpack_halves(jnp.where(real, x_ref[...], 0))
        x = jnp.concatenate([xa.astype(BF16), xb.astype(BF16)], axis=1)
        b = bgu_ref[...]
        glu = jnp.dot(x, wgu_bf[:, :D_FF], preferred_element_type=F32) + b[:, :D_FF]
        lin = jnp.dot(x, wgu_bf[:, D_FF:], preferred_element_type=F32) + b[:, D_FF:]
        glu = jnp.minimum(glu, SWIGLU_LIMIT)
        lin = jnp.clip(lin, -SWIGLU_LIMIT, SWIGLU_LIMIT)
        act = glu * (1.0 / (1.0 + jnp.exp(-SWIGLU_ALPHA * glu))) * (lin + 1.0)
        y = jnp.dot(act.astype(BF16), wd_bf[...], preferred_element_type=F32) + bd_ref[...]
        y_ref[...] = _pack_halves(y)

    @pl.when(i >= nv_ref[0])
    def _():
        y_ref[...] = jnp.zeros_like(y_ref)


def _moe(plan, x_slots, w_gate_up, b_gate_up4, w_down, b_down4, l):
    grid_spec = pltpu.PrefetchScalarGridSpec(
        num_scalar_prefetch=6,
        grid=(N_MOE_BLOCKS,),
        in_specs=[pl.BlockSpec((MOE_TILE, HALF), lambda i, be, *_: (i, 0)),
                  pl.BlockSpec(memory_space=pl.ANY),
                  pl.BlockSpec((None, None, 1, 2 * D_FF), lambda i, be, *_: (l, be[i], 0, 0)),
                  pl.BlockSpec(memory_space=pl.ANY),
                  pl.BlockSpec((None, None, 1, D_MODEL), lambda i, be, *_: (l, be[i], 0, 0))],
        out_specs=pl.BlockSpec((MOE_TILE, HALF), lambda i, be, *_: (i, 0)),
        scratch_shapes=[pltpu.VMEM((2, D_MODEL, 2 * D_FF), F32), pltpu.VMEM((2, D_FF, D_MODEL), F32),
                        pltpu.VMEM((D_MODEL, 2 * D_FF), BF16), pltpu.VMEM((D_FF, D_MODEL), BF16),
                        pltpu.SemaphoreType.DMA((2, 2))])
    return pl.pallas_call(
        functools.partial(_moe_kernel, l),
        grid_spec=grid_spec,
        out_shape=jax.ShapeDtypeStruct((N_SLOTS, HALF), jnp.int32),
        compiler_params=_cparams(("arbitrary",)),
        name="moe",
    )(*plan, x_slots, w_gate_up, b_gate_up4, w_down, b_down4)


def _combine_kernel(y0_ref, y1_ref, y2_ref, y3_ref, gate_ref, h_ref, g2_ref, gpost_ref, oc_ref, ol_ref):
    i = pl.program_id(0)
    gate = gate_ref[...]
    ffn = None
    for k, y_ref in enumerate((y0_ref, y1_ref, y2_ref, y3_ref)):
        ya, yb = _unpack_halves(y_ref[...])
        term = gate[:, k:k + 1] * jnp.concatenate([ya, yb], axis=1)
        ffn = term if ffn is None else ffn + term
    out = h_ref[...] + g2_ref[...] * (_rms(ffn) * gpost_ref[...])

    @pl.when(i < CTX_TILES)
    def _():
        oc_ref[...] = out

    @pl.when(i >= CTX_TILES)
    def _():
        ol_ref[...] = out


def _combine(y_tok, gate, h2, mods3, g_post_ffn3, l):
    row = lambda n: pl.BlockSpec((TM, n), lambda i: (i, 0))
    choice = lambda k: pl.BlockSpec((TM, HALF), lambda i: (k * N_TILES + i, 0))
    return pl.pallas_call(
        _combine_kernel,
        grid=(N_TILES,),
        in_specs=[choice(0), choice(1), choice(2), choice(3), row(TOP_K), row(D_MODEL),
                  pl.BlockSpec((None, 1, D_MODEL), lambda i: (_mod_index(i), 0, 5)),
                  pl.BlockSpec((None, 1, D_MODEL), lambda i: (l, 0, 0))],
        out_specs=[_ctx_spec(D_MODEL), _lat_spec(D_MODEL)],
        out_shape=[jax.ShapeDtypeStruct((N_CTX, D_MODEL), F32), jax.ShapeDtypeStruct((N_LAT, D_MODEL), F32)],
        compiler_params=_cparams(("arbitrary",)),
        name="combine",
    )(y_tok, y_tok, y_tok, y_tok, gate, h2, mods3, g_post_ffn3)


def _route_kernel(topk_ref, count_ref, dest_ref, gate_ref, base_ref):
    i = pl.program_id(0)

    @pl.when(i == 0)
    def _():
        total = count_ref[...]
        padded = jnp.floor((total + (MOE_TILE - 1.0)) * (1.0 / MOE_TILE)) * MOE_TILE
        before = (lax.broadcasted_iota(jnp.int32, (N_EXPERTS, N_EXPERTS), 0)
                  < lax.broadcasted_iota(jnp.int32, (N_EXPERTS, N_EXPERTS), 1)).astype(F32)
        base_ref[...] = jnp.dot(padded, before, preferred_element_type=F32, precision=lax.Precision.HIGHEST)

    topk = topk_ref[...]
    lane = lax.broadcasted_iota(jnp.int32, (ROUTE_TM, N_EXPERTS), 1).astype(F32)
    hots = [lane == topk[:, k:k + 1] for k in range(TOP_K)]
    vals = [topk[:, TOP_K + k:TOP_K + k + 1] for k in range(TOP_K)]
    chosen = functools.reduce(jnp.add, [h.astype(F32) for h in hots])
    earlier_row = (lax.broadcasted_iota(jnp.int32, (ROUTE_TM, ROUTE_TM), 1)
                   < lax.broadcasted_iota(jnp.int32, (ROUTE_TM, ROUTE_TM), 0)).astype(BF16)
    earlier = jnp.dot(earlier_row, chosen.astype(BF16), preferred_element_type=F32)
    offs = base_ref[0:1, :] + earlier
    dest = [jnp.sum(jnp.where(h, offs, 0.0), axis=1, keepdims=True) for h in hots]
    dest_ref[...] = jnp.concatenate(dest, axis=1).astype(jnp.int32)
    e = [jnp.exp(v - vals[0]) for v in vals]
    den = functools.reduce(jnp.add, e)
    gate_ref[...] = jnp.concatenate(e, axis=1) / den
    base_ref[...] = base_ref[...] + jnp.sum(chosen, axis=0, keepdims=True)


ROUTE_TM = 1024


def _route(topk, counts):
    tile = lambda n: pl.BlockSpec((ROUTE_TM, n), lambda i: (i, 0))
    dest, gate = pl.pallas_call(
        _route_kernel,
        grid=(N_TOK // ROUTE_TM,),
        in_specs=[tile(2 * TOP_K), pl.BlockSpec((8, N_EXPERTS), lambda i: (0, 0))],
        out_specs=[tile(TOP_K), tile(TOP_K)],
        out_shape=[jax.ShapeDtypeStruct((N_TOK, TOP_K), jnp.int32),
                   jax.ShapeDtypeStruct((N_TOK, TOP_K), F32)],
        scratch_shapes=[pltpu.VMEM((8, N_EXPERTS), F32)],
        compiler_params=_cparams(("arbitrary",)),
        name="route",
    )(topk, counts)
    counts = counts[0].astype(jnp.int32)
    expert = jnp.arange(N_EXPERTS, dtype=jnp.int32)
    padded = (counts + MOE_TILE - 1) // MOE_TILE * MOE_TILE
    pad_end = jnp.sum(jnp.where(expert[None, :] <= expert[:, None], padded[None, :], 0), axis=1)
    block = jnp.arange(N_MOE_BLOCKS, dtype=jnp.int32)
    block_e = jnp.minimum(jnp.sum((pad_end[None, :] <= block[:, None] * MOE_TILE).astype(jnp.int32), axis=1),
                          N_EXPERTS - 1)
    n_valid = pad_end[-1] // MOE_TILE
    mine = expert[None, :] == block_e[:, None]
    pick = lambda v: jnp.sum(jnp.where(mine, v[None, :], 0), axis=1)
    offset = block * MOE_TILE - pick(pad_end - padded)
    valid = block < n_valid
    first = valid & (offset == 0)
    used = counts > 0
    slot = jnp.sum((used[None, :] & (expert[None, :] < block_e[:, None])).astype(jnp.int32), axis=1) % 2
    nxt = jnp.min(jnp.where(used[None, :] & (expert[None, :] > block_e[:, None]), expert[None, :], N_EXPERTS),
                  axis=1)
    nxt = jnp.where(nxt < N_EXPERTS, nxt, -1)
    rows = jnp.where(valid, jnp.clip(pick(counts) - offset, 0, MOE_TILE), 0)
    plan = tuple(a.astype(jnp.int32) for a in (block_e, n_valid[None], first, slot, nxt, rows))
    n_workers = N_TOK // SCATTER_TOKENS_PER_WORKER
    dest_sc = dest.reshape(n_workers, SCATTER_TOKENS_PER_WORKER // GATHER_CHUNK, GATHER_CHUNK, TOP_K)
    dest_sc = dest_sc.transpose(0, 1, 3, 2).reshape(n_workers, -1, GATHER_CHUNK)
    return gate, dest.T.reshape(-1), dest_sc, plan


def _rope_tables():
    t = jnp.arange(DEC_SEQ)
    inv = ROPE_THETA ** (-jnp.arange(ROPE_PAIRS, dtype=F32) / ROPE_PAIRS)
    row = (t // GRID_W).astype(F32)[:, None] * inv
    col = (t % GRID_W).astype(F32)[:, None] * inv
    zeros = jnp.zeros_like(row)
    cos = jnp.concatenate([jnp.cos(row), jnp.cos(row), jnp.cos(col), jnp.cos(col)], axis=1)
    s1 = jnp.concatenate([-jnp.sin(row), zeros, -jnp.sin(col), zeros], axis=1)
    s2 = jnp.concatenate([zeros, jnp.sin(row), zeros, jnp.sin(col)], axis=1)
    ident = lambda v: jnp.full((TM, HEAD_DIM), v, F32)
    tables = [jnp.concatenate([x, ident(v)], axis=0) for x, v in ((cos, 1.0), (s1, 0.0), (s2, 0.0))]
    return [jnp.tile(x, (1, 2)) for x in tables]


def kernel(x_prompt, x_sample, cache_a, cache_b, cache_c, c, c_ctx, w_ada, b_ada, g_pre_mix, g_post_mix,
           g_pre_ffn, g_post_ffn, w_in, g_q_b, g_k_b, sink_a, rpb_c, w_br_a, w_br_b, w_br_c, w_out,
           w_router, b_router, w_gate_up, b_gate_up, w_down, b_down):
    h = (x_prompt.reshape(N_CTX, D_MODEL), x_sample.reshape(N_LAT, D_MODEL))
    cond8 = jnp.concatenate([c_ctx[None], c, jnp.zeros((3, D_MODEL), F32)], axis=0)
    cache_a = cache_a.astype(BF16).reshape(DEC_BATCH, DEPTH, 2, PAST_LEN, W_KV)
    cache_b = cache_b.astype(BF16).reshape(DEC_BATCH, DEPTH, 2, PAST_LEN, W_KV)
    cache_c = cache_c.astype(BF16).reshape(DEC_BATCH, DEPTH, 2, PAST_LEN, W_HEADS)
    cos_t, s1_t, s2_t = _rope_tables()
    bd = jnp.kron(jnp.eye(256 // HEAD_DIM, dtype=F32),
                  jnp.full((HEAD_DIM, HEAD_DIM), 1.0 / HEAD_DIM, F32)).astype(BF16)
    vec3 = lambda a: a.reshape(DEPTH, 1, a.shape[-1])
    scale = HEAD_DIM ** -0.5
    states = []
    mods_all = _adaln(cond8, w_ada, vec3(b_ada))
    for l in range(DEPTH):
        w = w_in[l]
        pair_cols = lambda a: a.reshape(D_MODEL, 2, 4, HEAD_DIM).transpose(0, 2, 1, 3).reshape(D_MODEL, 512)
        pair_rows = lambda a: a.reshape(2, 4, HEAD_DIM, D_MODEL).transpose(1, 0, 2, 3).reshape(512, D_MODEL)
        w_in_p = jnp.concatenate(
            [pair_cols(w[:, 0:512]) * scale, pair_cols(w[:, 768:1280]), w[:, 1536:2048] * scale,
             w[:, 2048:2560], w[:, 2560:3072],
             w[:, 512:640], w[:, 1280:1408], w[:, 640:768], w[:, 1408:1536], w[:, 3072:]], axis=1).astype(BF16)
        gq = (jnp.tile(g_q_b[l], H_B) * scale)[None]
        gk = jnp.tile(g_k_b[l], KV_B)[None]
        w_br = jnp.stack([pair_rows(w_br_a[l]), pair_rows(w_br_b[l]), w_br_c[l]], axis=0).astype(BF16)
        w_out_b = w_out[l].astype(BF16)
        w_r_hi = w_router[l].astype(BF16)
        w_r_lo = (w_router[l] - w_r_hi.astype(F32)).astype(BF16)
        w_router2 = jnp.concatenate([w_r_hi, w_r_lo], axis=1)

        mods3 = mods_all[l].reshape(8, 1, 6 * D_MODEL)
        q_all, kvab, kvc, gates, *states = _proj(h, mods3, vec3(g_pre_mix), w_in_p, gq, gk, bd,
                                                 cos_t, s1_t, s2_t, tuple(states), l)
        o_ctx = _ctx_attn(sink_a, q_all, kvab, kvc, l)
        o_a = _win_attn(sink_a, q_all, kvab, cache_a, l)
        o_b = _dense_attn(q_all, kvab, cache_b, l)
        o_c = _nbr_attn(q_all, kvc, cache_c, _nbr_bias_table(rpb_c[l]), l)
        h2, hn2, topk, counts = _merge(o_ctx, o_a, o_b, o_c, gates, h, mods3, w_br, w_out_b,
                                       vec3(g_post_mix), vec3(g_pre_ffn), w_router2, vec3(b_router), l)
        gate, dest, dest_sc, plan = _route(topk, counts)
        x_slots = _scatter_rows(hn2, dest_sc, N_SLOTS)
        y_slots = _moe(plan, x_slots, w_gate_up,
                       b_gate_up.reshape(DEPTH, N_EXPERTS, 1, 2 * D_FF), w_down,
                       b_down.reshape(DEPTH, N_EXPERTS, 1, D_MODEL), l)
        y_tok = _gather_rows(y_slots, dest)
        h = _combine(y_tok, gate, h2, mods3, vec3(g_post_ffn), l)

    state_a, state_b, state_c = states
    heads = lambda s, n: s.reshape(BATCH, DEPTH, 2, SEQ, n, HEAD_DIM)
    return (h[0].reshape(BATCH, SEQ, D_MODEL), h[1].reshape(DEC_BATCH, DEC_SEQ, D_MODEL),
            heads(state_a, KV_A), heads(state_b, KV_B), heads(state_c, H_C))
```

```python
import functools

import jax
import jax.numpy as jnp
from jax import lax
from jax.experimental import pallas as pl
from jax.experimental.pallas import tpu as pltpu
from jax.experimental.pallas import tpu_sc as plsc

D_MODEL = 1024
BATCH = 32
SEQ = 256
DEPTH = 2
DEC_BATCH = 4
DEC_SEQ = 2048
PAST_LEN = 512
GRID_W = 64
HEAD_DIM = 64
H_A = 8
KV_A = 2
H_B = 8
KV_B = 2
H_C = 8
WINDOW_A = 128
NA_ROWS = 8
NA_COLS = 16
ROPE_THETA = 10000.0
ROPE_PAIRS = HEAD_DIM // 4
N_EXPERTS = 32
TOP_K = 4
D_FF = D_MODEL
SWIGLU_ALPHA = 1.702
SWIGLU_LIMIT = 7.0
EPS = 1e-6

W_HEADS = H_A * HEAD_DIM
W_KV = KV_A * HEAD_DIM
N_CTX = BATCH * SEQ
N_LAT = DEC_BATCH * DEC_SEQ
N_TOK = N_CTX + N_LAT
GRID_ROWS = DEC_SEQ // GRID_W
D_IN = 3 * W_HEADS + 4 * W_KV + 2 * W_HEADS + 3 * D_MODEL

TM = 256
N_TILES = N_TOK // TM
CTX_TILES = N_CTX // TM
LAT_TILES_PER_BATCH = DEC_SEQ // TM
TQ = 128
MOE_CHAIN = 256
MOE_TILE = 2 * MOE_CHAIN
N_SLOTS = N_TOK * TOP_K + N_EXPERTS * MOE_TILE
N_MOE_BLOCKS = N_SLOTS // MOE_TILE
NEG = -1e30
VMEM_LIMIT = 56 * 1024 * 1024

BF16 = jnp.bfloat16
F32 = jnp.float32


def _cparams(sem):
    return pltpu.CompilerParams(dimension_semantics=sem, vmem_limit_bytes=VMEM_LIMIT)


def _mod_index(i):
    return jnp.where(i < CTX_TILES, 0, 1 + (i - CTX_TILES) // LAT_TILES_PER_BATCH)


def _rms(x):
    return x * lax.rsqrt(jnp.mean(x * x, axis=-1, keepdims=True) + EPS)


HALF = D_MODEL // 2


def _pack_halves(x):
    hi = lax.bitcast_convert_type(x[:, :HALF].astype(BF16).astype(F32), jnp.uint32)
    lo = lax.bitcast_convert_type(x[:, HALF:].astype(BF16).astype(F32), jnp.uint32)
    return lax.bitcast_convert_type(hi | (lo >> 16), jnp.int32)


def _unpack_halves(w):
    u = lax.bitcast_convert_type(w, jnp.uint32)
    return (lax.bitcast_convert_type(u & jnp.uint32(0xFFFF0000), F32),
            lax.bitcast_convert_type(u << 16, F32))


GATHER_CHUNK = 64


def _gather_rows(table, idx):
    n = idx.shape[0]
    width = table.shape[1]
    info = plsc.get_sparse_core_info()
    n_workers = info.num_cores * info.num_subcores
    per_worker = n // n_workers
    n_chunks = per_worker // GATHER_CHUNK
    assert per_worker * n_workers == n and n_chunks * GATHER_CHUNK == per_worker and n_chunks % 2 == 0
    mesh = plsc.VectorSubcoreMesh(core_axis_name="core", subcore_axis_name="subcore")

    @functools.partial(
        pl.kernel, out_type=jax.ShapeDtypeStruct((n, width), table.dtype), mesh=mesh,
        scratch_types=[pltpu.VMEM((per_worker,), jnp.int32),
                       pltpu.VMEM((2, GATHER_CHUNK, width), table.dtype),
                       pltpu.SemaphoreType.DMA((2,)), pltpu.SemaphoreType.DMA((2,))])
    def gather(table_hbm, idx_hbm, out_hbm, idx_v, rows_v, gather_sem, write_sem):
        worker = lax.axis_index("subcore") * info.num_cores + lax.axis_index("core")
        base = worker * per_worker
        pltpu.sync_copy(idx_hbm.at[pl.ds(base, per_worker)], idx_v)

        def fetch(chunk, slot):
            rows = idx_v.at[pl.ds(chunk * GATHER_CHUNK, GATHER_CHUNK)]
            return pltpu.make_async_copy(table_hbm.at[rows], rows_v.at[slot], gather_sem.at[slot])

        def write(chunk, slot):
            dst = out_hbm.at[pl.ds(base + chunk * GATHER_CHUNK, GATHER_CHUNK)]
            return pltpu.make_async_copy(rows_v.at[slot], dst, write_sem.at[slot])

        fetch(0, 0).start()

        @pl.loop(0, n_chunks, step=2)
        def _(c):
            @pl.when(c > 0)
            def _():
                write(c - 1, 1).wait()

            fetch(c + 1, 1).start()
            fetch(c, 0).wait()
            write(c, 0).start()
            write(c, 0).wait()

            @pl.when(c + 2 < n_chunks)
            def _():
                fetch(c + 2, 0).start()

            fetch(c + 1, 1).wait()
            write(c + 1, 1).start()

        write(n_chunks - 1, 1).wait()

    return gather(table, idx)


SC_WORKERS_V7X = 32
SCATTER_TOKENS_PER_WORKER = N_TOK // SC_WORKERS_V7X


def _scatter_rows(table, dest_sc, n_out):
    width = table.shape[1]
    info = plsc.get_sparse_core_info()
    assert info.num_cores * info.num_subcores == SC_WORKERS_V7X
    per_worker = SCATTER_TOKENS_PER_WORKER
    n_chunks = per_worker // GATHER_CHUNK
    assert n_chunks % 2 == 0 and dest_sc.shape == (SC_WORKERS_V7X, n_chunks * TOP_K, GATHER_CHUNK)
    mesh = plsc.VectorSubcoreMesh(core_axis_name="core", subcore_axis_name="subcore")

    @functools.partial(
        pl.kernel, out_type=jax.ShapeDtypeStruct((n_out, width), table.dtype), mesh=mesh,
        scratch_types=[pltpu.VMEM((n_chunks * TOP_K, GATHER_CHUNK), jnp.int32),
                       pltpu.VMEM((2, GATHER_CHUNK, width), table.dtype),
                       pltpu.SemaphoreType.DMA((2,)), pltpu.SemaphoreType.DMA((2,))])
    def scatter(table_hbm, dest_hbm, out_hbm, idx_v, rows_v, read_sem, write_sem):
        worker = lax.axis_index("subcore") * info.num_cores + lax.axis_index("core")
        base = worker * per_worker
        pltpu.sync_copy(dest_hbm.at[worker], idx_v)

        def read(chunk, slot):
            src = table_hbm.at[pl.ds(base + chunk * GATHER_CHUNK, GATHER_CHUNK)]
            return pltpu.make_async_copy(src, rows_v.at[slot], read_sem.at[slot])

        def writes(chunk, slot):
            return [pltpu.make_async_copy(rows_v.at[slot], out_hbm.at[idx_v.at[chunk * TOP_K + k]],
                                          write_sem.at[slot]) for k in range(TOP_K)]

        read(0, 0).start()

        @pl.loop(0, n_chunks, step=2)
        def _(c):
            @pl.when(c > 0)
            def _():
                for cp in writes(c - 1, 1):
                    cp.wait()

            read(c + 1, 1).start()
            read(c, 0).wait()
            for cp in writes(c, 0):
                cp.start()
            for cp in writes(c, 0):
                cp.wait()

            @pl.when(c + 2 < n_chunks)
            def _():
                read(c + 2, 0).start()

            read(c + 1, 1).wait()
            for cp in writes(c + 1, 1):
                cp.start()

        for cp in writes(n_chunks - 1, 1):
            cp.wait()

    return scatter(table, dest_sc)


def _adaln_kernel(c_ref, w_ref, b_ref, o_ref):
    c = c_ref[...]
    s = c / (1.0 + jnp.exp(-c))
    o_ref[...] = jnp.dot(s, w_ref[...], preferred_element_type=F32,
                         precision=lax.Precision.HIGHEST) + b_ref[...]


def _adaln(cond8, w_ada, b_ada3):
    tn = 768
    return pl.pallas_call(
        _adaln_kernel,
        grid=(DEPTH, 6 * D_MODEL // tn),
        in_specs=[pl.BlockSpec((8, D_MODEL), lambda l, j: (0, 0)),
                  pl.BlockSpec((None, D_MODEL, tn), lambda l, j: (l, 0, j)),
                  pl.BlockSpec((None, 1, tn), lambda l, j: (l, 0, j))],
        out_specs=pl.BlockSpec((None, 8, tn), lambda l, j: (l, 0, j)),
        out_shape=jax.ShapeDtypeStruct((DEPTH, 8, 6 * D_MODEL), F32),
        compiler_params=_cparams(("arbitrary", "arbitrary")),
        name="adaln",
    )(cond8, w_ada, b_ada3)


C_QA, C_QB, C_QC, C_KC, C_VC, C_KAB, C_GL = 0, 512, 1024, 1536, 2048, 2560, 3072


def _ctx_spec(n):
    return pl.BlockSpec((TM, n), lambda i: (jnp.minimum(i, CTX_TILES - 1), 0))


def _lat_spec(n):
    return pl.BlockSpec((TM, n), lambda i: (jnp.maximum(i - CTX_TILES, 0), 0))


def _proj_kernel(hc_ref, hl_ref, sh_ref, sc_ref, gpre_ref, w_ref, gq_ref, gk_ref, bd_ref,
                 cos_ref, s1_ref, s2_ref, *rest):
    q_ref, kvab_ref, kvc_ref, gates_ref, sta_ref, stb_ref, stc_ref = rest[-7:]
    i = pl.program_id(0)
    bd = bd_ref[...]

    def rope(t, rs):
        cos, s1, s2 = cos_ref[rs, :], s1_ref[rs, :], s2_ref[rs, :]
        parts = []
        for g in range(t.shape[1] // 128):
            tg = t[:, g * 128:(g + 1) * 128]
            parts.append(tg * cos + pltpu.roll(tg, 112, 1) * s1 + pltpu.roll(tg, 16, 1) * s2)
        return parts[0] if len(parts) == 1 else jnp.concatenate(parts, axis=1)

    def headnorm(t, g):
        sq = (t * t).astype(BF16)
        n = t.shape[1]
        if n == 128:
            ms = jnp.dot(sq, bd[:128, :128], preferred_element_type=F32)
        else:
            ms = jnp.concatenate(
                [jnp.dot(sq[:, c:c + 256], bd, preferred_element_type=F32) for c in range(0, n, 256)],
                axis=1)
        return t * lax.rsqrt(ms + EPS) * g

    kv_f32 = []
    for c in range(PROJ_TM // TM):
        rs = slice(c * TM, (c + 1) * TM)
        hn = _rms(jnp.where(i < PROJ_CTX_TILES, hc_ref[rs, :], hl_ref[rs, :])) * gpre_ref[...]
        hb = (hn * (1.0 + sc_ref[...]) + sh_ref[...]).astype(BF16)
        proj = lambda c0, n, hb=hb: jnp.dot(hb, w_ref[:, c0:c0 + n], preferred_element_type=F32)
        q_ref[rs, 0:512] = rope(proj(C_QA, 512), rs).astype(BF16)
        q_ref[rs, 512:1024] = rope(headnorm(proj(C_QB, 512), gq_ref[...]), rs).astype(BF16)
        q_ref[rs, 1024:1536] = proj(C_QC, 512).astype(BF16)
        kc = proj(C_KC, 512)
        vc = proj(C_VC, 512)
        kvc_ref[rs, 0:512] = kc.astype(BF16)
        kvc_ref[rs, 512:1024] = vc.astype(BF16)
        kab = proj(C_KAB, 512)
        ka = kab[:, 0:128]
        kb = headnorm(kab[:, 128:256], gk_ref[...])
        kvab_ref[rs, 0:128] = rope(ka, rs).astype(BF16)
        kvab_ref[rs, 128:256] = rope(kb, rs).astype(BF16)
        kvab_ref[rs, 256:512] = kab[:, 256:512].astype(BF16)
        for j in range(6):
            gl = proj(C_GL + j * 512, 512)
            gates_ref[rs, j * 512:(j + 1) * 512] = (1.0 / (1.0 + jnp.exp(-gl))).astype(BF16)
        kv_f32.append(((ka, kab[:, 256:384]), (kb, kab[:, 384:512]), (kc, vc)))

    @pl.when(i < PROJ_CTX_TILES)
    def _():
        for c, per_mixer in enumerate(kv_f32):
            for st_ref, (k, v) in zip((sta_ref, stb_ref, stc_ref), per_mixer):
                if len(st_ref.shape) == 5:
                    st_ref[c, 0, 0] = k
                    st_ref[c, 0, 1] = v
                    st_ref[c, 1:] = jnp.zeros((DEPTH - 1,) + tuple(st_ref.shape[2:]), F32)
                else:
                    st_ref[c, 0] = k
                    st_ref[c, 1] = v


PROJ_TM = 2 * TM
PROJ_TILES = N_TOK // PROJ_TM
PROJ_CTX_TILES = N_CTX // PROJ_TM
PROJ_LAT_TILES_PER_BATCH = DEC_SEQ // PROJ_TM


def _proj(h, mods3, g_pre3, w_in_p, gq, gk, bd, cos_t, s1_t, s2_t, prev_states, l):
    lat_tile = lambda i: i - PROJ_CTX_TILES
    rope_idx = lambda i: jnp.where(i < PROJ_CTX_TILES, PROJ_LAT_TILES_PER_BATCH,
                                   lat_tile(i) % PROJ_LAT_TILES_PER_BATCH)
    mod_idx = lambda i: jnp.where(i < PROJ_CTX_TILES, 0, 1 + lat_tile(i) // PROJ_LAT_TILES_PER_BATCH)
    ctx_block = lambda i: jnp.minimum(i, PROJ_CTX_TILES - 1)
    const = lambda shape: pl.BlockSpec(shape, lambda i: (0,) * len(shape), pipeline_mode=pl.Buffered(1))
    rope_spec = pl.BlockSpec((PROJ_TM, 128), lambda i: (rope_idx(i), 0))
    row = lambda n: pl.BlockSpec((PROJ_TM, n), lambda i: (i, 0))
    per_step = PROJ_TM // SEQ
    if l == 0:
        state_spec = lambda n: pl.BlockSpec((per_step, DEPTH, 2, SEQ, n), lambda i: (ctx_block(i), 0, 0, 0, 0))
    else:
        state_spec = lambda n: pl.BlockSpec((per_step, None, 2, SEQ, n), lambda i: (ctx_block(i), l, 0, 0, 0))
    state_shape = lambda n: jax.ShapeDtypeStruct((BATCH, DEPTH, 2, SEQ, n), F32)
    n_in = 12
    return pl.pallas_call(
        _proj_kernel,
        grid=(PROJ_TILES,),
        in_specs=[pl.BlockSpec((PROJ_TM, D_MODEL), lambda i: (ctx_block(i), 0)),
                  pl.BlockSpec((PROJ_TM, D_MODEL), lambda i: (jnp.maximum(lat_tile(i), 0), 0)),
                  pl.BlockSpec((None, 1, D_MODEL), lambda i: (mod_idx(i), 0, 0)),
                  pl.BlockSpec((None, 1, D_MODEL), lambda i: (mod_idx(i), 0, 1)),
                  pl.BlockSpec((None, 1, D_MODEL), lambda i: (l, 0, 0)),
                  const((D_MODEL, D_IN)), const((1, 512)), const((1, 128)), const((256, 256)),
                  rope_spec, rope_spec, rope_spec] + [pl.BlockSpec(memory_space=pl.ANY)] * len(prev_states),
        out_specs=[row(1536), row(512), row(1024), row(3072),
                   state_spec(W_KV), state_spec(W_KV), state_spec(W_HEADS)],
        out_shape=[jax.ShapeDtypeStruct((N_TOK, 1536), BF16),
                   jax.ShapeDtypeStruct((N_TOK, 512), BF16),
                   jax.ShapeDtypeStruct((N_TOK, 1024), BF16),
                   jax.ShapeDtypeStruct((N_TOK, 3072), BF16),
                   state_shape(W_KV), state_shape(W_KV), state_shape(W_HEADS)],
        input_output_aliases={n_in + j: 4 + j for j in range(len(prev_states))},
        compiler_params=_cparams(("arbitrary",)),
        name="proj",
    )(*h, mods3, mods3, g_pre3, w_in_p, gq, gk, bd, cos_t, s1_t, s2_t, *prev_states)


def _qk(q, k):
    return lax.dot_general(q, k, (((1,), (1,)), ((), ())), preferred_element_type=F32)


def _softmax_pv(scores, values, sink=None, halves=None):
    m = functools.reduce(jnp.maximum, [jnp.max(s, axis=-1, keepdims=True) for s in scores])
    if sink is not None:
        m = jnp.maximum(m, sink)
    if halves is None:
        ps = [jnp.exp(s - m) for s in scores]
        den = functools.reduce(jnp.add, [jnp.sum(p, axis=-1, keepdims=True) for p in ps])
        if sink is not None:
            den = den + jnp.exp(sink - m)
        o = functools.reduce(jnp.add, [jnp.dot(p.astype(BF16), v, preferred_element_type=F32)
                                       for p, v in zip(ps, values)])
        return o / den
    ps = [jnp.exp(s - m).astype(BF16) for s in scores]
    rows = ps[0].shape[0]
    split = {"both": rows // 2, "lo": rows, "hi": 0}[halves]
    one = jnp.ones((), BF16)

    def pv(r0, r1, keep_lo):
        acc = None
        for p, v in zip(ps, values):
            lo = _lo_lanes(v.shape[0])
            v1 = jnp.where(lo, v, one) if keep_lo else jnp.where(lo, one, v)
            t = jnp.dot(p[r0:r1], v1, preferred_element_type=F32)
            acc = t if acc is None else acc + t
        return acc

    parts = ([pv(0, split, True)] if split > 0 else []) + ([pv(split, rows, False)] if split < rows else [])
    o = parts[0] if len(parts) == 1 else jnp.concatenate(parts, axis=0)
    if sink is not None:
        is_lo_row = lax.broadcasted_iota(jnp.int32, (rows, 128), 0) < split
        o = o + jnp.where(is_lo_row != _lo_lanes(rows), jnp.exp(sink - m), 0.0)
    return o * pltpu.roll(1.0 / o, HEAD_DIM, 1)


def _lo_lanes(rows):
    return lax.broadcasted_iota(jnp.int32, (rows, 128), 1) < HEAD_DIM


def _stack_pairs(q, n_pairs):
    lo = _lo_lanes(q.shape[0])
    zero = jnp.zeros((q.shape[0], 128), q.dtype)
    pairs = [q[:, p * 128:(p + 1) * 128] for p in range(n_pairs)]
    return jnp.concatenate([jnp.where(lo, x, zero) for x in pairs] + [jnp.where(lo, zero, x) for x in pairs],
                           axis=0)


def _unstack_pairs(o, n_pairs):
    rows = o.shape[0] // (2 * n_pairs)
    lo = _lo_lanes(rows)
    return jnp.concatenate(
        [jnp.where(lo, o[p * rows:(p + 1) * rows], o[(n_pairs + p) * rows:(n_pairs + p + 1) * rows])
         for p in range(n_pairs)], axis=1)


def _sink_column(sink_ref, l, rows):
    return jnp.concatenate([jnp.full((rows, 1), sink_ref[l, h], F32) for h in range(H_A)], axis=0)


def _ctx_attn_kernel(l, sink_ref, q_ref, kvab_ref, kvc_ref, o_ref):
    qa = _stack_pairs(q_ref[:, 0:512], 4)
    o = _softmax_pv([_qk(qa, kvab_ref[:, 0:128])], [kvab_ref[:, 256:384]], _sink_column(sink_ref, l, SEQ))
    o_ref[:, 0:512] = _unstack_pairs(o, 4).astype(BF16)
    qb = _stack_pairs(q_ref[:, 512:1024], 4)
    o = _softmax_pv([_qk(qb, kvab_ref[:, 128:256])], [kvab_ref[:, 384:512]])
    o_ref[:, 512:1024] = _unstack_pairs(o, 4).astype(BF16)
    for hp in range(H_C // 2):
        cs = slice(hp * 128, (hp + 1) * 128)
        qc = _stack_pairs(q_ref[:, 1024 + hp * 128:1024 + (hp + 1) * 128], 1)
        o = _softmax_pv([_qk(qc, kvc_ref[:, cs])], [kvc_ref[:, 512 + hp * 128:512 + (hp + 1) * 128]])
        o_ref[:, 1024 + hp * 128:1024 + (hp + 1) * 128] = _unstack_pairs(o, 1).astype(BF16)


def _ctx_attn(sink_a, q_all, kvab, kvc, l):
    row = lambda n: pl.BlockSpec((SEQ, n), lambda b: (b, 0))
    return pl.pallas_call(
        functools.partial(_ctx_attn_kernel, l),
        grid=(BATCH,),
        in_specs=[pl.BlockSpec(memory_space=pltpu.SMEM), row(1536), row(512), row(1024)],
        out_specs=row(1536),
        out_shape=jax.ShapeDtypeStruct((N_CTX, 1536), BF16),
        compiler_params=_cparams(("arbitrary",)),
        name="ctx_attn",
    )(sink_a, q_all, kvab, kvc)


def _win_attn_kernel(l, sink_ref, q_ref, prev_ref, cur_ref, nxt_ref, ck_ref, cv_ref, o_ref):
    n = pl.program_id(1)
    nb = DEC_SEQ // TQ
    rows = H_A * TQ
    qpos = lax.broadcasted_iota(jnp.int32, (rows, TQ), 0) % TQ
    kpos = lax.broadcasted_iota(jnp.int32, (rows, TQ), 1)
    mask_prev = (kpos >= qpos) & (n > 0)
    mask_next = (kpos <= qpos) & (n < nb - 1)
    ks, vs = slice(0, 128), slice(256, 384)
    qs = _stack_pairs(q_ref[...], 4)
    s_prev = jnp.where(mask_prev, _qk(qs, prev_ref[:, ks]), NEG)
    s_cur = _qk(qs, cur_ref[:, ks])
    s_next = jnp.where(mask_next, _qk(qs, nxt_ref[:, ks]), NEG)
    s_ctx = _qk(qs, ck_ref[...])
    o = _softmax_pv([s_prev, s_cur, s_next, s_ctx],
                    [prev_ref[:, vs], cur_ref[:, vs], nxt_ref[:, vs], cv_ref[...]],
                    _sink_column(sink_ref, l, TQ), halves="both")
    o_ref[...] = _unstack_pairs(o, 4).astype(BF16)


def _win_attn(sink_a, q_all, kvab, cache_a, l):
    nb = DEC_SEQ // TQ
    base = N_CTX // TQ
    kv_spec = lambda f: pl.BlockSpec((TQ, 512), lambda b, n: (base + b * nb + f(n), 0))
    cache_spec = lambda s: pl.BlockSpec((None, None, None, PAST_LEN, W_KV), lambda b, n: (b, l, s, 0, 0))
    return pl.pallas_call(
        functools.partial(_win_attn_kernel, l),
        grid=(DEC_BATCH, nb),
        in_specs=[pl.BlockSpec(memory_space=pltpu.SMEM),
                  pl.BlockSpec((TQ, 512), lambda b, n: (base + b * nb + n, 0)),
                  kv_spec(lambda n: jnp.maximum(n - 1, 0)), kv_spec(lambda n: n),
                  kv_spec(lambda n: jnp.minimum(n + 1, nb - 1)),
                  cache_spec(0), cache_spec(1)],
        out_specs=pl.BlockSpec((TQ, 512), lambda b, n: (b * nb + n, 0)),
        out_shape=jax.ShapeDtypeStruct((N_LAT, 512), BF16),
        compiler_params=_cparams(("arbitrary", "arbitrary")),
        name="win_attn",
    )(sink_a, q_all, kvab, kvab, kvab, cache_a, cache_a)


DENSE_KEY_CHUNK = 1024
DENSE_TQ = 256


def _online_softmax_pv(q, key_chunks, value_chunks, keep_lo):
    one = jnp.ones((), BF16)
    m = acc = None
    for k, v in zip(key_chunks, value_chunks):
        s = _qk(q, k)
        m_new = jnp.max(s, axis=-1, keepdims=True)
        if m is not None:
            m_new = jnp.maximum(m, m_new)
        p = jnp.exp(s - m_new).astype(BF16)
        lo = _lo_lanes(v.shape[0])
        pv = jnp.dot(p, jnp.where(lo, v, one) if keep_lo else jnp.where(lo, one, v), preferred_element_type=F32)
        acc = pv if acc is None else acc * jnp.exp(m - m_new) + pv
        m = m_new
    return acc * pltpu.roll(1.0 / acc, HEAD_DIM, 1)


def _dense_attn_kernel(q_ref, kv_ref, ck_ref, cv_ref, o_ref):
    qs = _stack_pairs(q_ref[...], 4)
    half = qs.shape[0] // 2
    starts = range(0, DEC_SEQ, DENSE_KEY_CHUNK)
    keys = [kv_ref[c:c + DENSE_KEY_CHUNK, 128:256] for c in starts] + [ck_ref[...]]
    values = [kv_ref[c:c + DENSE_KEY_CHUNK, 384:512] for c in starts] + [cv_ref[...]]
    outs = [_online_softmax_pv(qs[g * half:(g + 1) * half], keys, values, g == 0) for g in range(KV_B)]
    o_ref[...] = _unstack_pairs(jnp.concatenate(outs, axis=0), 4).astype(BF16)


def _dense_attn(q_all, kvab, cache_b, l):
    nb = DEC_SEQ // DENSE_TQ
    base = N_CTX // DENSE_TQ
    cache_spec = lambda s: pl.BlockSpec((None, None, None, PAST_LEN, W_KV), lambda b, n: (b, l, s, 0, 0))
    return pl.pallas_call(
        _dense_attn_kernel,
        grid=(DEC_BATCH, nb),
        in_specs=[pl.BlockSpec((DENSE_TQ, 512), lambda b, n: (base + b * nb + n, 1)),
                  pl.BlockSpec((DEC_SEQ, 512), lambda b, n: (N_CTX // DEC_SEQ + b, 0)),
                  cache_spec(0), cache_spec(1)],
        out_specs=pl.BlockSpec((DENSE_TQ, 512), lambda b, n: (b * nb + n, 0)),
        out_shape=jax.ShapeDtypeStruct((N_LAT, 512), BF16),
        compiler_params=_cparams(("arbitrary", "arbitrary")),
        name="dense_attn",
    )(q_all, kvab, cache_b, cache_b)


NBR_BAND = 4
NBR_Q = NBR_BAND * GRID_W
NBR_WIN_ROWS = 12
NBR_N_BANDS = GRID_ROWS // NBR_BAND
NBR_KBLK = NBR_Q
NBR_WIN_BLOCKS = NBR_WIN_ROWS * GRID_W // NBR_KBLK
NBR_LAST_KB = (GRID_ROWS - NBR_WIN_ROWS) * GRID_W // NBR_KBLK


def _nbr_window_block(band):
    return jnp.clip(band - 1, 0, NBR_LAST_KB)


def _nbr_attn_kernel(q_ref, k0_ref, k1_ref, k2_ref, ck_ref, cv_ref, bias_ref, o_ref):
    k_refs = (k0_ref, k1_ref, k2_ref)
    for hp in range(H_C // 2):
        cs = slice(hp * 128, (hp + 1) * 128)
        vs = slice(512 + hp * 128, 512 + (hp + 1) * 128)
        qs = _stack_pairs(q_ref[:, cs], 1)
        bias = jnp.concatenate([bias_ref[2 * hp], bias_ref[2 * hp + 1]], axis=0)
        scores = [_qk(qs, kr[:, cs]) + bias[:, j * NBR_KBLK:(j + 1) * NBR_KBLK] for j, kr in enumerate(k_refs)]
        scores.append(_qk(qs, ck_ref[:, cs]))
        o = _softmax_pv(scores, [kr[:, vs] for kr in k_refs] + [cv_ref[:, cs]])
        o_ref[:, cs] = _unstack_pairs(o, 1).astype(BF16)


def _nbr_attn(q_all, kvc, cache_c, bias_t, l):
    q_base = N_CTX // NBR_Q
    k_base = N_CTX // NBR_KBLK
    blocks_per_batch = DEC_SEQ // NBR_KBLK
    cache_spec = lambda s: pl.BlockSpec((None, None, None, PAST_LEN, W_HEADS), lambda band, b: (b, l, s, 0, 0))
    key_spec = lambda j: pl.BlockSpec(
        (NBR_KBLK, 1024), lambda band, b: (k_base + b * blocks_per_batch + _nbr_window_block(band) + j, 0))
    band_type = lambda band: jnp.where(band == 0, 0, jnp.where(band == NBR_N_BANDS - 1, 2, 1))
    return pl.pallas_call(
        _nbr_attn_kernel,
        grid=(NBR_N_BANDS, DEC_BATCH),
        in_specs=[pl.BlockSpec((NBR_Q, 512), lambda band, b: (q_base + b * NBR_N_BANDS + band, 2)),
                  key_spec(0), key_spec(1), key_spec(2), cache_spec(0), cache_spec(1),
                  pl.BlockSpec((None, H_C, NBR_Q, NBR_WIN_ROWS * GRID_W),
                               lambda band, b: (band_type(band), 0, 0, 0))],
        out_specs=pl.BlockSpec((NBR_Q, 512), lambda band, b: (b * NBR_N_BANDS + band, 0)),
        out_shape=jax.ShapeDtypeStruct((N_LAT, 512), BF16),
        compiler_params=_cparams(("arbitrary", "arbitrary")),
        name="nbr_attn",
    )(q_all, kvc, kvc, kvc, cache_c, cache_c, bias_t)


def _nbr_bias_table(rpb_l):
    c = jnp.arange(GRID_W)[:, None]
    kc = jnp.arange(GRID_W)[None, :]
    c_start = jnp.clip(c - NA_COLS // 2, 0, GRID_W - NA_COLS)
    valid = (kc >= c_start) & (kc < c_start + NA_COLS)
    pad = GRID_W - NA_COLS
    rpb_pad = jnp.pad(rpb_l.astype(F32), ((0, 0), (0, 0), (pad, pad)))
    toeplitz = jnp.stack([rpb_pad[:, :, GRID_W - 1 - q:2 * GRID_W - 1 - q] for q in range(GRID_W)], axis=2)
    t = jnp.where(valid[None, None], toeplitz, NEG)
    neg = jnp.full((H_C, GRID_W, GRID_W), NEG, F32)
    tables = []
    for r0 in (0, NBR_BAND, GRID_ROWS - NBR_BAND):
        k0 = min(max(r0 - NA_ROWS // 2, 0), GRID_ROWS - NBR_WIN_ROWS)
        rows = []
        for dq in range(NBR_BAND):
            r = r0 + dq
            start = min(max(r - NA_ROWS // 2, 0), GRID_ROWS - NA_ROWS)
            cols = []
            for i in range(NBR_WIN_ROWS):
                kr = k0 + i
                cols.append(t[:, kr - r + NA_ROWS - 1] if start <= kr < start + NA_ROWS else neg)
            rows.append(jnp.concatenate(cols, axis=2))
        tables.append(jnp.concatenate(rows, axis=1))
    return jnp.stack(tables, axis=0)


def _merge_kernel(octx_ref, oa_ref, ob_ref, oc_ref, gates_ref, hc_ref, hl_ref, g1_ref, sh2_ref, sc2_ref,
                  wbr_ref, wout_ref, gpost_ref, gpre_ref, wr_ref, br_ref,
                  h2_ref, hn2_ref, topk_ref, count_ref):
    i = pl.program_id(0)
    is_ctx = i < MERGE_CTX_TILES
    chosen = None
    for r0 in range(0, MERGE_TM, TM):
        rs = slice(r0, r0 + TM)
        merged = None
        for j, lat_ref in enumerate((oa_ref, ob_ref, oc_ref)):
            o = jnp.where(is_ctx, octx_ref[rs, j * 512:(j + 1) * 512], lat_ref[rs, :])
            br = jnp.dot(o, wbr_ref[j], preferred_element_type=F32)
            term = gates_ref[rs, j * D_MODEL:(j + 1) * D_MODEL].astype(F32) * br
            merged = term if merged is None else merged + term
        t = jnp.dot(merged.astype(BF16), wout_ref[...], preferred_element_type=F32)
        h2 = jnp.where(is_ctx, hc_ref[rs, :], hl_ref[rs, :]) + g1_ref[...] * (_rms(t) * gpost_ref[...])
        h2_ref[rs, :] = h2
        hn2 = _rms(h2) * gpre_ref[...] * (1.0 + sc2_ref[...]) + sh2_ref[...]
        hn2_ref[rs, :] = _pack_halves(hn2)
        x_hi = hn2.astype(BF16)
        x_lo = (hn2 - x_hi.astype(F32)).astype(BF16)
        hi = jnp.dot(x_hi, wr_ref[...], preferred_element_type=F32)
        lo = jnp.dot(x_lo, wr_ref[:, :N_EXPERTS], preferred_element_type=F32)
        logits = hi[:, :N_EXPERTS] + (hi[:, N_EXPERTS:] + lo) + br_ref[...]
        lane = lax.broadcasted_iota(jnp.int32, (TM, N_EXPERTS), 1)
        idxs, vals = [], []
        for _ in range(TOP_K):
            idx = jnp.argmax(logits, axis=1, keepdims=True).astype(jnp.int32)
            hot = lane == idx
            idxs.append(idx.astype(F32))
            vals.append(jnp.max(logits, axis=1, keepdims=True))
            chosen = hot.astype(F32) if chosen is None else chosen + hot.astype(F32)
            logits = jnp.where(hot, -jnp.inf, logits)
        topk_ref[rs, :] = jnp.concatenate(idxs + vals, axis=1)

    @pl.when(i == 0)
    def _():
        count_ref[...] = jnp.zeros_like(count_ref)

    count_ref[...] = count_ref[...] + jnp.sum(chosen, axis=0, keepdims=True)


MERGE_TM = 2 * TM
MERGE_TILES = N_TOK // MERGE_TM
MERGE_CTX_TILES = N_CTX // MERGE_TM
MERGE_LAT_TILES_PER_BATCH = DEC_SEQ // MERGE_TM


def _merge(o_ctx, o_a, o_b, o_c, gates, h, mods3, w_br, w_out_b, g_post3, g_pre_ffn3, w_router, b_router3, l):
    ctx = lambda n: pl.BlockSpec((MERGE_TM, n), lambda i: (jnp.minimum(i, MERGE_CTX_TILES - 1), 0))
    lat_n = lambda n: pl.BlockSpec((MERGE_TM, n), lambda i: (jnp.maximum(i - MERGE_CTX_TILES, 0), 0))
    lat = lambda: lat_n(512)
    mod_index = lambda i: jnp.where(i < MERGE_CTX_TILES, 0,
                                    1 + (i - MERGE_CTX_TILES) // MERGE_LAT_TILES_PER_BATCH)
    mod = lambda j: pl.BlockSpec((None, 1, D_MODEL), lambda i: (mod_index(i), 0, j))
    lw = lambda: pl.BlockSpec((None, 1, D_MODEL), lambda i: (l, 0, 0))
    row = lambda n: pl.BlockSpec((MERGE_TM, n), lambda i: (i, 0))
    const = lambda shape: pl.BlockSpec(shape, lambda i: (0,) * len(shape), pipeline_mode=pl.Buffered(1))
    return pl.pallas_call(
        _merge_kernel,
        grid=(MERGE_TILES,),
        in_specs=[ctx(1536),
                  lat(), lat(), lat(), row(3072), ctx(D_MODEL), lat_n(D_MODEL),
                  mod(2), mod(3), mod(4),
                  const((3, 512, D_MODEL)), const((D_MODEL, D_MODEL)), lw(), lw(),
                  const((D_MODEL, 2 * N_EXPERTS)),
                  pl.BlockSpec((None, 1, N_EXPERTS), lambda i: (l, 0, 0))],
        out_specs=[row(D_MODEL), row(HALF), row(2 * TOP_K),
                   pl.BlockSpec((8, N_EXPERTS), lambda i: (0, 0))],
        out_shape=[jax.ShapeDtypeStruct((N_TOK, D_MODEL), F32),
                   jax.ShapeDtypeStruct((N_TOK, HALF), jnp.int32),
                   jax.ShapeDtypeStruct((N_TOK, 2 * TOP_K), F32),
                   jax.ShapeDtypeStruct((8, N_EXPERTS), F32)],
        compiler_params=_cparams(("arbitrary",)),
        name="merge",
    )(o_ctx, o_a, o_b, o_c, gates, *h, mods3, mods3, mods3, w_br, w_out_b, g_post3, g_pre_ffn3,
      w_router, b_router3)


def _moe_kernel(l, be_ref, nv_ref, first_ref, slot_ref, nxt_ref, rows_ref, x_ref, wgu_hbm, bgu_ref, wd_hbm, bd_ref, y_ref,
                wgu_f32, wd_f32, wgu_bf, wd_bf, sem):
    i = pl.program_id(0)

    def fetch(e, s):
        return (pltpu.make_async_copy(wgu_hbm.at[l, e], wgu_f32.at[s], sem.at[0, s]),
                pltpu.make_async_copy(wd_hbm.at[l, e], wd_f32.at[s], sem.at[1, s]))

    @pl.when(first_ref[i] == 1)
    def _():
        s = slot_ref[i]

        @pl.when(i == 0)
        def _():
            for cp in fetch(be_ref[i], s):
                cp.start()

        for cp in fetch(be_ref[i], s):
            cp.wait()

        @pl.when(nxt_ref[i] >= 0)
        def _():
            for cp in fetch(nxt_ref[i], 1 - s):
                cp.start()

        wgu_bf[...] = wgu_f32[s].astype(BF16)
        wd_bf[...] = wd_f32[s].astype(BF16)

    n_real = rows_ref[i]

    def chain(r0):
        rs = slice(r0, r0 + MOE_CHAIN)
        real = lax.broadcasted_iota(jnp.int32, (MOE_CHAIN, HALF), 0) + r0 < n_real
        xa, xb = _unpack_halves(jnp.where(real, x_ref[rs, :], 0))
        x = jnp.concatenate([xa.astype(BF16), xb.astype(BF16)], axis=1)
        b = bgu_ref[...]
        glu = jnp.dot(x, wgu_bf[:, :D_FF], preferred_element_type=F32) + b[:, :D_FF]
        lin = jnp.dot(x, wgu_bf[:, D_FF:], preferred_element_type=F32) + b[:, D_FF:]
        glu = jnp.minimum(glu, SWIGLU_LIMIT)
        lin = jnp.clip(lin, -SWIGLU_LIMIT, SWIGLU_LIMIT)
        act = glu * (1.0 / (1.0 + jnp.exp(-SWIGLU_ALPHA * glu))) * (lin + 1.0)
        y = jnp.dot(act.astype(BF16), wd_bf[...], preferred_element_type=F32) + bd_ref[...]
        y_ref[rs, :] = _pack_halves(y)

    @pl.when(n_real > MOE_CHAIN)
    def _():
        chain(0)
        chain(MOE_CHAIN)

    @pl.when((n_real > 0) & (n_real <= MOE_CHAIN))
    def _():
        chain(0)
        y_ref[MOE_CHAIN:, :] = jnp.zeros((MOE_TILE - MOE_CHAIN, HALF), jnp.int32)

    @pl.when(n_real == 0)
    def _():
        y_ref[...] = jnp.zeros_like(y_ref)


def _moe(plan, x_slots, w_gate_up, b_gate_up4, w_down, b_down4, l):
    grid_spec = pltpu.PrefetchScalarGridSpec(
        num_scalar_prefetch=6,
        grid=(N_MOE_BLOCKS,),
        in_specs=[pl.BlockSpec((MOE_TILE, HALF), lambda i, be, *_: (i, 0)),
                  pl.BlockSpec(memory_space=pl.ANY),
                  pl.BlockSpec((None, None, 1, 2 * D_FF), lambda i, be, *_: (l, be[i], 0, 0)),
                  pl.BlockSpec(memory_space=pl.ANY),
                  pl.BlockSpec((None, None, 1, D_MODEL), lambda i, be, *_: (l, be[i], 0, 0))],
        out_specs=pl.BlockSpec((MOE_TILE, HALF), lambda i, be, *_: (i, 0)),
        scratch_shapes=[pltpu.VMEM((2, D_MODEL, 2 * D_FF), F32), pltpu.VMEM((2, D_FF, D_MODEL), F32),
                        pltpu.VMEM((D_MODEL, 2 * D_FF), BF16), pltpu.VMEM((D_FF, D_MODEL), BF16),
                        pltpu.SemaphoreType.DMA((2, 2))])
    return pl.pallas_call(
        functools.partial(_moe_kernel, l),
        grid_spec=grid_spec,
        out_shape=jax.ShapeDtypeStruct((N_SLOTS, HALF), jnp.int32),
        compiler_params=_cparams(("arbitrary",)),
        name="moe",
    )(*plan, x_slots, w_gate_up, b_gate_up4, w_down, b_down4)


def _combine_kernel(y0_ref, y1_ref, y2_ref, y3_ref, gate_ref, h_ref, g2_ref, gpost_ref, oc_ref, ol_ref):
    i = pl.program_id(0)
    gate = gate_ref[...]
    ffn = None
    for k, y_ref in enumerate((y0_ref, y1_ref, y2_ref, y3_ref)):
        ya, yb = _unpack_halves(y_ref[...])
        term = gate[:, k:k + 1] * jnp.concatenate([ya, yb], axis=1)
        ffn = term if ffn is None else ffn + term
    out = h_ref[...] + g2_ref[...] * (_rms(ffn) * gpost_ref[...])

    @pl.when(i < CTX_TILES)
    def _():
        oc_ref[...] = out

    @pl.when(i >= CTX_TILES)
    def _():
        ol_ref[...] = out


def _combine(y_tok, gate, h2, mods3, g_post_ffn3, l):
    row = lambda n: pl.BlockSpec((TM, n), lambda i: (i, 0))
    choice = lambda k: pl.BlockSpec((TM, HALF), lambda i: (k * N_TILES + i, 0))
    return pl.pallas_call(
        _combine_kernel,
        grid=(N_TILES,),
        in_specs=[choice(0), choice(1), choice(2), choice(3), row(TOP_K), row(D_MODEL),
                  pl.BlockSpec((None, 1, D_MODEL), lambda i: (_mod_index(i), 0, 5)),
                  pl.BlockSpec((None, 1, D_MODEL), lambda i: (l, 0, 0))],
        out_specs=[_ctx_spec(D_MODEL), _lat_spec(D_MODEL)],
        out_shape=[jax.ShapeDtypeStruct((N_CTX, D_MODEL), F32), jax.ShapeDtypeStruct((N_LAT, D_MODEL), F32)],
        compiler_params=_cparams(("arbitrary",)),
        name="combine",
    )(y_tok, y_tok, y_tok, y_tok, gate, h2, mods3, g_post_ffn3)


def _route_kernel(topk_ref, count_ref, dest_ref, gate_ref, base_ref):
    i = pl.program_id(0)

    @pl.when(i == 0)
    def _():
        total = count_ref[...]
        padded = jnp.floor((total + (MOE_TILE - 1.0)) * (1.0 / MOE_TILE)) * MOE_TILE
        before = (lax.broadcasted_iota(jnp.int32, (N_EXPERTS, N_EXPERTS), 0)
                  < lax.broadcasted_iota(jnp.int32, (N_EXPERTS, N_EXPERTS), 1)).astype(F32)
        base_ref[...] = jnp.dot(padded, before, preferred_element_type=F32, precision=lax.Precision.HIGHEST)

    topk = topk_ref[...]
    lane = lax.broadcasted_iota(jnp.int32, (ROUTE_TM, N_EXPERTS), 1).astype(F32)
    hots = [lane == topk[:, k:k + 1] for k in range(TOP_K)]
    vals = [topk[:, TOP_K + k:TOP_K + k + 1] for k in range(TOP_K)]
    chosen = functools.reduce(jnp.add, [h.astype(F32) for h in hots])
    earlier_row = (lax.broadcasted_iota(jnp.int32, (ROUTE_TM, ROUTE_TM), 1)
                   < lax.broadcasted_iota(jnp.int32, (ROUTE_TM, ROUTE_TM), 0)).astype(BF16)
    earlier = jnp.dot(earlier_row, chosen.astype(BF16), preferred_element_type=F32)
    offs = base_ref[0:1, :] + earlier
    dest = [jnp.sum(jnp.where(h, offs, 0.0), axis=1, keepdims=True) for h in hots]
    dest_ref[...] = jnp.concatenate(dest, axis=1).astype(jnp.int32)
    e = [jnp.exp(v - vals[0]) for v in vals]
    den = functools.reduce(jnp.add, e)
    gate_ref[...] = jnp.concatenate(e, axis=1) / den
    base_ref[...] = base_ref[...] + jnp.sum(chosen, axis=0, keepdims=True)


ROUTE_TM = 1024


def _route(topk, counts):
    tile = lambda n: pl.BlockSpec((ROUTE_TM, n), lambda i: (i, 0))
    dest, gate = pl.pallas_call(
        _route_kernel,
        grid=(N_TOK // ROUTE_TM,),
        in_specs=[tile(2 * TOP_K), pl.BlockSpec((8, N_EXPERTS), lambda i: (0, 0))],
        out_specs=[tile(TOP_K), tile(TOP_K)],
        out_shape=[jax.ShapeDtypeStruct((N_TOK, TOP_K), jnp.int32),
                   jax.ShapeDtypeStruct((N_TOK, TOP_K), F32)],
        scratch_shapes=[pltpu.VMEM((8, N_EXPERTS), F32)],
        compiler_params=_cparams(("arbitrary",)),
        name="route",
    )(topk, counts)
    counts = counts[0].astype(jnp.int32)
    expert = jnp.arange(N_EXPERTS, dtype=jnp.int32)
    padded = (counts + MOE_TILE - 1) // MOE_TILE * MOE_TILE
    pad_end = jnp.sum(jnp.where(expert[None, :] <= expert[:, None], padded[None, :], 0), axis=1)
    block = jnp.arange(N_MOE_BLOCKS, dtype=jnp.int32)
    block_e = jnp.minimum(jnp.sum((pad_end[None, :] <= block[:, None] * MOE_TILE).astype(jnp.int32), axis=1),
                          N_EXPERTS - 1)
    n_valid = pad_end[-1] // MOE_TILE
    mine = expert[None, :] == block_e[:, None]
    pick = lambda v: jnp.sum(jnp.where(mine, v[None, :], 0), axis=1)
    offset = block * MOE_TILE - pick(pad_end - padded)
    valid = block < n_valid
    first = valid & (offset == 0)
    used = counts > 0
    slot = jnp.sum((used[None, :] & (expert[None, :] < block_e[:, None])).astype(jnp.int32), axis=1) % 2
    nxt = jnp.min(jnp.where(used[None, :] & (expert[None, :] > block_e[:, None]), expert[None, :], N_EXPERTS),
                  axis=1)
    nxt = jnp.where(nxt < N_EXPERTS, nxt, -1)
    rows = jnp.where(valid, jnp.clip(pick(counts) - offset, 0, MOE_TILE), 0)
    plan = tuple(a.astype(jnp.int32) for a in (block_e, n_valid[None], first, slot, nxt, rows))
    n_workers = N_TOK // SCATTER_TOKENS_PER_WORKER
    dest_sc = dest.reshape(n_workers, SCATTER_TOKENS_PER_WORKER // GATHER_CHUNK, GATHER_CHUNK, TOP_K)
    dest_sc = dest_sc.transpose(0, 1, 3, 2).reshape(n_workers, -1, GATHER_CHUNK)
    return gate, dest.T.reshape(-1), dest_sc, plan


def _rope_tables():
    t = jnp.arange(DEC_SEQ)
    inv = ROPE_THETA ** (-jnp.arange(ROPE_PAIRS, dtype=F32) / ROPE_PAIRS)
    row = (t // GRID_W).astype(F32)[:, None] * inv
    col = (t % GRID_W).astype(F32)[:, None] * inv
    zeros = jnp.zeros_like(row)
    cos = jnp.concatenate([jnp.cos(row), jnp.cos(row), jnp.cos(col), jnp.cos(col)], axis=1)
    s1 = jnp.concatenate([-jnp.sin(row), zeros, -jnp.sin(col), zeros], axis=1)
    s2 = jnp.concatenate([zeros, jnp.sin(row), zeros, jnp.sin(col)], axis=1)
    ident = lambda v: jnp.full((PROJ_TM, HEAD_DIM), v, F32)
    tables = [jnp.concatenate([x, ident(v)], axis=0) for x, v in ((cos, 1.0), (s1, 0.0), (s2, 0.0))]
    return [jnp.tile(x, (1, 2)) for x in tables]


def kernel(x_prompt, x_sample, cache_a, cache_b, cache_c, c, c_ctx, w_ada, b_ada, g_pre_mix, g_post_mix,
           g_pre_ffn, g_post_ffn, w_in, g_q_b, g_k_b, sink_a, rpb_c, w_br_a, w_br_b, w_br_c, w_out,
           w_router, b_router, w_gate_up, b_gate_up, w_down, b_down):
    h = (x_prompt.reshape(N_CTX, D_MODEL), x_sample.reshape(N_LAT, D_MODEL))
    cond8 = jnp.concatenate([c_ctx[None], c, jnp.zeros((3, D_MODEL), F32)], axis=0)
    cache_a = cache_a.astype(BF16).reshape(DEC_BATCH, DEPTH, 2, PAST_LEN, W_KV)
    cache_b = cache_b.astype(BF16).reshape(DEC_BATCH, DEPTH, 2, PAST_LEN, W_KV)
    cache_c = cache_c.astype(BF16).reshape(DEC_BATCH, DEPTH, 2, PAST_LEN, W_HEADS)
    cos_t, s1_t, s2_t = _rope_tables()
    bd = jnp.kron(jnp.eye(256 // HEAD_DIM, dtype=F32),
                  jnp.full((HEAD_DIM, HEAD_DIM), 1.0 / HEAD_DIM, F32)).astype(BF16)
    vec3 = lambda a: a.reshape(DEPTH, 1, a.shape[-1])
    scale = HEAD_DIM ** -0.5
    states = []
    mods_all = _adaln(cond8, w_ada, vec3(b_ada))
    for l in range(DEPTH):
        w = w_in[l]
        pair_cols = lambda a: a.reshape(D_MODEL, 2, 4, HEAD_DIM).transpose(0, 2, 1, 3).reshape(D_MODEL, 512)
        pair_rows = lambda a: a.reshape(2, 4, HEAD_DIM, D_MODEL).transpose(1, 0, 2, 3).reshape(512, D_MODEL)
        w_in_p = jnp.concatenate(
            [pair_cols(w[:, 0:512]) * scale, pair_cols(w[:, 768:1280]), w[:, 1536:2048] * scale,
             w[:, 2048:2560], w[:, 2560:3072],
             w[:, 512:640], w[:, 1280:1408], w[:, 640:768], w[:, 1408:1536], w[:, 3072:]], axis=1).astype(BF16)
        gq = (jnp.tile(g_q_b[l], H_B) * scale)[None]
        gk = jnp.tile(g_k_b[l], KV_B)[None]
        w_br = jnp.stack([pair_rows(w_br_a[l]), pair_rows(w_br_b[l]), w_br_c[l]], axis=0).astype(BF16)
        w_out_b = w_out[l].astype(BF16)
        w_r_hi = w_router[l].astype(BF16)
        w_r_lo = (w_router[l] - w_r_hi.astype(F32)).astype(BF16)
        w_router2 = jnp.concatenate([w_r_hi, w_r_lo], axis=1)

        mods3 = mods_all[l].reshape(8, 1, 6 * D_MODEL)
        q_all, kvab, kvc, gates, *states = _proj(h, mods3, vec3(g_pre_mix), w_in_p, gq, gk, bd,
                                                 cos_t, s1_t, s2_t, tuple(states), l)
        o_ctx = _ctx_attn(sink_a, q_all, kvab, kvc, l)
        o_a = _win_attn(sink_a, q_all, kvab, cache_a, l)
        o_b = _dense_attn(q_all, kvab, cache_b, l)
        o_c = _nbr_attn(q_all, kvc, cache_c, _nbr_bias_table(rpb_c[l]), l)
        h2, hn2, topk, counts = _merge(o_ctx, o_a, o_b, o_c, gates, h, mods3, w_br, w_out_b,
                                       vec3(g_post_mix), vec3(g_pre_ffn), w_router2, vec3(b_router), l)
        gate, dest, dest_sc, plan = _route(topk, counts)
        x_slots = _scatter_rows(hn2, dest_sc, N_SLOTS)
        y_slots = _moe(plan, x_slots, w_gate_up,
                       b_gate_up.reshape(DEPTH, N_EXPERTS, 1, 2 * D_FF), w_down,
                       b_down.reshape(DEPTH, N_EXPERTS, 1, D_MODEL), l)
        y_tok = _gather_rows(y_slots, dest)
        h = _combine(y_tok, gate, h2, mods3, vec3(g_post_ffn), l)

    state_a, state_b, state_c = states
    heads = lambda s, n: s.reshape(BATCH, DEPTH, 2, SEQ, n, HEAD_DIM)
    return (h[0].reshape(BATCH, SEQ, D_MODEL), h[1].reshape(DEC_BATCH, DEC_SEQ, D_MODEL),
            heads(state_a, KV_A), heads(state_b, KV_B), heads(state_c, H_C))
```

```python
import functools

import jax
import jax.numpy as jnp
from jax import lax
from jax.experimental import pallas as pl
from jax.experimental.pallas import tpu as pltpu
from jax.experimental.pallas import tpu_sc as plsc

D_MODEL = 1024
BATCH = 32
SEQ = 256
DEPTH = 2
DEC_BATCH = 4
DEC_SEQ = 2048
PAST_LEN = 512
GRID_W = 64
HEAD_DIM = 64
H_A = 8
KV_A = 2
H_B = 8
KV_B = 2
H_C = 8
WINDOW_A = 128
NA_ROWS = 8
NA_COLS = 16
ROPE_THETA = 10000.0
ROPE_PAIRS = HEAD_DIM // 4
N_EXPERTS = 32
TOP_K = 4
D_FF = D_MODEL
SWIGLU_ALPHA = 1.702
SWIGLU_LIMIT = 7.0
EPS = 1e-6

W_HEADS = H_A * HEAD_DIM
W_KV = KV_A * HEAD_DIM
N_CTX = BATCH * SEQ
N_LAT = DEC_BATCH * DEC_SEQ
N_TOK = N_CTX + N_LAT
GRID_ROWS = DEC_SEQ // GRID_W
D_IN = 3 * W_HEADS + 4 * W_KV + 2 * W_HEADS + 3 * D_MODEL

TM = 256
N_TILES = N_TOK // TM
CTX_TILES = N_CTX // TM
LAT_TILES_PER_BATCH = DEC_SEQ // TM
TQ = 128
MOE_CHAIN = 256
MOE_TILE = 2 * MOE_CHAIN
N_SLOTS = N_TOK * TOP_K + N_EXPERTS * MOE_TILE
N_MOE_BLOCKS = N_SLOTS // MOE_TILE
NEG = -1e30
VMEM_LIMIT = 56 * 1024 * 1024

BF16 = jnp.bfloat16
F32 = jnp.float32


def _cparams(sem):
    return pltpu.CompilerParams(dimension_semantics=sem, vmem_limit_bytes=VMEM_LIMIT)


def _mod_index(i):
    return jnp.where(i < CTX_TILES, 0, 1 + (i - CTX_TILES) // LAT_TILES_PER_BATCH)


def _rms(x):
    return x * lax.rsqrt(jnp.mean(x * x, axis=-1, keepdims=True) + EPS)


HALF = D_MODEL // 2


def _pack_halves(x):
    hi = lax.bitcast_convert_type(x[:, :HALF].astype(BF16).astype(F32), jnp.uint32)
    lo = lax.bitcast_convert_type(x[:, HALF:].astype(BF16).astype(F32), jnp.uint32)
    return lax.bitcast_convert_type(hi | (lo >> 16), jnp.int32)


def _unpack_halves(w):
    u = lax.bitcast_convert_type(w, jnp.uint32)
    return (lax.bitcast_convert_type(u & jnp.uint32(0xFFFF0000), F32),
            lax.bitcast_convert_type(u << 16, F32))


GATHER_CHUNK = 64


def _gather_rows(table, idx):
    n = idx.shape[0]
    width = table.shape[1]
    info = plsc.get_sparse_core_info()
    n_workers = info.num_cores * info.num_subcores
    per_worker = n // n_workers
    n_chunks = per_worker // GATHER_CHUNK
    assert per_worker * n_workers == n and n_chunks * GATHER_CHUNK == per_worker and n_chunks % 2 == 0
    mesh = plsc.VectorSubcoreMesh(core_axis_name="core", subcore_axis_name="subcore")

    @functools.partial(
        pl.kernel, out_type=jax.ShapeDtypeStruct((n, width), table.dtype), mesh=mesh,
        scratch_types=[pltpu.VMEM((per_worker,), jnp.int32),
                       pltpu.VMEM((2, GATHER_CHUNK, width), table.dtype),
                       pltpu.SemaphoreType.DMA((2,)), pltpu.SemaphoreType.DMA((2,))])
    def gather(table_hbm, idx_hbm, out_hbm, idx_v, rows_v, gather_sem, write_sem):
        worker = lax.axis_index("subcore") * info.num_cores + lax.axis_index("core")
        base = worker * per_worker
        pltpu.sync_copy(idx_hbm.at[pl.ds(base, per_worker)], idx_v)

        def fetch(chunk, slot):
            rows = idx_v.at[pl.ds(chunk * GATHER_CHUNK, GATHER_CHUNK)]
            return pltpu.make_async_copy(table_hbm.at[rows], rows_v.at[slot], gather_sem.at[slot])

        def write(chunk, slot):
            dst = out_hbm.at[pl.ds(base + chunk * GATHER_CHUNK, GATHER_CHUNK)]
            return pltpu.make_async_copy(rows_v.at[slot], dst, write_sem.at[slot])

        fetch(0, 0).start()

        @pl.loop(0, n_chunks, step=2)
        def _(c):
            @pl.when(c > 0)
            def _():
                write(c - 1, 1).wait()

            fetch(c + 1, 1).start()
            fetch(c, 0).wait()
            write(c, 0).start()
            write(c, 0).wait()

            @pl.when(c + 2 < n_chunks)
            def _():
                fetch(c + 2, 0).start()

            fetch(c + 1, 1).wait()
            write(c + 1, 1).start()

        write(n_chunks - 1, 1).wait()

    return gather(table, idx)


SC_WORKERS_V7X = 32
SCATTER_TOKENS_PER_WORKER = N_TOK // SC_WORKERS_V7X


def _scatter_rows(table, dest_sc, n_out):
    width = table.shape[1]
    info = plsc.get_sparse_core_info()
    assert info.num_cores * info.num_subcores == SC_WORKERS_V7X
    per_worker = SCATTER_TOKENS_PER_WORKER
    n_chunks = per_worker // GATHER_CHUNK
    assert n_chunks % 2 == 0 and dest_sc.shape == (SC_WORKERS_V7X, n_chunks * TOP_K, GATHER_CHUNK)
    mesh = plsc.VectorSubcoreMesh(core_axis_name="core", subcore_axis_name="subcore")

    @functools.partial(
        pl.kernel, out_type=jax.ShapeDtypeStruct((n_out, width), table.dtype), mesh=mesh,
        scratch_types=[pltpu.VMEM((n_chunks * TOP_K, GATHER_CHUNK), jnp.int32),
                       pltpu.VMEM((2, GATHER_CHUNK, width), table.dtype),
                       pltpu.SemaphoreType.DMA((2,)), pltpu.SemaphoreType.DMA((2,))])
    def scatter(table_hbm, dest_hbm, out_hbm, idx_v, rows_v, read_sem, write_sem):
        worker = lax.axis_index("subcore") * info.num_cores + lax.axis_index("core")
        base = worker * per_worker
        pltpu.sync_copy(dest_hbm.at[worker], idx_v)

        def read(chunk, slot):
            src = table_hbm.at[pl.ds(base + chunk * GATHER_CHUNK, GATHER_CHUNK)]
            return pltpu.make_async_copy(src, rows_v.at[slot], read_sem.at[slot])

        def writes(chunk, slot):
            return [pltpu.make_async_copy(rows_v.at[slot], out_hbm.at[idx_v.at[chunk * TOP_K + k]],
                                          write_sem.at[slot]) for k in range(TOP_K)]

        read(0, 0).start()

        @pl.loop(0, n_chunks, step=2)
        def _(c):
            @pl.when(c > 0)
            def _():
                for cp in writes(c - 1, 1):
                    cp.wait()

            read(c + 1, 1).start()
            read(c, 0).wait()
            for cp in writes(c, 0):
                cp.start()
            for cp in writes(c, 0):
                cp.wait()

            @pl.when(c + 2 < n_chunks)
            def _():
                read(c + 2, 0).start()

            read(c + 1, 1).wait()
            for cp in writes(c + 1, 1):
                cp.start()

        for cp in writes(n_chunks - 1, 1):
            cp.wait()

    return scatter(table, dest_sc)


def _adaln_kernel(c_ref, w_ref, b_ref, o_ref):
    c = c_ref[...]
    s = c / (1.0 + jnp.exp(-c))
    o_ref[...] = jnp.dot(s, w_ref[...], preferred_element_type=F32,
                         precision=lax.Precision.HIGHEST) + b_ref[...]


def _adaln(cond8, w_ada, b_ada3):
    tn = 768
    return pl.pallas_call(
        _adaln_kernel,
        grid=(DEPTH, 6 * D_MODEL // tn),
        in_specs=[pl.BlockSpec((8, D_MODEL), lambda l, j: (0, 0)),
                  pl.BlockSpec((None, D_MODEL, tn), lambda l, j: (l, 0, j)),
                  pl.BlockSpec((None, 1, tn), lambda l, j: (l, 0, j))],
        out_specs=pl.BlockSpec((None, 8, tn), lambda l, j: (l, 0, j)),
        out_shape=jax.ShapeDtypeStruct((DEPTH, 8, 6 * D_MODEL), F32),
        compiler_params=_cparams(("arbitrary", "arbitrary")),
        name="adaln",
    )(cond8, w_ada, b_ada3)


C_QA, C_QB, C_QC, C_KC, C_VC, C_KAB, C_GL = 0, 512, 1024, 1536, 2048, 2560, 3072


def _proj_kernel(hc_ref, hl_ref, sh_ref, sc_ref, gpre_ref, w_ref, gq_ref, gk_ref, bd_ref,
                 cos_ref, s1_ref, s2_ref, *rest):
    q_ref, kvab_ref, kvc_ref, gates_ref, sta_ref, stb_ref, stc_ref = rest[-7:]
    i = pl.program_id(0)
    bd = bd_ref[...]

    def rope(t, rs):
        cos, s1, s2 = cos_ref[rs, :], s1_ref[rs, :], s2_ref[rs, :]
        parts = []
        for g in range(t.shape[1] // 128):
            tg = t[:, g * 128:(g + 1) * 128]
            parts.append(tg * cos + pltpu.roll(tg, 112, 1) * s1 + pltpu.roll(tg, 16, 1) * s2)
        return parts[0] if len(parts) == 1 else jnp.concatenate(parts, axis=1)

    def headnorm(t, g):
        sq = (t * t).astype(BF16)
        n = t.shape[1]
        if n == 128:
            ms = jnp.dot(sq, bd[:128, :128], preferred_element_type=F32)
        else:
            ms = jnp.concatenate(
                [jnp.dot(sq[:, c:c + 256], bd, preferred_element_type=F32) for c in range(0, n, 256)],
                axis=1)
        return t * lax.rsqrt(ms + EPS) * g

    kv_f32 = []
    for c in range(PROJ_TM // TM):
        rs = slice(c * TM, (c + 1) * TM)
        hn = _rms(jnp.where(i < PROJ_CTX_TILES, hc_ref[rs, :], hl_ref[rs, :])) * gpre_ref[...]
        hb = (hn * (1.0 + sc_ref[...]) + sh_ref[...]).astype(BF16)
        proj = lambda c0, n, hb=hb: jnp.dot(hb, w_ref[:, c0:c0 + n], preferred_element_type=F32)
        q_ref[rs, 0:512] = rope(proj(C_QA, 512), rs).astype(BF16)
        q_ref[rs, 512:1024] = rope(headnorm(proj(C_QB, 512), gq_ref[...]), rs).astype(BF16)
        q_ref[rs, 1024:1536] = proj(C_QC, 512).astype(BF16)
        kc = proj(C_KC, 512)
        vc = proj(C_VC, 512)
        kvc_ref[rs, 0:512] = kc.astype(BF16)
        kvc_ref[rs, 512:1024] = vc.astype(BF16)
        kab = proj(C_KAB, 512)
        ka = kab[:, 0:128]
        kb = headnorm(kab[:, 128:256], gk_ref[...])
        kvab_ref[rs, 0:128] = rope(ka, rs).astype(BF16)
        kvab_ref[rs, 128:256] = rope(kb, rs).astype(BF16)
        kvab_ref[rs, 256:512] = kab[:, 256:512].astype(BF16)
        for j in range(6):
            gl = proj(C_GL + j * 512, 512)
            gates_ref[rs, j * 512:(j + 1) * 512] = (1.0 / (1.0 + jnp.exp(-gl))).astype(BF16)
        kv_f32.append(((ka, kab[:, 256:384]), (kb, kab[:, 384:512]), (kc, vc)))

    @pl.when(i < PROJ_CTX_TILES)
    def _():
        for c, per_mixer in enumerate(kv_f32):
            for st_ref, (k, v) in zip((sta_ref, stb_ref, stc_ref), per_mixer):
                if len(st_ref.shape) == 5:
                    st_ref[c, 0, 0] = k
                    st_ref[c, 0, 1] = v
                    st_ref[c, 1:] = jnp.zeros((DEPTH - 1,) + tuple(st_ref.shape[2:]), F32)
                else:
                    st_ref[c, 0] = k
                    st_ref[c, 1] = v


PROJ_TM = 2 * TM
PROJ_TILES = N_TOK // PROJ_TM
PROJ_CTX_TILES = N_CTX // PROJ_TM
PROJ_LAT_TILES_PER_BATCH = DEC_SEQ // PROJ_TM


def _proj(h, mods3, g_pre3, w_in_p, gq, gk, bd, cos_t, s1_t, s2_t, prev_states, l):
    lat_tile = lambda i: i - PROJ_CTX_TILES
    rope_idx = lambda i: jnp.where(i < PROJ_CTX_TILES, PROJ_LAT_TILES_PER_BATCH,
                                   lat_tile(i) % PROJ_LAT_TILES_PER_BATCH)
    mod_idx = lambda i: jnp.where(i < PROJ_CTX_TILES, 0, 1 + lat_tile(i) // PROJ_LAT_TILES_PER_BATCH)
    ctx_block = lambda i: jnp.minimum(i, PROJ_CTX_TILES - 1)
    const = lambda shape: pl.BlockSpec(shape, lambda i: (0,) * len(shape), pipeline_mode=pl.Buffered(1))
    rope_spec = pl.BlockSpec((PROJ_TM, 128), lambda i: (rope_idx(i), 0))
    row = lambda n: pl.BlockSpec((PROJ_TM, n), lambda i: (i, 0))
    per_step = PROJ_TM // SEQ
    if l == 0:
        state_spec = lambda n: pl.BlockSpec((per_step, DEPTH, 2, SEQ, n), lambda i: (ctx_block(i), 0, 0, 0, 0))
    else:
        state_spec = lambda n: pl.BlockSpec((per_step, None, 2, SEQ, n), lambda i: (ctx_block(i), l, 0, 0, 0))
    state_shape = lambda n: jax.ShapeDtypeStruct((BATCH, DEPTH, 2, SEQ, n), F32)
    n_in = 12
    return pl.pallas_call(
        _proj_kernel,
        grid=(PROJ_TILES,),
        in_specs=[pl.BlockSpec((PROJ_TM, D_MODEL), lambda i: (ctx_block(i), 0)),
                  pl.BlockSpec((PROJ_TM, D_MODEL), lambda i: (jnp.maximum(lat_tile(i), 0), 0)),
                  pl.BlockSpec((None, 1, D_MODEL), lambda i: (mod_idx(i), 0, 0)),
                  pl.BlockSpec((None, 1, D_MODEL), lambda i: (mod_idx(i), 0, 1)),
                  pl.BlockSpec((None, 1, D_MODEL), lambda i: (l, 0, 0)),
                  const((D_MODEL, D_IN)), const((1, 512)), const((1, 128)), const((256, 256)),
                  rope_spec, rope_spec, rope_spec] + [pl.BlockSpec(memory_space=pl.ANY)] * len(prev_states),
        out_specs=[row(1536), row(512), row(1024), row(3072),
                   state_spec(W_KV), state_spec(W_KV), state_spec(W_HEADS)],
        out_shape=[jax.ShapeDtypeStruct((N_TOK, 1536), BF16),
                   jax.ShapeDtypeStruct((N_TOK, 512), BF16),
                   jax.ShapeDtypeStruct((N_TOK, 1024), BF16),
                   jax.ShapeDtypeStruct((N_TOK, 3072), BF16),
                   state_shape(W_KV), state_shape(W_KV), state_shape(W_HEADS)],
        input_output_aliases={n_in + j: 4 + j for j in range(len(prev_states))},
        compiler_params=_cparams(("arbitrary",)),
        name="proj",
    )(*h, mods3, mods3, g_pre3, w_in_p, gq, gk, bd, cos_t, s1_t, s2_t, *prev_states)


def _qk(q, k):
    return lax.dot_general(q, k, (((1,), (1,)), ((), ())), preferred_element_type=F32)


def _softmax_pv(scores, values, sink=None, halves=None):
    m = functools.reduce(jnp.maximum, [jnp.max(s, axis=-1, keepdims=True) for s in scores])
    if sink is not None:
        m = jnp.maximum(m, sink)
    if halves is None:
        ps = [jnp.exp(s - m) for s in scores]
        den = functools.reduce(jnp.add, [jnp.sum(p, axis=-1, keepdims=True) for p in ps])
        if sink is not None:
            den = den + jnp.exp(sink - m)
        o = functools.reduce(jnp.add, [jnp.dot(p.astype(BF16), v, preferred_element_type=F32)
                                       for p, v in zip(ps, values)])
        return o / den
    ps = [jnp.exp(s - m).astype(BF16) for s in scores]
    rows = ps[0].shape[0]
    split = {"both": rows // 2, "lo": rows, "hi": 0}[halves]
    one = jnp.ones((), BF16)

    def pv(r0, r1, keep_lo):
        acc = None
        for p, v in zip(ps, values):
            lo = _lo_lanes(v.shape[0])
            v1 = jnp.where(lo, v, one) if keep_lo else jnp.where(lo, one, v)
            t = jnp.dot(p[r0:r1], v1, preferred_element_type=F32)
            acc = t if acc is None else acc + t
        return acc

    parts = ([pv(0, split, True)] if split > 0 else []) + ([pv(split, rows, False)] if split < rows else [])
    o = parts[0] if len(parts) == 1 else jnp.concatenate(parts, axis=0)
    if sink is not None:
        is_lo_row = lax.broadcasted_iota(jnp.int32, (rows, 128), 0) < split
        o = o + jnp.where(is_lo_row != _lo_lanes(rows), jnp.exp(sink - m), 0.0)
    return o * pltpu.roll(1.0 / o, HEAD_DIM, 1)


def _lo_lanes(rows):
    return lax.broadcasted_iota(jnp.int32, (rows, 128), 1) < HEAD_DIM


def _stack_pairs(q, n_pairs):
    lo = _lo_lanes(q.shape[0])
    zero = jnp.zeros((q.shape[0], 128), q.dtype)
    pairs = [q[:, p * 128:(p + 1) * 128] for p in range(n_pairs)]
    return jnp.concatenate([jnp.where(lo, x, zero) for x in pairs] + [jnp.where(lo, zero, x) for x in pairs],
                           axis=0)


def _unstack_pairs(o, n_pairs):
    rows = o.shape[0] // (2 * n_pairs)
    lo = _lo_lanes(rows)
    return jnp.concatenate(
        [jnp.where(lo, o[p * rows:(p + 1) * rows], o[(n_pairs + p) * rows:(n_pairs + p + 1) * rows])
         for p in range(n_pairs)], axis=1)


def _sink_column(sink_ref, l, rows):
    return jnp.concatenate([jnp.full((rows, 1), sink_ref[l, h], F32) for h in range(H_A)], axis=0)


def _ctx_attn_kernel(l, sink_ref, q_ref, kvab_ref, kvc_ref, o_ref):
    qa = _stack_pairs(q_ref[:, 0:512], 4)
    o = _softmax_pv([_qk(qa, kvab_ref[:, 0:128])], [kvab_ref[:, 256:384]], _sink_column(sink_ref, l, SEQ))
    o_ref[:, 0:512] = _unstack_pairs(o, 4).astype(BF16)
    qb = _stack_pairs(q_ref[:, 512:1024], 4)
    o = _softmax_pv([_qk(qb, kvab_ref[:, 128:256])], [kvab_ref[:, 384:512]])
    o_ref[:, 512:1024] = _unstack_pairs(o, 4).astype(BF16)
    for hp in range(H_C // 2):
        cs = slice(hp * 128, (hp + 1) * 128)
        qc = _stack_pairs(q_ref[:, 1024 + hp * 128:1024 + (hp + 1) * 128], 1)
        o = _softmax_pv([_qk(qc, kvc_ref[:, cs])], [kvc_ref[:, 512 + hp * 128:512 + (hp + 1) * 128]])
        o_ref[:, 1024 + hp * 128:1024 + (hp + 1) * 128] = _unstack_pairs(o, 1).astype(BF16)


def _ctx_attn(sink_a, q_all, kvab, kvc, l):
    row = lambda n: pl.BlockSpec((SEQ, n), lambda b: (b, 0))
    return pl.pallas_call(
        functools.partial(_ctx_attn_kernel, l),
        grid=(BATCH,),
        in_specs=[pl.BlockSpec(memory_space=pltpu.SMEM), row(1536), row(512), row(1024)],
        out_specs=row(1536),
        out_shape=jax.ShapeDtypeStruct((N_CTX, 1536), BF16),
        compiler_params=_cparams(("arbitrary",)),
        name="ctx_attn",
    )(sink_a, q_all, kvab, kvc)


def _win_attn_kernel(l, sink_ref, q_ref, prev_ref, cur_ref, nxt_ref, ck_ref, cv_ref, o_ref):
    n = pl.program_id(1)
    nb = DEC_SEQ // TQ
    rows = H_A * TQ
    qpos = lax.broadcasted_iota(jnp.int32, (rows, TQ), 0) % TQ
    kpos = lax.broadcasted_iota(jnp.int32, (rows, TQ), 1)
    mask_prev = (kpos >= qpos) & (n > 0)
    mask_next = (kpos <= qpos) & (n < nb - 1)
    ks, vs = slice(0, 128), slice(256, 384)
    qs = _stack_pairs(q_ref[...], 4)
    s_prev = jnp.where(mask_prev, _qk(qs, prev_ref[:, ks]), NEG)
    s_cur = _qk(qs, cur_ref[:, ks])
    s_next = jnp.where(mask_next, _qk(qs, nxt_ref[:, ks]), NEG)
    s_ctx = _qk(qs, ck_ref[...])
    o = _softmax_pv([s_prev, s_cur, s_next, s_ctx],
                    [prev_ref[:, vs], cur_ref[:, vs], nxt_ref[:, vs], cv_ref[...]],
                    _sink_column(sink_ref, l, TQ), halves="both")
    o_ref[...] = _unstack_pairs(o, 4).astype(BF16)


def _win_attn(sink_a, q_all, kvab, cache_a, l):
    nb = DEC_SEQ // TQ
    base = N_CTX // TQ
    kv_spec = lambda f: pl.BlockSpec((TQ, 512), lambda b, n: (base + b * nb + f(n), 0))
    cache_spec = lambda s: pl.BlockSpec((None, None, None, PAST_LEN, W_KV), lambda b, n: (b, l, s, 0, 0))
    return pl.pallas_call(
        functools.partial(_win_attn_kernel, l),
        grid=(DEC_BATCH, nb),
        in_specs=[pl.BlockSpec(memory_space=pltpu.SMEM),
                  pl.BlockSpec((TQ, 512), lambda b, n: (base + b * nb + n, 0)),
                  kv_spec(lambda n: jnp.maximum(n - 1, 0)), kv_spec(lambda n: n),
                  kv_spec(lambda n: jnp.minimum(n + 1, nb - 1)),
                  cache_spec(0), cache_spec(1)],
        out_specs=pl.BlockSpec((TQ, 512), lambda b, n: (b * nb + n, 0)),
        out_shape=jax.ShapeDtypeStruct((N_LAT, 512), BF16),
        compiler_params=_cparams(("arbitrary", "arbitrary")),
        name="win_attn",
    )(sink_a, q_all, kvab, kvab, kvab, cache_a, cache_a)


DENSE_KEY_CHUNK = 1024
DENSE_TQ = 256


def _online_softmax_pv(q, key_chunks, value_chunks, keep_lo):
    one = jnp.ones((), BF16)
    m = acc = None
    for k, v in zip(key_chunks, value_chunks):
        s = _qk(q, k)
        m_new = jnp.max(s, axis=-1, keepdims=True)
        if m is not None:
            m_new = jnp.maximum(m, m_new)
        p = jnp.exp(s - m_new).astype(BF16)
        lo = _lo_lanes(v.shape[0])
        pv = jnp.dot(p, jnp.where(lo, v, one) if keep_lo else jnp.where(lo, one, v), preferred_element_type=F32)
        acc = pv if acc is None else acc * jnp.exp(m - m_new) + pv
        m = m_new
    return acc * pltpu.roll(1.0 / acc, HEAD_DIM, 1)


def _dense_attn_kernel(q_ref, kv_ref, ck_ref, cv_ref, o_ref):
    qs = _stack_pairs(q_ref[...], 4)
    half = qs.shape[0] // 2
    starts = range(0, DEC_SEQ, DENSE_KEY_CHUNK)
    keys = [kv_ref[c:c + DENSE_KEY_CHUNK, 128:256] for c in starts] + [ck_ref[...]]
    values = [kv_ref[c:c + DENSE_KEY_CHUNK, 384:512] for c in starts] + [cv_ref[...]]
    outs = [_online_softmax_pv(qs[g * half:(g + 1) * half], keys, values, g == 0) for g in range(KV_B)]
    o_ref[...] = _unstack_pairs(jnp.concatenate(outs, axis=0), 4).astype(BF16)


def _dense_attn(q_all, kvab, cache_b, l):
    nb = DEC_SEQ // DENSE_TQ
    base = N_CTX // DENSE_TQ
    cache_spec = lambda s: pl.BlockSpec((None, None, None, PAST_LEN, W_KV), lambda b, n: (b, l, s, 0, 0))
    return pl.pallas_call(
        _dense_attn_kernel,
        grid=(DEC_BATCH, nb),
        in_specs=[pl.BlockSpec((DENSE_TQ, 512), lambda b, n: (base + b * nb + n, 1)),
                  pl.BlockSpec((DEC_SEQ, 512), lambda b, n: (N_CTX // DEC_SEQ + b, 0)),
                  cache_spec(0), cache_spec(1)],
        out_specs=pl.BlockSpec((DENSE_TQ, 512), lambda b, n: (b * nb + n, 0)),
        out_shape=jax.ShapeDtypeStruct((N_LAT, 512), BF16),
        compiler_params=_cparams(("arbitrary", "arbitrary")),
        name="dense_attn",
    )(q_all, kvab, cache_b, cache_b)


NBR_BAND = 4
NBR_Q = NBR_BAND * GRID_W
NBR_WIN_ROWS = 12
NBR_N_BANDS = GRID_ROWS // NBR_BAND
NBR_KBLK = NBR_Q
NBR_WIN_BLOCKS = NBR_WIN_ROWS * GRID_W // NBR_KBLK
NBR_LAST_KB = (GRID_ROWS - NBR_WIN_ROWS) * GRID_W // NBR_KBLK


def _nbr_window_block(band):
    return jnp.clip(band - 1, 0, NBR_LAST_KB)


def _nbr_attn_kernel(q_ref, k0_ref, k1_ref, k2_ref, ck_ref, cv_ref, bias_ref, o_ref):
    k_refs = (k0_ref, k1_ref, k2_ref)
    for hp in range(H_C // 2):
        cs = slice(hp * 128, (hp + 1) * 128)
        vs = slice(512 + hp * 128, 512 + (hp + 1) * 128)
        qs = _stack_pairs(q_ref[:, cs], 1)
        bias = jnp.concatenate([bias_ref[2 * hp], bias_ref[2 * hp + 1]], axis=0)
        scores = [_qk(qs, kr[:, cs]) + bias[:, j * NBR_KBLK:(j + 1) * NBR_KBLK] for j, kr in enumerate(k_refs)]
        scores.append(_qk(qs, ck_ref[:, cs]))
        o = _softmax_pv(scores, [kr[:, vs] for kr in k_refs] + [cv_ref[:, cs]])
        o_ref[:, cs] = _unstack_pairs(o, 1).astype(BF16)


def _nbr_attn(q_all, kvc, cache_c, bias_t, l):
    q_base = N_CTX // NBR_Q
    k_base = N_CTX // NBR_KBLK
    blocks_per_batch = DEC_SEQ // NBR_KBLK
    cache_spec = lambda s: pl.BlockSpec((None, None, None, PAST_LEN, W_HEADS), lambda band, b: (b, l, s, 0, 0))
    key_spec = lambda j: pl.BlockSpec(
        (NBR_KBLK, 1024), lambda band, b: (k_base + b * blocks_per_batch + _nbr_window_block(band) + j, 0))
    band_type = lambda band: jnp.where(band == 0, 0, jnp.where(band == NBR_N_BANDS - 1, 2, 1))
    return pl.pallas_call(
        _nbr_attn_kernel,
        grid=(NBR_N_BANDS, DEC_BATCH),
        in_specs=[pl.BlockSpec((NBR_Q, 512), lambda band, b: (q_base + b * NBR_N_BANDS + band, 2)),
                  key_spec(0), key_spec(1), key_spec(2), cache_spec(0), cache_spec(1),
                  pl.BlockSpec((None, H_C, NBR_Q, NBR_WIN_ROWS * GRID_W),
                               lambda band, b: (band_type(band), 0, 0, 0))],
        out_specs=pl.BlockSpec((NBR_Q, 512), lambda band, b: (b * NBR_N_BANDS + band, 0)),
        out_shape=jax.ShapeDtypeStruct((N_LAT, 512), BF16),
        compiler_params=_cparams(("arbitrary", "arbitrary")),
        name="nbr_attn",
    )(q_all, kvc, kvc, kvc, cache_c, cache_c, bias_t)


def _nbr_bias_table(rpb_l):
    c = jnp.arange(GRID_W)[:, None]
    kc = jnp.arange(GRID_W)[None, :]
    c_start = jnp.clip(c - NA_COLS // 2, 0, GRID_W - NA_COLS)
    valid = (kc >= c_start) & (kc < c_start + NA_COLS)
    pad = GRID_W - NA_COLS
    rpb_pad = jnp.pad(rpb_l.astype(F32), ((0, 0), (0, 0), (pad, pad)))
    toeplitz = jnp.stack([rpb_pad[:, :, GRID_W - 1 - q:2 * GRID_W - 1 - q] for q in range(GRID_W)], axis=2)
    t = jnp.where(valid[None, None], toeplitz, NEG)
    neg = jnp.full((H_C, GRID_W, GRID_W), NEG, F32)
    tables = []
    for r0 in (0, NBR_BAND, GRID_ROWS - NBR_BAND):
        k0 = min(max(r0 - NA_ROWS // 2, 0), GRID_ROWS - NBR_WIN_ROWS)
        rows = []
        for dq in range(NBR_BAND):
            r = r0 + dq
            start = min(max(r - NA_ROWS // 2, 0), GRID_ROWS - NA_ROWS)
            cols = []
            for i in range(NBR_WIN_ROWS):
                kr = k0 + i
                cols.append(t[:, kr - r + NA_ROWS - 1] if start <= kr < start + NA_ROWS else neg)
            rows.append(jnp.concatenate(cols, axis=2))
        tables.append(jnp.concatenate(rows, axis=1))
    return jnp.stack(tables, axis=0)


def _merge_kernel(octx_ref, oa_ref, ob_ref, oc_ref, gates_ref, hc_ref, hl_ref, g1_ref, sh2_ref, sc2_ref,
                  wbr_ref, wout_ref, gpost_ref, gpre_ref, wr_ref, br_ref,
                  h2_ref, hn2_ref, topk_ref, count_ref):
    i = pl.program_id(0)
    is_ctx = i < MERGE_CTX_TILES
    chosen = None
    for r0 in range(0, MERGE_TM, TM):
        rs = slice(r0, r0 + TM)
        merged = None
        for j, lat_ref in enumerate((oa_ref, ob_ref, oc_ref)):
            o = jnp.where(is_ctx, octx_ref[rs, j * 512:(j + 1) * 512], lat_ref[rs, :])
            br = jnp.dot(o, wbr_ref[j], preferred_element_type=F32)
            term = gates_ref[rs, j * D_MODEL:(j + 1) * D_MODEL].astype(F32) * br
            merged = term if merged is None else merged + term
        t = jnp.dot(merged.astype(BF16), wout_ref[...], preferred_element_type=F32)
        h2 = jnp.where(is_ctx, hc_ref[rs, :], hl_ref[rs, :]) + g1_ref[...] * (_rms(t) * gpost_ref[...])
        h2_ref[rs, :] = h2
        hn2 = _rms(h2) * gpre_ref[...] * (1.0 + sc2_ref[...]) + sh2_ref[...]
        hn2_ref[rs, :] = _pack_halves(hn2)
        x_hi = hn2.astype(BF16)
        x_lo = (hn2 - x_hi.astype(F32)).astype(BF16)
        hi = jnp.dot(x_hi, wr_ref[...], preferred_element_type=F32)
        lo = jnp.dot(x_lo, wr_ref[:, :N_EXPERTS], preferred_element_type=F32)
        logits = hi[:, :N_EXPERTS] + (hi[:, N_EXPERTS:] + lo) + br_ref[...]
        lane = lax.broadcasted_iota(jnp.int32, (TM, N_EXPERTS), 1)
        idxs, vals = [], []
        for _ in range(TOP_K):
            idx = jnp.argmax(logits, axis=1, keepdims=True).astype(jnp.int32)
            hot = lane == idx
            idxs.append(idx.astype(F32))
            vals.append(jnp.max(logits, axis=1, keepdims=True))
            chosen = hot.astype(F32) if chosen is None else chosen + hot.astype(F32)
            logits = jnp.where(hot, -jnp.inf, logits)
        topk_ref[rs, :] = jnp.concatenate(idxs + vals, axis=1)

    @pl.when(i == 0)
    def _():
        count_ref[...] = jnp.zeros_like(count_ref)

    count_ref[...] = count_ref[...] + jnp.sum(chosen, axis=0, keepdims=True)


MERGE_TM = 2 * TM
MERGE_TILES = N_TOK // MERGE_TM
MERGE_CTX_TILES = N_CTX // MERGE_TM
MERGE_LAT_TILES_PER_BATCH = DEC_SEQ // MERGE_TM


def _merge(o_ctx, o_a, o_b, o_c, gates, h, mods3, w_br, w_out_b, g_post3, g_pre_ffn3, w_router, b_router3, l):
    ctx = lambda n: pl.BlockSpec((MERGE_TM, n), lambda i: (jnp.minimum(i, MERGE_CTX_TILES - 1), 0))
    lat_n = lambda n: pl.BlockSpec((MERGE_TM, n), lambda i: (jnp.maximum(i - MERGE_CTX_TILES, 0), 0))
    lat = lambda: lat_n(512)
    mod_index = lambda i: jnp.where(i < MERGE_CTX_TILES, 0,
                                    1 + (i - MERGE_CTX_TILES) // MERGE_LAT_TILES_PER_BATCH)
    mod = lambda j: pl.BlockSpec((None, 1, D_MODEL), lambda i: (mod_index(i), 0, j))
    lw = lambda: pl.BlockSpec((None, 1, D_MODEL), lambda i: (l, 0, 0))
    row = lambda n: pl.BlockSpec((MERGE_TM, n), lambda i: (i, 0))
    const = lambda shape: pl.BlockSpec(shape, lambda i: (0,) * len(shape), pipeline_mode=pl.Buffered(1))
    return pl.pallas_call(
        _merge_kernel,
        grid=(MERGE_TILES,),
        in_specs=[ctx(1536),
                  lat(), lat(), lat(), row(3072), ctx(D_MODEL), lat_n(D_MODEL),
                  mod(2), mod(3), mod(4),
                  const((3, 512, D_MODEL)), const((D_MODEL, D_MODEL)), lw(), lw(),
                  const((D_MODEL, 2 * N_EXPERTS)),
                  pl.BlockSpec((None, 1, N_EXPERTS), lambda i: (l, 0, 0))],
        out_specs=[row(D_MODEL), row(HALF), row(2 * TOP_K),
                   pl.BlockSpec((8, N_EXPERTS), lambda i: (0, 0))],
        out_shape=[jax.ShapeDtypeStruct((N_TOK, D_MODEL), F32),
                   jax.ShapeDtypeStruct((N_TOK, HALF), jnp.int32),
                   jax.ShapeDtypeStruct((N_TOK, 2 * TOP_K), F32),
                   jax.ShapeDtypeStruct((8, N_EXPERTS), F32)],
        compiler_params=_cparams(("arbitrary",)),
        name="merge",
    )(o_ctx, o_a, o_b, o_c, gates, *h, mods3, mods3, mods3, w_br, w_out_b, g_post3, g_pre_ffn3,
      w_router, b_router3)


def _moe_kernel(l, be_ref, nv_ref, first_ref, slot_ref, nxt_ref, rows_ref, x_ref, wgu_hbm, bgu_ref, wd_hbm, bd_ref, y_ref,
                wgu_f32, wd_f32, wgu_bf, wd_bf, sem):
    i = pl.program_id(0)

    def fetch(e, s):
        return (pltpu.make_async_copy(wgu_hbm.at[l, e], wgu_f32.at[s], sem.at[0, s]),
                pltpu.make_async_copy(wd_hbm.at[l, e], wd_f32.at[s], sem.at[1, s]))

    @pl.when(first_ref[i] == 1)
    def _():
        s = slot_ref[i]

        @pl.when(i == 0)
        def _():
            for cp in fetch(be_ref[i], s):
                cp.start()

        for cp in fetch(be_ref[i], s):
            cp.wait()

        @pl.when(nxt_ref[i] >= 0)
        def _():
            for cp in fetch(nxt_ref[i], 1 - s):
                cp.start()

        wgu_bf[...] = wgu_f32[s].astype(BF16)
        wd_bf[...] = wd_f32[s].astype(BF16)

    n_real = rows_ref[i]

    def chain(r0):
        rs = slice(r0, r0 + MOE_CHAIN)
        real = lax.broadcasted_iota(jnp.int32, (MOE_CHAIN, HALF), 0) + r0 < n_real
        xa, xb = _unpack_halves(jnp.where(real, x_ref[rs, :], 0))
        x = jnp.concatenate([xa.astype(BF16), xb.astype(BF16)], axis=1)
        b = bgu_ref[...]
        glu = jnp.dot(x, wgu_bf[:, :D_FF], preferred_element_type=F32) + b[:, :D_FF]
        lin = jnp.dot(x, wgu_bf[:, D_FF:], preferred_element_type=F32) + b[:, D_FF:]
        glu = jnp.minimum(glu, SWIGLU_LIMIT)
        lin = jnp.clip(lin, -SWIGLU_LIMIT, SWIGLU_LIMIT)
        act = glu * (1.0 / (1.0 + jnp.exp(-SWIGLU_ALPHA * glu))) * (lin + 1.0)
        y = jnp.dot(act.astype(BF16), wd_bf[...], preferred_element_type=F32) + bd_ref[...]
        y_ref[rs, :] = _pack_halves(y)

    @pl.when(n_real > MOE_CHAIN)
    def _():
        chain(0)
        chain(MOE_CHAIN)

    @pl.when((n_real > 0) & (n_real <= MOE_CHAIN))
    def _():
        chain(0)
        y_ref[MOE_CHAIN:, :] = jnp.zeros((MOE_TILE - MOE_CHAIN, HALF), jnp.int32)

    @pl.when(n_real == 0)
    def _():
        y_ref[...] = jnp.zeros_like(y_ref)


def _moe(plan, x_slots, w_gate_up, b_gate_up4, w_down, b_down4, l):
    grid_spec = pltpu.PrefetchScalarGridSpec(
        num_scalar_prefetch=6,
        grid=(N_MOE_BLOCKS,),
        in_specs=[pl.BlockSpec((MOE_TILE, HALF), lambda i, be, *_: (i, 0)),
                  pl.BlockSpec(memory_space=pl.ANY),
                  pl.BlockSpec((None, None, 1, 2 * D_FF), lambda i, be, *_: (l, be[i], 0, 0)),
                  pl.BlockSpec(memory_space=pl.ANY),
                  pl.BlockSpec((None, None, 1, D_MODEL), lambda i, be, *_: (l, be[i], 0, 0))],
        out_specs=pl.BlockSpec((MOE_TILE, HALF), lambda i, be, *_: (i, 0)),
        scratch_shapes=[pltpu.VMEM((2, D_MODEL, 2 * D_FF), F32), pltpu.VMEM((2, D_FF, D_MODEL), F32),
                        pltpu.VMEM((D_MODEL, 2 * D_FF), BF16), pltpu.VMEM((D_FF, D_MODEL), BF16),
                        pltpu.SemaphoreType.DMA((2, 2))])
    return pl.pallas_call(
        functools.partial(_moe_kernel, l),
        grid_spec=grid_spec,
        out_shape=jax.ShapeDtypeStruct((N_SLOTS, HALF), jnp.int32),
        compiler_params=_cparams(("arbitrary",)),
        name="moe",
    )(*plan, x_slots, w_gate_up, b_gate_up4, w_down, b_down4)


def _combine_kernel(y0_ref, y1_ref, y2_ref, y3_ref, gate_ref, h_ref, g2_ref, gpost_ref, o_ref):
    gate = gate_ref[...]
    ffn = None
    for k, y_ref in enumerate((y0_ref, y1_ref, y2_ref, y3_ref)):
        ya, yb = _unpack_halves(y_ref[...])
        term = gate[:, k:k + 1] * jnp.concatenate([ya, yb], axis=1)
        ffn = term if ffn is None else ffn + term
    o_ref[...] = h_ref[...] + g2_ref[...] * (_rms(ffn) * gpost_ref[...])


def _combine(y_part, gate, h2, mods3, g_post_ffn3, l, first_tile, n_tiles):
    row = lambda n: pl.BlockSpec((TM, n), lambda i: (first_tile + i, 0))
    choice = lambda k: pl.BlockSpec((TM, HALF), lambda i: (k * n_tiles + i, 0))
    return pl.pallas_call(
        _combine_kernel,
        grid=(n_tiles,),
        in_specs=[choice(0), choice(1), choice(2), choice(3), row(TOP_K), row(D_MODEL),
                  pl.BlockSpec((None, 1, D_MODEL), lambda i: (_mod_index(first_tile + i), 0, 5)),
                  pl.BlockSpec((None, 1, D_MODEL), lambda i: (l, 0, 0))],
        out_specs=pl.BlockSpec((TM, D_MODEL), lambda i: (i, 0)),
        out_shape=jax.ShapeDtypeStruct((n_tiles * TM, D_MODEL), F32),
        compiler_params=_cparams(("arbitrary",)),
        name="combine",
    )(y_part, y_part, y_part, y_part, gate, h2, mods3, g_post_ffn3)


def _route_kernel(topk_ref, count_ref, dest_ref, gate_ref, base_ref):
    i = pl.program_id(0)

    @pl.when(i == 0)
    def _():
        total = count_ref[...]
        padded = jnp.floor((total + (MOE_TILE - 1.0)) * (1.0 / MOE_TILE)) * MOE_TILE
        before = (lax.broadcasted_iota(jnp.int32, (N_EXPERTS, N_EXPERTS), 0)
                  < lax.broadcasted_iota(jnp.int32, (N_EXPERTS, N_EXPERTS), 1)).astype(F32)
        base_ref[...] = jnp.dot(padded, before, preferred_element_type=F32, precision=lax.Precision.HIGHEST)

    topk = topk_ref[...]
    lane = lax.broadcasted_iota(jnp.int32, (ROUTE_TM, N_EXPERTS), 1).astype(F32)
    hots = [lane == topk[:, k:k + 1] for k in range(TOP_K)]
    vals = [topk[:, TOP_K + k:TOP_K + k + 1] for k in range(TOP_K)]
    chosen = functools.reduce(jnp.add, [h.astype(F32) for h in hots])
    earlier_row = (lax.broadcasted_iota(jnp.int32, (ROUTE_TM, ROUTE_TM), 1)
                   < lax.broadcasted_iota(jnp.int32, (ROUTE_TM, ROUTE_TM), 0)).astype(BF16)
    earlier = jnp.dot(earlier_row, chosen.astype(BF16), preferred_element_type=F32)
    offs = base_ref[0:1, :] + earlier
    dest = [jnp.sum(jnp.where(h, offs, 0.0), axis=1, keepdims=True) for h in hots]
    dest_ref[...] = jnp.concatenate(dest, axis=1).astype(jnp.int32)
    e = [jnp.exp(v - vals[0]) for v in vals]
    den = functools.reduce(jnp.add, e)
    gate_ref[...] = jnp.concatenate(e, axis=1) / den
    base_ref[...] = base_ref[...] + jnp.sum(chosen, axis=0, keepdims=True)


ROUTE_TM = 1024


def _route(topk, counts):
    tile = lambda n: pl.BlockSpec((ROUTE_TM, n), lambda i: (i, 0))
    dest, gate = pl.pallas_call(
        _route_kernel,
        grid=(N_TOK // ROUTE_TM,),
        in_specs=[tile(2 * TOP_K), pl.BlockSpec((8, N_EXPERTS), lambda i: (0, 0))],
        out_specs=[tile(TOP_K), tile(TOP_K)],
        out_shape=[jax.ShapeDtypeStruct((N_TOK, TOP_K), jnp.int32),
                   jax.ShapeDtypeStruct((N_TOK, TOP_K), F32)],
        scratch_shapes=[pltpu.VMEM((8, N_EXPERTS), F32)],
        compiler_params=_cparams(("arbitrary",)),
        name="route",
    )(topk, counts)
    counts = counts[0].astype(jnp.int32)
    expert = jnp.arange(N_EXPERTS, dtype=jnp.int32)
    padded = (counts + MOE_TILE - 1) // MOE_TILE * MOE_TILE
    pad_end = jnp.sum(jnp.where(expert[None, :] <= expert[:, None], padded[None, :], 0), axis=1)
    block = jnp.arange(N_MOE_BLOCKS, dtype=jnp.int32)
    block_e = jnp.minimum(jnp.sum((pad_end[None, :] <= block[:, None] * MOE_TILE).astype(jnp.int32), axis=1),
                          N_EXPERTS - 1)
    n_valid = pad_end[-1] // MOE_TILE
    mine = expert[None, :] == block_e[:, None]
    pick = lambda v: jnp.sum(jnp.where(mine, v[None, :], 0), axis=1)
    offset = block * MOE_TILE - pick(pad_end - padded)
    valid = block < n_valid
    first = valid & (offset == 0)
    used = counts > 0
    slot = jnp.sum((used[None, :] & (expert[None, :] < block_e[:, None])).astype(jnp.int32), axis=1) % 2
    nxt = jnp.min(jnp.where(used[None, :] & (expert[None, :] > block_e[:, None]), expert[None, :], N_EXPERTS),
                  axis=1)
    nxt = jnp.where(nxt < N_EXPERTS, nxt, -1)
    rows = jnp.where(valid, jnp.clip(pick(counts) - offset, 0, MOE_TILE), 0)
    plan = tuple(a.astype(jnp.int32) for a in (block_e, n_valid[None], first, slot, nxt, rows))
    n_workers = N_TOK // SCATTER_TOKENS_PER_WORKER
    dest_sc = dest.reshape(n_workers, SCATTER_TOKENS_PER_WORKER // GATHER_CHUNK, GATHER_CHUNK, TOP_K)
    dest_sc = dest_sc.transpose(0, 1, 3, 2).reshape(n_workers, -1, GATHER_CHUNK)
    return gate, dest.T, dest_sc, plan


def _rope_tables():
    t = jnp.arange(DEC_SEQ)
    inv = ROPE_THETA ** (-jnp.arange(ROPE_PAIRS, dtype=F32) / ROPE_PAIRS)
    row = (t // GRID_W).astype(F32)[:, None] * inv
    col = (t % GRID_W).astype(F32)[:, None] * inv
    zeros = jnp.zeros_like(row)
    cos = jnp.concatenate([jnp.cos(row), jnp.cos(row), jnp.cos(col), jnp.cos(col)], axis=1)
    s1 = jnp.concatenate([-jnp.sin(row), zeros, -jnp.sin(col), zeros], axis=1)
    s2 = jnp.concatenate([zeros, jnp.sin(row), zeros, jnp.sin(col)], axis=1)
    ident = lambda v: jnp.full((PROJ_TM, HEAD_DIM), v, F32)
    tables = [jnp.concatenate([x, ident(v)], axis=0) for x, v in ((cos, 1.0), (s1, 0.0), (s2, 0.0))]
    return [jnp.tile(x, (1, 2)) for x in tables]


def kernel(x_prompt, x_sample, cache_a, cache_b, cache_c, c, c_ctx, w_ada, b_ada, g_pre_mix, g_post_mix,
           g_pre_ffn, g_post_ffn, w_in, g_q_b, g_k_b, sink_a, rpb_c, w_br_a, w_br_b, w_br_c, w_out,
           w_router, b_router, w_gate_up, b_gate_up, w_down, b_down):
    h = (x_prompt.reshape(N_CTX, D_MODEL), x_sample.reshape(N_LAT, D_MODEL))
    cond8 = jnp.concatenate([c_ctx[None], c, jnp.zeros((3, D_MODEL), F32)], axis=0)
    cache_a = cache_a.astype(BF16).reshape(DEC_BATCH, DEPTH, 2, PAST_LEN, W_KV)
    cache_b = cache_b.astype(BF16).reshape(DEC_BATCH, DEPTH, 2, PAST_LEN, W_KV)
    cache_c = cache_c.astype(BF16).reshape(DEC_BATCH, DEPTH, 2, PAST_LEN, W_HEADS)
    cos_t, s1_t, s2_t = _rope_tables()
    bd = jnp.kron(jnp.eye(256 // HEAD_DIM, dtype=F32),
                  jnp.full((HEAD_DIM, HEAD_DIM), 1.0 / HEAD_DIM, F32)).astype(BF16)
    vec3 = lambda a: a.reshape(DEPTH, 1, a.shape[-1])
    scale = HEAD_DIM ** -0.5
    states = []
    mods_all = _adaln(cond8, w_ada, vec3(b_ada))
    for l in range(DEPTH):
        w = w_in[l]
        pair_cols = lambda a: a.reshape(D_MODEL, 2, 4, HEAD_DIM).transpose(0, 2, 1, 3).reshape(D_MODEL, 512)
        pair_rows = lambda a: a.reshape(2, 4, HEAD_DIM, D_MODEL).transpose(1, 0, 2, 3).reshape(512, D_MODEL)
        w_in_p = jnp.concatenate(
            [pair_cols(w[:, 0:512]) * scale, pair_cols(w[:, 768:1280]), w[:, 1536:2048] * scale,
             w[:, 2048:2560], w[:, 2560:3072],
             w[:, 512:640], w[:, 1280:1408], w[:, 640:768], w[:, 1408:1536], w[:, 3072:]], axis=1).astype(BF16)
        gq = (jnp.tile(g_q_b[l], H_B) * scale)[None]
        gk = jnp.tile(g_k_b[l], KV_B)[None]
        w_br = jnp.stack([pair_rows(w_br_a[l]), pair_rows(w_br_b[l]), w_br_c[l]], axis=0).astype(BF16)
        w_out_b = w_out[l].astype(BF16)
        w_r_hi = w_router[l].astype(BF16)
        w_r_lo = (w_router[l] - w_r_hi.astype(F32)).astype(BF16)
        w_router2 = jnp.concatenate([w_r_hi, w_r_lo], axis=1)

        mods3 = mods_all[l].reshape(8, 1, 6 * D_MODEL)
        q_all, kvab, kvc, gates, *states = _proj(h, mods3, vec3(g_pre_mix), w_in_p, gq, gk, bd,
                                                 cos_t, s1_t, s2_t, tuple(states), l)
        if l == DEPTH - 1:
            states = [s.reshape(BATCH, DEPTH, 2, SEQ, n, HEAD_DIM) for s, n in zip(states, (KV_A, KV_B, H_C))]
            states, gates = lax.optimization_barrier((states, gates))
        o_ctx = _ctx_attn(sink_a, q_all, kvab, kvc, l)
        o_a = _win_attn(sink_a, q_all, kvab, cache_a, l)
        o_b = _dense_attn(q_all, kvab, cache_b, l)
        o_c = _nbr_attn(q_all, kvc, cache_c, _nbr_bias_table(rpb_c[l]), l)
        h2, hn2, topk, counts = _merge(o_ctx, o_a, o_b, o_c, gates, h, mods3, w_br, w_out_b,
                                       vec3(g_post_mix), vec3(g_pre_ffn), w_router2, vec3(b_router), l)
        gate, dest, dest_sc, plan = _route(topk, counts)
        x_slots = _scatter_rows(hn2, dest_sc, N_SLOTS)
        y_slots = _moe(plan, x_slots, w_gate_up,
                       b_gate_up.reshape(DEPTH, N_EXPERTS, 1, 2 * D_FF), w_down,
                       b_down.reshape(DEPTH, N_EXPERTS, 1, D_MODEL), l)
        h = tuple(
            _combine(_gather_rows(y_slots, dest[:, first * TM:(first + n) * TM].reshape(-1)),
                     gate, h2, mods3, vec3(g_post_ffn), l, first, n)
            for first, n in ((0, CTX_TILES), (CTX_TILES, N_TILES - CTX_TILES)))

    return (h[0].reshape(BATCH, SEQ, D_MODEL), h[1].reshape(DEC_BATCH, DEC_SEQ, D_MODEL), *states)
```

```python
import functools

import jax
import jax.numpy as jnp
import numpy as np
from jax import lax
from jax.experimental import pallas as pl
from jax.experimental.pallas import tpu as pltpu
from jax.experimental.pallas import tpu_sc as plsc

D_MODEL = 1024
BATCH = 32
SEQ = 256
DEPTH = 2
DEC_BATCH = 4
DEC_SEQ = 2048
PAST_LEN = 512
GRID_W = 64
HEAD_DIM = 64
H_A = 8
KV_A = 2
H_B = 8
KV_B = 2
H_C = 8
WINDOW_A = 128
NA_ROWS = 8
NA_COLS = 16
ROPE_THETA = 10000.0
ROPE_PAIRS = HEAD_DIM // 4
N_EXPERTS = 32
TOP_K = 4
D_FF = D_MODEL
SWIGLU_ALPHA = 1.702
SWIGLU_LIMIT = 7.0
EPS = 1e-6

W_HEADS = H_A * HEAD_DIM
W_KV = KV_A * HEAD_DIM
N_CTX = BATCH * SEQ
N_LAT = DEC_BATCH * DEC_SEQ
N_TOK = N_CTX + N_LAT
GRID_ROWS = DEC_SEQ // GRID_W
D_IN = 3 * W_HEADS + 4 * W_KV + 2 * W_HEADS + 3 * D_MODEL

TM = 256
N_TILES = N_TOK // TM
CTX_TILES = N_CTX // TM
LAT_TILES_PER_BATCH = DEC_SEQ // TM
TQ = 128
MOE_CHAIN = 256
MOE_TILE = 2 * MOE_CHAIN
N_SLOTS = N_TOK * TOP_K + N_EXPERTS * MOE_TILE
N_MOE_BLOCKS = N_SLOTS // MOE_TILE
NEG = -1e30
VMEM_LIMIT = 56 * 1024 * 1024

BF16 = jnp.bfloat16
F32 = jnp.float32


def _cparams(sem):
    return pltpu.CompilerParams(dimension_semantics=sem, vmem_limit_bytes=VMEM_LIMIT)


def _mod_index(i):
    return jnp.where(i < CTX_TILES, 0, 1 + (i - CTX_TILES) // LAT_TILES_PER_BATCH)


def _rms(x):
    return x * lax.rsqrt(jnp.mean(x * x, axis=-1, keepdims=True) + EPS)


HALF = D_MODEL // 2


def _pack_halves(x):
    hi = lax.bitcast_convert_type(x[:, :HALF].astype(BF16).astype(F32), jnp.uint32)
    lo = lax.bitcast_convert_type(x[:, HALF:].astype(BF16).astype(F32), jnp.uint32)
    return lax.bitcast_convert_type(hi | (lo >> 16), jnp.int32)


def _unpack_halves(w):
    u = lax.bitcast_convert_type(w, jnp.uint32)
    return (lax.bitcast_convert_type(u & jnp.uint32(0xFFFF0000), F32),
            lax.bitcast_convert_type(u << 16, F32))


GATHER_CHUNK = 64


def _gather_rows(table, idx):
    n = idx.shape[0]
    width = table.shape[1]
    info = plsc.get_sparse_core_info()
    n_workers = info.num_cores * info.num_subcores
    per_worker = n // n_workers
    n_chunks = per_worker // GATHER_CHUNK
    assert per_worker * n_workers == n and n_chunks * GATHER_CHUNK == per_worker and n_chunks % 2 == 0
    mesh = plsc.VectorSubcoreMesh(core_axis_name="core", subcore_axis_name="subcore")

    @functools.partial(
        pl.kernel, out_type=jax.ShapeDtypeStruct((n, width), table.dtype), mesh=mesh,
        scratch_types=[pltpu.VMEM((per_worker,), jnp.int32),
                       pltpu.VMEM((2, GATHER_CHUNK, width), table.dtype),
                       pltpu.SemaphoreType.DMA((2,)), pltpu.SemaphoreType.DMA((2,))])
    def gather(table_hbm, idx_hbm, out_hbm, idx_v, rows_v, gather_sem, write_sem):
        worker = lax.axis_index("subcore") * info.num_cores + lax.axis_index("core")
        base = worker * per_worker
        pltpu.sync_copy(idx_hbm.at[pl.ds(base, per_worker)], idx_v)

        def fetch(chunk, slot):
            rows = idx_v.at[pl.ds(chunk * GATHER_CHUNK, GATHER_CHUNK)]
            return pltpu.make_async_copy(table_hbm.at[rows], rows_v.at[slot], gather_sem.at[slot])

        def write(chunk, slot):
            dst = out_hbm.at[pl.ds(base + chunk * GATHER_CHUNK, GATHER_CHUNK)]
            return pltpu.make_async_copy(rows_v.at[slot], dst, write_sem.at[slot])

        fetch(0, 0).start()

        @pl.loop(0, n_chunks, step=2)
        def _(c):
            @pl.when(c > 0)
            def _():
                write(c - 1, 1).wait()

            fetch(c + 1, 1).start()
            fetch(c, 0).wait()
            write(c, 0).start()
            write(c, 0).wait()

            @pl.when(c + 2 < n_chunks)
            def _():
                fetch(c + 2, 0).start()

            fetch(c + 1, 1).wait()
            write(c + 1, 1).start()

        write(n_chunks - 1, 1).wait()

    return gather(table, idx)


SC_WORKERS_V7X = 32
SCATTER_TOKENS_PER_WORKER = N_TOK // SC_WORKERS_V7X


def _scatter_rows(table, dest_sc, n_out):
    width = table.shape[1]
    info = plsc.get_sparse_core_info()
    assert info.num_cores * info.num_subcores == SC_WORKERS_V7X
    per_worker = SCATTER_TOKENS_PER_WORKER
    n_chunks = per_worker // GATHER_CHUNK
    assert n_chunks % 2 == 0 and dest_sc.shape == (SC_WORKERS_V7X, n_chunks * TOP_K, GATHER_CHUNK)
    mesh = plsc.VectorSubcoreMesh(core_axis_name="core", subcore_axis_name="subcore")

    @functools.partial(
        pl.kernel, out_type=jax.ShapeDtypeStruct((n_out, width), table.dtype), mesh=mesh,
        scratch_types=[pltpu.VMEM((n_chunks * TOP_K, GATHER_CHUNK), jnp.int32),
                       pltpu.VMEM((2, GATHER_CHUNK, width), table.dtype),
                       pltpu.SemaphoreType.DMA((2,)), pltpu.SemaphoreType.DMA((2,))])
    def scatter(table_hbm, dest_hbm, out_hbm, idx_v, rows_v, read_sem, write_sem):
        worker = lax.axis_index("subcore") * info.num_cores + lax.axis_index("core")
        base = worker * per_worker
        pltpu.sync_copy(dest_hbm.at[worker], idx_v)

        def read(chunk, slot):
            src = table_hbm.at[pl.ds(base + chunk * GATHER_CHUNK, GATHER_CHUNK)]
            return pltpu.make_async_copy(src, rows_v.at[slot], read_sem.at[slot])

        def writes(chunk, slot):
            return [pltpu.make_async_copy(rows_v.at[slot], out_hbm.at[idx_v.at[chunk * TOP_K + k]],
                                          write_sem.at[slot]) for k in range(TOP_K)]

        read(0, 0).start()

        @pl.loop(0, n_chunks, step=2)
        def _(c):
            @pl.when(c > 0)
            def _():
                for cp in writes(c - 1, 1):
                    cp.wait()

            read(c + 1, 1).start()
            read(c, 0).wait()
            for cp in writes(c, 0):
                cp.start()
            for cp in writes(c, 0):
                cp.wait()

            @pl.when(c + 2 < n_chunks)
            def _():
                read(c + 2, 0).start()

            read(c + 1, 1).wait()
            for cp in writes(c + 1, 1):
                cp.start()

        for cp in writes(n_chunks - 1, 1):
            cp.wait()

    return scatter(table, dest_sc)


def _adaln_kernel(c_ref, w_ref, b_ref, o_ref):
    c = c_ref[...]
    s = c / (1.0 + jnp.exp(-c))
    o_ref[...] = jnp.dot(s, w_ref[...], preferred_element_type=F32,
                         precision=lax.Precision.HIGHEST) + b_ref[...]


def _adaln(cond8, w_ada, b_ada3):
    tn = 768
    return pl.pallas_call(
        _adaln_kernel,
        grid=(DEPTH, 6 * D_MODEL // tn),
        in_specs=[pl.BlockSpec((8, D_MODEL), lambda l, j: (0, 0)),
                  pl.BlockSpec((None, D_MODEL, tn), lambda l, j: (l, 0, j)),
                  pl.BlockSpec((None, 1, tn), lambda l, j: (l, 0, j))],
        out_specs=pl.BlockSpec((None, 8, tn), lambda l, j: (l, 0, j)),
        out_shape=jax.ShapeDtypeStruct((DEPTH, 8, 6 * D_MODEL), F32),
        compiler_params=_cparams(("arbitrary", "arbitrary")),
        name="adaln",
    )(cond8, w_ada, b_ada3)


C_QA, C_QB, C_QC, C_KC, C_VC, C_KAB, C_GL = 0, 512, 1024, 1536, 2048, 2560, 3072


def _proj_kernel(hc_ref, hl_ref, sh_ref, sc_ref, gpre_ref, w_ref, gq_ref, gk_ref, bd_ref,
                 cos_ref, s1_ref, s2_ref, *rest):
    q_ref, kvab_ref, kvc_ref, gates_ref, sta_ref, stb_ref, stc_ref = rest[-7:]
    i = pl.program_id(0)
    bd = bd_ref[...]

    def rope(t, rs):
        cos, s1, s2 = cos_ref[rs, :], s1_ref[rs, :], s2_ref[rs, :]
        parts = []
        for g in range(t.shape[1] // 128):
            tg = t[:, g * 128:(g + 1) * 128]
            parts.append(tg * cos + pltpu.roll(tg, 112, 1) * s1 + pltpu.roll(tg, 16, 1) * s2)
        return parts[0] if len(parts) == 1 else jnp.concatenate(parts, axis=1)

    def headnorm(t, g):
        sq = (t * t).astype(BF16)
        n = t.shape[1]
        if n == 128:
            ms = jnp.dot(sq, bd[:128, :128], preferred_element_type=F32)
        else:
            ms = jnp.concatenate(
                [jnp.dot(sq[:, c:c + 256], bd, preferred_element_type=F32) for c in range(0, n, 256)],
                axis=1)
        return t * lax.rsqrt(ms + EPS) * g

    kv_f32 = []
    for c in range(PROJ_TM // TM):
        rs = slice(c * TM, (c + 1) * TM)
        hn = _rms(jnp.where(i < PROJ_CTX_TILES, hc_ref[rs, :], hl_ref[rs, :])) * gpre_ref[...]
        hb = (hn * (1.0 + sc_ref[...]) + sh_ref[...]).astype(BF16)
        proj = lambda c0, n, hb=hb: jnp.dot(hb, w_ref[:, c0:c0 + n], preferred_element_type=F32)
        q_ref[rs, 0:512] = rope(proj(C_QA, 512), rs).astype(BF16)
        q_ref[rs, 512:1024] = rope(headnorm(proj(C_QB, 512), gq_ref[...]), rs).astype(BF16)
        q_ref[rs, 1024:1536] = proj(C_QC, 512).astype(BF16)
        kc = proj(C_KC, 512)
        vc = proj(C_VC, 512)
        kvc_ref[rs, 0:512] = kc.astype(BF16)
        kvc_ref[rs, 512:1024] = vc.astype(BF16)
        kab = proj(C_KAB, 512)
        ka = kab[:, 0:128]
        kb = headnorm(kab[:, 128:256], gk_ref[...])
        kvab_ref[rs, 0:128] = rope(ka, rs).astype(BF16)
        kvab_ref[rs, 128:256] = rope(kb, rs).astype(BF16)
        kvab_ref[rs, 256:512] = kab[:, 256:512].astype(BF16)
        for j in range(6):
            gl = proj(C_GL + j * 512, 512)
            gates_ref[rs, j * 512:(j + 1) * 512] = (1.0 / (1.0 + jnp.exp(-gl))).astype(BF16)
        kv_f32.append(((ka, kab[:, 256:384]), (kb, kab[:, 384:512]), (kc, vc)))

    @pl.when(i < PROJ_CTX_TILES)
    def _():
        for c, per_mixer in enumerate(kv_f32):
            for st_ref, (k, v) in zip((sta_ref, stb_ref, stc_ref), per_mixer):
                if len(st_ref.shape) == 5:
                    st_ref[c, 0, 0] = k
                    st_ref[c, 0, 1] = v
                    st_ref[c, 1:] = jnp.zeros((DEPTH - 1,) + tuple(st_ref.shape[2:]), F32)
                else:
                    st_ref[c, 0] = k
                    st_ref[c, 1] = v


PROJ_TM = 2 * TM
PROJ_TILES = N_TOK // PROJ_TM
PROJ_CTX_TILES = N_CTX // PROJ_TM
PROJ_LAT_TILES_PER_BATCH = DEC_SEQ // PROJ_TM


def _proj(h, mods3, g_pre3, w_in_p, gq, gk, bd, cos_t, s1_t, s2_t, prev_states, l):
    lat_tile = lambda i: i - PROJ_CTX_TILES
    rope_idx = lambda i: jnp.where(i < PROJ_CTX_TILES, PROJ_LAT_TILES_PER_BATCH,
                                   lat_tile(i) % PROJ_LAT_TILES_PER_BATCH)
    mod_idx = lambda i: jnp.where(i < PROJ_CTX_TILES, 0, 1 + lat_tile(i) // PROJ_LAT_TILES_PER_BATCH)
    ctx_block = lambda i: jnp.minimum(i, PROJ_CTX_TILES - 1)
    const = lambda shape: pl.BlockSpec(shape, lambda i: (0,) * len(shape), pipeline_mode=pl.Buffered(1))
    rope_spec = pl.BlockSpec((PROJ_TM, 128), lambda i: (rope_idx(i), 0))
    row = lambda n: pl.BlockSpec((PROJ_TM, n), lambda i: (i, 0))
    per_step = PROJ_TM // SEQ
    if l == 0:
        state_spec = lambda n: pl.BlockSpec((per_step, DEPTH, 2, SEQ, n), lambda i: (ctx_block(i), 0, 0, 0, 0))
    else:
        state_spec = lambda n: pl.BlockSpec((per_step, None, 2, SEQ, n), lambda i: (ctx_block(i), l, 0, 0, 0))
    state_shape = lambda n: jax.ShapeDtypeStruct((BATCH, DEPTH, 2, SEQ, n), F32)
    n_in = 12
    return pl.pallas_call(
        _proj_kernel,
        grid=(PROJ_TILES,),
        in_specs=[pl.BlockSpec((PROJ_TM, D_MODEL), lambda i: (ctx_block(i), 0)),
                  pl.BlockSpec((PROJ_TM, D_MODEL), lambda i: (jnp.maximum(lat_tile(i), 0), 0)),
                  pl.BlockSpec((None, 1, D_MODEL), lambda i: (mod_idx(i), 0, 0)),
                  pl.BlockSpec((None, 1, D_MODEL), lambda i: (mod_idx(i), 0, 1)),
                  pl.BlockSpec((None, 1, D_MODEL), lambda i: (l, 0, 0)),
                  const((D_MODEL, D_IN)), const((1, 512)), const((1, 128)), const((256, 256)),
                  rope_spec, rope_spec, rope_spec] + [pl.BlockSpec(memory_space=pl.ANY)] * len(prev_states),
        out_specs=[row(1536), row(512), row(1024), row(3072),
                   state_spec(W_KV), state_spec(W_KV), state_spec(W_HEADS)],
        out_shape=[jax.ShapeDtypeStruct((N_TOK, 1536), BF16),
                   jax.ShapeDtypeStruct((N_TOK, 512), BF16),
                   jax.ShapeDtypeStruct((N_TOK, 1024), BF16),
                   jax.ShapeDtypeStruct((N_TOK, 3072), BF16),
                   state_shape(W_KV), state_shape(W_KV), state_shape(W_HEADS)],
        input_output_aliases={n_in + j: 4 + j for j in range(len(prev_states))},
        compiler_params=_cparams(("arbitrary",)),
        name="proj",
    )(*h, mods3, mods3, g_pre3, w_in_p, gq, gk, bd, cos_t, s1_t, s2_t, *prev_states)


def _qk(q, k):
    return lax.dot_general(q, k, (((1,), (1,)), ((), ())), preferred_element_type=F32)


def _softmax_pv(scores, values, sink=None, halves=None):
    m = functools.reduce(jnp.maximum, [jnp.max(s, axis=-1, keepdims=True) for s in scores])
    if sink is not None:
        m = jnp.maximum(m, sink)
    if halves is None:
        ps = [jnp.exp(s - m) for s in scores]
        den = functools.reduce(jnp.add, [jnp.sum(p, axis=-1, keepdims=True) for p in ps])
        if sink is not None:
            den = den + jnp.exp(sink - m)
        o = functools.reduce(jnp.add, [jnp.dot(p.astype(BF16), v, preferred_element_type=F32)
                                       for p, v in zip(ps, values)])
        return o / den
    ps = [jnp.exp(s - m).astype(BF16) for s in scores]
    rows = ps[0].shape[0]
    split = {"both": rows // 2, "lo": rows, "hi": 0}[halves]
    one = jnp.ones((), BF16)

    def pv(r0, r1, keep_lo):
        acc = None
        for p, v in zip(ps, values):
            lo = _lo_lanes(v.shape[0])
            v1 = jnp.where(lo, v, one) if keep_lo else jnp.where(lo, one, v)
            t = jnp.dot(p[r0:r1], v1, preferred_element_type=F32)
            acc = t if acc is None else acc + t
        return acc

    parts = ([pv(0, split, True)] if split > 0 else []) + ([pv(split, rows, False)] if split < rows else [])
    o = parts[0] if len(parts) == 1 else jnp.concatenate(parts, axis=0)
    if sink is not None:
        is_lo_row = lax.broadcasted_iota(jnp.int32, (rows, 128), 0) < split
        o = o + jnp.where(is_lo_row != _lo_lanes(rows), jnp.exp(sink - m), 0.0)
    return o * pltpu.roll(1.0 / o, HEAD_DIM, 1)


def _lo_lanes(rows):
    return lax.broadcasted_iota(jnp.int32, (rows, 128), 1) < HEAD_DIM


def _stack_pairs(q, n_pairs):
    lo = _lo_lanes(q.shape[0])
    zero = jnp.zeros((q.shape[0], 128), q.dtype)
    pairs = [q[:, p * 128:(p + 1) * 128] for p in range(n_pairs)]
    return jnp.concatenate([jnp.where(lo, x, zero) for x in pairs] + [jnp.where(lo, zero, x) for x in pairs],
                           axis=0)


def _unstack_pairs(o, n_pairs):
    rows = o.shape[0] // (2 * n_pairs)
    lo = _lo_lanes(rows)
    return jnp.concatenate(
        [jnp.where(lo, o[p * rows:(p + 1) * rows], o[(n_pairs + p) * rows:(n_pairs + p + 1) * rows])
         for p in range(n_pairs)], axis=1)


def _sink_column(sink_ref, l, rows):
    return jnp.concatenate([jnp.full((rows, 1), sink_ref[l, h], F32) for h in range(H_A)], axis=0)


def _ctx_attn_kernel(l, sink_ref, q_ref, kvab_ref, kvc_ref, o_ref):
    qa = _stack_pairs(q_ref[:, 0:512], 4)
    o = _softmax_pv([_qk(qa, kvab_ref[:, 0:128])], [kvab_ref[:, 256:384]], _sink_column(sink_ref, l, SEQ))
    o_ref[:, 0:512] = _unstack_pairs(o, 4).astype(BF16)
    qb = _stack_pairs(q_ref[:, 512:1024], 4)
    o = _softmax_pv([_qk(qb, kvab_ref[:, 128:256])], [kvab_ref[:, 384:512]])
    o_ref[:, 512:1024] = _unstack_pairs(o, 4).astype(BF16)
    for hp in range(H_C // 2):
        cs = slice(hp * 128, (hp + 1) * 128)
        qc = _stack_pairs(q_ref[:, 1024 + hp * 128:1024 + (hp + 1) * 128], 1)
        o = _softmax_pv([_qk(qc, kvc_ref[:, cs])], [kvc_ref[:, 512 + hp * 128:512 + (hp + 1) * 128]])
        o_ref[:, 1024 + hp * 128:1024 + (hp + 1) * 128] = _unstack_pairs(o, 1).astype(BF16)


def _ctx_attn(sink_a, q_all, kvab, kvc, l):
    row = lambda n: pl.BlockSpec((SEQ, n), lambda b: (b, 0))
    return pl.pallas_call(
        functools.partial(_ctx_attn_kernel, l),
        grid=(BATCH,),
        in_specs=[pl.BlockSpec(memory_space=pltpu.SMEM), row(1536), row(512), row(1024)],
        out_specs=row(1536),
        out_shape=jax.ShapeDtypeStruct((N_CTX, 1536), BF16),
        compiler_params=_cparams(("arbitrary",)),
        name="ctx_attn",
    )(sink_a, q_all, kvab, kvc)


def _win_attn_kernel(l, sink_ref, q_ref, prev_ref, cur_ref, nxt_ref, ck_ref, cv_ref, o_ref):
    n = pl.program_id(1)
    nb = DEC_SEQ // TQ
    rows = H_A * TQ
    qpos = lax.broadcasted_iota(jnp.int32, (rows, TQ), 0) % TQ
    kpos = lax.broadcasted_iota(jnp.int32, (rows, TQ), 1)
    mask_prev = (kpos >= qpos) & (n > 0)
    mask_next = (kpos <= qpos) & (n < nb - 1)
    ks, vs = slice(0, 128), slice(256, 384)
    qs = _stack_pairs(q_ref[...], 4)
    s_prev = jnp.where(mask_prev, _qk(qs, prev_ref[:, ks]), NEG)
    s_cur = _qk(qs, cur_ref[:, ks])
    s_next = jnp.where(mask_next, _qk(qs, nxt_ref[:, ks]), NEG)
    s_ctx = _qk(qs, ck_ref[...])
    o = _softmax_pv([s_prev, s_cur, s_next, s_ctx],
                    [prev_ref[:, vs], cur_ref[:, vs], nxt_ref[:, vs], cv_ref[...]],
                    _sink_column(sink_ref, l, TQ), halves="both")
    o_ref[...] = _unstack_pairs(o, 4).astype(BF16)


def _win_attn(sink_a, q_all, kvab, cache_a, l):
    nb = DEC_SEQ // TQ
    base = N_CTX // TQ
    kv_spec = lambda f: pl.BlockSpec((TQ, 512), lambda b, n: (base + b * nb + f(n), 0))
    cache_spec = lambda s: pl.BlockSpec((None, None, None, PAST_LEN, W_KV), lambda b, n: (b, l, s, 0, 0))
    return pl.pallas_call(
        functools.partial(_win_attn_kernel, l),
        grid=(DEC_BATCH, nb),
        in_specs=[pl.BlockSpec(memory_space=pltpu.SMEM),
                  pl.BlockSpec((TQ, 512), lambda b, n: (base + b * nb + n, 0)),
                  kv_spec(lambda n: jnp.maximum(n - 1, 0)), kv_spec(lambda n: n),
                  kv_spec(lambda n: jnp.minimum(n + 1, nb - 1)),
                  cache_spec(0), cache_spec(1)],
        out_specs=pl.BlockSpec((TQ, 512), lambda b, n: (b * nb + n, 0)),
        out_shape=jax.ShapeDtypeStruct((N_LAT, 512), BF16),
        compiler_params=_cparams(("arbitrary", "arbitrary")),
        name="win_attn",
    )(sink_a, q_all, kvab, kvab, kvab, cache_a, cache_a)


DENSE_KEY_CHUNK = 1024
DENSE_TQ = 256


def _online_softmax_pv(q, key_chunks, value_chunks, keep_lo):
    one = jnp.ones((), BF16)
    m = acc = None
    for k, v in zip(key_chunks, value_chunks):
        s = _qk(q, k)
        m_new = jnp.max(s, axis=-1, keepdims=True)
        if m is not None:
            m_new = jnp.maximum(m, m_new)
        p = jnp.exp(s - m_new).astype(BF16)
        lo = _lo_lanes(v.shape[0])
        pv = jnp.dot(p, jnp.where(lo, v, one) if keep_lo else jnp.where(lo, one, v), preferred_element_type=F32)
        acc = pv if acc is None else acc * jnp.exp(m - m_new) + pv
        m = m_new
    return acc * pltpu.roll(1.0 / acc, HEAD_DIM, 1)


def _dense_attn_kernel(q_ref, kv_ref, ck_ref, cv_ref, o_ref):
    qs = _stack_pairs(q_ref[...], 4)
    half = qs.shape[0] // 2
    starts = range(0, DEC_SEQ, DENSE_KEY_CHUNK)
    keys = [kv_ref[c:c + DENSE_KEY_CHUNK, 128:256] for c in starts] + [ck_ref[...]]
    values = [kv_ref[c:c + DENSE_KEY_CHUNK, 384:512] for c in starts] + [cv_ref[...]]
    outs = [_online_softmax_pv(qs[g * half:(g + 1) * half], keys, values, g == 0) for g in range(KV_B)]
    o_ref[...] = _unstack_pairs(jnp.concatenate(outs, axis=0), 4).astype(BF16)


def _dense_attn(q_all, kvab, cache_b, l):
    nb = DEC_SEQ // DENSE_TQ
    base = N_CTX // DENSE_TQ
    cache_spec = lambda s: pl.BlockSpec((None, None, None, PAST_LEN, W_KV), lambda b, n: (b, l, s, 0, 0))
    return pl.pallas_call(
        _dense_attn_kernel,
        grid=(DEC_BATCH, nb),
        in_specs=[pl.BlockSpec((DENSE_TQ, 512), lambda b, n: (base + b * nb + n, 1)),
                  pl.BlockSpec((DEC_SEQ, 512), lambda b, n: (N_CTX // DEC_SEQ + b, 0)),
                  cache_spec(0), cache_spec(1)],
        out_specs=pl.BlockSpec((DENSE_TQ, 512), lambda b, n: (b * nb + n, 0)),
        out_shape=jax.ShapeDtypeStruct((N_LAT, 512), BF16),
        compiler_params=_cparams(("arbitrary", "arbitrary")),
        name="dense_attn",
    )(q_all, kvab, cache_b, cache_b)


NBR_BAND = 4
NBR_Q = NBR_BAND * GRID_W
NBR_WIN_ROWS = 12
NBR_N_BANDS = GRID_ROWS // NBR_BAND
NBR_KBLK = NBR_Q
NBR_WIN_BLOCKS = NBR_WIN_ROWS * GRID_W // NBR_KBLK
NBR_LAST_KB = (GRID_ROWS - NBR_WIN_ROWS) * GRID_W // NBR_KBLK


def _nbr_window_block(band):
    return jnp.clip(band - 1, 0, NBR_LAST_KB)


def _nbr_attn_kernel(q_ref, k0_ref, k1_ref, k2_ref, ck_ref, cv_ref, bias_ref, o_ref):
    k_refs = (k0_ref, k1_ref, k2_ref)
    for hp in range(H_C // 2):
        cs = slice(hp * 128, (hp + 1) * 128)
        vs = slice(512 + hp * 128, 512 + (hp + 1) * 128)
        qs = _stack_pairs(q_ref[:, cs], 1)
        bias = jnp.concatenate([bias_ref[2 * hp], bias_ref[2 * hp + 1]], axis=0)
        scores = [_qk(qs, kr[:, cs]) + bias[:, j * NBR_KBLK:(j + 1) * NBR_KBLK] for j, kr in enumerate(k_refs)]
        scores.append(_qk(qs, ck_ref[:, cs]))
        o = _softmax_pv(scores, [kr[:, vs] for kr in k_refs] + [cv_ref[:, cs]])
        o_ref[:, cs] = _unstack_pairs(o, 1).astype(BF16)


def _nbr_attn(q_all, kvc, cache_c, bias_t, l):
    q_base = N_CTX // NBR_Q
    k_base = N_CTX // NBR_KBLK
    blocks_per_batch = DEC_SEQ // NBR_KBLK
    cache_spec = lambda s: pl.BlockSpec((None, None, None, PAST_LEN, W_HEADS), lambda band, b: (b, l, s, 0, 0))
    key_spec = lambda j: pl.BlockSpec(
        (NBR_KBLK, 1024), lambda band, b: (k_base + b * blocks_per_batch + _nbr_window_block(band) + j, 0))
    band_type = lambda band: jnp.where(band == 0, 0, jnp.where(band == NBR_N_BANDS - 1, 2, 1))
    return pl.pallas_call(
        _nbr_attn_kernel,
        grid=(NBR_N_BANDS, DEC_BATCH),
        in_specs=[pl.BlockSpec((NBR_Q, 512), lambda band, b: (q_base + b * NBR_N_BANDS + band, 2)),
                  key_spec(0), key_spec(1), key_spec(2), cache_spec(0), cache_spec(1),
                  pl.BlockSpec((None, H_C, NBR_Q, NBR_WIN_ROWS * GRID_W),
                               lambda band, b: (band_type(band), 0, 0, 0))],
        out_specs=pl.BlockSpec((NBR_Q, 512), lambda band, b: (b * NBR_N_BANDS + band, 0)),
        out_shape=jax.ShapeDtypeStruct((N_LAT, 512), BF16),
        compiler_params=_cparams(("arbitrary", "arbitrary")),
        name="nbr_attn",
    )(q_all, kvc, kvc, kvc, cache_c, cache_c, bias_t)


def _nbr_bias_table(rpb_l):
    c = np.arange(GRID_W)[:, None]
    kc = np.arange(GRID_W)[None, :]
    c_start = np.clip(c - NA_COLS // 2, 0, GRID_W - NA_COLS)
    valid = (kc >= c_start) & (kc < c_start + NA_COLS)
    pad = GRID_W - NA_COLS
    rpb_pad = jnp.pad(rpb_l.astype(F32), ((0, 0), (0, 0), (pad, pad)))
    toeplitz = jnp.stack([rpb_pad[:, :, GRID_W - 1 - q:2 * GRID_W - 1 - q] for q in range(GRID_W)], axis=2)
    t = jnp.where(valid[None, None], toeplitz, NEG)
    neg = jnp.full((H_C, GRID_W, GRID_W), NEG, F32)
    tables = []
    for r0 in (0, NBR_BAND, GRID_ROWS - NBR_BAND):
        k0 = min(max(r0 - NA_ROWS // 2, 0), GRID_ROWS - NBR_WIN_ROWS)
        rows = []
        for dq in range(NBR_BAND):
            r = r0 + dq
            start = min(max(r - NA_ROWS // 2, 0), GRID_ROWS - NA_ROWS)
            cols = []
            for i in range(NBR_WIN_ROWS):
                kr = k0 + i
                cols.append(t[:, kr - r + NA_ROWS - 1] if start <= kr < start + NA_ROWS else neg)
            rows.append(jnp.concatenate(cols, axis=2))
        tables.append(jnp.concatenate(rows, axis=1))
    return jnp.stack(tables, axis=0)


def _merge_kernel(octx_ref, oa_ref, ob_ref, oc_ref, gates_ref, hc_ref, hl_ref, g1_ref, sh2_ref, sc2_ref,
                  wbr_ref, wout_ref, gpost_ref, gpre_ref, wr_ref, br_ref,
                  h2_ref, hn2_ref, topk_ref, count_ref):
    i = pl.program_id(0)
    is_ctx = i < MERGE_CTX_TILES
    chosen = None
    for r0 in range(0, MERGE_TM, TM):
        rs = slice(r0, r0 + TM)
        merged = None
        for j, lat_ref in enumerate((oa_ref, ob_ref, oc_ref)):
            o = jnp.where(is_ctx, octx_ref[rs, j * 512:(j + 1) * 512], lat_ref[rs, :])
            br = jnp.dot(o, wbr_ref[j], preferred_element_type=F32)
            term = gates_ref[rs, j * D_MODEL:(j + 1) * D_MODEL].astype(F32) * br
            merged = term if merged is None else merged + term
        t = jnp.dot(merged.astype(BF16), wout_ref[...], preferred_element_type=F32)
        h2 = jnp.where(is_ctx, hc_ref[rs, :], hl_ref[rs, :]) + g1_ref[...] * (_rms(t) * gpost_ref[...])
        h2_ref[rs, :] = h2
        hn2 = _rms(h2) * gpre_ref[...] * (1.0 + sc2_ref[...]) + sh2_ref[...]
        hn2_ref[rs, :] = _pack_halves(hn2)
        x_hi = hn2.astype(BF16)
        x_lo = (hn2 - x_hi.astype(F32)).astype(BF16)
        hi = jnp.dot(x_hi, wr_ref[...], preferred_element_type=F32)
        lo = jnp.dot(x_lo, wr_ref[:, :N_EXPERTS], preferred_element_type=F32)
        logits = hi[:, :N_EXPERTS] + (hi[:, N_EXPERTS:] + lo) + br_ref[...]
        lane = lax.broadcasted_iota(jnp.int32, (TM, N_EXPERTS), 1)
        idxs, vals = [], []
        for _ in range(TOP_K):
            idx = jnp.argmax(logits, axis=1, keepdims=True).astype(jnp.int32)
            hot = lane == idx
            idxs.append(idx.astype(F32))
            vals.append(jnp.max(logits, axis=1, keepdims=True))
            chosen = hot.astype(F32) if chosen is None else chosen + hot.astype(F32)
            logits = jnp.where(hot, -jnp.inf, logits)
        topk_ref[rs, :] = jnp.concatenate(idxs + vals, axis=1)

    @pl.when(i == 0)
    def _():
        count_ref[...] = jnp.zeros_like(count_ref)

    count_ref[...] = count_ref[...] + jnp.sum(chosen, axis=0, keepdims=True)


MERGE_TM = 2 * TM
MERGE_TILES = N_TOK // MERGE_TM
MERGE_CTX_TILES = N_CTX // MERGE_TM
MERGE_LAT_TILES_PER_BATCH = DEC_SEQ // MERGE_TM


def _merge(o_ctx, o_a, o_b, o_c, gates, h, mods3, w_br, w_out_b, g_post3, g_pre_ffn3, w_router, b_router3, l):
    ctx = lambda n: pl.BlockSpec((MERGE_TM, n), lambda i: (jnp.minimum(i, MERGE_CTX_TILES - 1), 0))
    lat_n = lambda n: pl.BlockSpec((MERGE_TM, n), lambda i: (jnp.maximum(i - MERGE_CTX_TILES, 0), 0))
    lat = lambda: lat_n(512)
    mod_index = lambda i: jnp.where(i < MERGE_CTX_TILES, 0,
                                    1 + (i - MERGE_CTX_TILES) // MERGE_LAT_TILES_PER_BATCH)
    mod = lambda j: pl.BlockSpec((None, 1, D_MODEL), lambda i: (mod_index(i), 0, j))
    lw = lambda: pl.BlockSpec((None, 1, D_MODEL), lambda i: (l, 0, 0))
    row = lambda n: pl.BlockSpec((MERGE_TM, n), lambda i: (i, 0))
    const = lambda shape: pl.BlockSpec(shape, lambda i: (0,) * len(shape), pipeline_mode=pl.Buffered(1))
    return pl.pallas_call(
        _merge_kernel,
        grid=(MERGE_TILES,),
        in_specs=[ctx(1536),
                  lat(), lat(), lat(), row(3072), ctx(D_MODEL), lat_n(D_MODEL),
                  mod(2), mod(3), mod(4),
                  const((3, 512, D_MODEL)), const((D_MODEL, D_MODEL)), lw(), lw(),
                  const((D_MODEL, 2 * N_EXPERTS)),
                  pl.BlockSpec((None, 1, N_EXPERTS), lambda i: (l, 0, 0))],
        out_specs=[row(D_MODEL), row(HALF), row(2 * TOP_K),
                   pl.BlockSpec((8, N_EXPERTS), lambda i: (0, 0))],
        out_shape=[jax.ShapeDtypeStruct((N_TOK, D_MODEL), F32),
                   jax.ShapeDtypeStruct((N_TOK, HALF), jnp.int32),
                   jax.ShapeDtypeStruct((N_TOK, 2 * TOP_K), F32),
                   jax.ShapeDtypeStruct((8, N_EXPERTS), F32)],
        compiler_params=_cparams(("arbitrary",)),
        name="merge",
    )(o_ctx, o_a, o_b, o_c, gates, *h, mods3, mods3, mods3, w_br, w_out_b, g_post3, g_pre_ffn3,
      w_router, b_router3)


def _moe_kernel(l, be_ref, nv_ref, first_ref, slot_ref, nxt_ref, rows_ref, x_ref, wgu_hbm, bgu_ref, wd_hbm, bd_ref, y_ref,
                wgu_f32, wd_f32, wgu_bf, wd_bf, sem):
    i = pl.program_id(0)

    def fetch(e, s):
        return (pltpu.make_async_copy(wgu_hbm.at[l, e], wgu_f32.at[s], sem.at[0, s]),
                pltpu.make_async_copy(wd_hbm.at[l, e], wd_f32.at[s], sem.at[1, s]))

    @pl.when(first_ref[i] == 1)
    def _():
        s = slot_ref[i]

        @pl.when(i == 0)
        def _():
            for cp in fetch(be_ref[i], s):
                cp.start()

        for cp in fetch(be_ref[i], s):
            cp.wait()

        @pl.when(nxt_ref[i] >= 0)
        def _():
            for cp in fetch(nxt_ref[i], 1 - s):
                cp.start()

        wgu_bf[...] = wgu_f32[s].astype(BF16)
        wd_bf[...] = wd_f32[s].astype(BF16)

    n_real = rows_ref[i]

    def chain(r0):
        rs = slice(r0, r0 + MOE_CHAIN)
        real = lax.broadcasted_iota(jnp.int32, (MOE_CHAIN, HALF), 0) + r0 < n_real
        xa, xb = _unpack_halves(jnp.where(real, x_ref[rs, :], 0))
        x = jnp.concatenate([xa.astype(BF16), xb.astype(BF16)], axis=1)
        b = bgu_ref[...]
        glu = jnp.dot(x, wgu_bf[:, :D_FF], preferred_element_type=F32) + b[:, :D_FF]
        lin = jnp.dot(x, wgu_bf[:, D_FF:], preferred_element_type=F32) + b[:, D_FF:]
        glu = jnp.minimum(glu, SWIGLU_LIMIT)
        lin = jnp.clip(lin, -SWIGLU_LIMIT, SWIGLU_LIMIT)
        act = glu * (1.0 / (1.0 + jnp.exp(-SWIGLU_ALPHA * glu))) * (lin + 1.0)
        y = jnp.dot(act.astype(BF16), wd_bf[...], preferred_element_type=F32) + bd_ref[...]
        y_ref[rs, :] = _pack_halves(y)

    @pl.when(n_real > MOE_CHAIN)
    def _():
        chain(0)
        chain(MOE_CHAIN)

    @pl.when((n_real > 0) & (n_real <= MOE_CHAIN))
    def _():
        chain(0)
        y_ref[MOE_CHAIN:, :] = jnp.zeros((MOE_TILE - MOE_CHAIN, HALF), jnp.int32)

    @pl.when(n_real == 0)
    def _():
        y_ref[...] = jnp.zeros_like(y_ref)


def _moe(plan, x_slots, w_gate_up, b_gate_up4, w_down, b_down4, l):
    grid_spec = pltpu.PrefetchScalarGridSpec(
        num_scalar_prefetch=6,
        grid=(N_MOE_BLOCKS,),
        in_specs=[pl.BlockSpec((MOE_TILE, HALF), lambda i, be, *_: (i, 0)),
                  pl.BlockSpec(memory_space=pl.ANY),
                  pl.BlockSpec((None, None, 1, 2 * D_FF), lambda i, be, *_: (l, be[i], 0, 0)),
                  pl.BlockSpec(memory_space=pl.ANY),
                  pl.BlockSpec((None, None, 1, D_MODEL), lambda i, be, *_: (l, be[i], 0, 0))],
        out_specs=pl.BlockSpec((MOE_TILE, HALF), lambda i, be, *_: (i, 0)),
        scratch_shapes=[pltpu.VMEM((2, D_MODEL, 2 * D_FF), F32), pltpu.VMEM((2, D_FF, D_MODEL), F32),
                        pltpu.VMEM((D_MODEL, 2 * D_FF), BF16), pltpu.VMEM((D_FF, D_MODEL), BF16),
                        pltpu.SemaphoreType.DMA((2, 2))])
    return pl.pallas_call(
        functools.partial(_moe_kernel, l),
        grid_spec=grid_spec,
        out_shape=jax.ShapeDtypeStruct((N_SLOTS, HALF), jnp.int32),
        compiler_params=_cparams(("arbitrary",)),
        name="moe",
    )(*plan, x_slots, w_gate_up, b_gate_up4, w_down, b_down4)


def _combine_kernel(y0_ref, y1_ref, y2_ref, y3_ref, gate_ref, h_ref, g2_ref, gpost_ref, oc_ref, ol_ref):
    i = pl.program_id(0)
    gate = gate_ref[...]
    ffn = None
    for k, y_ref in enumerate((y0_ref, y1_ref, y2_ref, y3_ref)):
        ya, yb = _unpack_halves(y_ref[...])
        term = gate[:, k:k + 1] * jnp.concatenate([ya, yb], axis=1)
        ffn = term if ffn is None else ffn + term
    out = h_ref[...] + g2_ref[...] * (_rms(ffn) * gpost_ref[...])

    @pl.when(i < CTX_TILES)
    def _():
        oc_ref[...] = out

    @pl.when(i >= CTX_TILES)
    def _():
        ol_ref[...] = out


def _combine(y_tok, gate, h2, mods3, g_post_ffn3, l):
    row = lambda n: pl.BlockSpec((TM, n), lambda i: (i, 0))
    choice = lambda k: pl.BlockSpec((TM, HALF), lambda i: (k * N_TILES + i, 0))
    return pl.pallas_call(
        _combine_kernel,
        grid=(N_TILES,),
        in_specs=[choice(0), choice(1), choice(2), choice(3), row(TOP_K), row(D_MODEL),
                  pl.BlockSpec((None, 1, D_MODEL), lambda i: (_mod_index(i), 0, 5)),
                  pl.BlockSpec((None, 1, D_MODEL), lambda i: (l, 0, 0))],
        out_specs=[pl.BlockSpec((TM, D_MODEL), lambda i: (jnp.minimum(i, CTX_TILES - 1), 0)),
                   pl.BlockSpec((TM, D_MODEL), lambda i: (jnp.maximum(i - CTX_TILES, 0), 0))],
        out_shape=[jax.ShapeDtypeStruct((N_CTX, D_MODEL), F32), jax.ShapeDtypeStruct((N_LAT, D_MODEL), F32)],
        compiler_params=_cparams(("arbitrary",)),
        name="combine",
    )(y_tok, y_tok, y_tok, y_tok, gate, h2, mods3, g_post_ffn3)


def _route_kernel(topk_ref, count_ref, before_ref, dest_ref, gate_ref, base_ref):
    i = pl.program_id(0)

    @pl.when(i == 0)
    def _():
        total = count_ref[...]
        padded = jnp.floor((total + (MOE_TILE - 1.0)) * (1.0 / MOE_TILE)) * MOE_TILE
        before = (lax.broadcasted_iota(jnp.int32, (N_EXPERTS, N_EXPERTS), 0)
                  < lax.broadcasted_iota(jnp.int32, (N_EXPERTS, N_EXPERTS), 1)).astype(F32)
        base_ref[...] = jnp.dot(padded, before, preferred_element_type=F32, precision=lax.Precision.HIGHEST)

    topk = topk_ref[...]
    lane = lax.broadcasted_iota(jnp.int32, (ROUTE_TM, N_EXPERTS), 1).astype(F32)
    hots = [lane == topk[:, k:k + 1] for k in range(TOP_K)]
    vals = [topk[:, TOP_K + k:TOP_K + k + 1] for k in range(TOP_K)]
    chosen = functools.reduce(jnp.add, [h.astype(F32) for h in hots])
    earlier = jnp.dot(before_ref[...], chosen.astype(BF16), preferred_element_type=F32)
    offs = base_ref[0:1, :] + earlier
    dest = [jnp.sum(jnp.where(h, offs, 0.0), axis=1, keepdims=True) for h in hots]
    dest_ref[...] = jnp.concatenate(dest, axis=1).astype(jnp.int32)
    e = [jnp.exp(v - vals[0]) for v in vals]
    den = functools.reduce(jnp.add, e)
    gate_ref[...] = jnp.concatenate(e, axis=1) / den
    base_ref[...] = base_ref[...] + jnp.sum(chosen, axis=0, keepdims=True)


ROUTE_TM = 1024


def _route(topk, counts):
    tile = lambda n: pl.BlockSpec((ROUTE_TM, n), lambda i: (i, 0))
    dest, gate = pl.pallas_call(
        _route_kernel,
        grid=(N_TOK // ROUTE_TM,),
        in_specs=[tile(2 * TOP_K), pl.BlockSpec((8, N_EXPERTS), lambda i: (0, 0)),
                  pl.BlockSpec((ROUTE_TM, ROUTE_TM), lambda i: (0, 0), pipeline_mode=pl.Buffered(1))],
        out_specs=[tile(TOP_K), tile(TOP_K)],
        out_shape=[jax.ShapeDtypeStruct((N_TOK, TOP_K), jnp.int32),
                   jax.ShapeDtypeStruct((N_TOK, TOP_K), F32)],
        scratch_shapes=[pltpu.VMEM((8, N_EXPERTS), F32)],
        compiler_params=_cparams(("arbitrary",)),
        name="route",
    )(topk, counts, jnp.asarray(np.tri(ROUTE_TM, ROUTE_TM, -1), dtype=BF16))
    counts = counts[0].astype(jnp.int32)
    expert = jnp.arange(N_EXPERTS, dtype=jnp.int32)
    padded = (counts + MOE_TILE - 1) // MOE_TILE * MOE_TILE
    pad_end = jnp.sum(jnp.where(expert[None, :] <= expert[:, None], padded[None, :], 0), axis=1)
    block = jnp.arange(N_MOE_BLOCKS, dtype=jnp.int32)
    block_e = jnp.minimum(jnp.sum((pad_end[None, :] <= block[:, None] * MOE_TILE).astype(jnp.int32), axis=1),
                          N_EXPERTS - 1)
    n_valid = pad_end[-1] // MOE_TILE
    mine = expert[None, :] == block_e[:, None]
    pick = lambda v: jnp.sum(jnp.where(mine, v[None, :], 0), axis=1)
    offset = block * MOE_TILE - pick(pad_end - padded)
    valid = block < n_valid
    first = valid & (offset == 0)
    used = counts > 0
    slot = jnp.sum((used[None, :] & (expert[None, :] < block_e[:, None])).astype(jnp.int32), axis=1) % 2
    nxt = jnp.min(jnp.where(used[None, :] & (expert[None, :] > block_e[:, None]), expert[None, :], N_EXPERTS),
                  axis=1)
    nxt = jnp.where(nxt < N_EXPERTS, nxt, -1)
    rows = jnp.where(valid, jnp.clip(pick(counts) - offset, 0, MOE_TILE), 0)
    plan = tuple(a.astype(jnp.int32) for a in (block_e, n_valid[None], first, slot, nxt, rows))
    n_workers = N_TOK // SCATTER_TOKENS_PER_WORKER
    dest_sc = dest.reshape(n_workers, SCATTER_TOKENS_PER_WORKER // GATHER_CHUNK, GATHER_CHUNK, TOP_K)
    dest_sc = dest_sc.transpose(0, 1, 3, 2).reshape(n_workers, -1, GATHER_CHUNK)
    return gate, dest.T, dest_sc, plan


def _rope_tables():
    f32 = np.float32
    t = np.arange(DEC_SEQ)
    inv = np.power(f32(ROPE_THETA), -np.arange(ROPE_PAIRS, dtype=f32) / f32(ROPE_PAIRS)).astype(f32)
    row = (t // GRID_W).astype(f32)[:, None] * inv
    col = (t % GRID_W).astype(f32)[:, None] * inv
    zeros = np.zeros_like(row)
    cos = np.concatenate([np.cos(row), np.cos(row), np.cos(col), np.cos(col)], axis=1)
    s1 = np.concatenate([-np.sin(row), zeros, -np.sin(col), zeros], axis=1)
    s2 = np.concatenate([zeros, np.sin(row), zeros, np.sin(col)], axis=1)
    ident = lambda v: np.full((PROJ_TM, HEAD_DIM), v, f32)
    tables = [np.concatenate([x, ident(v)], axis=0) for x, v in ((cos, 1.0), (s1, 0.0), (s2, 0.0))]
    return [jnp.asarray(np.tile(x, (1, 2)), dtype=F32) for x in tables]


def kernel(x_prompt, x_sample, cache_a, cache_b, cache_c, c, c_ctx, w_ada, b_ada, g_pre_mix, g_post_mix,
           g_pre_ffn, g_post_ffn, w_in, g_q_b, g_k_b, sink_a, rpb_c, w_br_a, w_br_b, w_br_c, w_out,
           w_router, b_router, w_gate_up, b_gate_up, w_down, b_down):
    h = (x_prompt.reshape(N_CTX, D_MODEL), x_sample.reshape(N_LAT, D_MODEL))
    cond8 = jnp.concatenate([c_ctx[None], c, jnp.zeros((3, D_MODEL), F32)], axis=0)
    cache_a = cache_a.astype(BF16).reshape(DEC_BATCH, DEPTH, 2, PAST_LEN, W_KV)
    cache_b = cache_b.astype(BF16).reshape(DEC_BATCH, DEPTH, 2, PAST_LEN, W_KV)
    cache_c = cache_c.astype(BF16).reshape(DEC_BATCH, DEPTH, 2, PAST_LEN, W_HEADS)
    cos_t, s1_t, s2_t = _rope_tables()
    bd = jnp.asarray(np.kron(np.eye(256 // HEAD_DIM), np.full((HEAD_DIM, HEAD_DIM), 1.0 / HEAD_DIM)), dtype=BF16)
    vec3 = lambda a: a.reshape(DEPTH, 1, a.shape[-1])
    scale = HEAD_DIM ** -0.5
    states = []
    mods_all = _adaln(cond8, w_ada, vec3(b_ada))
    for l in range(DEPTH):
        w = w_in[l]
        pair_cols = lambda a: a.reshape(D_MODEL, 2, 4, HEAD_DIM).transpose(0, 2, 1, 3).reshape(D_MODEL, 512)
        pair_rows = lambda a: a.reshape(2, 4, HEAD_DIM, D_MODEL).transpose(1, 0, 2, 3).reshape(512, D_MODEL)
        w_in_p = jnp.concatenate(
            [pair_cols(w[:, 0:512]) * scale, pair_cols(w[:, 768:1280]), w[:, 1536:2048] * scale,
             w[:, 2048:2560], w[:, 2560:3072],
             w[:, 512:640], w[:, 1280:1408], w[:, 640:768], w[:, 1408:1536], w[:, 3072:]], axis=1).astype(BF16)
        gq = (jnp.tile(g_q_b[l], H_B) * scale)[None]
        gk = jnp.tile(g_k_b[l], KV_B)[None]
        w_br = jnp.stack([pair_rows(w_br_a[l]), pair_rows(w_br_b[l]), w_br_c[l]], axis=0).astype(BF16)
        w_out_b = w_out[l].astype(BF16)
        w_r_hi = w_router[l].astype(BF16)
        w_r_lo = (w_router[l] - w_r_hi.astype(F32)).astype(BF16)
        w_router2 = jnp.concatenate([w_r_hi, w_r_lo], axis=1)

        mods3 = mods_all[l].reshape(8, 1, 6 * D_MODEL)
        q_all, kvab, kvc, gates, *states = _proj(h, mods3, vec3(g_pre_mix), w_in_p, gq, gk, bd,
                                                 cos_t, s1_t, s2_t, tuple(states), l)
        if l == DEPTH - 1:
            states = [s.reshape(BATCH, DEPTH, 2, SEQ, n, HEAD_DIM) for s, n in zip(states, (KV_A, KV_B, H_C))]
            states, gates = lax.optimization_barrier((states, gates))
        o_ctx = _ctx_attn(sink_a, q_all, kvab, kvc, l)
        o_a = _win_attn(sink_a, q_all, kvab, cache_a, l)
        o_b = _dense_attn(q_all, kvab, cache_b, l)
        o_c = _nbr_attn(q_all, kvc, cache_c, _nbr_bias_table(rpb_c[l]), l)
        h2, hn2, topk, counts = _merge(o_ctx, o_a, o_b, o_c, gates, h, mods3, w_br, w_out_b,
                                       vec3(g_post_mix), vec3(g_pre_ffn), w_router2, vec3(b_router), l)
        gate, dest, dest_sc, plan = _route(topk, counts)
        x_slots = _scatter_rows(hn2, dest_sc, N_SLOTS)
        y_slots = _moe(plan, x_slots, w_gate_up,
                       b_gate_up.reshape(DEPTH, N_EXPERTS, 1, 2 * D_FF), w_down,
                       b_down.reshape(DEPTH, N_EXPERTS, 1, D_MODEL), l)
        h = _combine(_gather_rows(y_slots, dest.reshape(-1)), gate, h2, mods3, vec3(g_post_ffn), l)

    return (h[0].reshape(BATCH, SEQ, D_MODEL), h[1].reshape(DEC_BATCH, DEC_SEQ, D_MODEL), *states)
```

```python
import functools

import jax
import jax.numpy as jnp
import numpy as np
from jax import lax
from jax.experimental import pallas as pl
from jax.experimental.pallas import tpu as pltpu
from jax.experimental.pallas import tpu_sc as plsc

D_MODEL = 1024
BATCH = 32
SEQ = 256
DEPTH = 2
DEC_BATCH = 4
DEC_SEQ = 2048
PAST_LEN = 512
GRID_W = 64
HEAD_DIM = 64
H_A = 8
KV_A = 2
H_B = 8
KV_B = 2
H_C = 8
WINDOW_A = 128
NA_ROWS = 8
NA_COLS = 16
ROPE_THETA = 10000.0
ROPE_PAIRS = HEAD_DIM // 4
N_EXPERTS = 32
TOP_K = 4
D_FF = D_MODEL
SWIGLU_ALPHA = 1.702
SWIGLU_LIMIT = 7.0
EPS = 1e-6

W_HEADS = H_A * HEAD_DIM
W_KV = KV_A * HEAD_DIM
N_CTX = BATCH * SEQ
N_LAT = DEC_BATCH * DEC_SEQ
N_TOK = N_CTX + N_LAT
GRID_ROWS = DEC_SEQ // GRID_W
D_IN = 3 * W_HEADS + 4 * W_KV + 2 * W_HEADS + 3 * D_MODEL

TM = 256
N_TILES = N_TOK // TM
CTX_TILES = N_CTX // TM
LAT_TILES_PER_BATCH = DEC_SEQ // TM
TQ = 128
MOE_CHAIN = 256
MOE_TILE = 2 * MOE_CHAIN
N_SLOTS = N_TOK * TOP_K + N_EXPERTS * MOE_TILE
N_MOE_BLOCKS = N_SLOTS // MOE_TILE
NEG = -1e30
VMEM_LIMIT = 56 * 1024 * 1024

BF16 = jnp.bfloat16
F32 = jnp.float32


def _cparams(sem):
    return pltpu.CompilerParams(dimension_semantics=sem, vmem_limit_bytes=VMEM_LIMIT)


def _mod_index(i):
    return jnp.where(i < CTX_TILES, 0, 1 + (i - CTX_TILES) // LAT_TILES_PER_BATCH)


def _rms(x):
    return x * lax.rsqrt(jnp.mean(x * x, axis=-1, keepdims=True) + EPS)


HALF = D_MODEL // 2


def _pack_halves(x):
    hi = lax.bitcast_convert_type(x[:, :HALF].astype(BF16).astype(F32), jnp.uint32)
    lo = lax.bitcast_convert_type(x[:, HALF:].astype(BF16).astype(F32), jnp.uint32)
    return lax.bitcast_convert_type(hi | (lo >> 16), jnp.int32)


def _unpack_halves(w):
    u = lax.bitcast_convert_type(w, jnp.uint32)
    return (lax.bitcast_convert_type(u & jnp.uint32(0xFFFF0000), F32),
            lax.bitcast_convert_type(u << 16, F32))


GATHER_CHUNK = 64


def _gather_rows(table, idx):
    n = idx.shape[0]
    width = table.shape[1]
    info = plsc.get_sparse_core_info()
    n_workers = info.num_cores * info.num_subcores
    per_worker = n // n_workers
    n_chunks = per_worker // GATHER_CHUNK
    assert per_worker * n_workers == n and n_chunks * GATHER_CHUNK == per_worker and n_chunks % 2 == 0
    mesh = plsc.VectorSubcoreMesh(core_axis_name="core", subcore_axis_name="subcore")

    @functools.partial(
        pl.kernel, out_type=jax.ShapeDtypeStruct((n, width), table.dtype), mesh=mesh,
        scratch_types=[pltpu.VMEM((per_worker,), jnp.int32),
                       pltpu.VMEM((2, GATHER_CHUNK, width), table.dtype),
                       pltpu.SemaphoreType.DMA((2,)), pltpu.SemaphoreType.DMA((2,))])
    def gather(table_hbm, idx_hbm, out_hbm, idx_v, rows_v, gather_sem, write_sem):
        worker = lax.axis_index("subcore") * info.num_cores + lax.axis_index("core")
        base = worker * per_worker
        pltpu.sync_copy(idx_hbm.at[pl.ds(base, per_worker)], idx_v)

        def fetch(chunk, slot):
            rows = idx_v.at[pl.ds(chunk * GATHER_CHUNK, GATHER_CHUNK)]
            return pltpu.make_async_copy(table_hbm.at[rows], rows_v.at[slot], gather_sem.at[slot])

        def write(chunk, slot):
            dst = out_hbm.at[pl.ds(base + chunk * GATHER_CHUNK, GATHER_CHUNK)]
            return pltpu.make_async_copy(rows_v.at[slot], dst, write_sem.at[slot])

        fetch(0, 0).start()

        @pl.loop(0, n_chunks, step=2)
        def _(c):
            @pl.when(c > 0)
            def _():
                write(c - 1, 1).wait()

            fetch(c + 1, 1).start()
            fetch(c, 0).wait()
            write(c, 0).start()
            write(c, 0).wait()

            @pl.when(c + 2 < n_chunks)
            def _():
                fetch(c + 2, 0).start()

            fetch(c + 1, 1).wait()
            write(c + 1, 1).start()

        write(n_chunks - 1, 1).wait()

    return gather(table, idx)


SC_WORKERS_V7X = 32
SCATTER_TOKENS_PER_WORKER = N_TOK // SC_WORKERS_V7X


def _scatter_rows(table, dest_sc, n_out):
    width = table.shape[1]
    info = plsc.get_sparse_core_info()
    assert info.num_cores * info.num_subcores == SC_WORKERS_V7X
    per_worker = SCATTER_TOKENS_PER_WORKER
    n_chunks = per_worker // GATHER_CHUNK
    assert n_chunks % 2 == 0 and dest_sc.shape == (SC_WORKERS_V7X, n_chunks * TOP_K, GATHER_CHUNK)
    mesh = plsc.VectorSubcoreMesh(core_axis_name="core", subcore_axis_name="subcore")

    @functools.partial(
        pl.kernel, out_type=jax.ShapeDtypeStruct((n_out, width), table.dtype), mesh=mesh,
        scratch_types=[pltpu.VMEM((n_chunks * TOP_K, GATHER_CHUNK), jnp.int32),
                       pltpu.VMEM((2, GATHER_CHUNK, width), table.dtype),
                       pltpu.SemaphoreType.DMA((2,)), pltpu.SemaphoreType.DMA((2,))])
    def scatter(table_hbm, dest_hbm, out_hbm, idx_v, rows_v, read_sem, write_sem):
        worker = lax.axis_index("subcore") * info.num_cores + lax.axis_index("core")
        base = worker * per_worker
        pltpu.sync_copy(dest_hbm.at[worker], idx_v)

        def read(chunk, slot):
            src = table_hbm.at[pl.ds(base + chunk * GATHER_CHUNK, GATHER_CHUNK)]
            return pltpu.make_async_copy(src, rows_v.at[slot], read_sem.at[slot])

        def writes(chunk, slot):
            return [pltpu.make_async_copy(rows_v.at[slot], out_hbm.at[idx_v.at[chunk * TOP_K + k]],
                                          write_sem.at[slot]) for k in range(TOP_K)]

        read(0, 0).start()

        @pl.loop(0, n_chunks, step=2)
        def _(c):
            @pl.when(c > 0)
            def _():
                for cp in writes(c - 1, 1):
                    cp.wait()

            read(c + 1, 1).start()
            read(c, 0).wait()
            for cp in writes(c, 0):
                cp.start()
            for cp in writes(c, 0):
                cp.wait()

            @pl.when(c + 2 < n_chunks)
            def _():
                read(c + 2, 0).start()

            read(c + 1, 1).wait()
            for cp in writes(c + 1, 1):
                cp.start()

        for cp in writes(n_chunks - 1, 1):
            cp.wait()

    return scatter(table, dest_sc)


def _adaln_kernel(c_ref, w_ref, b_ref, o_ref):
    c = c_ref[...]
    s = c / (1.0 + jnp.exp(-c))
    o_ref[...] = jnp.dot(s, w_ref[...], preferred_element_type=F32,
                         precision=lax.Precision.HIGHEST) + b_ref[...]


def _adaln(cond8, w_ada, b_ada3):
    tn = 768
    return pl.pallas_call(
        _adaln_kernel,
        grid=(DEPTH, 6 * D_MODEL // tn),
        in_specs=[pl.BlockSpec((8, D_MODEL), lambda l, j: (0, 0)),
                  pl.BlockSpec((None, D_MODEL, tn), lambda l, j: (l, 0, j)),
                  pl.BlockSpec((None, 1, tn), lambda l, j: (l, 0, j))],
        out_specs=pl.BlockSpec((None, 8, tn), lambda l, j: (l, 0, j)),
        out_shape=jax.ShapeDtypeStruct((DEPTH, 8, 6 * D_MODEL), F32),
        compiler_params=_cparams(("arbitrary", "arbitrary")),
        name="adaln",
    )(cond8, w_ada, b_ada3)


C_QA, C_QB, C_QC, C_KC, C_VC, C_KAB, C_GL = 0, 512, 1024, 1536, 2048, 2560, 3072


def _proj_kernel(hc_ref, hl_ref, sh_ref, sc_ref, gpre_ref, w_ref, gq_ref, gk_ref, bd_ref,
                 cos_ref, s1_ref, s2_ref, *rest):
    q_ref, kvab_ref, kvc_ref, gates_ref, sta_ref, stb_ref, stc_ref = rest[-7:]
    i = pl.program_id(0)
    bd = bd_ref[...]

    def rope(t, rs):
        cos, s1, s2 = cos_ref[rs, :], s1_ref[rs, :], s2_ref[rs, :]
        parts = []
        for g in range(t.shape[1] // 128):
            tg = t[:, g * 128:(g + 1) * 128]
            parts.append(tg * cos + pltpu.roll(tg, 112, 1) * s1 + pltpu.roll(tg, 16, 1) * s2)
        return parts[0] if len(parts) == 1 else jnp.concatenate(parts, axis=1)

    def headnorm(t, g):
        sq = (t * t).astype(BF16)
        n = t.shape[1]
        if n == 128:
            ms = jnp.dot(sq, bd[:128, :128], preferred_element_type=F32)
        else:
            ms = jnp.concatenate(
                [jnp.dot(sq[:, c:c + 256], bd, preferred_element_type=F32) for c in range(0, n, 256)],
                axis=1)
        return t * lax.rsqrt(ms + EPS) * g

    kv_f32 = []
    for c in range(PROJ_TM // TM):
        rs = slice(c * TM, (c + 1) * TM)
        hn = _rms(jnp.where(i < PROJ_CTX_TILES, hc_ref[rs, :], hl_ref[rs, :])) * gpre_ref[...]
        hb = (hn * (1.0 + sc_ref[...]) + sh_ref[...]).astype(BF16)
        proj = lambda c0, n, hb=hb: jnp.dot(hb, w_ref[:, c0:c0 + n], preferred_element_type=F32)
        q_ref[rs, 0:512] = rope(proj(C_QA, 512), rs).astype(BF16)
        q_ref[rs, 512:1024] = rope(headnorm(proj(C_QB, 512), gq_ref[...]), rs).astype(BF16)
        q_ref[rs, 1024:1536] = proj(C_QC, 512).astype(BF16)
        kc = proj(C_KC, 512)
        vc = proj(C_VC, 512)
        kvc_ref[rs, 0:512] = kc.astype(BF16)
        kvc_ref[rs, 512:1024] = vc.astype(BF16)
        kab = proj(C_KAB, 512)
        ka = kab[:, 0:128]
        kb = headnorm(kab[:, 128:256], gk_ref[...])
        kvab_ref[rs, 0:128] = rope(ka, rs).astype(BF16)
        kvab_ref[rs, 128:256] = rope(kb, rs).astype(BF16)
        kvab_ref[rs, 256:512] = kab[:, 256:512].astype(BF16)
        for j in range(6):
            gl = proj(C_GL + j * 512, 512)
            gates_ref[rs, j * 512:(j + 1) * 512] = (1.0 / (1.0 + jnp.exp(-gl))).astype(BF16)
        kv_f32.append(((ka, kab[:, 256:384]), (kb, kab[:, 384:512]), (kc, vc)))

    @pl.when(i < PROJ_CTX_TILES)
    def _():
        for c, per_mixer in enumerate(kv_f32):
            for st_ref, (k, v) in zip((sta_ref, stb_ref, stc_ref), per_mixer):
                if len(st_ref.shape) == 5:
                    st_ref[c, 0, 0] = k
                    st_ref[c, 0, 1] = v
                    st_ref[c, 1:] = jnp.zeros((DEPTH - 1,) + tuple(st_ref.shape[2:]), F32)
                else:
                    st_ref[c, 0] = k
                    st_ref[c, 1] = v


PROJ_TM = 2 * TM
PROJ_TILES = N_TOK // PROJ_TM
PROJ_CTX_TILES = N_CTX // PROJ_TM
PROJ_LAT_TILES_PER_BATCH = DEC_SEQ // PROJ_TM


def _proj(h, mods3, g_pre3, w_in_p, gq, gk, bd, cos_t, s1_t, s2_t, prev_states, l):
    lat_tile = lambda i: i - PROJ_CTX_TILES
    rope_idx = lambda i: jnp.where(i < PROJ_CTX_TILES, PROJ_LAT_TILES_PER_BATCH,
                                   lat_tile(i) % PROJ_LAT_TILES_PER_BATCH)
    mod_idx = lambda i: jnp.where(i < PROJ_CTX_TILES, 0, 1 + lat_tile(i) // PROJ_LAT_TILES_PER_BATCH)
    ctx_block = lambda i: jnp.minimum(i, PROJ_CTX_TILES - 1)
    const = lambda shape: pl.BlockSpec(shape, lambda i: (0,) * len(shape), pipeline_mode=pl.Buffered(1))
    rope_spec = pl.BlockSpec((PROJ_TM, 128), lambda i: (rope_idx(i), 0))
    row = lambda n: pl.BlockSpec((PROJ_TM, n), lambda i: (i, 0))
    per_step = PROJ_TM // SEQ
    if l == 0:
        state_spec = lambda n: pl.BlockSpec((per_step, DEPTH, 2, SEQ, n), lambda i: (ctx_block(i), 0, 0, 0, 0))
    else:
        state_spec = lambda n: pl.BlockSpec((per_step, None, 2, SEQ, n), lambda i: (ctx_block(i), l, 0, 0, 0))
    state_shape = lambda n: jax.ShapeDtypeStruct((BATCH, DEPTH, 2, SEQ, n), F32)
    n_in = 12
    return pl.pallas_call(
        _proj_kernel,
        grid=(PROJ_TILES,),
        in_specs=[pl.BlockSpec((PROJ_TM, D_MODEL), lambda i: (ctx_block(i), 0)),
                  pl.BlockSpec((PROJ_TM, D_MODEL), lambda i: (jnp.maximum(lat_tile(i), 0), 0)),
                  pl.BlockSpec((None, 1, D_MODEL), lambda i: (mod_idx(i), 0, 0)),
                  pl.BlockSpec((None, 1, D_MODEL), lambda i: (mod_idx(i), 0, 1)),
                  pl.BlockSpec((None, 1, D_MODEL), lambda i: (l, 0, 0)),
                  const((D_MODEL, D_IN)), const((1, 512)), const((1, 128)), const((256, 256)),
                  rope_spec, rope_spec, rope_spec] + [pl.BlockSpec(memory_space=pl.ANY)] * len(prev_states),
        out_specs=[row(1536), row(512), row(1024), row(3072),
                   state_spec(W_KV), state_spec(W_KV), state_spec(W_HEADS)],
        out_shape=[jax.ShapeDtypeStruct((N_TOK, 1536), BF16),
                   jax.ShapeDtypeStruct((N_TOK, 512), BF16),
                   jax.ShapeDtypeStruct((N_TOK, 1024), BF16),
                   jax.ShapeDtypeStruct((N_TOK, 3072), BF16),
                   state_shape(W_KV), state_shape(W_KV), state_shape(W_HEADS)],
        input_output_aliases={n_in + j: 4 + j for j in range(len(prev_states))},
        compiler_params=_cparams(("arbitrary",)),
        name="proj",
    )(*h, mods3, mods3, g_pre3, w_in_p, gq, gk, bd, cos_t, s1_t, s2_t, *prev_states)


def _qk(q, k):
    return lax.dot_general(q, k, (((1,), (1,)), ((), ())), preferred_element_type=F32)


def _softmax_pv(scores, values, sink=None, halves=None):
    m = functools.reduce(jnp.maximum, [jnp.max(s, axis=-1, keepdims=True) for s in scores])
    if sink is not None:
        m = jnp.maximum(m, sink)
    if halves is None:
        ps = [jnp.exp(s - m) for s in scores]
        den = functools.reduce(jnp.add, [jnp.sum(p, axis=-1, keepdims=True) for p in ps])
        if sink is not None:
            den = den + jnp.exp(sink - m)
        o = functools.reduce(jnp.add, [jnp.dot(p.astype(BF16), v, preferred_element_type=F32)
                                       for p, v in zip(ps, values)])
        return o / den
    ps = [jnp.exp(s - m).astype(BF16) for s in scores]
    rows = ps[0].shape[0]
    split = {"both": rows // 2, "lo": rows, "hi": 0}[halves]
    one = jnp.ones((), BF16)

    def pv(r0, r1, keep_lo):
        acc = None
        for p, v in zip(ps, values):
            lo = _lo_lanes(v.shape[0])
            v1 = jnp.where(lo, v, one) if keep_lo else jnp.where(lo, one, v)
            t = jnp.dot(p[r0:r1], v1, preferred_element_type=F32)
            acc = t if acc is None else acc + t
        return acc

    parts = ([pv(0, split, True)] if split > 0 else []) + ([pv(split, rows, False)] if split < rows else [])
    o = parts[0] if len(parts) == 1 else jnp.concatenate(parts, axis=0)
    if sink is not None:
        is_lo_row = lax.broadcasted_iota(jnp.int32, (rows, 128), 0) < split
        o = o + jnp.where(is_lo_row != _lo_lanes(rows), jnp.exp(sink - m), 0.0)
    return o * pltpu.roll(1.0 / o, HEAD_DIM, 1)


def _lo_lanes(rows):
    return lax.broadcasted_iota(jnp.int32, (rows, 128), 1) < HEAD_DIM


def _stack_pairs(q, n_pairs):
    lo = _lo_lanes(q.shape[0])
    zero = jnp.zeros((q.shape[0], 128), q.dtype)
    pairs = [q[:, p * 128:(p + 1) * 128] for p in range(n_pairs)]
    return jnp.concatenate([jnp.where(lo, x, zero) for x in pairs] + [jnp.where(lo, zero, x) for x in pairs],
                           axis=0)


def _unstack_pairs(o, n_pairs):
    rows = o.shape[0] // (2 * n_pairs)
    lo = _lo_lanes(rows)
    return jnp.concatenate(
        [jnp.where(lo, o[p * rows:(p + 1) * rows], o[(n_pairs + p) * rows:(n_pairs + p + 1) * rows])
         for p in range(n_pairs)], axis=1)


def _sink_column(sink_ref, l, rows):
    return jnp.concatenate([jnp.full((rows, 1), sink_ref[l, h], F32) for h in range(H_A)], axis=0)


def _ctx_attn_kernel(l, sink_ref, q_ref, kvab_ref, kvc_ref, o_ref):
    for p in range(H_A // 2):
        cs = slice(p * 128, (p + 1) * 128)
        sink = jnp.concatenate([jnp.full((SEQ, 1), sink_ref[l, h], F32) for h in (p, H_A // 2 + p)], axis=0)
        o = _softmax_pv([_qk(_stack_pairs(q_ref[:, cs], 1), kvab_ref[:, 0:128])], [kvab_ref[:, 256:384]], sink)
        o_ref[:, cs] = _unstack_pairs(o, 1).astype(BF16)
    for p in range(H_B // 2):
        cs = slice(512 + p * 128, 512 + (p + 1) * 128)
        o = _softmax_pv([_qk(_stack_pairs(q_ref[:, cs], 1), kvab_ref[:, 128:256])], [kvab_ref[:, 384:512]])
        o_ref[:, cs] = _unstack_pairs(o, 1).astype(BF16)
    for hp in range(H_C // 2):
        cs = slice(hp * 128, (hp + 1) * 128)
        qc = _stack_pairs(q_ref[:, 1024 + hp * 128:1024 + (hp + 1) * 128], 1)
        o = _softmax_pv([_qk(qc, kvc_ref[:, cs])], [kvc_ref[:, 512 + hp * 128:512 + (hp + 1) * 128]])
        o_ref[:, 1024 + hp * 128:1024 + (hp + 1) * 128] = _unstack_pairs(o, 1).astype(BF16)


def _ctx_attn(sink_a, q_all, kvab, kvc, l):
    row = lambda n: pl.BlockSpec((SEQ, n), lambda b: (b, 0))
    return pl.pallas_call(
        functools.partial(_ctx_attn_kernel, l),
        grid=(BATCH,),
        in_specs=[pl.BlockSpec(memory_space=pltpu.SMEM), row(1536), row(512), row(1024)],
        out_specs=row(1536),
        out_shape=jax.ShapeDtypeStruct((N_CTX, 1536), BF16),
        compiler_params=_cparams(("arbitrary",)),
        name="ctx_attn",
    )(sink_a, q_all, kvab, kvc)


def _win_attn_kernel(l, sink_ref, q_ref, prev_ref, cur_ref, nxt_ref, ck_ref, cv_ref, o_ref):
    n = pl.program_id(1)
    nb = DEC_SEQ // TQ
    rows = H_A * TQ
    qpos = lax.broadcasted_iota(jnp.int32, (rows, TQ), 0) % TQ
    kpos = lax.broadcasted_iota(jnp.int32, (rows, TQ), 1)
    mask_prev = (kpos >= qpos) & (n > 0)
    mask_next = (kpos <= qpos) & (n < nb - 1)
    ks, vs = slice(0, 128), slice(256, 384)
    qs = _stack_pairs(q_ref[...], 4)
    s_prev = jnp.where(mask_prev, _qk(qs, prev_ref[:, ks]), NEG)
    s_cur = _qk(qs, cur_ref[:, ks])
    s_next = jnp.where(mask_next, _qk(qs, nxt_ref[:, ks]), NEG)
    s_ctx = _qk(qs, ck_ref[...])
    o = _softmax_pv([s_prev, s_cur, s_next, s_ctx],
                    [prev_ref[:, vs], cur_ref[:, vs], nxt_ref[:, vs], cv_ref[...]],
                    _sink_column(sink_ref, l, TQ), halves="both")
    o_ref[...] = _unstack_pairs(o, 4).astype(BF16)


def _win_attn(sink_a, q_all, kvab, cache_a, l):
    nb = DEC_SEQ // TQ
    base = N_CTX // TQ
    kv_spec = lambda f: pl.BlockSpec((TQ, 512), lambda b, n: (base + b * nb + f(n), 0))
    cache_spec = lambda s: pl.BlockSpec((None, None, None, PAST_LEN, W_KV), lambda b, n: (b, l, s, 0, 0))
    return pl.pallas_call(
        functools.partial(_win_attn_kernel, l),
        grid=(DEC_BATCH, nb),
        in_specs=[pl.BlockSpec(memory_space=pltpu.SMEM),
                  pl.BlockSpec((TQ, 512), lambda b, n: (base + b * nb + n, 0)),
                  kv_spec(lambda n: jnp.maximum(n - 1, 0)), kv_spec(lambda n: n),
                  kv_spec(lambda n: jnp.minimum(n + 1, nb - 1)),
                  cache_spec(0), cache_spec(1)],
        out_specs=pl.BlockSpec((TQ, 512), lambda b, n: (b * nb + n, 0)),
        out_shape=jax.ShapeDtypeStruct((N_LAT, 512), BF16),
        compiler_params=_cparams(("arbitrary", "arbitrary")),
        name="win_attn",
    )(sink_a, q_all, kvab, kvab, kvab, cache_a, cache_a)


DENSE_KEY_CHUNK = 1024
DENSE_TQ = 256


def _online_softmax_pv(q, key_chunks, value_chunks, keep_lo):
    one = jnp.ones((), BF16)
    m = acc = None
    for k, v in zip(key_chunks, value_chunks):
        s = _qk(q, k)
        m_new = jnp.max(s, axis=-1, keepdims=True)
        if m is not None:
            m_new = jnp.maximum(m, m_new)
        p = jnp.exp(s - m_new).astype(BF16)
        lo = _lo_lanes(v.shape[0])
        pv = jnp.dot(p, jnp.where(lo, v, one) if keep_lo else jnp.where(lo, one, v), preferred_element_type=F32)
        acc = pv if acc is None else acc * jnp.exp(m - m_new) + pv
        m = m_new
    return acc * pltpu.roll(1.0 / acc, HEAD_DIM, 1)


def _dense_attn_kernel(q_ref, kv_ref, ck_ref, cv_ref, o_ref):
    qs = _stack_pairs(q_ref[...], 4)
    half = qs.shape[0] // 2
    starts = range(0, DEC_SEQ, DENSE_KEY_CHUNK)
    keys = [kv_ref[c:c + DENSE_KEY_CHUNK, 128:256] for c in starts] + [ck_ref[...]]
    values = [kv_ref[c:c + DENSE_KEY_CHUNK, 384:512] for c in starts] + [cv_ref[...]]
    outs = [_online_softmax_pv(qs[g * half:(g + 1) * half], keys, values, g == 0) for g in range(KV_B)]
    o_ref[...] = _unstack_pairs(jnp.concatenate(outs, axis=0), 4).astype(BF16)


def _dense_attn(q_all, kvab, cache_b, l):
    nb = DEC_SEQ // DENSE_TQ
    base = N_CTX // DENSE_TQ
    cache_spec = lambda s: pl.BlockSpec((None, None, None, PAST_LEN, W_KV), lambda b, n: (b, l, s, 0, 0))
    return pl.pallas_call(
        _dense_attn_kernel,
        grid=(DEC_BATCH, nb),
        in_specs=[pl.BlockSpec((DENSE_TQ, 512), lambda b, n: (base + b * nb + n, 1)),
                  pl.BlockSpec((DEC_SEQ, 512), lambda b, n: (N_CTX // DEC_SEQ + b, 0)),
                  cache_spec(0), cache_spec(1)],
        out_specs=pl.BlockSpec((DENSE_TQ, 512), lambda b, n: (b * nb + n, 0)),
        out_shape=jax.ShapeDtypeStruct((N_LAT, 512), BF16),
        compiler_params=_cparams(("arbitrary", "arbitrary")),
        name="dense_attn",
    )(q_all, kvab, cache_b, cache_b)


NBR_BAND = 4
NBR_Q = NBR_BAND * GRID_W
NBR_WIN_ROWS = 12
NBR_N_BANDS = GRID_ROWS // NBR_BAND
NBR_KBLK = NBR_Q
NBR_WIN_BLOCKS = NBR_WIN_ROWS * GRID_W // NBR_KBLK
NBR_LAST_KB = (GRID_ROWS - NBR_WIN_ROWS) * GRID_W // NBR_KBLK


def _nbr_window_block(band):
    return jnp.clip(band - 1, 0, NBR_LAST_KB)


def _nbr_attn_kernel(q_ref, k0_ref, k1_ref, k2_ref, ck_ref, cv_ref, bias_ref, o_ref):
    k_refs = (k0_ref, k1_ref, k2_ref)
    for hp in range(H_C // 2):
        cs = slice(hp * 128, (hp + 1) * 128)
        vs = slice(512 + hp * 128, 512 + (hp + 1) * 128)
        qs = _stack_pairs(q_ref[:, cs], 1)
        bias = jnp.concatenate([bias_ref[2 * hp], bias_ref[2 * hp + 1]], axis=0)
        scores = [_qk(qs, kr[:, cs]) + bias[:, j * NBR_KBLK:(j + 1) * NBR_KBLK] for j, kr in enumerate(k_refs)]
        scores.append(_qk(qs, ck_ref[:, cs]))
        o = _softmax_pv(scores, [kr[:, vs] for kr in k_refs] + [cv_ref[:, cs]])
        o_ref[:, cs] = _unstack_pairs(o, 1).astype(BF16)


def _nbr_attn(q_all, kvc, cache_c, bias_t, l):
    q_base = N_CTX // NBR_Q
    k_base = N_CTX // NBR_KBLK
    blocks_per_batch = DEC_SEQ // NBR_KBLK
    cache_spec = lambda s: pl.BlockSpec((None, None, None, PAST_LEN, W_HEADS), lambda band, b: (b, l, s, 0, 0))
    key_spec = lambda j: pl.BlockSpec(
        (NBR_KBLK, 1024), lambda band, b: (k_base + b * blocks_per_batch + _nbr_window_block(band) + j, 0))
    band_type = lambda band: jnp.where(band == 0, 0, jnp.where(band == NBR_N_BANDS - 1, 2, 1))
    return pl.pallas_call(
        _nbr_attn_kernel,
        grid=(NBR_N_BANDS, DEC_BATCH),
        in_specs=[pl.BlockSpec((NBR_Q, 512), lambda band, b: (q_base + b * NBR_N_BANDS + band, 2)),
                  key_spec(0), key_spec(1), key_spec(2), cache_spec(0), cache_spec(1),
                  pl.BlockSpec((None, H_C, NBR_Q, NBR_WIN_ROWS * GRID_W),
                               lambda band, b: (band_type(band), 0, 0, 0))],
        out_specs=pl.BlockSpec((NBR_Q, 512), lambda band, b: (b * NBR_N_BANDS + band, 0)),
        out_shape=jax.ShapeDtypeStruct((N_LAT, 512), BF16),
        compiler_params=_cparams(("arbitrary", "arbitrary")),
        name="nbr_attn",
    )(q_all, kvc, kvc, kvc, cache_c, cache_c, bias_t)


def _nbr_bias_table(rpb_l):
    c = np.arange(GRID_W)[:, None]
    kc = np.arange(GRID_W)[None, :]
    c_start = np.clip(c - NA_COLS // 2, 0, GRID_W - NA_COLS)
    valid = (kc >= c_start) & (kc < c_start + NA_COLS)
    pad = GRID_W - NA_COLS
    rpb_pad = jnp.pad(rpb_l.astype(F32), ((0, 0), (0, 0), (pad, pad)))
    toeplitz = jnp.stack([rpb_pad[:, :, GRID_W - 1 - q:2 * GRID_W - 1 - q] for q in range(GRID_W)], axis=2)
    t = jnp.where(valid[None, None], toeplitz, NEG)
    neg = jnp.full((H_C, GRID_W, GRID_W), NEG, F32)
    tables = []
    for r0 in (0, NBR_BAND, GRID_ROWS - NBR_BAND):
        k0 = min(max(r0 - NA_ROWS // 2, 0), GRID_ROWS - NBR_WIN_ROWS)
        rows = []
        for dq in range(NBR_BAND):
            r = r0 + dq
            start = min(max(r - NA_ROWS // 2, 0), GRID_ROWS - NA_ROWS)
            cols = []
            for i in range(NBR_WIN_ROWS):
                kr = k0 + i
                cols.append(t[:, kr - r + NA_ROWS - 1] if start <= kr < start + NA_ROWS else neg)
            rows.append(jnp.concatenate(cols, axis=2))
        tables.append(jnp.concatenate(rows, axis=1))
    return jnp.stack(tables, axis=0)


def _merge_kernel(octx_ref, oa_ref, ob_ref, oc_ref, gates_ref, hc_ref, hl_ref, g1_ref, sh2_ref, sc2_ref,
                  wbr_ref, wout_ref, gpost_ref, gpre_ref, wr_ref, br_ref,
                  h2_ref, hn2_ref, topk_ref, count_ref):
    i = pl.program_id(0)
    is_ctx = i < MERGE_CTX_TILES
    chosen = None
    for r0 in range(0, MERGE_TM, TM):
        rs = slice(r0, r0 + TM)
        merged = None
        for j, lat_ref in enumerate((oa_ref, ob_ref, oc_ref)):
            o = jnp.where(is_ctx, octx_ref[rs, j * 512:(j + 1) * 512], lat_ref[rs, :])
            br = jnp.dot(o, wbr_ref[j], preferred_element_type=F32)
            term = gates_ref[rs, j * D_MODEL:(j + 1) * D_MODEL].astype(F32) * br
            merged = term if merged is None else merged + term
        t = jnp.dot(merged.astype(BF16), wout_ref[...], preferred_element_type=F32)
        h2 = jnp.where(is_ctx, hc_ref[rs, :], hl_ref[rs, :]) + g1_ref[...] * (_rms(t) * gpost_ref[...])
        h2_ref[rs, :] = h2
        hn2 = _rms(h2) * gpre_ref[...] * (1.0 + sc2_ref[...]) + sh2_ref[...]
        hn2_ref[rs, :] = _pack_halves(hn2)
        x_hi = hn2.astype(BF16)
        x_lo = (hn2 - x_hi.astype(F32)).astype(BF16)
        hi = jnp.dot(x_hi, wr_ref[...], preferred_element_type=F32)
        lo = jnp.dot(x_lo, wr_ref[:, :N_EXPERTS], preferred_element_type=F32)
        logits = hi[:, :N_EXPERTS] + (hi[:, N_EXPERTS:] + lo) + br_ref[...]
        lane = lax.broadcasted_iota(jnp.int32, (TM, N_EXPERTS), 1)
        idxs, vals = [], []
        for _ in range(TOP_K):
            idx = jnp.argmax(logits, axis=1, keepdims=True).astype(jnp.int32)
            hot = lane == idx
            idxs.append(idx.astype(F32))
            vals.append(jnp.max(logits, axis=1, keepdims=True))
            chosen = hot.astype(F32) if chosen is None else chosen + hot.astype(F32)
            logits = jnp.where(hot, -jnp.inf, logits)
        topk_ref[rs, :] = jnp.concatenate(idxs + vals, axis=1)

    @pl.when(i == 0)
    def _():
        count_ref[...] = jnp.zeros_like(count_ref)

    count_ref[...] = count_ref[...] + jnp.sum(chosen, axis=0, keepdims=True)


MERGE_TM = 2 * TM
MERGE_TILES = N_TOK // MERGE_TM
MERGE_CTX_TILES = N_CTX // MERGE_TM
MERGE_LAT_TILES_PER_BATCH = DEC_SEQ // MERGE_TM


def _merge(o_ctx, o_a, o_b, o_c, gates, h, mods3, w_br, w_out_b, g_post3, g_pre_ffn3, w_router, b_router3, l):
    ctx = lambda n: pl.BlockSpec((MERGE_TM, n), lambda i: (jnp.minimum(i, MERGE_CTX_TILES - 1), 0))
    lat_n = lambda n: pl.BlockSpec((MERGE_TM, n), lambda i: (jnp.maximum(i - MERGE_CTX_TILES, 0), 0))
    lat = lambda: lat_n(512)
    mod_index = lambda i: jnp.where(i < MERGE_CTX_TILES, 0,
                                    1 + (i - MERGE_CTX_TILES) // MERGE_LAT_TILES_PER_BATCH)
    mod = lambda j: pl.BlockSpec((None, 1, D_MODEL), lambda i: (mod_index(i), 0, j))
    lw = lambda: pl.BlockSpec((None, 1, D_MODEL), lambda i: (l, 0, 0))
    row = lambda n: pl.BlockSpec((MERGE_TM, n), lambda i: (i, 0))
    const = lambda shape: pl.BlockSpec(shape, lambda i: (0,) * len(shape), pipeline_mode=pl.Buffered(1))
    return pl.pallas_call(
        _merge_kernel,
        grid=(MERGE_TILES,),
        in_specs=[ctx(1536),
                  lat(), lat(), lat(), row(3072), ctx(D_MODEL), lat_n(D_MODEL),
                  mod(2), mod(3), mod(4),
                  const((3, 512, D_MODEL)), const((D_MODEL, D_MODEL)), lw(), lw(),
                  const((D_MODEL, 2 * N_EXPERTS)),
                  pl.BlockSpec((None, 1, N_EXPERTS), lambda i: (l, 0, 0))],
        out_specs=[row(D_MODEL), row(HALF), row(2 * TOP_K),
                   pl.BlockSpec((8, N_EXPERTS), lambda i: (0, 0))],
        out_shape=[jax.ShapeDtypeStruct((N_TOK, D_MODEL), F32),
                   jax.ShapeDtypeStruct((N_TOK, HALF), jnp.int32),
                   jax.ShapeDtypeStruct((N_TOK, 2 * TOP_K), F32),
                   jax.ShapeDtypeStruct((8, N_EXPERTS), F32)],
        compiler_params=_cparams(("arbitrary",)),
        name="merge",
    )(o_ctx, o_a, o_b, o_c, gates, *h, mods3, mods3, mods3, w_br, w_out_b, g_post3, g_pre_ffn3,
      w_router, b_router3)


def _moe_kernel(l, be_ref, nv_ref, first_ref, slot_ref, nxt_ref, rows_ref, x_ref, wgu_hbm, bgu_ref, wd_hbm, bd_ref, y_ref,
                wgu_f32, wd_f32, wgu_bf, wd_bf, sem):
    i = pl.program_id(0)

    def fetch(e, s):
        return (pltpu.make_async_copy(wgu_hbm.at[l, e], wgu_f32.at[s], sem.at[0, s]),
                pltpu.make_async_copy(wd_hbm.at[l, e], wd_f32.at[s], sem.at[1, s]))

    @pl.when(first_ref[i] == 1)
    def _():
        s = slot_ref[i]

        @pl.when(i == 0)
        def _():
            for cp in fetch(be_ref[i], s):
                cp.start()

        for cp in fetch(be_ref[i], s):
            cp.wait()

        @pl.when(nxt_ref[i] >= 0)
        def _():
            for cp in fetch(nxt_ref[i], 1 - s):
                cp.start()

        wgu_bf[...] = wgu_f32[s].astype(BF16)
        wd_bf[...] = wd_f32[s].astype(BF16)

    n_real = rows_ref[i]

    def chain(r0):
        rs = slice(r0, r0 + MOE_CHAIN)
        real = lax.broadcasted_iota(jnp.int32, (MOE_CHAIN, HALF), 0) + r0 < n_real
        xa, xb = _unpack_halves(jnp.where(real, x_ref[rs, :], 0))
        x = jnp.concatenate([xa.astype(BF16), xb.astype(BF16)], axis=1)
        b = bgu_ref[...]
        glu = jnp.dot(x, wgu_bf[:, :D_FF], preferred_element_type=F32) + b[:, :D_FF]
        lin = jnp.dot(x, wgu_bf[:, D_FF:], preferred_element_type=F32) + b[:, D_FF:]
        glu = jnp.minimum(glu, SWIGLU_LIMIT)
        lin = jnp.clip(lin, -SWIGLU_LIMIT, SWIGLU_LIMIT)
        act = glu * (1.0 / (1.0 + jnp.exp(-SWIGLU_ALPHA * glu))) * (lin + 1.0)
        y = jnp.dot(act.astype(BF16), wd_bf[...], preferred_element_type=F32) + bd_ref[...]
        y_ref[rs, :] = _pack_halves(y)

    @pl.when(n_real > MOE_CHAIN)
    def _():
        chain(0)
        chain(MOE_CHAIN)

    @pl.when((n_real > 0) & (n_real <= MOE_CHAIN))
    def _():
        chain(0)
        y_ref[MOE_CHAIN:, :] = jnp.zeros((MOE_TILE - MOE_CHAIN, HALF), jnp.int32)

    @pl.when(n_real == 0)
    def _():
        y_ref[...] = jnp.zeros_like(y_ref)


def _moe(plan, x_slots, w_gate_up, b_gate_up4, w_down, b_down4, l):
    grid_spec = pltpu.PrefetchScalarGridSpec(
        num_scalar_prefetch=6,
        grid=(N_MOE_BLOCKS,),
        in_specs=[pl.BlockSpec((MOE_TILE, HALF), lambda i, be, *_: (i, 0)),
                  pl.BlockSpec(memory_space=pl.ANY),
                  pl.BlockSpec((None, None, 1, 2 * D_FF), lambda i, be, *_: (l, be[i], 0, 0)),
                  pl.BlockSpec(memory_space=pl.ANY),
                  pl.BlockSpec((None, None, 1, D_MODEL), lambda i, be, *_: (l, be[i], 0, 0))],
        out_specs=pl.BlockSpec((MOE_TILE, HALF), lambda i, be, *_: (i, 0)),
        scratch_shapes=[pltpu.VMEM((2, D_MODEL, 2 * D_FF), F32), pltpu.VMEM((2, D_FF, D_MODEL), F32),
                        pltpu.VMEM((D_MODEL, 2 * D_FF), BF16), pltpu.VMEM((D_FF, D_MODEL), BF16),
                        pltpu.SemaphoreType.DMA((2, 2))])
    return pl.pallas_call(
        functools.partial(_moe_kernel, l),
        grid_spec=grid_spec,
        out_shape=jax.ShapeDtypeStruct((N_SLOTS, HALF), jnp.int32),
        compiler_params=_cparams(("arbitrary",)),
        name="moe",
    )(*plan, x_slots, w_gate_up, b_gate_up4, w_down, b_down4)


def _combine_kernel(y0_ref, y1_ref, y2_ref, y3_ref, gate_ref, h_ref, g2_ref, gpost_ref, oc_ref, ol_ref):
    i = pl.program_id(0)
    gate = gate_ref[...]
    ffn = None
    for k, y_ref in enumerate((y0_ref, y1_ref, y2_ref, y3_ref)):
        ya, yb = _unpack_halves(y_ref[...])
        term = gate[:, k:k + 1] * jnp.concatenate([ya, yb], axis=1)
        ffn = term if ffn is None else ffn + term
    out = h_ref[...] + g2_ref[...] * (_rms(ffn) * gpost_ref[...])

    @pl.when(i < CTX_TILES)
    def _():
        oc_ref[...] = out

    @pl.when(i >= CTX_TILES)
    def _():
        ol_ref[...] = out


def _combine(y_tok, gate, h2, mods3, g_post_ffn3, l):
    row = lambda n: pl.BlockSpec((TM, n), lambda i: (i, 0))
    choice = lambda k: pl.BlockSpec((TM, HALF), lambda i: (k * N_TILES + i, 0))
    return pl.pallas_call(
        _combine_kernel,
        grid=(N_TILES,),
        in_specs=[choice(0), choice(1), choice(2), choice(3), row(TOP_K), row(D_MODEL),
                  pl.BlockSpec((None, 1, D_MODEL), lambda i: (_mod_index(i), 0, 5)),
                  pl.BlockSpec((None, 1, D_MODEL), lambda i: (l, 0, 0))],
        out_specs=[pl.BlockSpec((TM, D_MODEL), lambda i: (jnp.minimum(i, CTX_TILES - 1), 0)),
                   pl.BlockSpec((TM, D_MODEL), lambda i: (jnp.maximum(i - CTX_TILES, 0), 0))],
        out_shape=[jax.ShapeDtypeStruct((N_CTX, D_MODEL), F32), jax.ShapeDtypeStruct((N_LAT, D_MODEL), F32)],
        compiler_params=_cparams(("arbitrary",)),
        name="combine",
    )(y_tok, y_tok, y_tok, y_tok, gate, h2, mods3, g_post_ffn3)


def _route_kernel(topk_ref, count_ref, before_ref, dest_ref, gate_ref, base_ref):
    i = pl.program_id(0)

    @pl.when(i == 0)
    def _():
        total = count_ref[...]
        padded = jnp.floor((total + (MOE_TILE - 1.0)) * (1.0 / MOE_TILE)) * MOE_TILE
        before = (lax.broadcasted_iota(jnp.int32, (N_EXPERTS, N_EXPERTS), 0)
                  < lax.broadcasted_iota(jnp.int32, (N_EXPERTS, N_EXPERTS), 1)).astype(F32)
        base_ref[...] = jnp.dot(padded, before, preferred_element_type=F32, precision=lax.Precision.HIGHEST)

    topk = topk_ref[...]
    lane = lax.broadcasted_iota(jnp.int32, (ROUTE_TM, N_EXPERTS), 1).astype(F32)
    hots = [lane == topk[:, k:k + 1] for k in range(TOP_K)]
    vals = [topk[:, TOP_K + k:TOP_K + k + 1] for k in range(TOP_K)]
    chosen = functools.reduce(jnp.add, [h.astype(F32) for h in hots])
    earlier = jnp.dot(before_ref[...], chosen.astype(BF16), preferred_element_type=F32)
    offs = base_ref[0:1, :] + earlier
    dest = [jnp.sum(jnp.where(h, offs, 0.0), axis=1, keepdims=True) for h in hots]
    dest_ref[...] = jnp.concatenate(dest, axis=1).astype(jnp.int32)
    e = [jnp.exp(v - vals[0]) for v in vals]
    den = functools.reduce(jnp.add, e)
    gate_ref[...] = jnp.concatenate(e, axis=1) / den
    base_ref[...] = base_ref[...] + jnp.sum(chosen, axis=0, keepdims=True)


ROUTE_TM = 1024


def _route(topk, counts):
    tile = lambda n: pl.BlockSpec((ROUTE_TM, n), lambda i: (i, 0))
    dest, gate = pl.pallas_call(
        _route_kernel,
        grid=(N_TOK // ROUTE_TM,),
        in_specs=[tile(2 * TOP_K), pl.BlockSpec((8, N_EXPERTS), lambda i: (0, 0)),
                  pl.BlockSpec((ROUTE_TM, ROUTE_TM), lambda i: (0, 0), pipeline_mode=pl.Buffered(1))],
        out_specs=[tile(TOP_K), tile(TOP_K)],
        out_shape=[jax.ShapeDtypeStruct((N_TOK, TOP_K), jnp.int32),
                   jax.ShapeDtypeStruct((N_TOK, TOP_K), F32)],
        scratch_shapes=[pltpu.VMEM((8, N_EXPERTS), F32)],
        compiler_params=_cparams(("arbitrary",)),
        name="route",
    )(topk, counts, jnp.asarray(np.tri(ROUTE_TM, ROUTE_TM, -1), dtype=BF16))
    counts = counts[0].astype(jnp.int32)
    expert = jnp.arange(N_EXPERTS, dtype=jnp.int32)
    padded = (counts + MOE_TILE - 1) // MOE_TILE * MOE_TILE
    pad_end = jnp.sum(jnp.where(expert[None, :] <= expert[:, None], padded[None, :], 0), axis=1)
    block = jnp.arange(N_MOE_BLOCKS, dtype=jnp.int32)
    block_e = jnp.minimum(jnp.sum((pad_end[None, :] <= block[:, None] * MOE_TILE).astype(jnp.int32), axis=1),
                          N_EXPERTS - 1)
    n_valid = pad_end[-1] // MOE_TILE
    mine = expert[None, :] == block_e[:, None]
    pick = lambda v: jnp.sum(jnp.where(mine, v[None, :], 0), axis=1)
    offset = block * MOE_TILE - pick(pad_end - padded)
    valid = block < n_valid
    first = valid & (offset == 0)
    used = counts > 0
    slot = jnp.sum((used[None, :] & (expert[None, :] < block_e[:, None])).astype(jnp.int32), axis=1) % 2
    nxt = jnp.min(jnp.where(used[None, :] & (expert[None, :] > block_e[:, None]), expert[None, :], N_EXPERTS),
                  axis=1)
    nxt = jnp.where(nxt < N_EXPERTS, nxt, -1)
    rows = jnp.where(valid, jnp.clip(pick(counts) - offset, 0, MOE_TILE), 0)
    plan = tuple(a.astype(jnp.int32) for a in (block_e, n_valid[None], first, slot, nxt, rows))
    n_workers = N_TOK // SCATTER_TOKENS_PER_WORKER
    dest_sc = dest.reshape(n_workers, SCATTER_TOKENS_PER_WORKER // GATHER_CHUNK, GATHER_CHUNK, TOP_K)
    dest_sc = dest_sc.transpose(0, 1, 3, 2).reshape(n_workers, -1, GATHER_CHUNK)
    return gate, dest.T, dest_sc, plan


def _rope_tables():
    f32 = np.float32
    t = np.arange(DEC_SEQ)
    inv = np.power(f32(ROPE_THETA), -np.arange(ROPE_PAIRS, dtype=f32) / f32(ROPE_PAIRS)).astype(f32)
    row = (t // GRID_W).astype(f32)[:, None] * inv
    col = (t % GRID_W).astype(f32)[:, None] * inv
    zeros = np.zeros_like(row)
    cos = np.concatenate([np.cos(row), np.cos(row), np.cos(col), np.cos(col)], axis=1)
    s1 = np.concatenate([-np.sin(row), zeros, -np.sin(col), zeros], axis=1)
    s2 = np.concatenate([zeros, np.sin(row), zeros, np.sin(col)], axis=1)
    ident = lambda v: np.full((PROJ_TM, HEAD_DIM), v, f32)
    tables = [np.concatenate([x, ident(v)], axis=0) for x, v in ((cos, 1.0), (s1, 0.0), (s2, 0.0))]
    return [jnp.asarray(np.tile(x, (1, 2)), dtype=F32) for x in tables]


def kernel(x_prompt, x_sample, cache_a, cache_b, cache_c, c, c_ctx, w_ada, b_ada, g_pre_mix, g_post_mix,
           g_pre_ffn, g_post_ffn, w_in, g_q_b, g_k_b, sink_a, rpb_c, w_br_a, w_br_b, w_br_c, w_out,
           w_router, b_router, w_gate_up, b_gate_up, w_down, b_down):
    h = (x_prompt.reshape(N_CTX, D_MODEL), x_sample.reshape(N_LAT, D_MODEL))
    cond8 = jnp.concatenate([c_ctx[None], c, jnp.zeros((3, D_MODEL), F32)], axis=0)
    cache_a = cache_a.astype(BF16).reshape(DEC_BATCH, DEPTH, 2, PAST_LEN, W_KV)
    cache_b = cache_b.astype(BF16).reshape(DEC_BATCH, DEPTH, 2, PAST_LEN, W_KV)
    cache_c = cache_c.astype(BF16).reshape(DEC_BATCH, DEPTH, 2, PAST_LEN, W_HEADS)
    cos_t, s1_t, s2_t = _rope_tables()
    bd = jnp.asarray(np.kron(np.eye(256 // HEAD_DIM), np.full((HEAD_DIM, HEAD_DIM), 1.0 / HEAD_DIM)), dtype=BF16)
    vec3 = lambda a: a.reshape(DEPTH, 1, a.shape[-1])
    scale = HEAD_DIM ** -0.5
    states = []
    mods_all = _adaln(cond8, w_ada, vec3(b_ada))

    def prep_layer(stacked, l):
        w_in_, g_q_b_, g_k_b_, w_br_a_, w_br_b_, w_br_c_, w_out_, w_router_, rpb_c_ = stacked
        w = w_in_[l]
        pair_cols = lambda a: a.reshape(D_MODEL, 2, 4, HEAD_DIM).transpose(0, 2, 1, 3).reshape(D_MODEL, 512)
        pair_rows = lambda a: a.reshape(2, 4, HEAD_DIM, D_MODEL).transpose(1, 0, 2, 3).reshape(512, D_MODEL)
        w_in_p = jnp.concatenate(
            [pair_cols(w[:, 0:512]) * scale, pair_cols(w[:, 768:1280]), w[:, 1536:2048] * scale,
             w[:, 2048:2560], w[:, 2560:3072],
             w[:, 512:640], w[:, 1280:1408], w[:, 640:768], w[:, 1408:1536], w[:, 3072:]], axis=1).astype(BF16)
        gq = (jnp.tile(g_q_b_[l], H_B) * scale)[None]
        gk = jnp.tile(g_k_b_[l], KV_B)[None]
        w_br = jnp.stack([pair_rows(w_br_a_[l]), pair_rows(w_br_b_[l]), w_br_c_[l]], axis=0).astype(BF16)
        w_out_b = w_out_[l].astype(BF16)
        w_r_hi = w_router_[l].astype(BF16)
        w_r_lo = (w_router_[l] - w_r_hi.astype(F32)).astype(BF16)
        w_router2 = jnp.concatenate([w_r_hi, w_r_lo], axis=1)
        return w_in_p, gq, gk, w_br, w_out_b, w_router2, _nbr_bias_table(rpb_c_[l])

    stacked = (w_in, g_q_b, g_k_b, w_br_a, w_br_b, w_br_c, w_out, w_router, rpb_c)
    prepped = prep_layer(stacked, 0)
    for l in range(DEPTH):
        w_in_p, gq, gk, w_br, w_out_b, w_router2, bias_t = prepped
        mods3 = mods_all[l].reshape(8, 1, 6 * D_MODEL)
        q_all, kvab, kvc, gates, *states = _proj(h, mods3, vec3(g_pre_mix), w_in_p, gq, gk, bd,
                                                 cos_t, s1_t, s2_t, tuple(states), l)
        if l == DEPTH - 1:
            states = [s.reshape(BATCH, DEPTH, 2, SEQ, n, HEAD_DIM) for s, n in zip(states, (KV_A, KV_B, H_C))]
            states, gates = lax.optimization_barrier((states, gates))
        o_ctx = _ctx_attn(sink_a, q_all, kvab, kvc, l)
        o_a = _win_attn(sink_a, q_all, kvab, cache_a, l)
        o_b = _dense_attn(q_all, kvab, cache_b, l)
        o_c = _nbr_attn(q_all, kvc, cache_c, bias_t, l)
        h2, hn2, topk, counts = _merge(o_ctx, o_a, o_b, o_c, gates, h, mods3, w_br, w_out_b,
                                       vec3(g_post_mix), vec3(g_pre_ffn), w_router2, vec3(b_router), l)
        gate, dest, dest_sc, plan = _route(topk, counts)
        if l + 1 < DEPTH:
            stacked, dest_sc = lax.optimization_barrier((stacked, dest_sc))
            prepped = prep_layer(stacked, l + 1)
        x_slots = _scatter_rows(hn2, dest_sc, N_SLOTS)
        y_slots = _moe(plan, x_slots, w_gate_up,
                       b_gate_up.reshape(DEPTH, N_EXPERTS, 1, 2 * D_FF), w_down,
                       b_down.reshape(DEPTH, N_EXPERTS, 1, D_MODEL), l)
        h = _combine(_gather_rows(y_slots, dest.reshape(-1)), gate, h2, mods3, vec3(g_post_ffn), l)

    return (h[0].reshape(BATCH, SEQ, D_MODEL), h[1].reshape(DEC_BATCH, DEC_SEQ, D_MODEL), *states)
```

```python
import functools

import jax
import jax.numpy as jnp
import numpy as np
from jax import lax
from jax.experimental import pallas as pl
from jax.experimental.pallas import tpu as pltpu
from jax.experimental.pallas import tpu_sc as plsc

D_MODEL = 1024
BATCH = 32
SEQ = 256
DEPTH = 2
DEC_BATCH = 4
DEC_SEQ = 2048
PAST_LEN = 512
GRID_W = 64
HEAD_DIM = 64
H_A = 8
KV_A = 2
H_B = 8
KV_B = 2
H_C = 8
WINDOW_A = 128
NA_ROWS = 8
NA_COLS = 16
ROPE_THETA = 10000.0
ROPE_PAIRS = HEAD_DIM // 4
N_EXPERTS = 32
TOP_K = 4
D_FF = D_MODEL
SWIGLU_ALPHA = 1.702
SWIGLU_LIMIT = 7.0
EPS = 1e-6

W_HEADS = H_A * HEAD_DIM
W_KV = KV_A * HEAD_DIM
N_CTX = BATCH * SEQ
N_LAT = DEC_BATCH * DEC_SEQ
N_TOK = N_CTX + N_LAT
GRID_ROWS = DEC_SEQ // GRID_W
D_IN = 3 * W_HEADS + 4 * W_KV + 2 * W_HEADS + 3 * D_MODEL

TM = 256
N_TILES = N_TOK // TM
CTX_TILES = N_CTX // TM
LAT_TILES_PER_BATCH = DEC_SEQ // TM
TQ = 128
MOE_CHAIN = 256
MOE_TILE = 2 * MOE_CHAIN
N_SLOTS = N_TOK * TOP_K + N_EXPERTS * MOE_TILE
N_MOE_BLOCKS = N_SLOTS // MOE_TILE
NEG = -1e30
VMEM_LIMIT = 56 * 1024 * 1024

BF16 = jnp.bfloat16
F32 = jnp.float32


def _cparams(sem):
    return pltpu.CompilerParams(dimension_semantics=sem, vmem_limit_bytes=VMEM_LIMIT)


def _mod_index(i):
    return jnp.where(i < CTX_TILES, 0, 1 + (i - CTX_TILES) // LAT_TILES_PER_BATCH)


def _rms(x):
    return x * lax.rsqrt(jnp.mean(x * x, axis=-1, keepdims=True) + EPS)


HALF = D_MODEL // 2


def _pack_halves(x):
    hi = lax.bitcast_convert_type(x[:, :HALF].astype(BF16).astype(F32), jnp.uint32)
    lo = lax.bitcast_convert_type(x[:, HALF:].astype(BF16).astype(F32), jnp.uint32)
    return lax.bitcast_convert_type(hi | (lo >> 16), jnp.int32)


def _unpack_halves(w):
    u = lax.bitcast_convert_type(w, jnp.uint32)
    return (lax.bitcast_convert_type(u & jnp.uint32(0xFFFF0000), F32),
            lax.bitcast_convert_type(u << 16, F32))


GATHER_CHUNK = 64


def _gather_rows(table, idx):
    n = idx.shape[0]
    width = table.shape[1]
    info = plsc.get_sparse_core_info()
    n_workers = info.num_cores * info.num_subcores
    per_worker = n // n_workers
    n_chunks = per_worker // GATHER_CHUNK
    assert per_worker * n_workers == n and n_chunks * GATHER_CHUNK == per_worker and n_chunks % 2 == 0
    mesh = plsc.VectorSubcoreMesh(core_axis_name="core", subcore_axis_name="subcore")

    @functools.partial(
        pl.kernel, out_type=jax.ShapeDtypeStruct((n, width), table.dtype), mesh=mesh,
        scratch_types=[pltpu.VMEM((per_worker,), jnp.int32),
                       pltpu.VMEM((2, GATHER_CHUNK, width), table.dtype),
                       pltpu.SemaphoreType.DMA((2,)), pltpu.SemaphoreType.DMA((2,))])
    def gather(table_hbm, idx_hbm, out_hbm, idx_v, rows_v, gather_sem, write_sem):
        worker = lax.axis_index("subcore") * info.num_cores + lax.axis_index("core")
        base = worker * per_worker
        pltpu.sync_copy(idx_hbm.at[pl.ds(base, per_worker)], idx_v)

        def fetch(chunk, slot):
            rows = idx_v.at[pl.ds(chunk * GATHER_CHUNK, GATHER_CHUNK)]
            return pltpu.make_async_copy(table_hbm.at[rows], rows_v.at[slot], gather_sem.at[slot])

        def write(chunk, slot):
            dst = out_hbm.at[pl.ds(base + chunk * GATHER_CHUNK, GATHER_CHUNK)]
            return pltpu.make_async_copy(rows_v.at[slot], dst, write_sem.at[slot])

        fetch(0, 0).start()

        @pl.loop(0, n_chunks, step=2)
        def _(c):
            @pl.when(c > 0)
            def _():
                write(c - 1, 1).wait()

            fetch(c + 1, 1).start()
            fetch(c, 0).wait()
            write(c, 0).start()
            write(c, 0).wait()

            @pl.when(c + 2 < n_chunks)
            def _():
                fetch(c + 2, 0).start()

            fetch(c + 1, 1).wait()
            write(c + 1, 1).start()

        write(n_chunks - 1, 1).wait()

    return gather(table, idx)


SC_WORKERS_V7X = 32
SCATTER_TOKENS_PER_WORKER = N_TOK // SC_WORKERS_V7X


def _scatter_rows(table, dest_sc, n_out):
    width = table.shape[1]
    info = plsc.get_sparse_core_info()
    assert info.num_cores * info.num_subcores == SC_WORKERS_V7X
    per_worker = SCATTER_TOKENS_PER_WORKER
    n_chunks = per_worker // GATHER_CHUNK
    assert n_chunks % 2 == 0 and dest_sc.shape == (SC_WORKERS_V7X, n_chunks * TOP_K, GATHER_CHUNK)
    mesh = plsc.VectorSubcoreMesh(core_axis_name="core", subcore_axis_name="subcore")

    @functools.partial(
        pl.kernel, out_type=jax.ShapeDtypeStruct((n_out, width), table.dtype), mesh=mesh,
        scratch_types=[pltpu.VMEM((n_chunks * TOP_K, GATHER_CHUNK), jnp.int32),
                       pltpu.VMEM((2, GATHER_CHUNK, width), table.dtype),
                       pltpu.SemaphoreType.DMA((2,)), pltpu.SemaphoreType.DMA((2,))])
    def scatter(table_hbm, dest_hbm, out_hbm, idx_v, rows_v, read_sem, write_sem):
        worker = lax.axis_index("subcore") * info.num_cores + lax.axis_index("core")
        base = worker * per_worker
        pltpu.sync_copy(dest_hbm.at[worker], idx_v)

        def read(chunk, slot):
            src = table_hbm.at[pl.ds(base + chunk * GATHER_CHUNK, GATHER_CHUNK)]
            return pltpu.make_async_copy(src, rows_v.at[slot], read_sem.at[slot])

        def writes(chunk, slot):
            return [pltpu.make_async_copy(rows_v.at[slot], out_hbm.at[idx_v.at[chunk * TOP_K + k]],
                                          write_sem.at[slot]) for k in range(TOP_K)]

        read(0, 0).start()

        @pl.loop(0, n_chunks, step=2)
        def _(c):
            @pl.when(c > 0)
            def _():
                for cp in writes(c - 1, 1):
                    cp.wait()

            read(c + 1, 1).start()
            read(c, 0).wait()
            for cp in writes(c, 0):
                cp.start()
            for cp in writes(c, 0):
                cp.wait()

            @pl.when(c + 2 < n_chunks)
            def _():
                read(c + 2, 0).start()

            read(c + 1, 1).wait()
            for cp in writes(c + 1, 1):
                cp.start()

        for cp in writes(n_chunks - 1, 1):
            cp.wait()

    return scatter(table, dest_sc)


def _adaln_kernel(c_ref, w_ref, b_ref, o_ref):
    c = c_ref[...]
    s = c / (1.0 + jnp.exp(-c))
    o_ref[...] = jnp.dot(s, w_ref[...], preferred_element_type=F32,
                         precision=lax.Precision.HIGHEST) + b_ref[...]


def _adaln(cond8, w_ada, b_ada3):
    tn = 768
    return pl.pallas_call(
        _adaln_kernel,
        grid=(DEPTH, 6 * D_MODEL // tn),
        in_specs=[pl.BlockSpec((8, D_MODEL), lambda l, j: (0, 0)),
                  pl.BlockSpec((None, D_MODEL, tn), lambda l, j: (l, 0, j)),
                  pl.BlockSpec((None, 1, tn), lambda l, j: (l, 0, j))],
        out_specs=pl.BlockSpec((None, 8, tn), lambda l, j: (l, 0, j)),
        out_shape=jax.ShapeDtypeStruct((DEPTH, 8, 6 * D_MODEL), F32),
        compiler_params=_cparams(("arbitrary", "arbitrary")),
        name="adaln",
    )(cond8, w_ada, b_ada3)


C_QA, C_QB, C_QC, C_KC, C_VC, C_KAB, C_GL = 0, 512, 1024, 1536, 2048, 2560, 3072

PROJ_TM = 2 * TM
PROJ_TILES = N_TOK // PROJ_TM
PROJ_CTX_TILES = N_CTX // PROJ_TM
PROJ_LAT_TILES_PER_BATCH = DEC_SEQ // PROJ_TM


def _proj_kernel(hc_ref, hl_ref, sh_ref, sc_ref, gpre_ref, w_ref, gq_ref, gk_ref, bd_ref,
                 cos_ref, s1_ref, s2_ref, *rest):
    q_ref, kvab_ref, kvc_ref, gates_ref, sta_ref, stb_ref, stc_ref = rest[-7:]
    i = pl.program_id(0)
    bd = bd_ref[...]

    def rope(t, rs):
        cos, s1, s2 = cos_ref[rs, :], s1_ref[rs, :], s2_ref[rs, :]
        parts = []
        for g in range(t.shape[1] // 128):
            tg = t[:, g * 128:(g + 1) * 128]
            parts.append(tg * cos + pltpu.roll(tg, 112, 1) * s1 + pltpu.roll(tg, 16, 1) * s2)
        return parts[0] if len(parts) == 1 else jnp.concatenate(parts, axis=1)

    def headnorm(t, g):
        sq = (t * t).astype(BF16)
        n = t.shape[1]
        if n == 128:
            ms = jnp.dot(sq, bd[:128, :128], preferred_element_type=F32)
        else:
            ms = jnp.concatenate(
                [jnp.dot(sq[:, c:c + 256], bd, preferred_element_type=F32) for c in range(0, n, 256)],
                axis=1)
        return t * lax.rsqrt(ms + EPS) * g

    kv_f32 = []
    for c in range(PROJ_TM // TM):
        rs = slice(c * TM, (c + 1) * TM)
        hn = _rms(jnp.where(i < PROJ_CTX_TILES, hc_ref[rs, :], hl_ref[rs, :])) * gpre_ref[...]
        hb = (hn * (1.0 + sc_ref[...]) + sh_ref[...]).astype(BF16)
        proj = lambda c0, n, hb=hb: jnp.dot(hb, w_ref[:, c0:c0 + n], preferred_element_type=F32)
        q_ref[rs, 0:512] = rope(proj(C_QA, 512), rs).astype(BF16)
        q_ref[rs, 512:1024] = rope(headnorm(proj(C_QB, 512), gq_ref[...]), rs).astype(BF16)
        q_ref[rs, 1024:1536] = proj(C_QC, 512).astype(BF16)
        kc = proj(C_KC, 512)
        vc = proj(C_VC, 512)
        kvc_ref[rs, 0:512] = kc.astype(BF16)
        kvc_ref[rs, 512:1024] = vc.astype(BF16)
        kab = proj(C_KAB, 512)
        ka = kab[:, 0:128]
        kb = headnorm(kab[:, 128:256], gk_ref[...])
        kvab_ref[rs, 0:128] = rope(ka, rs).astype(BF16)
        kvab_ref[rs, 128:256] = rope(kb, rs).astype(BF16)
        kvab_ref[rs, 256:512] = kab[:, 256:512].astype(BF16)
        for j in range(6):
            gl = proj(C_GL + j * 512, 512)
            gates_ref[rs, j * 512:(j + 1) * 512] = (1.0 / (1.0 + jnp.exp(-gl))).astype(BF16)
        kv_f32.append(((ka, kab[:, 256:384]), (kb, kab[:, 384:512]), (kc, vc)))

    @pl.when(i < PROJ_CTX_TILES)
    def _():
        for c, per_mixer in enumerate(kv_f32):
            for st_ref, (k, v) in zip((sta_ref, stb_ref, stc_ref), per_mixer):
                if len(st_ref.shape) == 5:
                    st_ref[c, 0, 0] = k
                    st_ref[c, 0, 1] = v
                    st_ref[c, 1:] = jnp.zeros((DEPTH - 1,) + tuple(st_ref.shape[2:]), F32)
                else:
                    st_ref[c, 0] = k
                    st_ref[c, 1] = v


def _proj(h, mods3, g_pre3, w_in_p, gq, gk, bd, cos_t, s1_t, s2_t, prev_states, l):
    lat_tile = lambda i: i - PROJ_CTX_TILES
    rope_idx = lambda i: jnp.where(i < PROJ_CTX_TILES, PROJ_LAT_TILES_PER_BATCH,
                                   lat_tile(i) % PROJ_LAT_TILES_PER_BATCH)
    mod_idx = lambda i: jnp.where(i < PROJ_CTX_TILES, 0, 1 + lat_tile(i) // PROJ_LAT_TILES_PER_BATCH)
    ctx_block = lambda i: jnp.minimum(i, PROJ_CTX_TILES - 1)
    const = lambda shape: pl.BlockSpec(shape, lambda i: (0,) * len(shape), pipeline_mode=pl.Buffered(1))
    rope_spec = pl.BlockSpec((PROJ_TM, 128), lambda i: (rope_idx(i), 0))
    row = lambda n: pl.BlockSpec((PROJ_TM, n), lambda i: (i, 0))
    per_step = PROJ_TM // SEQ
    if l == 0:
        state_spec = lambda n: pl.BlockSpec((per_step, DEPTH, 2, SEQ, n), lambda i: (ctx_block(i), 0, 0, 0, 0))
    else:
        state_spec = lambda n: pl.BlockSpec((per_step, None, 2, SEQ, n), lambda i: (ctx_block(i), l, 0, 0, 0))
    state_shape = lambda n: jax.ShapeDtypeStruct((BATCH, DEPTH, 2, SEQ, n), F32)
    n_in = 12
    return pl.pallas_call(
        _proj_kernel,
        grid=(PROJ_TILES,),
        in_specs=[pl.BlockSpec((PROJ_TM, D_MODEL), lambda i: (ctx_block(i), 0)),
                  pl.BlockSpec((PROJ_TM, D_MODEL), lambda i: (jnp.maximum(lat_tile(i), 0), 0)),
                  pl.BlockSpec((None, 1, D_MODEL), lambda i: (mod_idx(i), 0, 0)),
                  pl.BlockSpec((None, 1, D_MODEL), lambda i: (mod_idx(i), 0, 1)),
                  pl.BlockSpec((None, 1, D_MODEL), lambda i: (l, 0, 0)),
                  const((D_MODEL, D_IN)), const((1, 512)), const((1, 128)), const((256, 256)),
                  rope_spec, rope_spec, rope_spec] + [pl.BlockSpec(memory_space=pl.ANY)] * len(prev_states),
        out_specs=[row(1536), row(512), row(1024), row(3072),
                   state_spec(W_KV), state_spec(W_KV), state_spec(W_HEADS)],
        out_shape=[jax.ShapeDtypeStruct((N_TOK, 1536), BF16),
                   jax.ShapeDtypeStruct((N_TOK, 512), BF16),
                   jax.ShapeDtypeStruct((N_TOK, 1024), BF16),
                   jax.ShapeDtypeStruct((N_TOK, 3072), BF16),
                   state_shape(W_KV), state_shape(W_KV), state_shape(W_HEADS)],
        input_output_aliases={n_in + j: 4 + j for j in range(len(prev_states))},
        compiler_params=_cparams(("arbitrary",)),
        name="proj",
    )(*h, mods3, mods3, g_pre3, w_in_p, gq, gk, bd, cos_t, s1_t, s2_t, *prev_states)


def _qk(q, k):
    return lax.dot_general(q, k, (((1,), (1,)), ((), ())), preferred_element_type=F32)


def _softmax_pv(scores, values, sink=None):
    m = functools.reduce(jnp.maximum, [jnp.max(s, axis=-1, keepdims=True) for s in scores])
    if sink is not None:
        m = jnp.maximum(m, sink)
    ps = [jnp.exp(s - m) for s in scores]
    den = functools.reduce(jnp.add, [jnp.sum(p, axis=-1, keepdims=True) for p in ps])
    if sink is not None:
        den = den + jnp.exp(sink - m)
    o = functools.reduce(jnp.add, [jnp.dot(p.astype(BF16), v, preferred_element_type=F32)
                                   for p, v in zip(ps, values)])
    return o / den


def _lo_lanes(rows):
    return lax.broadcasted_iota(jnp.int32, (rows, 128), 1) < HEAD_DIM


def _stack_pairs(q, n_pairs):
    lo = _lo_lanes(q.shape[0])
    zero = jnp.zeros((q.shape[0], 128), q.dtype)
    pairs = [q[:, p * 128:(p + 1) * 128] for p in range(n_pairs)]
    return jnp.concatenate([jnp.where(lo, x, zero) for x in pairs] + [jnp.where(lo, zero, x) for x in pairs],
                           axis=0)


def _unstack_pairs(o, n_pairs):
    rows = o.shape[0] // (2 * n_pairs)
    lo = _lo_lanes(rows)
    return jnp.concatenate(
        [jnp.where(lo, o[p * rows:(p + 1) * rows], o[(n_pairs + p) * rows:(n_pairs + p + 1) * rows])
         for p in range(n_pairs)], axis=1)


def _pair_sink(sink_ref, l, p, rows):
    return jnp.concatenate([jnp.full((rows, 1), sink_ref[l, h], F32) for h in (p, H_A // 2 + p)], axis=0)


def _ctx_attn_kernel(l, sink_ref, q_ref, kvab_ref, kvc_ref, o_ref):
    for p in range(H_A // 2):
        cs = slice(p * 128, (p + 1) * 128)
        o = _softmax_pv([_qk(_stack_pairs(q_ref[:, cs], 1), kvab_ref[:, 0:128])], [kvab_ref[:, 256:384]],
                        _pair_sink(sink_ref, l, p, SEQ))
        o_ref[:, cs] = _unstack_pairs(o, 1).astype(BF16)
    for p in range(H_B // 2):
        cs = slice(512 + p * 128, 512 + (p + 1) * 128)
        o = _softmax_pv([_qk(_stack_pairs(q_ref[:, cs], 1), kvab_ref[:, 128:256])], [kvab_ref[:, 384:512]])
        o_ref[:, cs] = _unstack_pairs(o, 1).astype(BF16)
    for hp in range(H_C // 2):
        cs = slice(hp * 128, (hp + 1) * 128)
        qc = _stack_pairs(q_ref[:, 1024 + hp * 128:1024 + (hp + 1) * 128], 1)
        o = _softmax_pv([_qk(qc, kvc_ref[:, cs])], [kvc_ref[:, 512 + hp * 128:512 + (hp + 1) * 128]])
        o_ref[:, 1024 + hp * 128:1024 + (hp + 1) * 128] = _unstack_pairs(o, 1).astype(BF16)


def _ctx_attn(sink_a, q_all, kvab, kvc, l):
    row = lambda n: pl.BlockSpec((SEQ, n), lambda b: (b, 0))
    return pl.pallas_call(
        functools.partial(_ctx_attn_kernel, l),
        grid=(BATCH,),
        in_specs=[pl.BlockSpec(memory_space=pltpu.SMEM), row(1536), row(512), row(1024)],
        out_specs=row(1536),
        out_shape=jax.ShapeDtypeStruct((N_CTX, 1536), BF16),
        compiler_params=_cparams(("arbitrary",)),
        name="ctx_attn",
    )(sink_a, q_all, kvab, kvc)


def _win_attn_kernel(l, sink_ref, q_ref, prev_ref, cur_ref, nxt_ref, ck_ref, cv_ref, o_ref):
    n = pl.program_id(1)
    nb = DEC_SEQ // TQ
    rows = 2 * TQ
    qpos = lax.broadcasted_iota(jnp.int32, (rows, TQ), 0) % TQ
    kpos = lax.broadcasted_iota(jnp.int32, (rows, TQ), 1)
    mask_prev = (kpos >= qpos) & (n > 0)
    mask_next = (kpos <= qpos) & (n < nb - 1)
    ks, vs = slice(0, 128), slice(256, 384)
    for p in range(H_A // 2):
        cs = slice(p * 128, (p + 1) * 128)
        qs = _stack_pairs(q_ref[:, cs], 1)
        s_prev = jnp.where(mask_prev, _qk(qs, prev_ref[:, ks]), NEG)
        s_cur = _qk(qs, cur_ref[:, ks])
        s_next = jnp.where(mask_next, _qk(qs, nxt_ref[:, ks]), NEG)
        s_ctx = _qk(qs, ck_ref[...])
        o = _softmax_pv([s_prev, s_cur, s_next, s_ctx],
                        [prev_ref[:, vs], cur_ref[:, vs], nxt_ref[:, vs], cv_ref[...]],
                        _pair_sink(sink_ref, l, p, TQ))
        o_ref[:, cs] = _unstack_pairs(o, 1).astype(BF16)


def _win_attn(sink_a, q_all, kvab, cache_a, l):
    nb = DEC_SEQ // TQ
    base = N_CTX // TQ
    kv_spec = lambda f: pl.BlockSpec((TQ, 512), lambda b, n: (base + b * nb + f(n), 0))
    cache_spec = lambda s: pl.BlockSpec((None, None, None, PAST_LEN, W_KV), lambda b, n: (b, l, s, 0, 0))
    return pl.pallas_call(
        functools.partial(_win_attn_kernel, l),
        grid=(DEC_BATCH, nb),
        in_specs=[pl.BlockSpec(memory_space=pltpu.SMEM),
                  pl.BlockSpec((TQ, 512), lambda b, n: (base + b * nb + n, 0)),
                  kv_spec(lambda n: jnp.maximum(n - 1, 0)), kv_spec(lambda n: n),
                  kv_spec(lambda n: jnp.minimum(n + 1, nb - 1)),
                  cache_spec(0), cache_spec(1)],
        out_specs=pl.BlockSpec((TQ, 512), lambda b, n: (b * nb + n, 0)),
        out_shape=jax.ShapeDtypeStruct((N_LAT, 512), BF16),
        compiler_params=_cparams(("arbitrary", "arbitrary")),
        name="win_attn",
    )(sink_a, q_all, kvab, kvab, kvab, cache_a, cache_a)


DENSE_KEY_CHUNK = 1024
DENSE_TQ = 256


def _online_softmax_pv(q, key_chunks, value_chunks, keep_lo):
    one = jnp.ones((), BF16)
    m = acc = None
    for k, v in zip(key_chunks, value_chunks):
        s = _qk(q, k)
        m_new = jnp.max(s, axis=-1, keepdims=True)
        if m is not None:
            m_new = jnp.maximum(m, m_new)
        p = jnp.exp(s - m_new).astype(BF16)
        lo = _lo_lanes(v.shape[0])
        pv = jnp.dot(p, jnp.where(lo, v, one) if keep_lo else jnp.where(lo, one, v), preferred_element_type=F32)
        acc = pv if acc is None else acc * jnp.exp(m - m_new) + pv
        m = m_new
    return acc * pltpu.roll(1.0 / acc, HEAD_DIM, 1)


def _dense_attn_kernel(q_ref, kv_ref, ck_ref, cv_ref, o_ref):
    qs = _stack_pairs(q_ref[...], 4)
    half = qs.shape[0] // 2
    starts = range(0, DEC_SEQ, DENSE_KEY_CHUNK)
    keys = [kv_ref[c:c + DENSE_KEY_CHUNK, 128:256] for c in starts] + [ck_ref[...]]
    values = [kv_ref[c:c + DENSE_KEY_CHUNK, 384:512] for c in starts] + [cv_ref[...]]
    outs = [_online_softmax_pv(qs[g * half:(g + 1) * half], keys, values, g == 0) for g in range(KV_B)]
    o_ref[...] = _unstack_pairs(jnp.concatenate(outs, axis=0), 4).astype(BF16)


def _dense_attn(q_all, kvab, cache_b, l):
    nb = DEC_SEQ // DENSE_TQ
    base = N_CTX // DENSE_TQ
    cache_spec = lambda s: pl.BlockSpec((None, None, None, PAST_LEN, W_KV), lambda b, n: (b, l, s, 0, 0))
    return pl.pallas_call(
        _dense_attn_kernel,
        grid=(DEC_BATCH, nb),
        in_specs=[pl.BlockSpec((DENSE_TQ, 512), lambda b, n: (base + b * nb + n, 1)),
                  pl.BlockSpec((DEC_SEQ, 512), lambda b, n: (N_CTX // DEC_SEQ + b, 0)),
                  cache_spec(0), cache_spec(1)],
        out_specs=pl.BlockSpec((DENSE_TQ, 512), lambda b, n: (b * nb + n, 0)),
        out_shape=jax.ShapeDtypeStruct((N_LAT, 512), BF16),
        compiler_params=_cparams(("arbitrary", "arbitrary")),
        name="dense_attn",
    )(q_all, kvab, cache_b, cache_b)


NBR_BAND = 4
NBR_Q = NBR_BAND * GRID_W
NBR_WIN_ROWS = 12
NBR_N_BANDS = GRID_ROWS // NBR_BAND
NBR_KBLK = NBR_Q
NBR_WIN_BLOCKS = NBR_WIN_ROWS * GRID_W // NBR_KBLK
NBR_LAST_KB = (GRID_ROWS - NBR_WIN_ROWS) * GRID_W // NBR_KBLK


def _nbr_window_block(band):
    return jnp.clip(band - 1, 0, NBR_LAST_KB)


def _nbr_attn_kernel(q_ref, k0_ref, k1_ref, k2_ref, ck_ref, cv_ref, bias_ref, o_ref):
    k_refs = (k0_ref, k1_ref, k2_ref)
    for hp in range(H_C // 2):
        cs = slice(hp * 128, (hp + 1) * 128)
        vs = slice(512 + hp * 128, 512 + (hp + 1) * 128)
        qs = _stack_pairs(q_ref[:, cs], 1)
        bias = jnp.concatenate([bias_ref[2 * hp], bias_ref[2 * hp + 1]], axis=0)
        scores = [_qk(qs, kr[:, cs]) + bias[:, j * NBR_KBLK:(j + 1) * NBR_KBLK] for j, kr in enumerate(k_refs)]
        scores.append(_qk(qs, ck_ref[:, cs]))
        o = _softmax_pv(scores, [kr[:, vs] for kr in k_refs] + [cv_ref[:, cs]])
        o_ref[:, cs] = _unstack_pairs(o, 1).astype(BF16)


def _nbr_attn(q_all, kvc, cache_c, bias_t, l):
    q_base = N_CTX // NBR_Q
    k_base = N_CTX // NBR_KBLK
    blocks_per_batch = DEC_SEQ // NBR_KBLK
    cache_spec = lambda s: pl.BlockSpec((None, None, None, PAST_LEN, W_HEADS), lambda band, b: (b, l, s, 0, 0))
    key_spec = lambda j: pl.BlockSpec(
        (NBR_KBLK, 1024), lambda band, b: (k_base + b * blocks_per_batch + _nbr_window_block(band) + j, 0))
    band_type = lambda band: jnp.where(band == 0, 0, jnp.where(band == NBR_N_BANDS - 1, 2, 1))
    return pl.pallas_call(
        _nbr_attn_kernel,
        grid=(NBR_N_BANDS, DEC_BATCH),
        in_specs=[pl.BlockSpec((NBR_Q, 512), lambda band, b: (q_base + b * NBR_N_BANDS + band, 2)),
                  key_spec(0), key_spec(1), key_spec(2), cache_spec(0), cache_spec(1),
                  pl.BlockSpec((None, H_C, NBR_Q, NBR_WIN_ROWS * GRID_W),
                               lambda band, b: (band_type(band), 0, 0, 0))],
        out_specs=pl.BlockSpec((NBR_Q, 512), lambda band, b: (b * NBR_N_BANDS + band, 0)),
        out_shape=jax.ShapeDtypeStruct((N_LAT, 512), BF16),
        compiler_params=_cparams(("arbitrary", "arbitrary")),
        name="nbr_attn",
    )(q_all, kvc, kvc, kvc, cache_c, cache_c, bias_t)


def _nbr_bias_table(rpb_l):
    c = np.arange(GRID_W)[:, None]
    kc = np.arange(GRID_W)[None, :]
    c_start = np.clip(c - NA_COLS // 2, 0, GRID_W - NA_COLS)
    valid = (kc >= c_start) & (kc < c_start + NA_COLS)
    pad = GRID_W - NA_COLS
    rpb_pad = jnp.pad(rpb_l.astype(F32), ((0, 0), (0, 0), (pad, pad)))
    toeplitz = jnp.stack([rpb_pad[:, :, GRID_W - 1 - q:2 * GRID_W - 1 - q] for q in range(GRID_W)], axis=2)
    t = jnp.where(valid[None, None], toeplitz, NEG)
    neg = jnp.full((H_C, GRID_W, GRID_W), NEG, F32)
    tables = []
    for r0 in (0, NBR_BAND, GRID_ROWS - NBR_BAND):
        k0 = min(max(r0 - NA_ROWS // 2, 0), GRID_ROWS - NBR_WIN_ROWS)
        rows = []
        for dq in range(NBR_BAND):
            r = r0 + dq
            start = min(max(r - NA_ROWS // 2, 0), GRID_ROWS - NA_ROWS)
            cols = []
            for i in range(NBR_WIN_ROWS):
                kr = k0 + i
                cols.append(t[:, kr - r + NA_ROWS - 1] if start <= kr < start + NA_ROWS else neg)
            rows.append(jnp.concatenate(cols, axis=2))
        tables.append(jnp.concatenate(rows, axis=1))
    return jnp.stack(tables, axis=0)


MERGE_TM = 2 * TM
MERGE_TILES = N_TOK // MERGE_TM
MERGE_CTX_TILES = N_CTX // MERGE_TM
MERGE_LAT_TILES_PER_BATCH = DEC_SEQ // MERGE_TM


def _merge_kernel(octx_ref, oa_ref, ob_ref, oc_ref, gates_ref, hc_ref, hl_ref, g1_ref, sh2_ref, sc2_ref,
                  wbr_ref, wout_ref, gpost_ref, gpre_ref, wr_ref, br_ref,
                  h2_ref, hn2_ref, topk_ref, count_ref):
    i = pl.program_id(0)
    is_ctx = i < MERGE_CTX_TILES
    chosen = None
    for r0 in range(0, MERGE_TM, TM):
        rs = slice(r0, r0 + TM)
        merged = None
        for j, lat_ref in enumerate((oa_ref, ob_ref, oc_ref)):
            o = jnp.where(is_ctx, octx_ref[rs, j * 512:(j + 1) * 512], lat_ref[rs, :])
            br = jnp.dot(o, wbr_ref[j], preferred_element_type=F32)
            term = gates_ref[rs, j * D_MODEL:(j + 1) * D_MODEL].astype(F32) * br
            merged = term if merged is None else merged + term
        t = jnp.dot(merged.astype(BF16), wout_ref[...], preferred_element_type=F32)
        h2 = jnp.where(is_ctx, hc_ref[rs, :], hl_ref[rs, :]) + g1_ref[...] * (_rms(t) * gpost_ref[...])
        h2_ref[rs, :] = h2
        hn2 = _rms(h2) * gpre_ref[...] * (1.0 + sc2_ref[...]) + sh2_ref[...]
        hn2_ref[rs, :] = _pack_halves(hn2)
        x_hi = hn2.astype(BF16)
        x_lo = (hn2 - x_hi.astype(F32)).astype(BF16)
        hi = jnp.dot(x_hi, wr_ref[...], preferred_element_type=F32)
        lo = jnp.dot(x_lo, wr_ref[:, :N_EXPERTS], preferred_element_type=F32)
        logits = hi[:, :N_EXPERTS] + (hi[:, N_EXPERTS:] + lo) + br_ref[...]
        lane = lax.broadcasted_iota(jnp.int32, (TM, N_EXPERTS), 1).astype(F32)
        idxs, vals = [], []
        for _ in range(TOP_K):
            m = jnp.max(logits, axis=1, keepdims=True)
            idx = jnp.min(jnp.where(logits == m, lane, float(N_EXPERTS)), axis=1, keepdims=True)
            hot = lane == idx
            idxs.append(idx)
            vals.append(m)
            chosen = hot.astype(F32) if chosen is None else chosen + hot.astype(F32)
            logits = jnp.where(hot, -jnp.inf, logits)
        topk_ref[rs, :] = jnp.concatenate(idxs + vals, axis=1)

    @pl.when(i == 0)
    def _():
        count_ref[...] = jnp.zeros_like(count_ref)

    count_ref[...] = count_ref[...] + jnp.sum(chosen, axis=0, keepdims=True)


def _merge(o_ctx, o_a, o_b, o_c, gates, h, mods3, w_br, w_out_b, g_post3, g_pre_ffn3, w_router, b_router3, l):
    ctx = lambda n: pl.BlockSpec((MERGE_TM, n), lambda i: (jnp.minimum(i, MERGE_CTX_TILES - 1), 0))
    lat_n = lambda n: pl.BlockSpec((MERGE_TM, n), lambda i: (jnp.maximum(i - MERGE_CTX_TILES, 0), 0))
    lat = lambda: lat_n(512)
    mod_index = lambda i: jnp.where(i < MERGE_CTX_TILES, 0,
                                    1 + (i - MERGE_CTX_TILES) // MERGE_LAT_TILES_PER_BATCH)
    mod = lambda j: pl.BlockSpec((None, 1, D_MODEL), lambda i: (mod_index(i), 0, j))
    lw = lambda: pl.BlockSpec((None, 1, D_MODEL), lambda i: (l, 0, 0))
    row = lambda n: pl.BlockSpec((MERGE_TM, n), lambda i: (i, 0))
    const = lambda shape: pl.BlockSpec(shape, lambda i: (0,) * len(shape), pipeline_mode=pl.Buffered(1))
    return pl.pallas_call(
        _merge_kernel,
        grid=(MERGE_TILES,),
        in_specs=[ctx(1536),
                  lat(), lat(), lat(), row(3072), ctx(D_MODEL), lat_n(D_MODEL),
                  mod(2), mod(3), mod(4),
                  const((3, 512, D_MODEL)), const((D_MODEL, D_MODEL)), lw(), lw(),
                  const((D_MODEL, 2 * N_EXPERTS)),
                  pl.BlockSpec((None, 1, N_EXPERTS), lambda i: (l, 0, 0))],
        out_specs=[row(D_MODEL), row(HALF), row(2 * TOP_K),
                   pl.BlockSpec((8, N_EXPERTS), lambda i: (0, 0))],
        out_shape=[jax.ShapeDtypeStruct((N_TOK, D_MODEL), F32),
                   jax.ShapeDtypeStruct((N_TOK, HALF), jnp.int32),
                   jax.ShapeDtypeStruct((N_TOK, 2 * TOP_K), F32),
                   jax.ShapeDtypeStruct((8, N_EXPERTS), F32)],
        compiler_params=_cparams(("arbitrary",)),
        name="merge",
    )(o_ctx, o_a, o_b, o_c, gates, *h, mods3, mods3, mods3, w_br, w_out_b, g_post3, g_pre_ffn3,
      w_router, b_router3)


def _moe_kernel(l, be_ref, first_ref, slot_ref, nxt_ref, rows_ref, x_ref, wgu_hbm, bgu_ref, wd_hbm, bd_ref, y_ref,
                wgu_f32, wd_f32, wgu_bf, wd_bf, sem):
    i = pl.program_id(0)

    def fetch(e, s):
        return (pltpu.make_async_copy(wgu_hbm.at[l, e], wgu_f32.at[s], sem.at[0, s]),
                pltpu.make_async_copy(wd_hbm.at[l, e], wd_f32.at[s], sem.at[1, s]))

    @pl.when(first_ref[i] == 1)
    def _():
        s = slot_ref[i]

        @pl.when(i == 0)
        def _():
            for cp in fetch(be_ref[i], s):
                cp.start()

        for cp in fetch(be_ref[i], s):
            cp.wait()

        @pl.when(nxt_ref[i] >= 0)
        def _():
            for cp in fetch(nxt_ref[i], 1 - s):
                cp.start()

        wgu_bf[...] = wgu_f32[s].astype(BF16)
        wd_bf[...] = wd_f32[s].astype(BF16)

    n_real = rows_ref[i]

    def chain(r0):
        rs = slice(r0, r0 + MOE_CHAIN)
        real = lax.broadcasted_iota(jnp.int32, (MOE_CHAIN, HALF), 0) + r0 < n_real
        xa, xb = _unpack_halves(jnp.where(real, x_ref[rs, :], 0))
        x = jnp.concatenate([xa.astype(BF16), xb.astype(BF16)], axis=1)
        b = bgu_ref[...]
        glu = jnp.dot(x, wgu_bf[:, :D_FF], preferred_element_type=F32) + b[:, :D_FF]
        lin = jnp.dot(x, wgu_bf[:, D_FF:], preferred_element_type=F32) + b[:, D_FF:]
        glu = jnp.minimum(glu, SWIGLU_LIMIT)
        lin = jnp.clip(lin, -SWIGLU_LIMIT, SWIGLU_LIMIT)
        act = glu * (1.0 / (1.0 + jnp.exp(-SWIGLU_ALPHA * glu))) * (lin + 1.0)
        y = jnp.dot(act.astype(BF16), wd_bf[...], preferred_element_type=F32) + bd_ref[...]
        y_ref[rs, :] = _pack_halves(y)

    @pl.when(n_real > MOE_CHAIN)
    def _():
        chain(0)
        chain(MOE_CHAIN)

    @pl.when((n_real > 0) & (n_real <= MOE_CHAIN))
    def _():
        chain(0)
        y_ref[MOE_CHAIN:, :] = jnp.zeros((MOE_TILE - MOE_CHAIN, HALF), jnp.int32)

    @pl.when(n_real == 0)
    def _():
        y_ref[...] = jnp.zeros_like(y_ref)


def _moe(plan, x_slots, w_gate_up, b_gate_up4, w_down, b_down4, l):
    grid_spec = pltpu.PrefetchScalarGridSpec(
        num_scalar_prefetch=5,
        grid=(N_MOE_BLOCKS,),
        in_specs=[pl.BlockSpec((MOE_TILE, HALF), lambda i, be, *_: (i, 0)),
                  pl.BlockSpec(memory_space=pl.ANY),
                  pl.BlockSpec((None, None, 1, 2 * D_FF), lambda i, be, *_: (l, be[i], 0, 0)),
                  pl.BlockSpec(memory_space=pl.ANY),
                  pl.BlockSpec((None, None, 1, D_MODEL), lambda i, be, *_: (l, be[i], 0, 0))],
        out_specs=pl.BlockSpec((MOE_TILE, HALF), lambda i, be, *_: (i, 0)),
        scratch_shapes=[pltpu.VMEM((2, D_MODEL, 2 * D_FF), F32), pltpu.VMEM((2, D_FF, D_MODEL), F32),
                        pltpu.VMEM((D_MODEL, 2 * D_FF), BF16), pltpu.VMEM((D_FF, D_MODEL), BF16),
                        pltpu.SemaphoreType.DMA((2, 2))])
    return pl.pallas_call(
        functools.partial(_moe_kernel, l),
        grid_spec=grid_spec,
        out_shape=jax.ShapeDtypeStruct((N_SLOTS, HALF), jnp.int32),
        compiler_params=_cparams(("arbitrary",)),
        name="moe",
    )(*plan, x_slots, w_gate_up, b_gate_up4, w_down, b_down4)


def _combine_kernel(y0_ref, y1_ref, y2_ref, y3_ref, gate_ref, h_ref, g2_ref, gpost_ref, oc_ref, ol_ref):
    i = pl.program_id(0)
    gate = gate_ref[...]
    ffn = None
    for k, y_ref in enumerate((y0_ref, y1_ref, y2_ref, y3_ref)):
        ya, yb = _unpack_halves(y_ref[...])
        term = gate[:, k:k + 1] * jnp.concatenate([ya, yb], axis=1)
        ffn = term if ffn is None else ffn + term
    out = h_ref[...] + g2_ref[...] * (_rms(ffn) * gpost_ref[...])

    @pl.when(i < CTX_TILES)
    def _():
        oc_ref[...] = out

    @pl.when(i >= CTX_TILES)
    def _():
        ol_ref[...] = out


def _combine(y_tok, gate, h2, mods3, g_post_ffn3, l):
    row = lambda n: pl.BlockSpec((TM, n), lambda i: (i, 0))
    choice = lambda k: pl.BlockSpec((TM, HALF), lambda i: (k * N_TILES + i, 0))
    return pl.pallas_call(
        _combine_kernel,
        grid=(N_TILES,),
        in_specs=[choice(0), choice(1), choice(2), choice(3), row(TOP_K), row(D_MODEL),
                  pl.BlockSpec((None, 1, D_MODEL), lambda i: (_mod_index(i), 0, 5)),
                  pl.BlockSpec((None, 1, D_MODEL), lambda i: (l, 0, 0))],
        out_specs=[pl.BlockSpec((TM, D_MODEL), lambda i: (jnp.minimum(i, CTX_TILES - 1), 0)),
                   pl.BlockSpec((TM, D_MODEL), lambda i: (jnp.maximum(i - CTX_TILES, 0), 0))],
        out_shape=[jax.ShapeDtypeStruct((N_CTX, D_MODEL), F32), jax.ShapeDtypeStruct((N_LAT, D_MODEL), F32)],
        compiler_params=_cparams(("arbitrary",)),
        name="combine",
    )(y_tok, y_tok, y_tok, y_tok, gate, h2, mods3, g_post_ffn3)


def _route_kernel(topk_ref, count_ref, before_ref, dest_ref, gate_ref, base_ref):
    i = pl.program_id(0)

    @pl.when(i == 0)
    def _():
        total = count_ref[...]
        padded = jnp.floor((total + (MOE_TILE - 1.0)) * (1.0 / MOE_TILE)) * MOE_TILE
        before = (lax.broadcasted_iota(jnp.int32, (N_EXPERTS, N_EXPERTS), 0)
                  < lax.broadcasted_iota(jnp.int32, (N_EXPERTS, N_EXPERTS), 1)).astype(F32)
        base_ref[...] = jnp.dot(padded, before, preferred_element_type=F32, precision=lax.Precision.HIGHEST)

    topk = topk_ref[...]
    lane = lax.broadcasted_iota(jnp.int32, (ROUTE_TM, N_EXPERTS), 1).astype(F32)
    hots = [lane == topk[:, k:k + 1] for k in range(TOP_K)]
    vals = [topk[:, TOP_K + k:TOP_K + k + 1] for k in range(TOP_K)]
    chosen = functools.reduce(jnp.add, [h.astype(F32) for h in hots])
    earlier = jnp.dot(before_ref[...], chosen.astype(BF16), preferred_element_type=F32)
    offs = base_ref[0:1, :] + earlier
    dest = [jnp.sum(jnp.where(h, offs, 0.0), axis=1, keepdims=True) for h in hots]
    dest_ref[...] = jnp.concatenate(dest, axis=1).astype(jnp.int32)
    e = [jnp.exp(v - vals[0]) for v in vals]
    den = functools.reduce(jnp.add, e)
    gate_ref[...] = jnp.concatenate(e, axis=1) / den
    base_ref[...] = base_ref[...] + jnp.sum(chosen, axis=0, keepdims=True)


ROUTE_TM = 1024


def _route(topk, counts):
    tile = lambda n: pl.BlockSpec((ROUTE_TM, n), lambda i: (i, 0))
    dest, gate = pl.pallas_call(
        _route_kernel,
        grid=(N_TOK // ROUTE_TM,),
        in_specs=[tile(2 * TOP_K), pl.BlockSpec((8, N_EXPERTS), lambda i: (0, 0)),
                  pl.BlockSpec((ROUTE_TM, ROUTE_TM), lambda i: (0, 0), pipeline_mode=pl.Buffered(1))],
        out_specs=[tile(TOP_K), tile(TOP_K)],
        out_shape=[jax.ShapeDtypeStruct((N_TOK, TOP_K), jnp.int32),
                   jax.ShapeDtypeStruct((N_TOK, TOP_K), F32)],
        scratch_shapes=[pltpu.VMEM((8, N_EXPERTS), F32)],
        compiler_params=_cparams(("arbitrary",)),
        name="route",
    )(topk, counts, jnp.asarray(np.tri(ROUTE_TM, ROUTE_TM, -1), dtype=BF16))
    counts = counts[0].astype(jnp.int32)
    expert = jnp.arange(N_EXPERTS, dtype=jnp.int32)
    padded = (counts + MOE_TILE - 1) // MOE_TILE * MOE_TILE
    pad_end = jnp.sum(jnp.where(expert[None, :] <= expert[:, None], padded[None, :], 0), axis=1)
    block = jnp.arange(N_MOE_BLOCKS, dtype=jnp.int32)
    block_e = jnp.minimum(jnp.sum((pad_end[None, :] <= block[:, None] * MOE_TILE).astype(jnp.int32), axis=1),
                          N_EXPERTS - 1)
    n_valid = pad_end[-1] // MOE_TILE
    mine = expert[None, :] == block_e[:, None]
    pick = lambda v: jnp.sum(jnp.where(mine, v[None, :], 0), axis=1)
    offset = block * MOE_TILE - pick(pad_end - padded)
    valid = block < n_valid
    first = valid & (offset == 0)
    used = counts > 0
    slot = jnp.sum((used[None, :] & (expert[None, :] < block_e[:, None])).astype(jnp.int32), axis=1) % 2
    nxt = jnp.min(jnp.where(used[None, :] & (expert[None, :] > block_e[:, None]), expert[None, :], N_EXPERTS),
                  axis=1)
    nxt = jnp.where(nxt < N_EXPERTS, nxt, -1)
    rows = jnp.where(valid, jnp.clip(pick(counts) - offset, 0, MOE_TILE), 0)
    plan = tuple(a.astype(jnp.int32) for a in (block_e, first, slot, nxt, rows))
    n_workers = N_TOK // SCATTER_TOKENS_PER_WORKER
    dest_sc = dest.reshape(n_workers, SCATTER_TOKENS_PER_WORKER // GATHER_CHUNK, GATHER_CHUNK, TOP_K)
    dest_sc = dest_sc.transpose(0, 1, 3, 2).reshape(n_workers, -1, GATHER_CHUNK)
    return gate, dest.T, dest_sc, plan


def _rope_tables():
    f32 = np.float32
    t = np.arange(DEC_SEQ)
    inv = np.power(f32(ROPE_THETA), -np.arange(ROPE_PAIRS, dtype=f32) / f32(ROPE_PAIRS)).astype(f32)
    row = (t // GRID_W).astype(f32)[:, None] * inv
    col = (t % GRID_W).astype(f32)[:, None] * inv
    zeros = np.zeros_like(row)
    cos = np.concatenate([np.cos(row), np.cos(row), np.cos(col), np.cos(col)], axis=1)
    s1 = np.concatenate([-np.sin(row), zeros, -np.sin(col), zeros], axis=1)
    s2 = np.concatenate([zeros, np.sin(row), zeros, np.sin(col)], axis=1)
    ident = lambda v: np.full((PROJ_TM, HEAD_DIM), v, f32)
    tables = [np.concatenate([x, ident(v)], axis=0) for x, v in ((cos, 1.0), (s1, 0.0), (s2, 0.0))]
    return [jnp.asarray(np.tile(x, (1, 2)), dtype=F32) for x in tables]


def kernel(x_prompt, x_sample, cache_a, cache_b, cache_c, c, c_ctx, w_ada, b_ada, g_pre_mix, g_post_mix,
           g_pre_ffn, g_post_ffn, w_in, g_q_b, g_k_b, sink_a, rpb_c, w_br_a, w_br_b, w_br_c, w_out,
           w_router, b_router, w_gate_up, b_gate_up, w_down, b_down):
    h = (x_prompt.reshape(N_CTX, D_MODEL), x_sample.reshape(N_LAT, D_MODEL))
    cond8 = jnp.concatenate([c_ctx[None], c, jnp.zeros((3, D_MODEL), F32)], axis=0)
    cache_a = cache_a.astype(BF16).reshape(DEC_BATCH, DEPTH, 2, PAST_LEN, W_KV)
    cache_b = cache_b.astype(BF16).reshape(DEC_BATCH, DEPTH, 2, PAST_LEN, W_KV)
    cache_c = cache_c.astype(BF16).reshape(DEC_BATCH, DEPTH, 2, PAST_LEN, W_HEADS)
    cos_t, s1_t, s2_t = _rope_tables()
    bd = jnp.asarray(np.kron(np.eye(256 // HEAD_DIM), np.full((HEAD_DIM, HEAD_DIM), 1.0 / HEAD_DIM)), dtype=BF16)
    vec3 = lambda a: a.reshape(DEPTH, 1, a.shape[-1])
    scale = HEAD_DIM ** -0.5
    states = []
    mods_all = _adaln(cond8, w_ada, vec3(b_ada))

    def prep_layer(stacked, l):
        w_in_, g_q_b_, g_k_b_, w_br_a_, w_br_b_, w_br_c_, w_out_, w_router_, rpb_c_ = stacked
        w = w_in_[l]
        pair_cols = lambda a: a.reshape(D_MODEL, 2, 4, HEAD_DIM).transpose(0, 2, 1, 3).reshape(D_MODEL, 512)
        pair_rows = lambda a: a.reshape(2, 4, HEAD_DIM, D_MODEL).transpose(1, 0, 2, 3).reshape(512, D_MODEL)
        w_in_p = jnp.concatenate(
            [pair_cols(w[:, 0:512]) * scale, pair_cols(w[:, 768:1280]), w[:, 1536:2048] * scale,
             w[:, 2048:2560], w[:, 2560:3072],
             w[:, 512:640], w[:, 1280:1408], w[:, 640:768], w[:, 1408:1536], w[:, 3072:]], axis=1).astype(BF16)
        gq = (jnp.tile(g_q_b_[l], H_B) * scale)[None]
        gk = jnp.tile(g_k_b_[l], KV_B)[None]
        w_br = jnp.stack([pair_rows(w_br_a_[l]), pair_rows(w_br_b_[l]), w_br_c_[l]], axis=0).astype(BF16)
        w_out_b = w_out_[l].astype(BF16)
        w_r_hi = w_router_[l].astype(BF16)
        w_r_lo = (w_router_[l] - w_r_hi.astype(F32)).astype(BF16)
        w_router2 = jnp.concatenate([w_r_hi, w_r_lo], axis=1)
        return w_in_p, gq, gk, w_br, w_out_b, w_router2, _nbr_bias_table(rpb_c_[l])

    stacked = (w_in, g_q_b, g_k_b, w_br_a, w_br_b, w_br_c, w_out, w_router, rpb_c)
    prepped = prep_layer(stacked, 0)
    for l in range(DEPTH):
        w_in_p, gq, gk, w_br, w_out_b, w_router2, bias_t = prepped
        mods3 = mods_all[l].reshape(8, 1, 6 * D_MODEL)
        q_all, kvab, kvc, gates, *states = _proj(h, mods3, vec3(g_pre_mix), w_in_p, gq, gk, bd,
                                                 cos_t, s1_t, s2_t, tuple(states), l)
        if l == DEPTH - 1:
            states = [s.reshape(BATCH, DEPTH, 2, SEQ, n, HEAD_DIM) for s, n in zip(states, (KV_A, KV_B, H_C))]
            states, gates = lax.optimization_barrier((states, gates))
        o_ctx = _ctx_attn(sink_a, q_all, kvab, kvc, l)
        o_a = _win_attn(sink_a, q_all, kvab, cache_a, l)
        o_b = _dense_attn(q_all, kvab, cache_b, l)
        o_c = _nbr_attn(q_all, kvc, cache_c, bias_t, l)
        h2, hn2, topk, counts = _merge(o_ctx, o_a, o_b, o_c, gates, h, mods3, w_br, w_out_b,
                                       vec3(g_post_mix), vec3(g_pre_ffn), w_router2, vec3(b_router), l)
        gate, dest, dest_sc, plan = _route(topk, counts)
        if l + 1 < DEPTH:
            stacked, dest_sc = lax.optimization_barrier((stacked, dest_sc))
            prepped = prep_layer(stacked, l + 1)
        x_slots = _scatter_rows(hn2, dest_sc, N_SLOTS)
        y_slots = _moe(plan, x_slots, w_gate_up,
                       b_gate_up.reshape(DEPTH, N_EXPERTS, 1, 2 * D_FF), w_down,
                       b_down.reshape(DEPTH, N_EXPERTS, 1, D_MODEL), l)
        h = _combine(_gather_rows(y_slots, dest.reshape(-1)), gate, h2, mods3, vec3(g_post_ffn), l)

    return (h[0].reshape(BATCH, SEQ, D_MODEL), h[1].reshape(DEC_BATCH, DEC_SEQ, D_MODEL), *states)
```

```python
import functools

import jax
import jax.numpy as jnp
import numpy as np
from jax import lax
from jax.experimental import pallas as pl
from jax.experimental.pallas import tpu as pltpu
from jax.experimental.pallas import tpu_sc as plsc

D_MODEL = 1024
BATCH = 32
SEQ = 256
DEPTH = 2
DEC_BATCH = 4
DEC_SEQ = 2048
PAST_LEN = 512
GRID_W = 64
HEAD_DIM = 64
H_A = 8
KV_A = 2
H_B = 8
KV_B = 2
H_C = 8
WINDOW_A = 128
NA_ROWS = 8
NA_COLS = 16
ROPE_THETA = 10000.0
ROPE_PAIRS = HEAD_DIM // 4
N_EXPERTS = 32
TOP_K = 4
D_FF = D_MODEL
SWIGLU_ALPHA = 1.702
SWIGLU_LIMIT = 7.0
EPS = 1e-6

W_HEADS = H_A * HEAD_DIM
W_KV = KV_A * HEAD_DIM
N_CTX = BATCH * SEQ
N_LAT = DEC_BATCH * DEC_SEQ
N_TOK = N_CTX + N_LAT
GRID_ROWS = DEC_SEQ // GRID_W
D_IN = 3 * W_HEADS + 4 * W_KV + 2 * W_HEADS + 3 * D_MODEL

TM = 256
N_TILES = N_TOK // TM
CTX_TILES = N_CTX // TM
LAT_TILES_PER_BATCH = DEC_SEQ // TM
TQ = 128
MOE_CHAIN = 256
MOE_TILE = 2 * MOE_CHAIN
N_SLOTS = N_TOK * TOP_K + N_EXPERTS * MOE_TILE
N_MOE_BLOCKS = N_SLOTS // MOE_TILE
NEG = -1e30
VMEM_LIMIT = 56 * 1024 * 1024

BF16 = jnp.bfloat16
F32 = jnp.float32


def _cparams(sem):
    return pltpu.CompilerParams(dimension_semantics=sem, vmem_limit_bytes=VMEM_LIMIT)


def _mod_index(i):
    return jnp.where(i < CTX_TILES, 0, 1 + (i - CTX_TILES) // LAT_TILES_PER_BATCH)


def _rms(x):
    return x * lax.rsqrt(jnp.mean(x * x, axis=-1, keepdims=True) + EPS)


HALF = D_MODEL // 2


def _pack_halves(x):
    hi = lax.bitcast_convert_type(x[:, :HALF].astype(BF16).astype(F32), jnp.uint32)
    lo = lax.bitcast_convert_type(x[:, HALF:].astype(BF16).astype(F32), jnp.uint32)
    return lax.bitcast_convert_type(hi | (lo >> 16), jnp.int32)


def _unpack_halves(w):
    u = lax.bitcast_convert_type(w, jnp.uint32)
    return (lax.bitcast_convert_type(u & jnp.uint32(0xFFFF0000), F32),
            lax.bitcast_convert_type(u << 16, F32))


GATHER_CHUNK = 64


def _gather_rows(table, idx):
    n = idx.shape[0]
    width = table.shape[1]
    info = plsc.get_sparse_core_info()
    n_workers = info.num_cores * info.num_subcores
    per_worker = n // n_workers
    n_chunks = per_worker // GATHER_CHUNK
    assert per_worker * n_workers == n and n_chunks * GATHER_CHUNK == per_worker and n_chunks % 2 == 0
    mesh = plsc.VectorSubcoreMesh(core_axis_name="core", subcore_axis_name="subcore")

    @functools.partial(
        pl.kernel, out_type=jax.ShapeDtypeStruct((n, width), table.dtype), mesh=mesh,
        scratch_types=[pltpu.VMEM((per_worker,), jnp.int32),
                       pltpu.VMEM((2, GATHER_CHUNK, width), table.dtype),
                       pltpu.SemaphoreType.DMA((2,)), pltpu.SemaphoreType.DMA((2,))])
    def gather(table_hbm, idx_hbm, out_hbm, idx_v, rows_v, gather_sem, write_sem):
        worker = lax.axis_index("subcore") * info.num_cores + lax.axis_index("core")
        base = worker * per_worker
        pltpu.sync_copy(idx_hbm.at[pl.ds(base, per_worker)], idx_v)

        def fetch(chunk, slot):
            rows = idx_v.at[pl.ds(chunk * GATHER_CHUNK, GATHER_CHUNK)]
            return pltpu.make_async_copy(table_hbm.at[rows], rows_v.at[slot], gather_sem.at[slot])

        def write(chunk, slot):
            dst = out_hbm.at[pl.ds(base + chunk * GATHER_CHUNK, GATHER_CHUNK)]
            return pltpu.make_async_copy(rows_v.at[slot], dst, write_sem.at[slot])

        fetch(0, 0).start()

        @pl.loop(0, n_chunks, step=2)
        def _(c):
            @pl.when(c > 0)
            def _():
                write(c - 1, 1).wait()

            fetch(c + 1, 1).start()
            fetch(c, 0).wait()
            write(c, 0).start()
            write(c, 0).wait()

            @pl.when(c + 2 < n_chunks)
            def _():
                fetch(c + 2, 0).start()

            fetch(c + 1, 1).wait()
            write(c + 1, 1).start()

        write(n_chunks - 1, 1).wait()

    return gather(table, idx)


SC_WORKERS_V7X = 32
SCATTER_TOKENS_PER_WORKER = N_TOK // SC_WORKERS_V7X


def _scatter_rows(table, dest_sc, n_out):
    width = table.shape[1]
    info = plsc.get_sparse_core_info()
    assert info.num_cores * info.num_subcores == SC_WORKERS_V7X
    per_worker = SCATTER_TOKENS_PER_WORKER
    n_chunks = per_worker // GATHER_CHUNK
    assert n_chunks % 2 == 0 and dest_sc.shape == (SC_WORKERS_V7X, n_chunks * TOP_K, GATHER_CHUNK)
    mesh = plsc.VectorSubcoreMesh(core_axis_name="core", subcore_axis_name="subcore")

    @functools.partial(
        pl.kernel, out_type=jax.ShapeDtypeStruct((n_out, width), table.dtype), mesh=mesh,
        scratch_types=[pltpu.VMEM((n_chunks * TOP_K, GATHER_CHUNK), jnp.int32),
                       pltpu.VMEM((2, GATHER_CHUNK, width), table.dtype),
                       pltpu.SemaphoreType.DMA((2,)), pltpu.SemaphoreType.DMA((2,))])
    def scatter(table_hbm, dest_hbm, out_hbm, idx_v, rows_v, read_sem, write_sem):
        worker = lax.axis_index("subcore") * info.num_cores + lax.axis_index("core")
        base = worker * per_worker
        pltpu.sync_copy(dest_hbm.at[worker], idx_v)

        def read(chunk, slot):
            src = table_hbm.at[pl.ds(base + chunk * GATHER_CHUNK, GATHER_CHUNK)]
            return pltpu.make_async_copy(src, rows_v.at[slot], read_sem.at[slot])

        def writes(chunk, slot):
            return [pltpu.make_async_copy(rows_v.at[slot], out_hbm.at[idx_v.at[chunk * TOP_K + k]],
                                          write_sem.at[slot]) for k in range(TOP_K)]

        read(0, 0).start()

        @pl.loop(0, n_chunks, step=2)
        def _(c):
            @pl.when(c > 0)
            def _():
                for cp in writes(c - 1, 1):
                    cp.wait()

            read(c + 1, 1).start()
            read(c, 0).wait()
            for cp in writes(c, 0):
                cp.start()
            for cp in writes(c, 0):
                cp.wait()

            @pl.when(c + 2 < n_chunks)
            def _():
                read(c + 2, 0).start()

            read(c + 1, 1).wait()
            for cp in writes(c + 1, 1):
                cp.start()

        for cp in writes(n_chunks - 1, 1):
            cp.wait()

    return scatter(table, dest_sc)


def _adaln_kernel(c_ref, w_ref, b_ref, o_ref):
    c = c_ref[...]
    s = c / (1.0 + jnp.exp(-c))
    o_ref[...] = jnp.dot(s, w_ref[...], preferred_element_type=F32,
                         precision=lax.Precision.HIGHEST) + b_ref[...]


def _adaln(cond8, w_ada, b_ada3):
    tn = 768
    return pl.pallas_call(
        _adaln_kernel,
        grid=(DEPTH, 6 * D_MODEL // tn),
        in_specs=[pl.BlockSpec((8, D_MODEL), lambda l, j: (0, 0)),
                  pl.BlockSpec((None, D_MODEL, tn), lambda l, j: (l, 0, j)),
                  pl.BlockSpec((None, 1, tn), lambda l, j: (l, 0, j))],
        out_specs=pl.BlockSpec((None, 8, tn), lambda l, j: (l, 0, j)),
        out_shape=jax.ShapeDtypeStruct((DEPTH, 8, 6 * D_MODEL), F32),
        compiler_params=_cparams(("arbitrary", "arbitrary")),
        name="adaln",
    )(cond8, w_ada, b_ada3)


C_QA, C_QB, C_QC, C_KC, C_VC, C_KAB, C_GL = 0, 512, 1024, 1536, 2048, 2560, 3072

PROJ_TM = 2 * TM
PROJ_TILES = N_TOK // PROJ_TM
PROJ_CTX_TILES = N_CTX // PROJ_TM
PROJ_LAT_TILES_PER_BATCH = DEC_SEQ // PROJ_TM


def _proj_kernel(hc_ref, hl_ref, sh_ref, sc_ref, gpre_ref, w_ref, gq_ref, gk_ref, bd_ref,
                 cos_ref, s1_ref, s2_ref, *rest):
    q_ref, kvab_ref, kvc_ref, gates_ref, sta_ref, stb_ref, stc_ref = rest[-7:]
    i = pl.program_id(0)
    bd = bd_ref[...]

    def rope(t, rs):
        cos, s1, s2 = cos_ref[rs, :], s1_ref[rs, :], s2_ref[rs, :]
        parts = []
        for g in range(t.shape[1] // 128):
            tg = t[:, g * 128:(g + 1) * 128]
            parts.append(tg * cos + pltpu.roll(tg, 112, 1) * s1 + pltpu.roll(tg, 16, 1) * s2)
        return parts[0] if len(parts) == 1 else jnp.concatenate(parts, axis=1)

    def headnorm(t, g):
        sq = (t * t).astype(BF16)
        n = t.shape[1]
        if n == 128:
            ms = jnp.dot(sq, bd[:128, :128], preferred_element_type=F32)
        else:
            ms = jnp.concatenate(
                [jnp.dot(sq[:, c:c + 256], bd, preferred_element_type=F32) for c in range(0, n, 256)],
                axis=1)
        return t * lax.rsqrt(ms + EPS) * g

    kv_f32 = []
    for c in range(PROJ_TM // TM):
        rs = slice(c * TM, (c + 1) * TM)
        hn = _rms(jnp.where(i < PROJ_CTX_TILES, hc_ref[rs, :], hl_ref[rs, :])) * gpre_ref[...]
        hb = (hn * (1.0 + sc_ref[...]) + sh_ref[...]).astype(BF16)
        proj = lambda c0, n, hb=hb: jnp.dot(hb, w_ref[:, c0:c0 + n], preferred_element_type=F32)
        q_ref[rs, 0:512] = rope(proj(C_QA, 512), rs).astype(BF16)
        q_ref[rs, 512:1024] = rope(headnorm(proj(C_QB, 512), gq_ref[...]), rs).astype(BF16)
        q_ref[rs, 1024:1536] = proj(C_QC, 512).astype(BF16)
        kc = proj(C_KC, 512)
        vc = proj(C_VC, 512)
        kvc_ref[rs, 0:512] = kc.astype(BF16)
        kvc_ref[rs, 512:1024] = vc.astype(BF16)
        kab = proj(C_KAB, 512)
        ka = kab[:, 0:128]
        kb = headnorm(kab[:, 128:256], gk_ref[...])
        kvab_ref[rs, 0:128] = rope(ka, rs).astype(BF16)
        kvab_ref[rs, 128:256] = rope(kb, rs).astype(BF16)
        kvab_ref[rs, 256:512] = kab[:, 256:512].astype(BF16)
        for j in range(6):
            gl = proj(C_GL + j * 512, 512)
            gates_ref[rs, j * 512:(j + 1) * 512] = (1.0 / (1.0 + jnp.exp(-gl))).astype(BF16)
        kv_f32.append(((ka, kab[:, 256:384]), (kb, kab[:, 384:512]), (kc, vc)))

    @pl.when(i < PROJ_CTX_TILES)
    def _():
        for c, per_mixer in enumerate(kv_f32):
            for st_ref, (k, v) in zip((sta_ref, stb_ref, stc_ref), per_mixer):
                if len(st_ref.shape) == 5:
                    st_ref[c, 0, 0] = k
                    st_ref[c, 0, 1] = v
                    st_ref[c, 1:] = jnp.zeros((DEPTH - 1,) + tuple(st_ref.shape[2:]), F32)
                else:
                    st_ref[c, 0] = k
                    st_ref[c, 1] = v


def _proj(h, mods3, g_pre3, w_in_p, gq, gk, bd, cos_t, s1_t, s2_t, prev_states, l):
    lat_tile = lambda i: i - PROJ_CTX_TILES
    rope_idx = lambda i: jnp.where(i < PROJ_CTX_TILES, PROJ_LAT_TILES_PER_BATCH,
                                   lat_tile(i) % PROJ_LAT_TILES_PER_BATCH)
    mod_idx = lambda i: jnp.where(i < PROJ_CTX_TILES, 0, 1 + lat_tile(i) // PROJ_LAT_TILES_PER_BATCH)
    ctx_block = lambda i: jnp.minimum(i, PROJ_CTX_TILES - 1)
    const = lambda shape: pl.BlockSpec(shape, lambda i: (0,) * len(shape), pipeline_mode=pl.Buffered(1))
    rope_spec = pl.BlockSpec((PROJ_TM, 128), lambda i: (rope_idx(i), 0))
    row = lambda n: pl.BlockSpec((PROJ_TM, n), lambda i: (i, 0))
    per_step = PROJ_TM // SEQ
    if l == 0:
        state_spec = lambda n: pl.BlockSpec((per_step, DEPTH, 2, SEQ, n), lambda i: (ctx_block(i), 0, 0, 0, 0))
    else:
        state_spec = lambda n: pl.BlockSpec((per_step, None, 2, SEQ, n), lambda i: (ctx_block(i), l, 0, 0, 0))
    state_shape = lambda n: jax.ShapeDtypeStruct((BATCH, DEPTH, 2, SEQ, n), F32)
    n_in = 12
    return pl.pallas_call(
        _proj_kernel,
        grid=(PROJ_TILES,),
        in_specs=[pl.BlockSpec((PROJ_TM, D_MODEL), lambda i: (ctx_block(i), 0)),
                  pl.BlockSpec((PROJ_TM, D_MODEL), lambda i: (jnp.maximum(lat_tile(i), 0), 0)),
                  pl.BlockSpec((None, 1, D_MODEL), lambda i: (mod_idx(i), 0, 0)),
                  pl.BlockSpec((None, 1, D_MODEL), lambda i: (mod_idx(i), 0, 1)),
                  pl.BlockSpec((None, 1, D_MODEL), lambda i: (l, 0, 0)),
                  const((D_MODEL, D_IN)), const((1, 512)), const((1, 128)), const((256, 256)),
                  rope_spec, rope_spec, rope_spec] + [pl.BlockSpec(memory_space=pl.ANY)] * len(prev_states),
        out_specs=[row(1536), row(512), row(1024), row(3072),
                   state_spec(W_KV), state_spec(W_KV), state_spec(W_HEADS)],
        out_shape=[jax.ShapeDtypeStruct((N_TOK, 1536), BF16),
                   jax.ShapeDtypeStruct((N_TOK, 512), BF16),
                   jax.ShapeDtypeStruct((N_TOK, 1024), BF16),
                   jax.ShapeDtypeStruct((N_TOK, 3072), BF16),
                   state_shape(W_KV), state_shape(W_KV), state_shape(W_HEADS)],
        input_output_aliases={n_in + j: 4 + j for j in range(len(prev_states))},
        compiler_params=_cparams(("arbitrary",)),
        name="proj",
    )(*h, mods3, mods3, g_pre3, w_in_p, gq, gk, bd, cos_t, s1_t, s2_t, *prev_states)


def _qk(q, k):
    return lax.dot_general(q, k, (((1,), (1,)), ((), ())), preferred_element_type=F32)


def _softmax_pv(scores, values, sink=None):
    m = functools.reduce(jnp.maximum, [jnp.max(s, axis=-1, keepdims=True) for s in scores])
    if sink is not None:
        m = jnp.maximum(m, sink)
    ps = [jnp.exp(s - m) for s in scores]
    den = functools.reduce(jnp.add, [jnp.sum(p, axis=-1, keepdims=True) for p in ps])
    if sink is not None:
        den = den + jnp.exp(sink - m)
    o = functools.reduce(jnp.add, [jnp.dot(p.astype(BF16), v, preferred_element_type=F32)
                                   for p, v in zip(ps, values)])
    return o / den


def _lo_lanes(rows):
    return lax.broadcasted_iota(jnp.int32, (rows, 128), 1) < HEAD_DIM


def _stack_pairs(q, n_pairs):
    lo = _lo_lanes(q.shape[0])
    zero = jnp.zeros((q.shape[0], 128), q.dtype)
    pairs = [q[:, p * 128:(p + 1) * 128] for p in range(n_pairs)]
    return jnp.concatenate([jnp.where(lo, x, zero) for x in pairs] + [jnp.where(lo, zero, x) for x in pairs],
                           axis=0)


def _unstack_pairs(o, n_pairs):
    rows = o.shape[0] // (2 * n_pairs)
    lo = _lo_lanes(rows)
    return jnp.concatenate(
        [jnp.where(lo, o[p * rows:(p + 1) * rows], o[(n_pairs + p) * rows:(n_pairs + p + 1) * rows])
         for p in range(n_pairs)], axis=1)


def _pair_sink(sink_ref, l, p, rows):
    return jnp.concatenate([jnp.full((rows, 1), sink_ref[l, h], F32) for h in (p, H_A // 2 + p)], axis=0)


def _ctx_attn_kernel(l, sink_ref, q_ref, kvab_ref, kvc_ref, o_ref):
    for p in range(H_A // 2):
        cs = slice(p * 128, (p + 1) * 128)
        o = _softmax_pv([_qk(_stack_pairs(q_ref[:, cs], 1), kvab_ref[:, 0:128])], [kvab_ref[:, 256:384]],
                        _pair_sink(sink_ref, l, p, SEQ))
        o_ref[:, cs] = _unstack_pairs(o, 1).astype(BF16)
    for p in range(H_B // 2):
        cs = slice(512 + p * 128, 512 + (p + 1) * 128)
        o = _softmax_pv([_qk(_stack_pairs(q_ref[:, cs], 1), kvab_ref[:, 128:256])], [kvab_ref[:, 384:512]])
        o_ref[:, cs] = _unstack_pairs(o, 1).astype(BF16)
    for hp in range(H_C // 2):
        cs = slice(hp * 128, (hp + 1) * 128)
        qc = _stack_pairs(q_ref[:, 1024 + hp * 128:1024 + (hp + 1) * 128], 1)
        o = _softmax_pv([_qk(qc, kvc_ref[:, cs])], [kvc_ref[:, 512 + hp * 128:512 + (hp + 1) * 128]])
        o_ref[:, 1024 + hp * 128:1024 + (hp + 1) * 128] = _unstack_pairs(o, 1).astype(BF16)


def _ctx_attn(sink_a, q_all, kvab, kvc, l):
    row = lambda n: pl.BlockSpec((SEQ, n), lambda b: (b, 0))
    return pl.pallas_call(
        functools.partial(_ctx_attn_kernel, l),
        grid=(BATCH,),
        in_specs=[pl.BlockSpec(memory_space=pltpu.SMEM), row(1536), row(512), row(1024)],
        out_specs=row(1536),
        out_shape=jax.ShapeDtypeStruct((N_CTX, 1536), BF16),
        compiler_params=_cparams(("arbitrary",)),
        name="ctx_attn",
    )(sink_a, q_all, kvab, kvc)


def _win_attn_kernel(l, sink_ref, q_ref, prev_ref, cur_ref, nxt_ref, ck_ref, cv_ref, o_ref):
    n = pl.program_id(1)
    nb = DEC_SEQ // TQ
    n_pairs = H_A // 2
    rows = H_A * TQ
    qpos = lax.broadcasted_iota(jnp.int32, (rows, TQ), 0) % TQ
    kpos = lax.broadcasted_iota(jnp.int32, (rows, TQ), 1)
    mask_prev = (kpos >= qpos) & (n > 0)
    mask_next = (kpos <= qpos) & (n < nb - 1)
    ks, vs = slice(0, 128), slice(256, 384)
    qs = _stack_pairs(q_ref[...], n_pairs)
    sink = jnp.concatenate([jnp.full((TQ, 1), sink_ref[l, h], F32) for h in range(H_A)], axis=0)
    s_prev = jnp.where(mask_prev, _qk(qs, prev_ref[:, ks]), NEG)
    s_cur = _qk(qs, cur_ref[:, ks])
    s_next = jnp.where(mask_next, _qk(qs, nxt_ref[:, ks]), NEG)
    s_ctx = _qk(qs, ck_ref[...])
    o = _softmax_pv([s_prev, s_cur, s_next, s_ctx],
                    [prev_ref[:, vs], cur_ref[:, vs], nxt_ref[:, vs], cv_ref[...]], sink)
    o_ref[...] = _unstack_pairs(o, n_pairs).astype(BF16)


def _win_attn(sink_a, q_all, kvab, cache_a, l):
    nb = DEC_SEQ // TQ
    base = N_CTX // TQ
    kv_spec = lambda f: pl.BlockSpec((TQ, 512), lambda b, n: (base + b * nb + f(n), 0))
    cache_spec = lambda s: pl.BlockSpec((None, None, None, PAST_LEN, W_KV), lambda b, n: (b, l, s, 0, 0))
    return pl.pallas_call(
        functools.partial(_win_attn_kernel, l),
        grid=(DEC_BATCH, nb),
        in_specs=[pl.BlockSpec(memory_space=pltpu.SMEM),
                  pl.BlockSpec((TQ, 512), lambda b, n: (base + b * nb + n, 0)),
                  kv_spec(lambda n: jnp.maximum(n - 1, 0)), kv_spec(lambda n: n),
                  kv_spec(lambda n: jnp.minimum(n + 1, nb - 1)),
                  cache_spec(0), cache_spec(1)],
        out_specs=pl.BlockSpec((TQ, 512), lambda b, n: (b * nb + n, 0)),
        out_shape=jax.ShapeDtypeStruct((N_LAT, 512), BF16),
        compiler_params=_cparams(("arbitrary", "arbitrary")),
        name="win_attn",
    )(sink_a, q_all, kvab, kvab, kvab, cache_a, cache_a)


DENSE_KEY_CHUNK = 1024
DENSE_TQ = 512


def _online_softmax_pv(q, key_chunks, value_chunks, keep_lo):
    one = jnp.ones((), BF16)
    m = acc = None
    for k, v in zip(key_chunks, value_chunks):
        s = _qk(q, k)
        m_new = jnp.max(s, axis=-1, keepdims=True)
        if m is not None:
            m_new = jnp.maximum(m, m_new)
        p = jnp.exp(s - m_new).astype(BF16)
        lo = _lo_lanes(v.shape[0])
        pv = jnp.dot(p, jnp.where(lo, v, one) if keep_lo else jnp.where(lo, one, v), preferred_element_type=F32)
        acc = pv if acc is None else acc * jnp.exp(m - m_new) + pv
        m = m_new
    return acc * pltpu.roll(1.0 / acc, HEAD_DIM, 1)


def _dense_attn_kernel(q_ref, kv_ref, ck_ref, cv_ref, o_ref):
    qs = _stack_pairs(q_ref[...], 4)
    half = qs.shape[0] // 2
    starts = range(0, DEC_SEQ, DENSE_KEY_CHUNK)
    keys = [kv_ref[c:c + DENSE_KEY_CHUNK, 128:256] for c in starts] + [ck_ref[...]]
    values = [kv_ref[c:c + DENSE_KEY_CHUNK, 384:512] for c in starts] + [cv_ref[...]]
    outs = [_online_softmax_pv(qs[g * half:(g + 1) * half], keys, values, g == 0) for g in range(KV_B)]
    o_ref[...] = _unstack_pairs(jnp.concatenate(outs, axis=0), 4).astype(BF16)


def _dense_attn(q_all, kvab, cache_b, l):
    nb = DEC_SEQ // DENSE_TQ
    base = N_CTX // DENSE_TQ
    cache_spec = lambda s: pl.BlockSpec((None, None, None, PAST_LEN, W_KV), lambda b, n: (b, l, s, 0, 0))
    return pl.pallas_call(
        _dense_attn_kernel,
        grid=(DEC_BATCH, nb),
        in_specs=[pl.BlockSpec((DENSE_TQ, 512), lambda b, n: (base + b * nb + n, 1)),
                  pl.BlockSpec((DEC_SEQ, 512), lambda b, n: (N_CTX // DEC_SEQ + b, 0)),
                  cache_spec(0), cache_spec(1)],
        out_specs=pl.BlockSpec((DENSE_TQ, 512), lambda b, n: (b * nb + n, 0)),
        out_shape=jax.ShapeDtypeStruct((N_LAT, 512), BF16),
        compiler_params=_cparams(("arbitrary", "arbitrary")),
        name="dense_attn",
    )(q_all, kvab, cache_b, cache_b)


NBR_BAND = 4
NBR_Q = NBR_BAND * GRID_W
NBR_WIN_ROWS = 12
NBR_N_BANDS = GRID_ROWS // NBR_BAND
NBR_KBLK = NBR_Q
NBR_WIN_BLOCKS = NBR_WIN_ROWS * GRID_W // NBR_KBLK
NBR_LAST_KB = (GRID_ROWS - NBR_WIN_ROWS) * GRID_W // NBR_KBLK


def _nbr_window_block(band):
    return jnp.clip(band - 1, 0, NBR_LAST_KB)


def _nbr_attn_kernel(q_ref, k0_ref, k1_ref, k2_ref, ck_ref, cv_ref, bias_ref, o_ref):
    k_refs = (k0_ref, k1_ref, k2_ref)
    for hp in range(H_C // 2):
        cs = slice(hp * 128, (hp + 1) * 128)
        vs = slice(512 + hp * 128, 512 + (hp + 1) * 128)
        qs = _stack_pairs(q_ref[:, cs], 1)
        bias = jnp.concatenate([bias_ref[2 * hp], bias_ref[2 * hp + 1]], axis=0)
        scores = [_qk(qs, kr[:, cs]) + bias[:, j * NBR_KBLK:(j + 1) * NBR_KBLK] for j, kr in enumerate(k_refs)]
        scores.append(_qk(qs, ck_ref[:, cs]))
        o = _softmax_pv(scores, [kr[:, vs] for kr in k_refs] + [cv_ref[:, cs]])
        o_ref[:, cs] = _unstack_pairs(o, 1).astype(BF16)


def _nbr_attn(q_all, kvc, cache_c, bias_t, l):
    q_base = N_CTX // NBR_Q
    k_base = N_CTX // NBR_KBLK
    blocks_per_batch = DEC_SEQ // NBR_KBLK
    cache_spec = lambda s: pl.BlockSpec((None, None, None, PAST_LEN, W_HEADS), lambda band, b: (b, l, s, 0, 0))
    key_spec = lambda j: pl.BlockSpec(
        (NBR_KBLK, 1024), lambda band, b: (k_base + b * blocks_per_batch + _nbr_window_block(band) + j, 0))
    band_type = lambda band: jnp.where(band == 0, 0, jnp.where(band == NBR_N_BANDS - 1, 2, 1))
    return pl.pallas_call(
        _nbr_attn_kernel,
        grid=(NBR_N_BANDS, DEC_BATCH),
        in_specs=[pl.BlockSpec((NBR_Q, 512), lambda band, b: (q_base + b * NBR_N_BANDS + band, 2)),
                  key_spec(0), key_spec(1), key_spec(2), cache_spec(0), cache_spec(1),
                  pl.BlockSpec((None, H_C, NBR_Q, NBR_WIN_ROWS * GRID_W),
                               lambda band, b: (band_type(band), 0, 0, 0))],
        out_specs=pl.BlockSpec((NBR_Q, 512), lambda band, b: (b * NBR_N_BANDS + band, 0)),
        out_shape=jax.ShapeDtypeStruct((N_LAT, 512), BF16),
        compiler_params=_cparams(("arbitrary", "arbitrary")),
        name="nbr_attn",
    )(q_all, kvc, kvc, kvc, cache_c, cache_c, bias_t)


def _nbr_bias_table(rpb_l):
    c = np.arange(GRID_W)[:, None]
    kc = np.arange(GRID_W)[None, :]
    c_start = np.clip(c - NA_COLS // 2, 0, GRID_W - NA_COLS)
    valid = (kc >= c_start) & (kc < c_start + NA_COLS)
    pad = GRID_W - NA_COLS
    rpb_pad = jnp.pad(rpb_l.astype(F32), ((0, 0), (0, 0), (pad, pad)))
    toeplitz = jnp.stack([rpb_pad[:, :, GRID_W - 1 - q:2 * GRID_W - 1 - q] for q in range(GRID_W)], axis=2)
    t = jnp.where(valid[None, None], toeplitz, NEG)
    neg = jnp.full((H_C, GRID_W, GRID_W), NEG, F32)
    tables = []
    for r0 in (0, NBR_BAND, GRID_ROWS - NBR_BAND):
        k0 = min(max(r0 - NA_ROWS // 2, 0), GRID_ROWS - NBR_WIN_ROWS)
        rows = []
        for dq in range(NBR_BAND):
            r = r0 + dq
            start = min(max(r - NA_ROWS // 2, 0), GRID_ROWS - NA_ROWS)
            cols = []
            for i in range(NBR_WIN_ROWS):
                kr = k0 + i
                cols.append(t[:, kr - r + NA_ROWS - 1] if start <= kr < start + NA_ROWS else neg)
            rows.append(jnp.concatenate(cols, axis=2))
        tables.append(jnp.concatenate(rows, axis=1))
    return jnp.stack(tables, axis=0)


MERGE_TM = 2 * TM
MERGE_TILES = N_TOK // MERGE_TM
MERGE_CTX_TILES = N_CTX // MERGE_TM
MERGE_LAT_TILES_PER_BATCH = DEC_SEQ // MERGE_TM


def _merge_kernel(octx_ref, oa_ref, ob_ref, oc_ref, gates_ref, hc_ref, hl_ref, g1_ref, sh2_ref, sc2_ref,
                  wbr_ref, wout_ref, gpost_ref, gpre_ref, wr_ref, br_ref,
                  h2_ref, hn2_ref, topk_ref, count_ref):
    i = pl.program_id(0)
    is_ctx = i < MERGE_CTX_TILES
    chosen = None
    for r0 in range(0, MERGE_TM, TM):
        rs = slice(r0, r0 + TM)
        merged = None
        for j, lat_ref in enumerate((oa_ref, ob_ref, oc_ref)):
            o = jnp.where(is_ctx, octx_ref[rs, j * 512:(j + 1) * 512], lat_ref[rs, :])
            br = jnp.dot(o, wbr_ref[j], preferred_element_type=F32)
            term = gates_ref[rs, j * D_MODEL:(j + 1) * D_MODEL].astype(F32) * br
            merged = term if merged is None else merged + term
        t = jnp.dot(merged.astype(BF16), wout_ref[...], preferred_element_type=F32)
        h2 = jnp.where(is_ctx, hc_ref[rs, :], hl_ref[rs, :]) + g1_ref[...] * (_rms(t) * gpost_ref[...])
        h2_ref[rs, :] = h2
        hn2 = _rms(h2) * gpre_ref[...] * (1.0 + sc2_ref[...]) + sh2_ref[...]
        hn2_ref[rs, :] = _pack_halves(hn2)
        x_hi = hn2.astype(BF16)
        x_lo = (hn2 - x_hi.astype(F32)).astype(BF16)
        hi = jnp.dot(x_hi, wr_ref[...], preferred_element_type=F32)
        lo = jnp.dot(x_lo, wr_ref[:, :N_EXPERTS], preferred_element_type=F32)
        logits = hi[:, :N_EXPERTS] + (hi[:, N_EXPERTS:] + lo) + br_ref[...]
        lane = lax.broadcasted_iota(jnp.int32, (TM, N_EXPERTS), 1).astype(F32)
        idxs, vals = [], []
        for _ in range(TOP_K):
            m = jnp.max(logits, axis=1, keepdims=True)
            idx = jnp.min(jnp.where(logits == m, lane, float(N_EXPERTS)), axis=1, keepdims=True)
            hot = lane == idx
            idxs.append(idx)
            vals.append(m)
            chosen = hot.astype(F32) if chosen is None else chosen + hot.astype(F32)
            logits = jnp.where(hot, -jnp.inf, logits)
        topk_ref[rs, :] = jnp.concatenate(idxs + vals, axis=1)

    @pl.when(i == 0)
    def _():
        count_ref[...] = jnp.zeros_like(count_ref)

    count_ref[...] = count_ref[...] + jnp.sum(chosen, axis=0, keepdims=True)


def _merge(o_ctx, o_a, o_b, o_c, gates, h, mods3, w_br, w_out_b, g_post3, g_pre_ffn3, w_router, b_router3, l):
    ctx = lambda n: pl.BlockSpec((MERGE_TM, n), lambda i: (jnp.minimum(i, MERGE_CTX_TILES - 1), 0))
    lat_n = lambda n: pl.BlockSpec((MERGE_TM, n), lambda i: (jnp.maximum(i - MERGE_CTX_TILES, 0), 0))
    lat = lambda: lat_n(512)
    mod_index = lambda i: jnp.where(i < MERGE_CTX_TILES, 0,
                                    1 + (i - MERGE_CTX_TILES) // MERGE_LAT_TILES_PER_BATCH)
    mod = lambda j: pl.BlockSpec((None, 1, D_MODEL), lambda i: (mod_index(i), 0, j))
    lw = lambda: pl.BlockSpec((None, 1, D_MODEL), lambda i: (l, 0, 0))
    row = lambda n: pl.BlockSpec((MERGE_TM, n), lambda i: (i, 0))
    const = lambda shape: pl.BlockSpec(shape, lambda i: (0,) * len(shape), pipeline_mode=pl.Buffered(1))
    return pl.pallas_call(
        _merge_kernel,
        grid=(MERGE_TILES,),
        in_specs=[ctx(1536),
                  lat(), lat(), lat(), row(3072), ctx(D_MODEL), lat_n(D_MODEL),
                  mod(2), mod(3), mod(4),
                  const((3, 512, D_MODEL)), const((D_MODEL, D_MODEL)), lw(), lw(),
                  const((D_MODEL, 2 * N_EXPERTS)),
                  pl.BlockSpec((None, 1, N_EXPERTS), lambda i: (l, 0, 0))],
        out_specs=[row(D_MODEL), row(HALF), row(2 * TOP_K),
                   pl.BlockSpec((8, N_EXPERTS), lambda i: (0, 0))],
        out_shape=[jax.ShapeDtypeStruct((N_TOK, D_MODEL), F32),
                   jax.ShapeDtypeStruct((N_TOK, HALF), jnp.int32),
                   jax.ShapeDtypeStruct((N_TOK, 2 * TOP_K), F32),
                   jax.ShapeDtypeStruct((8, N_EXPERTS), F32)],
        compiler_params=_cparams(("arbitrary",)),
        name="merge",
    )(o_ctx, o_a, o_b, o_c, gates, *h, mods3, mods3, mods3, w_br, w_out_b, g_post3, g_pre_ffn3,
      w_router, b_router3)


def _moe_kernel(l, be_ref, first_ref, slot_ref, nxt_ref, rows_ref, x_ref, wgu_hbm, bgu_ref, wd_hbm, bd_ref, y_ref,
                wgu_f32, wd_f32, wgu_bf, wd_bf, sem):
    i = pl.program_id(0)

    def fetch(e, s):
        return (pltpu.make_async_copy(wgu_hbm.at[l, e], wgu_f32.at[s], sem.at[0, s]),
                pltpu.make_async_copy(wd_hbm.at[l, e], wd_f32.at[s], sem.at[1, s]))

    @pl.when(first_ref[i] == 1)
    def _():
        s = slot_ref[i]

        @pl.when(i == 0)
        def _():
            for cp in fetch(be_ref[i], s):
                cp.start()

        for cp in fetch(be_ref[i], s):
            cp.wait()

        @pl.when(nxt_ref[i] >= 0)
        def _():
            for cp in fetch(nxt_ref[i], 1 - s):
                cp.start()

        wgu_bf[...] = wgu_f32[s].astype(BF16)
        wd_bf[...] = wd_f32[s].astype(BF16)

    n_real = rows_ref[i]

    def chain(r0):
        rs = slice(r0, r0 + MOE_CHAIN)
        real = lax.broadcasted_iota(jnp.int32, (MOE_CHAIN, HALF), 0) + r0 < n_real
        xa, xb = _unpack_halves(jnp.where(real, x_ref[rs, :], 0))
        x = jnp.concatenate([xa.astype(BF16), xb.astype(BF16)], axis=1)
        b = bgu_ref[...]
        glu = jnp.dot(x, wgu_bf[:, :D_FF], preferred_element_type=F32) + b[:, :D_FF]
        lin = jnp.dot(x, wgu_bf[:, D_FF:], preferred_element_type=F32) + b[:, D_FF:]
        glu = jnp.minimum(glu, SWIGLU_LIMIT)
        lin = jnp.clip(lin, -SWIGLU_LIMIT, SWIGLU_LIMIT)
        act = glu * (1.0 / (1.0 + jnp.exp(-SWIGLU_ALPHA * glu))) * (lin + 1.0)
        y = jnp.dot(act.astype(BF16), wd_bf[...], preferred_element_type=F32) + bd_ref[...]
        y_ref[rs, :] = _pack_halves(y)

    @pl.when(n_real > MOE_CHAIN)
    def _():
        chain(0)
        chain(MOE_CHAIN)

    @pl.when((n_real > 0) & (n_real <= MOE_CHAIN))
    def _():
        chain(0)
        y_ref[MOE_CHAIN:, :] = jnp.zeros((MOE_TILE - MOE_CHAIN, HALF), jnp.int32)

    @pl.when(n_real == 0)
    def _():
        y_ref[...] = jnp.zeros_like(y_ref)


def _moe(plan, x_slots, w_gate_up, b_gate_up4, w_down, b_down4, l):
    grid_spec = pltpu.PrefetchScalarGridSpec(
        num_scalar_prefetch=5,
        grid=(N_MOE_BLOCKS,),
        in_specs=[pl.BlockSpec((MOE_TILE, HALF), lambda i, be, *_: (i, 0)),
                  pl.BlockSpec(memory_space=pl.ANY),
                  pl.BlockSpec((None, None, 1, 2 * D_FF), lambda i, be, *_: (l, be[i], 0, 0)),
                  pl.BlockSpec(memory_space=pl.ANY),
                  pl.BlockSpec((None, None, 1, D_MODEL), lambda i, be, *_: (l, be[i], 0, 0))],
        out_specs=pl.BlockSpec((MOE_TILE, HALF), lambda i, be, *_: (i, 0)),
        scratch_shapes=[pltpu.VMEM((2, D_MODEL, 2 * D_FF), F32), pltpu.VMEM((2, D_FF, D_MODEL), F32),
                        pltpu.VMEM((D_MODEL, 2 * D_FF), BF16), pltpu.VMEM((D_FF, D_MODEL), BF16),
                        pltpu.SemaphoreType.DMA((2, 2))])
    return pl.pallas_call(
        functools.partial(_moe_kernel, l),
        grid_spec=grid_spec,
        out_shape=jax.ShapeDtypeStruct((N_SLOTS, HALF), jnp.int32),
        compiler_params=_cparams(("arbitrary",)),
        name="moe",
    )(*plan, x_slots, w_gate_up, b_gate_up4, w_down, b_down4)


def _combine_kernel(y0_ref, y1_ref, y2_ref, y3_ref, gate_ref, h_ref, g2_ref, gpost_ref, oc_ref, ol_ref):
    i = pl.program_id(0)
    gate = gate_ref[...]
    ffn = None
    for k, y_ref in enumerate((y0_ref, y1_ref, y2_ref, y3_ref)):
        ya, yb = _unpack_halves(y_ref[...])
        term = gate[:, k:k + 1] * jnp.concatenate([ya, yb], axis=1)
        ffn = term if ffn is None else ffn + term
    out = h_ref[...] + g2_ref[...] * (_rms(ffn) * gpost_ref[...])

    @pl.when(i < CTX_TILES)
    def _():
        oc_ref[...] = out

    @pl.when(i >= CTX_TILES)
    def _():
        ol_ref[...] = out


def _combine(y_tok, gate, h2, mods3, g_post_ffn3, l):
    row = lambda n: pl.BlockSpec((TM, n), lambda i: (i, 0))
    choice = lambda k: pl.BlockSpec((TM, HALF), lambda i: (k * N_TILES + i, 0))
    return pl.pallas_call(
        _combine_kernel,
        grid=(N_TILES,),
        in_specs=[choice(0), choice(1), choice(2), choice(3), row(TOP_K), row(D_MODEL),
                  pl.BlockSpec((None, 1, D_MODEL), lambda i: (_mod_index(i), 0, 5)),
                  pl.BlockSpec((None, 1, D_MODEL), lambda i: (l, 0, 0))],
        out_specs=[pl.BlockSpec((TM, D_MODEL), lambda i: (jnp.minimum(i, CTX_TILES - 1), 0)),
                   pl.BlockSpec((TM, D_MODEL), lambda i: (jnp.maximum(i - CTX_TILES, 0), 0))],
        out_shape=[jax.ShapeDtypeStruct((N_CTX, D_MODEL), F32), jax.ShapeDtypeStruct((N_LAT, D_MODEL), F32)],
        compiler_params=_cparams(("arbitrary",)),
        name="combine",
    )(y_tok, y_tok, y_tok, y_tok, gate, h2, mods3, g_post_ffn3)


def _route_kernel(topk_ref, count_ref, before_ref, dest_ref, gate_ref, base_ref):
    i = pl.program_id(0)

    @pl.when(i == 0)
    def _():
        total = count_ref[...]
        padded = jnp.floor((total + (MOE_TILE - 1.0)) * (1.0 / MOE_TILE)) * MOE_TILE
        before = (lax.broadcasted_iota(jnp.int32, (N_EXPERTS, N_EXPERTS), 0)
                  < lax.broadcasted_iota(jnp.int32, (N_EXPERTS, N_EXPERTS), 1)).astype(F32)
        base_ref[...] = jnp.dot(padded, before, preferred_element_type=F32, precision=lax.Precision.HIGHEST)

    topk = topk_ref[...]
    lane = lax.broadcasted_iota(jnp.int32, (ROUTE_TM, N_EXPERTS), 1).astype(F32)
    hots = [lane == topk[:, k:k + 1] for k in range(TOP_K)]
    vals = [topk[:, TOP_K + k:TOP_K + k + 1] for k in range(TOP_K)]
    chosen = functools.reduce(jnp.add, [h.astype(F32) for h in hots])
    earlier = jnp.dot(before_ref[...], chosen.astype(BF16), preferred_element_type=F32)
    offs = base_ref[0:1, :] + earlier
    dest = [jnp.sum(jnp.where(h, offs, 0.0), axis=1, keepdims=True) for h in hots]
    dest_ref[...] = jnp.concatenate(dest, axis=1).astype(jnp.int32)
    e = [jnp.exp(v - vals[0]) for v in vals]
    den = functools.reduce(jnp.add, e)
    gate_ref[...] = jnp.concatenate(e, axis=1) / den
    base_ref[...] = base_ref[...] + jnp.sum(chosen, axis=0, keepdims=True)


ROUTE_TM = 1024


def _route(topk, counts):
    tile = lambda n: pl.BlockSpec((ROUTE_TM, n), lambda i: (i, 0))
    dest, gate = pl.pallas_call(
        _route_kernel,
        grid=(N_TOK // ROUTE_TM,),
        in_specs=[tile(2 * TOP_K), pl.BlockSpec((8, N_EXPERTS), lambda i: (0, 0)),
                  pl.BlockSpec((ROUTE_TM, ROUTE_TM), lambda i: (0, 0), pipeline_mode=pl.Buffered(1))],
        out_specs=[tile(TOP_K), tile(TOP_K)],
        out_shape=[jax.ShapeDtypeStruct((N_TOK, TOP_K), jnp.int32),
                   jax.ShapeDtypeStruct((N_TOK, TOP_K), F32)],
        scratch_shapes=[pltpu.VMEM((8, N_EXPERTS), F32)],
        compiler_params=_cparams(("arbitrary",)),
        name="route",
    )(topk, counts, jnp.asarray(np.tri(ROUTE_TM, ROUTE_TM, -1), dtype=BF16))
    counts = counts[0].astype(jnp.int32)
    expert = jnp.arange(N_EXPERTS, dtype=jnp.int32)
    padded = (counts + MOE_TILE - 1) // MOE_TILE * MOE_TILE
    pad_end = jnp.sum(jnp.where(expert[None, :] <= expert[:, None], padded[None, :], 0), axis=1)
    block = jnp.arange(N_MOE_BLOCKS, dtype=jnp.int32)
    block_e = jnp.minimum(jnp.sum((pad_end[None, :] <= block[:, None] * MOE_TILE).astype(jnp.int32), axis=1),
                          N_EXPERTS - 1)
    n_valid = pad_end[-1] // MOE_TILE
    mine = expert[None, :] == block_e[:, None]
    pick = lambda v: jnp.sum(jnp.where(mine, v[None, :], 0), axis=1)
    offset = block * MOE_TILE - pick(pad_end - padded)
    valid = block < n_valid
    first = valid & (offset == 0)
    used = counts > 0
    slot = jnp.sum((used[None, :] & (expert[None, :] < block_e[:, None])).astype(jnp.int32), axis=1) % 2
    nxt = jnp.min(jnp.where(used[None, :] & (expert[None, :] > block_e[:, None]), expert[None, :], N_EXPERTS),
                  axis=1)
    nxt = jnp.where(nxt < N_EXPERTS, nxt, -1)
    rows = jnp.where(valid, jnp.clip(pick(counts) - offset, 0, MOE_TILE), 0)
    plan = tuple(a.astype(jnp.int32) for a in (block_e, first, slot, nxt, rows))
    n_workers = N_TOK // SCATTER_TOKENS_PER_WORKER
    dest_sc = dest.reshape(n_workers, SCATTER_TOKENS_PER_WORKER // GATHER_CHUNK, GATHER_CHUNK, TOP_K)
    dest_sc = dest_sc.transpose(0, 1, 3, 2).reshape(n_workers, -1, GATHER_CHUNK)
    return gate, dest.T, dest_sc, plan


def _rope_tables():
    f32 = np.float32
    t = np.arange(DEC_SEQ)
    inv = np.power(f32(ROPE_THETA), -np.arange(ROPE_PAIRS, dtype=f32) / f32(ROPE_PAIRS)).astype(f32)
    row = (t // GRID_W).astype(f32)[:, None] * inv
    col = (t % GRID_W).astype(f32)[:, None] * inv
    zeros = np.zeros_like(row)
    cos = np.concatenate([np.cos(row), np.cos(row), np.cos(col), np.cos(col)], axis=1)
    s1 = np.concatenate([-np.sin(row), zeros, -np.sin(col), zeros], axis=1)
    s2 = np.concatenate([zeros, np.sin(row), zeros, np.sin(col)], axis=1)
    ident = lambda v: np.full((PROJ_TM, HEAD_DIM), v, f32)
    tables = [np.concatenate([x, ident(v)], axis=0) for x, v in ((cos, 1.0), (s1, 0.0), (s2, 0.0))]
    return [jnp.asarray(np.tile(x, (1, 2)), dtype=F32) for x in tables]


def kernel(x_prompt, x_sample, cache_a, cache_b, cache_c, c, c_ctx, w_ada, b_ada, g_pre_mix, g_post_mix,
           g_pre_ffn, g_post_ffn, w_in, g_q_b, g_k_b, sink_a, rpb_c, w_br_a, w_br_b, w_br_c, w_out,
           w_router, b_router, w_gate_up, b_gate_up, w_down, b_down):
    h = (x_prompt.reshape(N_CTX, D_MODEL), x_sample.reshape(N_LAT, D_MODEL))
    cond8 = jnp.concatenate([c_ctx[None], c, jnp.zeros((3, D_MODEL), F32)], axis=0)
    cache_a = cache_a.astype(BF16).reshape(DEC_BATCH, DEPTH, 2, PAST_LEN, W_KV)
    cache_b = cache_b.astype(BF16).reshape(DEC_BATCH, DEPTH, 2, PAST_LEN, W_KV)
    cache_c = cache_c.astype(BF16).reshape(DEC_BATCH, DEPTH, 2, PAST_LEN, W_HEADS)
    cos_t, s1_t, s2_t = _rope_tables()
    bd = jnp.asarray(np.kron(np.eye(256 // HEAD_DIM), np.full((HEAD_DIM, HEAD_DIM), 1.0 / HEAD_DIM)), dtype=BF16)
    vec3 = lambda a: a.reshape(DEPTH, 1, a.shape[-1])
    scale = HEAD_DIM ** -0.5
    states = []
    mods_all = _adaln(cond8, w_ada, vec3(b_ada))

    def prep_layer(stacked, l):
        w_in_, g_q_b_, g_k_b_, w_br_a_, w_br_b_, w_br_c_, w_out_, w_router_, rpb_c_ = stacked
        w = w_in_[l]
        pair_cols = lambda a: a.reshape(D_MODEL, 2, 4, HEAD_DIM).transpose(0, 2, 1, 3).reshape(D_MODEL, 512)
        pair_rows = lambda a: a.reshape(2, 4, HEAD_DIM, D_MODEL).transpose(1, 0, 2, 3).reshape(512, D_MODEL)
        w_in_p = jnp.concatenate(
            [pair_cols(w[:, 0:512]) * scale, pair_cols(w[:, 768:1280]), w[:, 1536:2048] * scale,
             w[:, 2048:2560], w[:, 2560:3072],
             w[:, 512:640], w[:, 1280:1408], w[:, 640:768], w[:, 1408:1536], w[:, 3072:]], axis=1).astype(BF16)
        gq = (jnp.tile(g_q_b_[l], H_B) * scale)[None]
        gk = jnp.tile(g_k_b_[l], KV_B)[None]
        w_br = jnp.stack([pair_rows(w_br_a_[l]), pair_rows(w_br_b_[l]), w_br_c_[l]], axis=0).astype(BF16)
        w_out_b = w_out_[l].astype(BF16)
        w_r_hi = w_router_[l].astype(BF16)
        w_r_lo = (w_router_[l] - w_r_hi.astype(F32)).astype(BF16)
        w_router2 = jnp.concatenate([w_r_hi, w_r_lo], axis=1)
        return w_in_p, gq, gk, w_br, w_out_b, w_router2, _nbr_bias_table(rpb_c_[l])

    stacked = (w_in, g_q_b, g_k_b, w_br_a, w_br_b, w_br_c, w_out, w_router, rpb_c)
    prepped = prep_layer(stacked, 0)
    for l in range(DEPTH):
        w_in_p, gq, gk, w_br, w_out_b, w_router2, bias_t = prepped
        mods3 = mods_all[l].reshape(8, 1, 6 * D_MODEL)
        q_all, kvab, kvc, gates, *states = _proj(h, mods3, vec3(g_pre_mix), w_in_p, gq, gk, bd,
                                                 cos_t, s1_t, s2_t, tuple(states), l)
        if l == DEPTH - 1:
            states = [s.reshape(BATCH, DEPTH, 2, SEQ, n, HEAD_DIM) for s, n in zip(states, (KV_A, KV_B, H_C))]
            states, gates = lax.optimization_barrier((states, gates))
        o_ctx = _ctx_attn(sink_a, q_all, kvab, kvc, l)
        o_a = _win_attn(sink_a, q_all, kvab, cache_a, l)
        o_b = _dense_attn(q_all, kvab, cache_b, l)
        o_c = _nbr_attn(q_all, kvc, cache_c, bias_t, l)
        h2, hn2, topk, counts = _merge(o_ctx, o_a, o_b, o_c, gates, h, mods3, w_br, w_out_b,
                                       vec3(g_post_mix), vec3(g_pre_ffn), w_router2, vec3(b_router), l)
        gate, dest, dest_sc, plan = _route(topk, counts)
        if l + 1 < DEPTH:
            stacked, dest_sc = lax.optimization_barrier((stacked, dest_sc))
            prepped = prep_layer(stacked, l + 1)
        x_slots = _scatter_rows(hn2, dest_sc, N_SLOTS)
        y_slots = _moe(plan, x_slots, w_gate_up,
                       b_gate_up.reshape(DEPTH, N_EXPERTS, 1, 2 * D_FF), w_down,
                       b_down.reshape(DEPTH, N_EXPERTS, 1, D_MODEL), l)
        h = _combine(_gather_rows(y_slots, dest.reshape(-1)), gate, h2, mods3, vec3(g_post_ffn), l)

    return (h[0].reshape(BATCH, SEQ, D_MODEL), h[1].reshape(DEC_BATCH, DEC_SEQ, D_MODEL), *states)
```
